```python
import math
import jax, jax.numpy as jnp
from jax import lax
import numpy as np

D_MODEL = 1024
BATCH = 8
SEQ = 2048
DEPTH = 2
DEC_BATCH = 32
DEC_SEQ = 4
PAST_LEN = 8192
PAGE_SIZE = 128

N_EVEN = (DEPTH + 1) // 2
N_ODD = DEPTH // 2
H_A = 8
D_A = 64
Q_BLOCK = 128
MIX_A = H_A * D_A
H_B = 4
DK_B = 128
DV_B = 128
CONV_W = 4
DELTA_CHUNK = 64
CONV_DIM = H_B * (2 * DK_B + DV_B)
MIX_B = H_B * DV_B
MIX_EVEN = MIX_A + MIX_B
EVEN_WIDTHS = (MIX_A, MIX_A, MIX_A, H_A, CONV_DIM, H_B, H_B, MIX_B)
EVEN_PROJ = MIX_A * 3 + H_A + CONV_DIM + 2 * H_B + MIX_B
D_C = 2 * D_MODEL
H_C = 8
GD_C = D_C // H_C
CHUNK_C = 128
N_GROUPS = 4
EXPERTS_PER_GROUP = 8
N_EXPERTS = N_GROUPS * EXPERTS_PER_GROUP
TOP_K_INNER = 2
D_EXPERT = 256
EPS = 1e-6

kernel_name = 'fox_deltanet_gmlp_hmoe_step'


def rms_norm(x, w):
    xf = x.astype(jnp.float32)
    y = xf * lax.rsqrt(jnp.mean(xf * xf, axis=-1, keepdims=True) + EPS)
    return (y * w.astype(jnp.float32)).astype(x.dtype)


def layer_norm(x, w, b):
    xf = x.astype(jnp.float32)
    mu = jnp.mean(xf, axis=-1, keepdims=True)
    xc = xf - mu
    y = xc * lax.rsqrt(jnp.mean(xc * xc, axis=-1, keepdims=True) + EPS)
    return (y * w.astype(jnp.float32) + b.astype(jnp.float32)).astype(x.dtype)


def l2_normalize(x):
    return x * lax.rsqrt(jnp.sum(x * x, axis=-1, keepdims=True) + EPS)


def split_cols(z, widths):
    out, start = [], 0
    for w in widths:
        out.append(z[..., start:start + w])
        start += w
    return out


def causal_conv(xc, w):
    L = xc.shape[1] - (CONV_W - 1)
    out = xc[:, 0:L] * w[0]
    for i in range(1, CONV_W):
        out = out + xc[:, i:i + L] * w[i]
    return out


def fox_prompt(q, k, v, logf):
    n, L, H, D = q.shape
    scale = D ** -0.5
    c = jnp.swapaxes(jnp.cumsum(logf, axis=1), 1, 2)
    kpos = jnp.arange(L)

    def block(i):
        start = i * Q_BLOCK
        qb = lax.dynamic_slice_in_dim(q, start, Q_BLOCK, axis=1)
        cq = lax.dynamic_slice_in_dim(c, start, Q_BLOCK, axis=2)
        s = jnp.einsum('nqhd,nkhd->nhqk', qb, k).astype(jnp.float32) * scale
        s = s + cq[..., :, None] - c[:, :, None, :]
        qpos = start + jnp.arange(Q_BLOCK)
        s = jnp.where(kpos[None, :] <= qpos[:, None], s, -jnp.inf)
        p = jax.nn.softmax(s, axis=-1).astype(v.dtype)
        return jnp.einsum('nhqk,nkhd->nqhd', p, v)

    out = lax.map(block, jnp.arange(L // Q_BLOCK))
    return jnp.moveaxis(out, 0, 1).reshape(n, L, H, D)


def fox_sample(q, k, v, logf, k_past, v_past, logf_past):
    n, Q, H, D = q.shape
    P = k_past.shape[1]
    scale = D ** -0.5
    c_past = jnp.cumsum(logf_past.astype(jnp.float32), axis=1)
    c_new = c_past[:, -1:] + jnp.cumsum(logf, axis=1)
    cq = jnp.swapaxes(c_new, 1, 2)
    s_past = jnp.einsum('nqhd,nkhd->nhqk', q, k_past).astype(jnp.float32) * scale
    s_past = s_past + cq[..., :, None] - jnp.swapaxes(c_past, 1, 2)[:, :, None, :]
    s_new = jnp.einsum('nqhd,nkhd->nhqk', q, k).astype(jnp.float32) * scale
    s_new = s_new + cq[..., :, None] - cq[:, :, None, :]
    causal = jnp.arange(Q)[:, None] >= jnp.arange(Q)[None, :]
    s_new = jnp.where(causal, s_new, -jnp.inf)
    p = jax.nn.softmax(jnp.concatenate([s_past, s_new], axis=-1), axis=-1)
    return (jnp.einsum('nhqk,nkhd->nqhd', p[..., :P].astype(v.dtype), v_past)
            + jnp.einsum('nhqk,nkhd->nqhd', p[..., P:].astype(v.dtype), v))


def gated_delta_chunked(q, k, v, g, beta, s0):
    n, L = q.shape[0], q.shape[1]
    C = DELTA_CHUNK
    pad = (-L) % C
    nc = (L + pad) // C

    def to_chunks(a):
        a = jnp.pad(a, [(0, 0), (0, pad)] + [(0, 0)] * (a.ndim - 2))
        a = a.reshape((n, nc, C) + a.shape[2:])
        return jnp.swapaxes(jnp.moveaxis(a, 1, 0), 2, 3)

    qc, kc, vc, gch, bc = to_chunks(q), to_chunks(k), to_chunks(v), to_chunks(g), to_chunks(beta)
    gc = jnp.cumsum(gch, axis=-1)
    idx = jnp.arange(C)
    incl = idx[:, None] >= idx[None, :]
    strict = idx[:, None] > idx[None, :]
    decay = jnp.exp(jnp.where(incl, gc[..., :, None] - gc[..., None, :], -jnp.inf))
    kb = kc * bc[..., None]
    vb = vc * bc[..., None]
    a_mat = jnp.where(strict, jnp.einsum('...id,...jd->...ij', kb, kc) * decay, 0.0)
    eye = jnp.broadcast_to(jnp.eye(C, dtype=jnp.float32), a_mat.shape)
    t_mat = lax.linalg.triangular_solve(a_mat, eye, left_side=True, lower=True, unit_diagonal=True)
    w_v = jnp.einsum('...ij,...je->...ie', t_mat, vb)
    w_k = jnp.einsum('...ij,...jd->...id', t_mat, kb * jnp.exp(gc)[..., None])
    qk = jnp.where(incl, jnp.einsum('...id,...jd->...ij', qc, kc) * decay, 0.0)
    q_dec = qc * jnp.exp(gc)[..., None]
    k_dec = kc * jnp.exp(gc[..., -1:] - gc)[..., None]
    g_last = jnp.exp(gc[..., -1])

    def step(S, xs):
        w_v_c, w_k_c, qk_c, q_dec_c, k_dec_c, gl = xs
        v_new = w_v_c - jnp.einsum('nhcd,nhde->nhce', w_k_c, S)
        o = jnp.einsum('nhcd,nhde->nhce', q_dec_c, S) + jnp.einsum('nhij,nhje->nhie', qk_c, v_new)
        S = S * gl[..., None, None] + jnp.einsum('nhcd,nhce->nhde', k_dec_c, v_new)
        return S, o

    s_final, o = lax.scan(step, s0, (w_v, w_k, qk, q_dec, k_dec, g_last))
    o = jnp.moveaxis(jnp.swapaxes(o, 2, 3), 0, 1).reshape(n, nc * C, H_B, v.shape[-1])
    return o[:, :L], s_final


def even_mixer(h, conv_prefix, s0, past, w_in, w_out, f_bias, conv_w, a_log, dt_bias, o_norm_w):
    n, L, _ = h.shape
    z = jnp.einsum('nld,dc->nlc', h, w_in)
    qa, ka, va, fa, qkv_b, a_b, b_b, g_b = split_cols(z, EVEN_WIDTHS)
    q = qa.reshape(n, L, H_A, D_A)
    k = ka.reshape(n, L, H_A, D_A)
    v = va.reshape(n, L, H_A, D_A)
    logf = jax.nn.log_sigmoid(fa.astype(jnp.float32) + f_bias.astype(jnp.float32))
    if past is None:
        o_a = fox_prompt(q, k, v, logf)
    else:
        o_a = fox_sample(q, k, v, logf, *past)
    xc = jnp.concatenate([conv_prefix.astype(qkv_b.dtype), qkv_b], axis=1)
    new_conv = xc[:, L:]
    conv = jax.nn.silu(causal_conv(xc, conv_w)).astype(jnp.float32)
    qd, kd, vd = split_cols(conv, (H_B * DK_B, H_B * DK_B, H_B * DV_B))
    qd = l2_normalize(qd.reshape(n, L, H_B, DK_B)) * (DK_B ** -0.5)
    kd = l2_normalize(kd.reshape(n, L, H_B, DK_B))
    vd = vd.reshape(n, L, H_B, DV_B)
    g = -jnp.exp(a_log.astype(jnp.float32)) * jax.nn.softplus(a_b.astype(jnp.float32) + dt_bias.astype(jnp.float32))
    beta = jax.nn.sigmoid(b_b.astype(jnp.float32))
    o_b, s_new = gated_delta_chunked(qd, kd, vd, g, beta, s0.astype(jnp.float32))
    o_b = rms_norm(o_b, o_norm_w) * jax.nn.silu(g_b.astype(jnp.float32).reshape(n, L, H_B, DV_B))
    mixed = jnp.concatenate([o_a.reshape(n, L, MIX_A), o_b.reshape(n, L, MIX_B).astype(h.dtype)], axis=-1)
    out = jnp.einsum('nlc,cd->nld', mixed, w_out)
    return out, k, v, logf, s_new, new_conv


def odd_mixer(h, w_in, ln_w, ln_b, w_s, b_s, w_out):
    n, L, _ = h.shape
    z = jax.nn.gelu(jnp.einsum('nld,dc->nlc', h, w_in), approximate=False)
    u, v = z[..., :D_C], z[..., D_C:]
    v = layer_norm(v, ln_w, ln_b)
    pad = (-L) % CHUNK_C
    nc = (L + pad) // CHUNK_C
    vb = jnp.pad(v, ((0, 0), (0, pad), (0, 0))).reshape(n, nc, CHUNK_C, H_C, GD_C)
    idx = jnp.arange(CHUNK_C)
    w_m = jnp.where(idx[:, None] >= idx[None, :], w_s, 0.0).astype(v.dtype)
    mixed = jnp.einsum('hts,ncshd->ncthd', w_m, vb) + jnp.swapaxes(b_s, 0, 1)[None, None, :, :, None]
    mixed = mixed.reshape(n, nc * CHUNK_C, D_C)[:, :L]
    out = jnp.einsum('nlc,cd->nld', u * mixed, w_out)
    return out, v


def hier_moe(h, w_rg, w_re, w_gate, w_up, w_down):
    pg_logits = jnp.einsum('nld,dg->nlg', h, w_rg).astype(jnp.float32)
    pg = jax.nn.softmax(pg_logits, axis=-1)
    g_sel = jnp.argmax(pg_logits, axis=-1)
    p_sel = jnp.take_along_axis(pg, g_sel[..., None], axis=-1)[..., 0]
    le = jnp.einsum('nld,de->nle', h, w_re).astype(jnp.float32)
    le = le.reshape(le.shape[:-1] + (N_GROUPS, EXPERTS_PER_GROUP))
    le_sel = jnp.take_along_axis(le, g_sel[..., None, None], axis=2)[..., 0, :]
    top_v, top_i = lax.top_k(le_sel, TOP_K_INNER)
    weights = p_sel[..., None] * jax.nn.softmax(top_v, axis=-1)
    expert_idx = g_sel[..., None] * EXPERTS_PER_GROUP + top_i
    gates = jnp.sum(jax.nn.one_hot(expert_idx, N_EXPERTS, dtype=jnp.float32) * weights[..., None], axis=-2)
    hg = jnp.einsum('nld,edf->nlef', h, w_gate)
    hu = jnp.einsum('nld,edf->nlef', h, w_up)
    act = jax.nn.silu(hg) * hu * gates.astype(h.dtype)[..., None]
    return jnp.einsum('nlef,efd->nld', act, w_down)


def setup_inputs(seed: int = 0) -> dict:
    key = jax.random.key(seed)
    ks = jax.random.split(key, 32)
    f32 = jnp.float32
    n_pages = PAST_LEN // PAGE_SIZE
    n_used = DEC_BATCH * n_pages
    n_pool = n_used + max(1, n_used // 4)

    def nrm(k, shape, scale):
        return jax.random.normal(k, shape, f32) * scale

    x_prompt = nrm(ks[0], (BATCH, SEQ, D_MODEL), 1.0)
    x_sample = nrm(ks[1], (DEC_BATCH, DEC_SEQ, D_MODEL), 1.0)
    cache_k = nrm(ks[2], (N_EVEN, n_pool, PAGE_SIZE, H_A, D_A), 1.0)
    cache_v = nrm(ks[3], (N_EVEN, n_pool, PAGE_SIZE, H_A, D_A), 1.0)
    cache_logf = jax.nn.log_sigmoid(nrm(ks[4], (N_EVEN, n_pool, PAGE_SIZE, H_A), 1.0) + 3.0)
    state_delta = nrm(ks[5], (N_EVEN, DEC_BATCH, H_B, DK_B, DV_B), 0.1)
    state_conv = nrm(ks[6], (N_EVEN, DEC_BATCH, CONV_W - 1, CONV_DIM), 1.0)
    page_table = jax.random.permutation(ks[7], n_pool)[:n_used].reshape(DEC_BATCH, n_pages).astype(jnp.int32)
    norm_mix = 1.0 + nrm(ks[8], (DEPTH, D_MODEL), 0.02)
    norm_ffn = 1.0 + nrm(ks[9], (DEPTH, D_MODEL), 0.02)
    norm_final = 1.0 + nrm(ks[10], (D_MODEL,), 0.02)
    w_in_even = nrm(ks[11], (N_EVEN, D_MODEL, EVEN_PROJ), D_MODEL ** -0.5)
    w_out_even = nrm(ks[12], (N_EVEN, MIX_EVEN, D_MODEL), MIX_EVEN ** -0.5)
    fox_forget_bias = jax.random.uniform(ks[13], (N_EVEN, H_A), f32, minval=1.0, maxval=4.0)
    dn_conv_w = nrm(ks[14], (N_EVEN, CONV_W, CONV_DIM), CONV_W ** -0.5)
    dn_a_log = jnp.log(jax.random.uniform(ks[15], (N_EVEN, H_B), f32, minval=1.0, maxval=16.0))
    dt = jnp.exp(jax.random.uniform(ks[16], (N_EVEN, H_B), f32, minval=math.log(1e-3), maxval=math.log(1e-1)))
    dn_dt_bias = dt + jnp.log(-jnp.expm1(-dt))
    dn_norm_w = 1.0 + nrm(ks[17], (N_EVEN, DV_B), 0.02)
    w_in_odd = nrm(ks[18], (N_ODD, D_MODEL, 2 * D_C), D_MODEL ** -0.5)
    gm_ln_w = 1.0 + nrm(ks[19], (N_ODD, D_C), 0.02)
    gm_ln_b = nrm(ks[20], (N_ODD, D_C), 0.02)
    gm_spatial_w = nrm(ks[21], (N_ODD, H_C, CHUNK_C, CHUNK_C), CHUNK_C ** -0.5)
    gm_spatial_b = 1.0 + nrm(ks[22], (N_ODD, H_C, CHUNK_C), 0.02)
    w_out_odd = nrm(ks[23], (N_ODD, D_C, D_MODEL), D_C ** -0.5)
    moe_router_group = nrm(ks[24], (DEPTH, D_MODEL, N_GROUPS), D_MODEL ** -0.5)
    moe_router_expert = nrm(ks[25], (DEPTH, D_MODEL, N_EXPERTS), D_MODEL ** -0.5)
    moe_w_gate = nrm(ks[26], (DEPTH, N_EXPERTS, D_MODEL, D_EXPERT), D_MODEL ** -0.5)
    moe_w_up = nrm(ks[27], (DEPTH, N_EXPERTS, D_MODEL, D_EXPERT), D_MODEL ** -0.5)
    moe_w_down = nrm(ks[28], (DEPTH, N_EXPERTS, D_EXPERT, D_MODEL), D_EXPERT ** -0.5)
    return {'x_prompt': x_prompt, 'x_sample': x_sample, 'cache_k': cache_k, 'cache_v': cache_v,
            'cache_logf': cache_logf, 'state_delta': state_delta, 'state_conv': state_conv,
            'page_table': page_table, 'norm_mix': norm_mix, 'norm_ffn': norm_ffn, 'norm_final': norm_final,
            'w_in_even': w_in_even, 'w_out_even': w_out_even, 'fox_forget_bias': fox_forget_bias,
            'dn_conv_w': dn_conv_w, 'dn_a_log': dn_a_log, 'dn_dt_bias': dn_dt_bias, 'dn_norm_w': dn_norm_w,
            'w_in_odd': w_in_odd, 'gm_ln_w': gm_ln_w, 'gm_ln_b': gm_ln_b, 'gm_spatial_w': gm_spatial_w,
            'gm_spatial_b': gm_spatial_b, 'w_out_odd': w_out_odd, 'moe_router_group': moe_router_group,
            'moe_router_expert': moe_router_expert, 'moe_w_gate': moe_w_gate, 'moe_w_up': moe_w_up,
            'moe_w_down': moe_w_down}


def reference(x_prompt, x_sample, cache_k, cache_v, cache_logf, state_delta, state_conv, page_table,
              norm_mix, norm_ffn, norm_final, w_in_even, w_out_even, fox_forget_bias, dn_conv_w,
              dn_a_log, dn_dt_bias, dn_norm_w, w_in_odd, gm_ln_w, gm_ln_b, gm_spatial_w, gm_spatial_b,
              w_out_odd, moe_router_group, moe_router_expert, moe_w_gate, moe_w_up, moe_w_down):
    n_p = x_prompt.shape[0]
    n_s = x_sample.shape[0]
    past_len = page_table.shape[1] * cache_k.shape[2]
    xp, xs = x_prompt, x_sample
    kp, vp, lfp, sdp, scp = [], [], [], [], []
    ksm, vsm, lfs, sds, scs = [], [], [], [], []
    chunk_v = []
    for layer in range(DEPTH):
        hp = rms_norm(xp, norm_mix[layer])
        hs = rms_norm(xs, norm_mix[layer])
        if layer % 2 == 0:
            e = layer // 2
            prm = (w_in_even[e], w_out_even[e], fox_forget_bias[e], dn_conv_w[e], dn_a_log[e],
                   dn_dt_bias[e], dn_norm_w[e])
            conv0 = jnp.zeros((n_p, CONV_W - 1, CONV_DIM), xp.dtype)
            s0 = jnp.zeros((n_p, H_B, DK_B, DV_B), jnp.float32)
            mp, k1, v1, lf1, sd1, sc1 = even_mixer(hp, conv0, s0, None, *prm)
            past = (cache_k[e, page_table].reshape(n_s, past_len, H_A, D_A),
                    cache_v[e, page_table].reshape(n_s, past_len, H_A, D_A),
                    cache_logf[e, page_table].reshape(n_s, past_len, H_A))
            ms, k2, v2, lf2, sd2, sc2 = even_mixer(hs, state_conv[e], state_delta[e], past, *prm)
            kp.append(k1); vp.append(v1); lfp.append(lf1); sdp.append(sd1); scp.append(sc1)
            ksm.append(k2); vsm.append(v2); lfs.append(lf2); sds.append(sd2); scs.append(sc2)
        else:
            o = layer // 2
            prm = (w_in_odd[o], gm_ln_w[o], gm_ln_b[o], gm_spatial_w[o], gm_spatial_b[o], w_out_odd[o])
            mp, _ = odd_mixer(hp, *prm)
            ms, v_rows = odd_mixer(hs, *prm)
            chunk_v.append(v_rows)
        xp = xp + mp
        xs = xs + ms
        mprm = (moe_router_group[layer], moe_router_expert[layer], moe_w_gate[layer], moe_w_up[layer],
                moe_w_down[layer])
        xp = xp + hier_moe(rms_norm(xp, norm_ffn[layer]), *mprm)
        xs = xs + hier_moe(rms_norm(xs, norm_ffn[layer]), *mprm)
    y_prompt = rms_norm(xp, norm_final)
    y_sample = rms_norm(xs, norm_final)
    return (y_prompt, y_sample, jnp.stack(kp), jnp.stack(vp), jnp.stack(lfp), jnp.stack(sdp), jnp.stack(scp),
            jnp.stack(ksm), jnp.stack(vsm), jnp.stack(lfs), jnp.stack(sds), jnp.stack(scs), jnp.stack(chunk_v))
```

```python
import functools

import jax
import jax.numpy as jnp
from jax import lax
from jax.experimental import pallas as pl
from jax.experimental.pallas import tpu as pltpu

F32 = jnp.float32
BF16 = jnp.bfloat16
EPS = 1e-6
LANE = 128
VMEM_LIMIT = 56 * 1024 * 1024
HIGHEST = lax.Precision.HIGHEST

H_A, D_A = 8, 64
H_B, DK_B = 4, 128
CONV_W = 4
DELTA_CHUNK = 64
N_GROUPS, EXPERTS_PER_GROUP = 4, 8
N_EXPERTS = N_GROUPS * EXPERTS_PER_GROUP
CHUNK_C = 128
H_C = 8


def _params(*sem):
    return pltpu.CompilerParams(dimension_semantics=sem, vmem_limit_bytes=VMEM_LIMIT)


def _row_tile(t, pref):
    return pref if t % pref == 0 else t


def _rms(x, w):
    return x * lax.rsqrt(jnp.mean(x * x, axis=-1, keepdims=True) + EPS) * w


def _softplus_tail(z):
    return jnp.log1p(jnp.exp(-jnp.abs(z)))


def _even_in_kernel(x_ref, nw_ref, w_ref, bias_ref, alog_ref,
                    q_ref, k_ref, v_ref, c_ref, gz_ref, s_ref):
    hb = _rms(x_ref[...], nw_ref[...]).astype(BF16)

    def proj(lo, hi):
        return jnp.dot(hb, w_ref[:, lo:hi], preferred_element_type=F32)

    q_ref[...] = proj(0, 512)
    k_ref[...] = proj(512, 1024)
    v_ref[...] = proj(1024, 1536)
    c_ref[...] = proj(1536, 3072)
    gz_ref[...] = proj(3072, 3584)
    z = proj(3584, 3712) + bias_ref[...]
    tail = _softplus_tail(z)
    logf = jnp.minimum(z, 0.0) - tail
    g = -jnp.exp(alog_ref[...]) * (jnp.maximum(z, 0.0) + tail)
    beta = 1.0 / (1.0 + jnp.exp(-z))
    lane = lax.broadcasted_iota(jnp.int32, z.shape, 1)
    s_ref[...] = jnp.where(lane < 8, logf, jnp.where(lane < 12, g, jnp.where(lane < 16, beta, 0.0)))


def _even_in(x, nw, w, bias, alog):
    t, d = x.shape
    tm = _row_tile(t, 512)
    widths = (512, 512, 512, 1536, 512, LANE)
    row = lambda wd: pl.BlockSpec((tm, wd), lambda i: (i, 0))
    full = lambda a: pl.BlockSpec(a.shape, lambda i: (0, 0))
    return pl.pallas_call(
        _even_in_kernel,
        grid=(t // tm,),
        in_specs=[row(d), full(nw), full(w), full(bias), full(alog)],
        out_specs=[row(wd) for wd in widths],
        out_shape=[jax.ShapeDtypeStruct((t, wd), F32) for wd in widths],
        compiler_params=_params("parallel"),
        name="even_in_proj",
    )(x, nw, w, bias, alog)


def _cumsum_kernel(x_ref, o_ref):
    rows, length = x_ref.shape
    r = lax.broadcasted_iota(jnp.int32, (LANE, LANE), 0)
    c = lax.broadcasted_iota(jnp.int32, (LANE, LANE), 1)
    upper = (r <= c).astype(F32)
    carry = jnp.zeros((rows, 1), F32)
    for b in range(length // LANE):
        blk = jnp.dot(x_ref[:, b * LANE:(b + 1) * LANE], upper,
                      preferred_element_type=F32, precision=HIGHEST) + carry
        o_ref[:, b * LANE:(b + 1) * LANE] = blk
        carry = blk[:, LANE - 1:LANE]


def _cumsum_lanes(x):
    rows, length = x.shape
    tr = 8
    return pl.pallas_call(
        _cumsum_kernel,
        grid=(rows // tr,),
        in_specs=[pl.BlockSpec((tr, length), lambda i: (i, 0))],
        out_specs=pl.BlockSpec((tr, length), lambda i: (i, 0)),
        out_shape=jax.ShapeDtypeStruct((rows, length), F32),
        compiler_params=_params("parallel"),
        name="logf_cumsum",
    )(x)


def _fox_prompt_kernel(q_ref, k_ref, v_ref, c_ref, o_ref, *, tq):
    qi = pl.program_id(2)
    hp = pl.program_id(1)
    scale = D_A ** -0.5
    lane = lax.broadcasted_iota(jnp.int32, (tq, LANE), 1)
    first = lane < D_A
    q = q_ref[0] * scale
    qs = jnp.concatenate([jnp.where(first, q, 0.0), jnp.where(first, 0.0, q)], axis=0).astype(BF16)
    q0 = pl.multiple_of(qi * tq, tq)

    def crow(h, start):
        return c_ref[0, pl.ds(2 * hp + h, 1), pl.ds(start, tq)]

    cq = [crow(h, q0)[:, 0:1] for h in range(2)]

    def scores(j0):
        kb = k_ref[0, pl.ds(j0, tq), :].astype(BF16)
        s = lax.dot_general(qs, kb, (((1,), (1,)), ((), ())), preferred_element_type=F32)
        bias = jnp.concatenate(
            [jnp.broadcast_to(cq[h] - crow(h, j0), (tq, tq)) for h in range(2)], axis=0)
        return s + bias

    def update(carry, s, j0):
        m, l, acc = carry
        m_new = jnp.maximum(m, jnp.max(s, axis=-1, keepdims=True))
        alpha = jnp.exp(m - m_new)
        p = jnp.exp(s - m_new)
        l = alpha * l + jnp.sum(p, axis=-1, keepdims=True)
        vb = v_ref[0, pl.ds(j0, tq), :].astype(BF16)
        acc = alpha * acc + jnp.dot(p.astype(BF16), vb, preferred_element_type=F32)
        return m_new, l, acc

    def body(j, carry):
        j0 = pl.multiple_of(j * tq, tq)
        return update(carry, scores(j0), j0)

    init = (jnp.full((2 * tq, 1), -jnp.inf, F32), jnp.zeros((2 * tq, 1), F32),
            jnp.zeros((2 * tq, LANE), F32))
    carry = lax.fori_loop(0, qi, body, init)
    r = lax.broadcasted_iota(jnp.int32, (2 * tq, tq), 0)
    cidx = lax.broadcasted_iota(jnp.int32, (2 * tq, tq), 1)
    causal = cidx <= jnp.where(r >= tq, r - tq, r)
    _, l, acc = update(carry, jnp.where(causal, scores(q0), -jnp.inf), q0)
    out = acc / l
    o_ref[0] = jnp.where(first, out[:tq], out[tq:]).astype(o_ref.dtype)


def _fox_prompt(q, k, v, c):
    n, length, width = q.shape
    tq = 256 if length % 256 == 0 else length
    grid = (n, width // LANE, length // tq)
    return pl.pallas_call(
        functools.partial(_fox_prompt_kernel, tq=tq),
        grid=grid,
        in_specs=[
            pl.BlockSpec((1, tq, LANE), lambda b, h, i: (b, i, h)),
            pl.BlockSpec((1, length, LANE), lambda b, h, i: (b, 0, h)),
            pl.BlockSpec((1, length, LANE), lambda b, h, i: (b, 0, h)),
            pl.BlockSpec((1, H_A, length), lambda b, h, i: (b, 0, 0)),
        ],
        out_specs=pl.BlockSpec((1, tq, LANE), lambda b, h, i: (b, i, h)),
        out_shape=jax.ShapeDtypeStruct((n, length, width), BF16),
        compiler_params=_params("parallel", "parallel", "arbitrary"),
        name="fox_prompt",
    )(q, k, v, c)


def _even_out_kernel(x_ref, a_ref, b_ref, w_ref, o_ref):
    half = a_ref.shape[1]
    acc = jnp.dot(a_ref[...], w_ref[:half, :], preferred_element_type=F32)
    acc += jnp.dot(b_ref[...], w_ref[half:, :], preferred_element_type=F32)
    o_ref[...] = x_ref[...] + acc


def _even_out(x, oa, ob, w):
    t, d = x.shape
    tm = _row_tile(t, 512)
    row = lambda a: pl.BlockSpec((tm, a.shape[1]), lambda i: (i, 0))
    return pl.pallas_call(
        _even_out_kernel,
        grid=(t // tm,),
        in_specs=[row(x), row(oa), row(ob), pl.BlockSpec(w.shape, lambda i: (0, 0))],
        out_specs=row(x),
        out_shape=jax.ShapeDtypeStruct((t, d), F32),
        compiler_params=_params("parallel"),
        name="even_out_proj",
    )(x, oa, ob, w)


def _split3(x):
    hi = x.astype(BF16)
    r1 = x - hi.astype(F32)
    mid = r1.astype(BF16)
    lo = (r1 - mid.astype(F32)).astype(BF16)
    return hi, mid, lo


def _dot(a, b):
    return jnp.dot(a, b, preferred_element_type=F32)


def _dot_nt(a, b):
    return lax.dot_general(a, b, (((1,), (1,)), ((), ())), preferred_element_type=F32)


def _dot_tn(a, b):
    return lax.dot_general(a, b, (((0,), (0,)), ((), ())), preferred_element_type=F32)


def _silu(x):
    return x / (1.0 + jnp.exp(-x))


def _delta_prep_kernel(x_ref, xprev_ref, pre_ref, cw_ref, sm_ref, smt_ref,
                       wv_ref, wk_ref, qd_ref, kd_ref, qk_ref, gl_ref, *, ta, valid):
    i = pl.program_id(1)
    c = DELTA_CHUNK
    dk = DK_B
    x = x_ref[0]
    prev = jnp.where(i == 0, pre_ref[0], xprev_ref[0])
    xcat = jnp.concatenate([prev, x], axis=0)
    conv = x * cw_ref[CONV_W - 1:CONV_W, :]
    for s in range(1, CONV_W):
        conv += pltpu.roll(xcat, s, axis=0)[8:] * cw_ref[CONV_W - 1 - s:CONV_W - s, :]
    act = _silu(conv)

    ri = lax.broadcasted_iota(jnp.int32, (c, c), 0)
    ci = lax.broadcasted_iota(jnp.int32, (c, c), 1)
    incl = ri >= ci
    strict = ri > ci
    tri = incl.astype(BF16)
    upper = (ri <= ci).astype(BF16)
    lane = lax.broadcasted_iota(jnp.int32, (c, LANE), 1)
    sm = sm_ref[0]
    smt = smt_ref[0]

    for h in range(H_B):
        qa = act[:, h * dk:(h + 1) * dk]
        ka = act[:, (H_B + h) * dk:(H_B + h + 1) * dk]
        va = act[:, (2 * H_B + h) * dk:(2 * H_B + h + 1) * dk]
        qn = qa * lax.rsqrt(jnp.sum(qa * qa, axis=-1, keepdims=True) + EPS) * (dk ** -0.5)
        kn = ka * lax.rsqrt(jnp.sum(ka * ka, axis=-1, keepdims=True) + EPS)
        for s in range(ta // c):
            r0 = s * c
            rows = lax.broadcasted_iota(jnp.int32, (c, 1), 0) + (i * ta + r0)
            cols = lax.broadcasted_iota(jnp.int32, (1, c), 1) + (i * ta + r0)
            smc = sm[r0:r0 + c]
            g_col = jnp.where(rows < valid, jnp.sum(jnp.where(lane == 8 + h, smc, 0.0), axis=-1, keepdims=True), 0.0)
            b_col = jnp.where(rows < valid, jnp.sum(jnp.where(lane == 12 + h, smc, 0.0), axis=-1, keepdims=True), 0.0)
            g_row = jnp.where(cols < valid, smt[8 + h:9 + h, r0:r0 + c], 0.0)
            gcb = sum(_dot(tri, jnp.broadcast_to(p, (c, LANE))) for p in _split3(g_col))
            gcr = sum(_dot(jnp.broadcast_to(p, (8, c)), upper) for p in _split3(g_row))[0:1]
            g_last = gcb[c - 1:c, :]
            decay = jnp.exp(jnp.where(incl, gcb[:, :c] - gcr, -jnp.inf))
            eg = jnp.exp(gcb)
            q, k, v = qn[r0:r0 + c], kn[r0:r0 + c], va[r0:r0 + c]
            kb = k * b_col
            kbf = k.astype(BF16)
            a_mat = jnp.where(strict, _dot_nt(kb.astype(BF16), kbf) * decay, 0.0)
            qk = jnp.where(incl, _dot_nt(q.astype(BF16), kbf) * decay, 0.0)
            y = -a_mat
            p = y
            for _ in range(5):
                yb = y.astype(BF16)
                y = _dot(yb, yb)
                p = p + y + _dot(p.astype(BF16), y.astype(BF16))
            rhs = jnp.concatenate([v * b_col, kb * eg], axis=1)
            w = rhs + _dot(p.astype(BF16), rhs.astype(BF16))
            hs = slice(h * dk, (h + 1) * dk)
            wv_ref[0, r0:r0 + c, hs] = w[:, :dk]
            wk_ref[0, r0:r0 + c, hs] = w[:, dk:].astype(BF16)
            qd_ref[0, r0:r0 + c, hs] = (q * eg).astype(BF16)
            kd_ref[0, r0:r0 + c, hs] = (k * jnp.exp(g_last - gcb)).astype(BF16)
            qk_ref[0, h, r0:r0 + c, :] = qk.astype(BF16)
            gl_ref[0, s, h:h + 1, :] = jnp.exp(g_last)


def _delta_prep(conv_in, prefix, conv_w, small, small_t, valid):
    n, lp, cd = conv_in.shape
    ta = LANE if lp % LANE == 0 else lp
    nt = lp // ta
    tb = ta // 8
    f = lambda dt, w: jax.ShapeDtypeStruct((n, lp, w), dt)
    blk = lambda w: pl.BlockSpec((1, ta, w), lambda b, i: (b, i, 0))
    width = H_B * DK_B
    return pl.pallas_call(
        functools.partial(_delta_prep_kernel, ta=ta, valid=valid),
        grid=(n, nt),
        in_specs=[
            blk(cd),
            pl.BlockSpec((1, 8, cd), lambda b, i: (b, jnp.maximum(i * tb - 1, 0), 0)),
            pl.BlockSpec((1, 8, cd), lambda b, i: (b, 0, 0)),
            pl.BlockSpec((8, cd), lambda b, i: (0, 0)),
            blk(LANE),
            pl.BlockSpec((1, 16, ta), lambda b, i: (b, 0, i)),
        ],
        out_specs=[
            blk(width), blk(width), blk(width), blk(width),
            pl.BlockSpec((1, H_B, ta, DELTA_CHUNK), lambda b, i: (b, 0, i, 0)),
            pl.BlockSpec((1, ta // DELTA_CHUNK, H_B, LANE), lambda b, i: (b, i, 0, 0)),
        ],
        out_shape=[
            f(F32, width), f(BF16, width), f(BF16, width), f(BF16, width),
            jax.ShapeDtypeStruct((n, H_B, lp, DELTA_CHUNK), BF16),
            jax.ShapeDtypeStruct((n, lp // DELTA_CHUNK, H_B, LANE), F32),
        ],
        compiler_params=_params("parallel", "parallel"),
        name="delta_prep",
    )(conv_in, conv_in, prefix, conv_w, small, small_t)


def _delta_scan_kernel(wv_ref, wk_ref, qd_ref, kd_ref, qk_ref, gl_ref, s0_ref, gz_ref, onw_ref,
                       o_ref, s_ref, *, nb, n_chunks):
    c = DELTA_CHUNK
    dk = DK_B

    @pl.when(pl.program_id(1) == 0)
    def _():
        s_ref[...] = s0_ref[...]

    onw = onw_ref[...]

    def body(ci, _):
        r0 = pl.multiple_of(ci * c, c)
        for b in range(nb):
            for h in range(H_B):
                hs = slice(h * dk, (h + 1) * dk)
                state = s_ref[b, h]
                sb = state.astype(BF16)
                lhs = jnp.concatenate([wk_ref[b, pl.ds(r0, c), hs], qd_ref[b, pl.ds(r0, c), hs]], axis=0)
                prod = _dot(lhs, sb)
                v_new = wv_ref[b, pl.ds(r0, c), hs] - prod[:c]
                vb = v_new.astype(BF16)
                o = prod[c:] + _dot(qk_ref[b, h, pl.ds(r0, c), :], vb)
                gl = gl_ref[b, pl.ds(ci, 1), h, :]
                s_ref[b, h] = state * gl + _dot_tn(kd_ref[b, pl.ds(r0, c), hs], vb)
                on = o * lax.rsqrt(jnp.mean(o * o, axis=-1, keepdims=True) + EPS) * onw
                o_ref[b, pl.ds(r0, c), hs] = (on * _silu(gz_ref[b, pl.ds(r0, c), hs])).astype(o_ref.dtype)
        return 0

    lax.fori_loop(0, n_chunks, body, 0)


def _delta_scan(wv, wk, qd, kd, qk, gl, s0, gz, onw, nb):
    n, lp, width = wv.shape
    tl = _row_tile(lp, 512)
    n_chunks = tl // DELTA_CHUNK
    seq = pl.BlockSpec((nb, tl, width), lambda i, t: (i, t, 0))
    state = pl.BlockSpec((nb,) + s0.shape[1:], lambda i, t: (i, 0, 0, 0))
    return pl.pallas_call(
        functools.partial(_delta_scan_kernel, nb=nb, n_chunks=n_chunks),
        grid=(n // nb, lp // tl),
        in_specs=[seq, seq, seq, seq,
                  pl.BlockSpec((nb, H_B, tl, DELTA_CHUNK), lambda i, t: (i, 0, t, 0)),
                  pl.BlockSpec((nb, n_chunks, H_B, LANE), lambda i, t: (i, t, 0, 0)),
                  state, seq, pl.BlockSpec(onw.shape, lambda i, t: (0, 0))],
        out_specs=[seq, state],
        out_shape=[jax.ShapeDtypeStruct((n, lp, width), BF16), jax.ShapeDtypeStruct(s0.shape, F32)],
        compiler_params=_params("parallel", "arbitrary"),
        name="delta_scan",
    )(wv, wk, qd, kd, qk, gl, s0, gz, onw)


def _fox_sample_kernel(pt_ref, q_ref, kn_ref, vn_ref, lfn_ref, kc_ref, vc_ref, lfc_ref, o_ref,
                       qbd_ref, m_ref, l_ref, acc_ref, carry_ref, *, n_q):
    del pt_ref
    j = pl.program_id(1)
    rows = n_q * H_A
    page = kc_ref.shape[1]
    ri = lax.broadcasted_iota(jnp.int32, (page, page), 0)
    ci = lax.broadcasted_iota(jnp.int32, (page, page), 1)

    def attend(kblk, vblk, bias8, mask):
        s = _dot_nt(qbd_ref[...], kblk.astype(BF16)) + jnp.concatenate([bias8] * n_q, axis=0)
        if mask is not None:
            s = jnp.where(mask, s, -jnp.inf)
        m_old = m_ref[...]
        m_new = jnp.maximum(m_old, jnp.max(s, axis=-1, keepdims=True))
        alpha = jnp.exp(m_old - m_new)
        p = jnp.exp(s - m_new)
        l_ref[...] = alpha * l_ref[...] + jnp.sum(p, axis=-1, keepdims=True)
        acc_ref[...] = alpha * acc_ref[...] + _dot(p.astype(BF16), vblk.astype(BF16))
        m_ref[...] = m_new

    @pl.when(j == 0)
    def _():
        q = q_ref[0] * (D_A ** -0.5)
        qrep = jnp.concatenate([jnp.broadcast_to(q[t:t + 1], (H_A, q.shape[1])) for t in range(n_q)], axis=0)
        r = lax.broadcasted_iota(jnp.int32, qrep.shape, 0)
        ln = lax.broadcasted_iota(jnp.int32, qrep.shape, 1)
        qbd_ref[...] = jnp.where(ln // D_A == r % H_A, qrep, 0.0).astype(BF16)
        m_ref[...] = jnp.full(m_ref.shape, -jnp.inf, F32)
        l_ref[...] = jnp.zeros(l_ref.shape, F32)
        acc_ref[...] = jnp.zeros(acc_ref.shape, F32)
        carry_ref[...] = jnp.zeros(carry_ref.shape, F32)
        upper = (ri <= ci).astype(BF16)
        csum = sum(_dot(p, upper) for p in _split3(lfn_ref[0]))
        r2 = lax.broadcasted_iota(jnp.int32, (rows, page), 0)
        c2 = lax.broadcasted_iota(jnp.int32, (rows, page), 1)
        attend(kn_ref[0], vn_ref[0], -csum, c2 <= r2 // H_A)

    @pl.when(j > 0)
    def _():
        lf = lfc_ref[0]
        after = (ri > ci).astype(BF16)
        suffix = sum(_dot(p, after) for p in _split3(lf))
        carry = carry_ref[...]
        attend(kc_ref[0], vc_ref[0], carry + suffix, None)
        carry_ref[...] = carry + jnp.sum(lf, axis=-1, keepdims=True)

    @pl.when(j == pl.num_programs(1) - 1)
    def _():
        out = acc_ref[...] / l_ref[...]
        r = lax.broadcasted_iota(jnp.int32, out.shape, 0)
        ln = lax.broadcasted_iota(jnp.int32, out.shape, 1)
        out = jnp.where(ln // D_A == r % H_A, out, 0.0)
        o_ref[0] = jnp.sum(out.reshape(n_q, H_A, out.shape[1]), axis=1)


def _fox_sample(page_table, q, kn, vn, lfn, cache_k, cache_v, cache_lf):
    b, n_q, width = q.shape
    n_pages = page_table.shape[1]
    page = cache_k.shape[1]
    rows = n_q * H_A
    cur = lambda i, j, pt: (i, 0, 0)
    past = lambda i, j, pt: (pt[i, n_pages - jnp.maximum(j, 1)], 0, 0)
    grid_spec = pltpu.PrefetchScalarGridSpec(
        num_scalar_prefetch=1,
        grid=(b, n_pages + 1),
        in_specs=[
            pl.BlockSpec((1, n_q, width), cur),
            pl.BlockSpec((1, page, width), cur),
            pl.BlockSpec((1, page, width), cur),
            pl.BlockSpec((1, H_A, page), cur),
            pl.BlockSpec((1, page, width), past),
            pl.BlockSpec((1, page, width), past),
            pl.BlockSpec((1, H_A, page), past),
        ],
        out_specs=pl.BlockSpec((1, n_q, width), cur),
        scratch_shapes=[
            pltpu.VMEM((rows, width), BF16),
            pltpu.VMEM((rows, 1), F32),
            pltpu.VMEM((rows, 1), F32),
            pltpu.VMEM((rows, width), F32),
            pltpu.VMEM((H_A, 1), F32),
        ],
    )
    return pl.pallas_call(
        functools.partial(_fox_sample_kernel, n_q=n_q),
        grid_spec=grid_spec,
        out_shape=jax.ShapeDtypeStruct((b, n_q, width), F32),
        compiler_params=_params("parallel", "arbitrary"),
        name="fox_sample",
    )(page_table, q, kn, vn, lfn, cache_k, cache_v, cache_lf)


def _router_kernel(x_ref, nw_ref, wr_ref, g_ref):
    h = _rms(x_ref[...], nw_ref[...])
    logits = jnp.dot(h, wr_ref[...], preferred_element_type=F32, precision=HIGHEST)
    lane = lax.broadcasted_iota(jnp.int32, logits.shape, 1)
    lanef = lane.astype(F32)
    neg = -jnp.inf
    is_group = (lane >= N_EXPERTS) & (lane < N_EXPERTS + N_GROUPS)
    gl = jnp.where(is_group, logits, neg)
    gmax = jnp.max(gl, axis=-1, keepdims=True)
    g_sel = jnp.min(jnp.where(gl == gmax, lanef - N_EXPERTS, 1e9), axis=-1, keepdims=True)
    p_sel = 1.0 / jnp.sum(jnp.where(is_group, jnp.exp(logits - gmax), 0.0), axis=-1, keepdims=True)
    in_sel = (lane < N_EXPERTS) & ((lane // EXPERTS_PER_GROUP).astype(F32) == g_sel)
    el = jnp.where(in_sel, logits, neg)
    m1 = jnp.max(el, axis=-1, keepdims=True)
    i1 = jnp.min(jnp.where(el == m1, lanef, 1e9), axis=-1, keepdims=True)
    el2 = jnp.where(lanef == i1, neg, el)
    m2 = jnp.max(el2, axis=-1, keepdims=True)
    i2 = jnp.min(jnp.where(el2 == m2, lanef, 1e9), axis=-1, keepdims=True)
    e2 = jnp.exp(m2 - m1)
    w1 = p_sel / (1.0 + e2)
    g_ref[...] = jnp.where(lanef == i1, w1, jnp.where(lanef == i2, w1 * e2, 0.0))


def _router(x, nw, wr):
    t, d = x.shape
    tm = _row_tile(t, 512)
    return pl.pallas_call(
        _router_kernel,
        grid=(t // tm,),
        in_specs=[pl.BlockSpec((tm, d), lambda i: (i, 0)), pl.BlockSpec(nw.shape, lambda i: (0, 0)),
                  pl.BlockSpec(wr.shape, lambda i: (0, 0))],
        out_specs=pl.BlockSpec((tm, LANE), lambda i: (i, 0)),
        out_shape=jax.ShapeDtypeStruct((t, LANE), F32),
        compiler_params=_params("parallel"),
        name="moe_router",
    )(x, nw, wr)


def _moe_kernel(x_ref, nw_ref, g_ref, wg_ref, wu_ref, wd_ref, nf_ref, o_ref, hb_ref, acc_ref, *, final_norm):
    e = pl.program_id(1)

    @pl.when(e == 0)
    def _():
        hb_ref[...] = _rms(x_ref[...], nw_ref[...]).astype(BF16)
        acc_ref[...] = jnp.zeros(acc_ref.shape, F32)

    gates = g_ref[...]
    lane = lax.broadcasted_iota(jnp.int32, gates.shape, 1)
    gcol = jnp.sum(jnp.where(lane == e, gates, 0.0), axis=-1, keepdims=True)
    hb = hb_ref[...]
    act = _silu(_dot(hb, wg_ref[0])) * _dot(hb, wu_ref[0]) * gcol
    acc_ref[...] += _dot(act.astype(BF16), wd_ref[0])

    @pl.when(e == pl.num_programs(1) - 1)
    def _():
        y = x_ref[...] + acc_ref[...]
        o_ref[...] = _rms(y, nf_ref[...]) if final_norm else y


def _moe_dense(x, nw, gates, wg, wu, wd, nf, final_norm):
    t, d = x.shape
    tm = _row_tile(t, 1024)
    n_e, _, f = wg.shape
    return pl.pallas_call(
        functools.partial(_moe_kernel, final_norm=final_norm),
        grid=(t // tm, n_e),
        in_specs=[
            pl.BlockSpec((tm, d), lambda i, e: (i, 0)),
            pl.BlockSpec(nw.shape, lambda i, e: (0, 0)),
            pl.BlockSpec((tm, LANE), lambda i, e: (i, 0)),
            pl.BlockSpec((1, d, f), lambda i, e: (e, 0, 0)),
            pl.BlockSpec((1, d, f), lambda i, e: (e, 0, 0)),
            pl.BlockSpec((1, f, d), lambda i, e: (e, 0, 0)),
            pl.BlockSpec(nf.shape, lambda i, e: (0, 0)),
        ],
        out_specs=pl.BlockSpec((tm, d), lambda i, e: (i, 0)),
        out_shape=jax.ShapeDtypeStruct((t, d), F32),
        scratch_shapes=[pltpu.VMEM((tm, d), BF16), pltpu.VMEM((tm, d), F32)],
        compiler_params=_params("parallel", "arbitrary"),
        name="moe_experts",
    )(x, nw, gates, wg, wu, wd, nf)


def _odd_kernel(x_ref, nw_ref, win_ref, lnw_ref, lnb_ref, wm_ref, bs_ref, wout_ref, *refs, emit_v):
    if emit_v:
        o_ref, v_ref, gated_ref = refs
    else:
        o_ref, gated_ref = refs
    tm = x_ref.shape[0]
    dc = lnw_ref.shape[1]
    gd = dc // H_C
    x = x_ref[...]
    hb = _rms(x, nw_ref[...]).astype(BF16)
    z = _dot(hb, win_ref[...])
    z = 0.5 * z * (1.0 + lax.erf(z * (2.0 ** -0.5)))
    v = z[:, dc:]
    mu = jnp.mean(v, axis=-1, keepdims=True)
    vc = v - mu
    vn = vc * lax.rsqrt(jnp.mean(vc * vc, axis=-1, keepdims=True) + EPS) * lnw_ref[...] + lnb_ref[...]
    if emit_v:
        v_ref[...] = vn
    vb = vn.astype(BF16)
    for ci in range(tm // CHUNK_C):
        rs = slice(ci * CHUNK_C, (ci + 1) * CHUNK_C)
        for g in range(H_C):
            cs = slice(g * gd, (g + 1) * gd)
            mixed = _dot(wm_ref[g], vb[rs, cs]) + jnp.concatenate([bs_ref[g]] * (gd // LANE), axis=1)
            gated_ref[rs, cs] = (z[rs, cs] * mixed).astype(BF16)
    o_ref[...] = x + _dot(gated_ref[...], wout_ref[...])


def _odd_mixer(x, nw, w_in, ln_w, ln_b, wm, bs, w_out, emit_v):
    t, d = x.shape
    dc = ln_w.shape[1]
    tm = _row_tile(t, 512)
    row = lambda w: pl.BlockSpec((tm, w), lambda i: (i, 0))
    full = lambda a: pl.BlockSpec(a.shape, lambda i: (0,) * a.ndim)
    out_specs = [row(d)] + ([row(dc)] if emit_v else [])
    out_shape = [jax.ShapeDtypeStruct((t, d), F32)] + ([jax.ShapeDtypeStruct((t, dc), F32)] if emit_v else [])
    return pl.pallas_call(
        functools.partial(_odd_kernel, emit_v=emit_v),
        grid=(t // tm,),
        in_specs=[row(d), full(nw), full(w_in), full(ln_w), full(ln_b), full(wm), full(bs), full(w_out)],
        out_specs=out_specs,
        out_shape=out_shape,
        scratch_shapes=[pltpu.VMEM((tm, dc), BF16)],
        compiler_params=_params("parallel"),
        name="odd_mixer",
    )(x, nw, w_in, ln_w, ln_b, wm, bs, w_out)


def _prep_even_weights(w_in, f_bias, a_log, dt_bias):
    d = w_in.shape[0]
    small = jnp.concatenate(
        [w_in[:, 1536:1544], w_in[:, 3080:3088], jnp.zeros((d, LANE - 16), w_in.dtype)], axis=1)
    w = jnp.concatenate([w_in[:, 0:1536], w_in[:, 1544:3080], w_in[:, 3088:3600], small], axis=1)
    bias = jnp.zeros((1, LANE), F32).at[0, 0:8].set(f_bias).at[0, 8:12].set(dt_bias)
    alog = jnp.zeros((1, LANE), F32).at[0, 8:12].set(a_log)
    return w.astype(BF16), bias, alog


def kernel(x_prompt, x_sample, cache_k, cache_v, cache_logf, state_delta, state_conv, page_table, norm_mix, norm_ffn, norm_final, w_in_even, w_out_even, fox_forget_bias, dn_conv_w, dn_a_log, dn_dt_bias, dn_norm_w, w_in_odd, gm_ln_w, gm_ln_b, gm_spatial_w, gm_spatial_b, w_out_odd, moe_router_group, moe_router_expert, moe_w_gate, moe_w_up, moe_w_down):
    n_p, seq, d = x_prompt.shape
    n_s, dec = x_sample.shape[:2]
    page = cache_k.shape[2]
    pool = cache_k.shape[1]
    mix_a = H_A * D_A
    xp = x_prompt.reshape(n_p * seq, d)
    xs = x_sample.reshape(n_s * dec, d)
    row = lambda a: a[None, :]

    w_e, bias_e, alog_e = _prep_even_weights(w_in_even[0], fox_forget_bias[0], dn_a_log[0], dn_dt_bias[0])
    w_out_e = w_out_even[0].astype(BF16)
    conv_w = jnp.zeros((8, dn_conv_w.shape[2]), F32).at[:CONV_W].set(dn_conv_w[0])
    onw = row(dn_norm_w[0])
    nmix0 = row(norm_mix[0])

    def delta(conv3, prefix, small3, gz3, s0, valid, nb):
        small_t = jnp.swapaxes(small3[:, :, :16], 1, 2)
        wv, wk, qd, kd, qk, gl = _delta_prep(conv3, prefix, conv_w, small3, small_t, valid)
        return _delta_scan(wv, wk, qd, kd, qk, gl, s0, gz3, onw, nb)

    q, k, v, c, gz, s = _even_in(xp, nmix0, w_e, bias_e, alog_e)
    logf_p = s[:, :H_A].reshape(n_p, seq, H_A)
    csum = _cumsum_lanes(jnp.swapaxes(logf_p, 1, 2).reshape(n_p * H_A, seq)).reshape(n_p, H_A, seq)
    as3 = lambda a: a.reshape(n_p, seq, a.shape[-1])
    oa_p = _fox_prompt(as3(q), as3(k), as3(v), csum)
    conv_p = as3(c)
    ob_p, sd_p = delta(conv_p, jnp.zeros((n_p, 8, conv_p.shape[-1]), F32), as3(s), as3(gz),
                       jnp.zeros((n_p, H_B, DK_B, DK_B), F32), seq, 4)
    xp = _even_out(xp, oa_p.reshape(n_p * seq, mix_a), ob_p.reshape(n_p * seq, -1), w_out_e)
    k_p = k.reshape(1, n_p, seq, H_A, D_A)
    v_p = v.reshape(1, n_p, seq, H_A, D_A)
    conv_state_p = conv_p[:, seq - (CONV_W - 1):][None]

    q, k, v, c, gz, s = _even_in(xs, nmix0, w_e, bias_e, alog_e)
    as3 = lambda a: a.reshape(n_s, dec, a.shape[-1])
    padrows = lambda a, n: jnp.pad(a, ((0, 0), (0, n - a.shape[1]), (0, 0)))
    logf_s = s[:, :H_A].reshape(n_s, dec, H_A)
    lfn = jnp.pad(jnp.swapaxes(logf_s, 1, 2), ((0, 0), (0, 0), (0, page - dec)))
    oa_s = _fox_sample(page_table, as3(q), padrows(as3(k), page), padrows(as3(v), page), lfn,
                       cache_k[0].reshape(pool, page, mix_a), cache_v[0].reshape(pool, page, mix_a),
                       jnp.swapaxes(cache_logf[0], 1, 2))
    conv_s = as3(c)
    lp = DELTA_CHUNK
    prefix_s = jnp.pad(state_conv[0], ((0, 0), (8 - (CONV_W - 1), 0), (0, 0)))
    ob_s, sd_s = delta(padrows(conv_s, lp), prefix_s, padrows(as3(s), lp), padrows(as3(gz), lp),
                       state_delta[0], dec, 8)
    xs = _even_out(xs, oa_s.reshape(n_s * dec, mix_a).astype(BF16), ob_s[:, :dec].reshape(n_s * dec, -1), w_out_e)
    k_s = k.reshape(1, n_s, dec, H_A, D_A)
    v_s = v.reshape(1, n_s, dec, H_A, D_A)
    conv_state_s = jnp.concatenate([state_conv[0], conv_s], axis=1)[:, dec:][None]

    def moe(x, layer, final):
        wr = jnp.concatenate([moe_router_expert[layer], moe_router_group[layer],
                              jnp.zeros((d, LANE - N_EXPERTS - N_GROUPS), F32)], axis=1)
        gates = _router(x, row(norm_ffn[layer]), wr)
        return _moe_dense(x, row(norm_ffn[layer]), gates, wg[layer], wu[layer], wd[layer], row(norm_final), final)

    wg, wu, wd = moe_w_gate.astype(BF16), moe_w_up.astype(BF16), moe_w_down.astype(BF16)
    xp = moe(xp, 0, False)
    xs = moe(xs, 0, False)

    idx = jnp.arange(CHUNK_C)
    w_m = jnp.where(idx[:, None] >= idx[None, :], gm_spatial_w[0], 0.0)
    b_s = gm_spatial_b[0]
    lanes = lambda b: jnp.broadcast_to(b[:, :, None], b.shape + (LANE,))
    reps = CHUNK_C // dec
    assert n_s * dec == CHUNK_C, "sample group must fill exactly one 128-row mixing tile"
    w_m_s = (jnp.eye(reps, dtype=F32)[None, :, None, :, None] * w_m[:, None, :dec, None, :dec]).reshape(H_C, CHUNK_C, CHUNK_C)
    b_s_s = jnp.tile(b_s[:, :dec], (1, reps))
    odd_w = (row(norm_mix[1]), w_in_odd[0].astype(BF16), row(gm_ln_w[0]), row(gm_ln_b[0]))
    w_out_o = w_out_odd[0].astype(BF16)
    (xp,) = _odd_mixer(xp, *odd_w, w_m.astype(BF16), lanes(b_s), w_out_o, False)
    xs, v_rows = _odd_mixer(xs, *odd_w, w_m_s.astype(BF16), lanes(b_s_s), w_out_o, True)

    y_p = moe(xp, 1, True).reshape(n_p, seq, d)
    y_s = moe(xs, 1, True).reshape(n_s, dec, d)
    return (y_p, y_s, k_p, v_p, logf_p[None], sd_p[None], conv_state_p,
            k_s, v_s, logf_s[None], sd_s[None], conv_state_s, v_rows.reshape(1, n_s, dec, -1))
```

```python
import functools

import jax
import jax.numpy as jnp
from jax import lax
from jax.experimental import pallas as pl
from jax.experimental.pallas import tpu as pltpu

F32 = jnp.float32
BF16 = jnp.bfloat16
EPS = 1e-6
LANE = 128
VMEM_LIMIT = 56 * 1024 * 1024
HIGHEST = lax.Precision.HIGHEST

H_A, D_A = 8, 64
H_B, DK_B = 4, 128
CONV_W = 4
DELTA_CHUNK = 64
N_GROUPS, EXPERTS_PER_GROUP = 4, 8
N_EXPERTS = N_GROUPS * EXPERTS_PER_GROUP
CHUNK_C = 128
H_C = 8


def _params(*sem):
    return pltpu.CompilerParams(dimension_semantics=sem, vmem_limit_bytes=VMEM_LIMIT)


def _row_tile(t, pref):
    return pref if t % pref == 0 else t


def _rms(x, w):
    return x * lax.rsqrt(jnp.mean(x * x, axis=-1, keepdims=True) + EPS) * w


def _softplus_tail(z):
    return jnp.log1p(jnp.exp(-jnp.abs(z)))


def _even_in_kernel(x_ref, nw_ref, w_ref, bias_ref, alog_ref,
                    q_ref, k_ref, v_ref, c_ref, gz_ref, s_ref):
    hb = _rms(x_ref[...], nw_ref[...]).astype(BF16)

    def proj(lo, hi):
        return jnp.dot(hb, w_ref[:, lo:hi], preferred_element_type=F32)

    q_ref[...] = proj(0, 512)
    k_ref[...] = proj(512, 1024)
    v_ref[...] = proj(1024, 1536)
    c_ref[...] = proj(1536, 3072)
    gz_ref[...] = proj(3072, 3584)
    z = proj(3584, 3712) + bias_ref[...]
    tail = _softplus_tail(z)
    logf = jnp.minimum(z, 0.0) - tail
    g = -jnp.exp(alog_ref[...]) * (jnp.maximum(z, 0.0) + tail)
    beta = 1.0 / (1.0 + jnp.exp(-z))
    lane = lax.broadcasted_iota(jnp.int32, z.shape, 1)
    s_ref[...] = jnp.where(lane < 8, logf, jnp.where(lane < 12, g, jnp.where(lane < 16, beta, 0.0)))


def _even_in(x, nw, w, bias, alog):
    t, d = x.shape
    tm = _row_tile(t, 512)
    widths = (512, 512, 512, 1536, 512, LANE)
    row = lambda wd: pl.BlockSpec((tm, wd), lambda i: (i, 0))
    full = lambda a: pl.BlockSpec(a.shape, lambda i: (0, 0))
    return pl.pallas_call(
        _even_in_kernel,
        grid=(t // tm,),
        in_specs=[row(d), full(nw), full(w), full(bias), full(alog)],
        out_specs=[row(wd) for wd in widths],
        out_shape=[jax.ShapeDtypeStruct((t, wd), F32) for wd in widths],
        compiler_params=_params("parallel"),
        name="even_in_proj",
    )(x, nw, w, bias, alog)


def _cumsum_kernel(x_ref, o_ref):
    rows, length = x_ref.shape
    r = lax.broadcasted_iota(jnp.int32, (LANE, LANE), 0)
    c = lax.broadcasted_iota(jnp.int32, (LANE, LANE), 1)
    upper = (r <= c).astype(F32)
    carry = jnp.zeros((rows, 1), F32)
    for b in range(length // LANE):
        blk = jnp.dot(x_ref[:, b * LANE:(b + 1) * LANE], upper,
                      preferred_element_type=F32, precision=HIGHEST) + carry
        o_ref[:, b * LANE:(b + 1) * LANE] = blk
        carry = blk[:, LANE - 1:LANE]


def _cumsum_lanes(x):
    rows, length = x.shape
    tr = 8
    return pl.pallas_call(
        _cumsum_kernel,
        grid=(rows // tr,),
        in_specs=[pl.BlockSpec((tr, length), lambda i: (i, 0))],
        out_specs=pl.BlockSpec((tr, length), lambda i: (i, 0)),
        out_shape=jax.ShapeDtypeStruct((rows, length), F32),
        compiler_params=_params("parallel"),
        name="logf_cumsum",
    )(x)


def _fox_prompt_kernel(q_ref, k_ref, v_ref, c_ref, o_ref, *, tq):
    qi = pl.program_id(2)
    hp = pl.program_id(1)
    scale = D_A ** -0.5
    lane = lax.broadcasted_iota(jnp.int32, (tq, LANE), 1)
    first = lane < D_A
    q = q_ref[0] * scale
    qs = jnp.concatenate([jnp.where(first, q, 0.0), jnp.where(first, 0.0, q)], axis=0).astype(BF16)
    q0 = pl.multiple_of(qi * tq, tq)

    def crow(h, start):
        return c_ref[0, pl.ds(2 * hp + h, 1), pl.ds(start, tq)]

    cq = [crow(h, q0)[:, 0:1] for h in range(2)]

    def scores(j0):
        kb = k_ref[0, pl.ds(j0, tq), :].astype(BF16)
        s = lax.dot_general(qs, kb, (((1,), (1,)), ((), ())), preferred_element_type=F32)
        bias = jnp.concatenate(
            [jnp.broadcast_to(cq[h] - crow(h, j0), (tq, tq)) for h in range(2)], axis=0)
        return s + bias

    def update(carry, s, j0):
        m, l, acc = carry
        m_new = jnp.maximum(m, jnp.max(s, axis=-1, keepdims=True))
        alpha = jnp.exp(m - m_new)
        p = jnp.exp(s - m_new)
        l = alpha * l + jnp.sum(p, axis=-1, keepdims=True)
        vb = v_ref[0, pl.ds(j0, tq), :].astype(BF16)
        acc = alpha * acc + jnp.dot(p.astype(BF16), vb, preferred_element_type=F32)
        return m_new, l, acc

    def body(j, carry):
        j0 = pl.multiple_of(j * tq, tq)
        return update(carry, scores(j0), j0)

    init = (jnp.full((2 * tq, 1), -jnp.inf, F32), jnp.zeros((2 * tq, 1), F32),
            jnp.zeros((2 * tq, LANE), F32))
    carry = lax.fori_loop(0, qi, body, init)
    r = lax.broadcasted_iota(jnp.int32, (2 * tq, tq), 0)
    cidx = lax.broadcasted_iota(jnp.int32, (2 * tq, tq), 1)
    causal = cidx <= jnp.where(r >= tq, r - tq, r)
    _, l, acc = update(carry, jnp.where(causal, scores(q0), -jnp.inf), q0)
    out = acc / l
    o_ref[0] = jnp.where(first, out[:tq], out[tq:]).astype(o_ref.dtype)


def _fox_prompt(q, k, v, c):
    n, length, width = q.shape
    tq = 256 if length % 256 == 0 else length
    grid = (n, width // LANE, length // tq)
    return pl.pallas_call(
        functools.partial(_fox_prompt_kernel, tq=tq),
        grid=grid,
        in_specs=[
            pl.BlockSpec((1, tq, LANE), lambda b, h, i: (b, i, h)),
            pl.BlockSpec((1, length, LANE), lambda b, h, i: (b, 0, h)),
            pl.BlockSpec((1, length, LANE), lambda b, h, i: (b, 0, h)),
            pl.BlockSpec((1, H_A, length), lambda b, h, i: (b, 0, 0)),
        ],
        out_specs=pl.BlockSpec((1, tq, LANE), lambda b, h, i: (b, i, h)),
        out_shape=jax.ShapeDtypeStruct((n, length, width), BF16),
        compiler_params=_params("parallel", "parallel", "arbitrary"),
        name="fox_prompt",
    )(q, k, v, c)


def _even_out_kernel(x_ref, a_ref, b_ref, w_ref, o_ref):
    half = a_ref.shape[1]
    acc = jnp.dot(a_ref[...], w_ref[:half, :], preferred_element_type=F32)
    acc += jnp.dot(b_ref[...], w_ref[half:, :], preferred_element_type=F32)
    o_ref[...] = x_ref[...] + acc


def _even_out(x, oa, ob, w):
    t, d = x.shape
    tm = _row_tile(t, 512)
    row = lambda a: pl.BlockSpec((tm, a.shape[1]), lambda i: (i, 0))
    return pl.pallas_call(
        _even_out_kernel,
        grid=(t // tm,),
        in_specs=[row(x), row(oa), row(ob), pl.BlockSpec(w.shape, lambda i: (0, 0))],
        out_specs=row(x),
        out_shape=jax.ShapeDtypeStruct((t, d), F32),
        compiler_params=_params("parallel"),
        name="even_out_proj",
    )(x, oa, ob, w)


def _split3(x):
    hi = x.astype(BF16)
    r1 = x - hi.astype(F32)
    mid = r1.astype(BF16)
    lo = (r1 - mid.astype(F32)).astype(BF16)
    return hi, mid, lo


def _dot(a, b):
    return jnp.dot(a, b, preferred_element_type=F32)


def _dot_nt(a, b):
    return lax.dot_general(a, b, (((1,), (1,)), ((), ())), preferred_element_type=F32)


def _dot_tn(a, b):
    return lax.dot_general(a, b, (((0,), (0,)), ((), ())), preferred_element_type=F32)


def _silu(x):
    return x / (1.0 + jnp.exp(-x))


def _delta_prep_kernel(x_ref, xprev_ref, pre_ref, cw_ref, sm_ref, smt_ref,
                       wv_ref, wk_ref, qd_ref, kd_ref, qk_ref, gl_ref, *, ta, valid):
    i = pl.program_id(1)
    c = DELTA_CHUNK
    dk = DK_B
    x = x_ref[0]
    prev = jnp.where(i == 0, pre_ref[0], xprev_ref[0])
    xcat = jnp.concatenate([prev, x], axis=0)
    conv = x * cw_ref[CONV_W - 1:CONV_W, :]
    for s in range(1, CONV_W):
        conv += pltpu.roll(xcat, s, axis=0)[8:] * cw_ref[CONV_W - 1 - s:CONV_W - s, :]
    act = _silu(conv)

    ri = lax.broadcasted_iota(jnp.int32, (c, c), 0)
    ci = lax.broadcasted_iota(jnp.int32, (c, c), 1)
    incl = ri >= ci
    strict = ri > ci
    tri = incl.astype(BF16)
    upper = (ri <= ci).astype(BF16)
    lane = lax.broadcasted_iota(jnp.int32, (c, LANE), 1)
    sm = sm_ref[0]
    smt = smt_ref[0]

    for h in range(H_B):
        qa = act[:, h * dk:(h + 1) * dk]
        ka = act[:, (H_B + h) * dk:(H_B + h + 1) * dk]
        va = act[:, (2 * H_B + h) * dk:(2 * H_B + h + 1) * dk]
        qn = qa * lax.rsqrt(jnp.sum(qa * qa, axis=-1, keepdims=True) + EPS) * (dk ** -0.5)
        kn = ka * lax.rsqrt(jnp.sum(ka * ka, axis=-1, keepdims=True) + EPS)
        for s in range(ta // c):
            r0 = s * c
            rows = lax.broadcasted_iota(jnp.int32, (c, 1), 0) + (i * ta + r0)
            cols = lax.broadcasted_iota(jnp.int32, (1, c), 1) + (i * ta + r0)
            smc = sm[r0:r0 + c]
            g_col = jnp.where(rows < valid, jnp.sum(jnp.where(lane == 8 + h, smc, 0.0), axis=-1, keepdims=True), 0.0)
            b_col = jnp.where(rows < valid, jnp.sum(jnp.where(lane == 12 + h, smc, 0.0), axis=-1, keepdims=True), 0.0)
            g_row = jnp.where(cols < valid, smt[8 + h:9 + h, r0:r0 + c], 0.0)
            gcb = sum(_dot(tri, jnp.broadcast_to(p, (c, LANE))) for p in _split3(g_col))
            gcr = sum(_dot(jnp.broadcast_to(p, (8, c)), upper) for p in _split3(g_row))[0:1]
            g_last = gcb[c - 1:c, :]
            decay = jnp.exp(jnp.where(incl, gcb[:, :c] - gcr, -jnp.inf))
            eg = jnp.exp(gcb)
            q, k, v = qn[r0:r0 + c], kn[r0:r0 + c], va[r0:r0 + c]
            kb = k * b_col
            kbf = k.astype(BF16)
            a_mat = jnp.where(strict, _dot_nt(kb.astype(BF16), kbf) * decay, 0.0)
            qk = jnp.where(incl, _dot_nt(q.astype(BF16), kbf) * decay, 0.0)
            y = -a_mat
            p = y
            for _ in range(5):
                yb = y.astype(BF16)
                y = _dot(yb, yb)
                p = p + y + _dot(p.astype(BF16), y.astype(BF16))
            rhs = jnp.concatenate([v * b_col, kb * eg], axis=1)
            w = rhs + _dot(p.astype(BF16), rhs.astype(BF16))
            hs = slice(h * dk, (h + 1) * dk)
            wv_ref[0, r0:r0 + c, hs] = w[:, :dk]
            wk_ref[0, r0:r0 + c, hs] = w[:, dk:].astype(BF16)
            qd_ref[0, r0:r0 + c, hs] = (q * eg).astype(BF16)
            kd_ref[0, r0:r0 + c, hs] = (k * jnp.exp(g_last - gcb)).astype(BF16)
            qk_ref[0, h, r0:r0 + c, :] = qk.astype(BF16)
            gl_ref[0, s, h:h + 1, :] = jnp.exp(g_last)


def _delta_prep(conv_in, prefix, conv_w, small, small_t, valid):
    n, lp, cd = conv_in.shape
    ta = LANE if lp % LANE == 0 else lp
    nt = lp // ta
    tb = ta // 8
    f = lambda dt, w: jax.ShapeDtypeStruct((n, lp, w), dt)
    blk = lambda w: pl.BlockSpec((1, ta, w), lambda b, i: (b, i, 0))
    width = H_B * DK_B
    return pl.pallas_call(
        functools.partial(_delta_prep_kernel, ta=ta, valid=valid),
        grid=(n, nt),
        in_specs=[
            blk(cd),
            pl.BlockSpec((1, 8, cd), lambda b, i: (b, jnp.maximum(i * tb - 1, 0), 0)),
            pl.BlockSpec((1, 8, cd), lambda b, i: (b, 0, 0)),
            pl.BlockSpec((8, cd), lambda b, i: (0, 0)),
            blk(LANE),
            pl.BlockSpec((1, 16, ta), lambda b, i: (b, 0, i)),
        ],
        out_specs=[
            blk(width), blk(width), blk(width), blk(width),
            pl.BlockSpec((1, H_B, ta, DELTA_CHUNK), lambda b, i: (b, 0, i, 0)),
            pl.BlockSpec((1, ta // DELTA_CHUNK, H_B, LANE), lambda b, i: (b, i, 0, 0)),
        ],
        out_shape=[
            f(F32, width), f(BF16, width), f(BF16, width), f(BF16, width),
            jax.ShapeDtypeStruct((n, H_B, lp, DELTA_CHUNK), BF16),
            jax.ShapeDtypeStruct((n, lp // DELTA_CHUNK, H_B, LANE), F32),
        ],
        compiler_params=_params("parallel", "parallel"),
        name="delta_prep",
    )(conv_in, conv_in, prefix, conv_w, small, small_t)


def _delta_scan_kernel(wv_ref, wk_ref, qd_ref, kd_ref, qk_ref, gl_ref, s0_ref, gz_ref, onw_ref,
                       o_ref, s_ref, *, nb, n_chunks):
    c = DELTA_CHUNK
    dk = DK_B

    @pl.when(pl.program_id(1) == 0)
    def _():
        s_ref[...] = s0_ref[...]

    onw = onw_ref[...]

    def body(ci, _):
        r0 = pl.multiple_of(ci * c, c)
        for b in range(nb):
            for h in range(H_B):
                hs = slice(h * dk, (h + 1) * dk)
                state = s_ref[b, h]
                sb = state.astype(BF16)
                lhs = jnp.concatenate([wk_ref[b, pl.ds(r0, c), hs], qd_ref[b, pl.ds(r0, c), hs]], axis=0)
                prod = _dot(lhs, sb)
                v_new = wv_ref[b, pl.ds(r0, c), hs] - prod[:c]
                vb = v_new.astype(BF16)
                o = prod[c:] + _dot(qk_ref[b, h, pl.ds(r0, c), :], vb)
                gl = gl_ref[b, pl.ds(ci, 1), h, :]
                s_ref[b, h] = state * gl + _dot_tn(kd_ref[b, pl.ds(r0, c), hs], vb)
                on = o * lax.rsqrt(jnp.mean(o * o, axis=-1, keepdims=True) + EPS) * onw
                o_ref[b, pl.ds(r0, c), hs] = (on * _silu(gz_ref[b, pl.ds(r0, c), hs])).astype(o_ref.dtype)
        return 0

    lax.fori_loop(0, n_chunks, body, 0)


def _delta_scan(wv, wk, qd, kd, qk, gl, s0, gz, onw, nb):
    n, lp, width = wv.shape
    tl = _row_tile(lp, 512)
    n_chunks = tl // DELTA_CHUNK
    seq = pl.BlockSpec((nb, tl, width), lambda i, t: (i, t, 0))
    state = pl.BlockSpec((nb,) + s0.shape[1:], lambda i, t: (i, 0, 0, 0))
    return pl.pallas_call(
        functools.partial(_delta_scan_kernel, nb=nb, n_chunks=n_chunks),
        grid=(n // nb, lp // tl),
        in_specs=[seq, seq, seq, seq,
                  pl.BlockSpec((nb, H_B, tl, DELTA_CHUNK), lambda i, t: (i, 0, t, 0)),
                  pl.BlockSpec((nb, n_chunks, H_B, LANE), lambda i, t: (i, t, 0, 0)),
                  state, seq, pl.BlockSpec(onw.shape, lambda i, t: (0, 0))],
        out_specs=[seq, state],
        out_shape=[jax.ShapeDtypeStruct((n, lp, width), BF16), jax.ShapeDtypeStruct(s0.shape, F32)],
        compiler_params=_params("parallel", "arbitrary"),
        name="delta_scan",
    )(wv, wk, qd, kd, qk, gl, s0, gz, onw)


def _fox_sample_kernel(pt_ref, q_ref, kn_ref, vn_ref, lfn_ref, *refs, pages_per_step):
    del pt_ref
    pps = pages_per_step
    kc_refs, vc_refs, lfc_refs = refs[:pps], refs[pps:2 * pps], refs[2 * pps:3 * pps]
    o_ref, qb_ref, m_ref, l_ref, acc_ref, carry_ref = refs[3 * pps:]
    j = pl.program_id(1)
    rows = q_ref.shape[1]
    page = kc_refs[0].shape[1]
    per_row = LANE // H_A

    def attend(blocks):
        qb = qb_ref[...]
        ss = [jnp.where(mask, _dot_nt(qb, kf.astype(BF16)) + bias, -jnp.inf) for kf, _, bias, mask in blocks]
        m_old = m_ref[...]
        m_new = m_old
        for s in ss:
            m_new = jnp.maximum(m_new, jnp.max(s, axis=-1, keepdims=True))
        alpha = jnp.exp(m_old - m_new)
        l_new = alpha * l_ref[...]
        acc = alpha * acc_ref[...]
        for s, (_, vf, _, _) in zip(ss, blocks):
            p = jnp.exp(s - m_new)
            l_new += jnp.sum(p, axis=-1, keepdims=True)
            acc += _dot(p.astype(BF16), vf.astype(BF16))
        l_ref[...] = l_new
        acc_ref[...] = acc
        m_ref[...] = m_new

    def head_match(n_cols):
        r = lax.broadcasted_iota(jnp.int32, (rows, n_cols), 0)
        c = lax.broadcasted_iota(jnp.int32, (rows, n_cols), 1)
        return r, c, (r % H_A) == (c % H_A)

    @pl.when(j == 0)
    def _():
        qb_ref[...] = (q_ref[0] * (D_A ** -0.5)).astype(BF16)
        m_ref[...] = jnp.full(m_ref.shape, -jnp.inf, F32)
        l_ref[...] = jnp.zeros(l_ref.shape, F32)
        acc_ref[...] = jnp.zeros(acc_ref.shape, F32)
        carry_ref[...] = jnp.zeros(carry_ref.shape, F32)
        ri = lax.broadcasted_iota(jnp.int32, (LANE, LANE), 0)
        ci = lax.broadcasted_iota(jnp.int32, (LANE, LANE), 1)
        prefix = ((ri % H_A == ci % H_A) & (ri <= ci)).astype(BF16)
        csum = sum(_dot(jnp.broadcast_to(p, (8, LANE)), prefix) for p in _split3(lfn_ref[0]))[0:1]
        r, c, same = head_match(LANE)
        attend([(kn_ref[0], vn_ref[0], -csum, same & (c // H_A <= r // H_A))])

    @pl.when(j > 0)
    def _():
        hi = lax.broadcasted_iota(jnp.int32, (H_A, LANE), 0)
        li = lax.broadcasted_iota(jnp.int32, (H_A, LANE), 1)
        spread = (li % H_A == hi).astype(BF16)
        key = lax.broadcasted_iota(jnp.int32, (page, LANE), 0)
        col_key = lax.broadcasted_iota(jnp.int32, (page, LANE), 1) // H_A
        _, _, same = head_match(page * H_A)
        carry = carry_ref[...]
        blocks = []
        for kc_ref, vc_ref, lfc_ref in zip(kc_refs, vc_refs, lfc_refs):
            lf16 = sum(_dot(p, spread) for p in _split3(lfc_ref[0]))
            parts = []
            for s in range(page // per_row):
                later = key > col_key + s * per_row
                suffix = jnp.sum(jnp.where(later, lf16, 0.0), axis=0, keepdims=True)
                parts.append(jnp.broadcast_to(carry + suffix, (rows, LANE)))
            blocks.append((kc_ref[0].reshape(page * H_A, D_A), vc_ref[0].reshape(page * H_A, D_A),
                           jnp.concatenate(parts, axis=1), same))
            carry = carry + jnp.sum(lf16, axis=0, keepdims=True)
        attend(blocks)
        carry_ref[...] = carry

    @pl.when(j == pl.num_programs(1) - 1)
    def _():
        o_ref[0] = acc_ref[...] / l_ref[...]


def _fox_sample(page_table, q, kn, vn, lfn, cache_k, cache_v, cache_lf):
    b, rows, dh = q.shape
    n_pages = page_table.shape[1]
    page = cache_k.shape[1]
    pps = 8 if n_pages % 8 == 0 else 1
    cur = lambda i, j, pt: (i, 0, 0)

    def past(p, nd):
        def index_map(i, j, pt):
            return (pt[i, n_pages - 1 - ((jnp.maximum(j, 1) - 1) * pps + p)],) + (0,) * nd
        return index_map

    kv_specs = [pl.BlockSpec((1, page, H_A, dh), past(p, 3)) for p in range(pps)]
    lf_specs = [pl.BlockSpec((1, page, H_A), past(p, 2)) for p in range(pps)]
    grid_spec = pltpu.PrefetchScalarGridSpec(
        num_scalar_prefetch=1,
        grid=(b, n_pages // pps + 1),
        in_specs=[
            pl.BlockSpec((1, rows, dh), cur),
            pl.BlockSpec((1, LANE, dh), cur),
            pl.BlockSpec((1, LANE, dh), cur),
            pl.BlockSpec((1, 1, LANE), cur),
        ] + kv_specs + kv_specs + lf_specs,
        out_specs=pl.BlockSpec((1, rows, dh), cur),
        scratch_shapes=[
            pltpu.VMEM((rows, dh), BF16),
            pltpu.VMEM((rows, 1), F32),
            pltpu.VMEM((rows, 1), F32),
            pltpu.VMEM((rows, dh), F32),
            pltpu.VMEM((1, LANE), F32),
        ],
    )
    return pl.pallas_call(
        functools.partial(_fox_sample_kernel, pages_per_step=pps),
        grid_spec=grid_spec,
        out_shape=jax.ShapeDtypeStruct((b, rows, dh), F32),
        compiler_params=_params("parallel", "arbitrary"),
        name="fox_sample",
    )(page_table, q, kn, vn, lfn, *([cache_k] * pps), *([cache_v] * pps), *([cache_lf] * pps))


def _router_kernel(x_ref, nw_ref, wr_ref, g_ref):
    h = _rms(x_ref[...], nw_ref[...])
    logits = jnp.dot(h, wr_ref[...], preferred_element_type=F32, precision=HIGHEST)
    lane = lax.broadcasted_iota(jnp.int32, logits.shape, 1)
    lanef = lane.astype(F32)
    neg = -jnp.inf
    is_group = (lane >= N_EXPERTS) & (lane < N_EXPERTS + N_GROUPS)
    gl = jnp.where(is_group, logits, neg)
    gmax = jnp.max(gl, axis=-1, keepdims=True)
    g_sel = jnp.min(jnp.where(gl == gmax, lanef - N_EXPERTS, 1e9), axis=-1, keepdims=True)
    p_sel = 1.0 / jnp.sum(jnp.where(is_group, jnp.exp(logits - gmax), 0.0), axis=-1, keepdims=True)
    in_sel = (lane < N_EXPERTS) & ((lane // EXPERTS_PER_GROUP).astype(F32) == g_sel)
    el = jnp.where(in_sel, logits, neg)
    m1 = jnp.max(el, axis=-1, keepdims=True)
    i1 = jnp.min(jnp.where(el == m1, lanef, 1e9), axis=-1, keepdims=True)
    el2 = jnp.where(lanef == i1, neg, el)
    m2 = jnp.max(el2, axis=-1, keepdims=True)
    i2 = jnp.min(jnp.where(el2 == m2, lanef, 1e9), axis=-1, keepdims=True)
    e2 = jnp.exp(m2 - m1)
    w1 = p_sel / (1.0 + e2)
    g_ref[...] = jnp.where(lanef == i1, w1, jnp.where(lanef == i2, w1 * e2, jnp.where(lane == N_EXPERTS, g_sel, 0.0)))


def _router(x, nw, wr):
    t, d = x.shape
    tm = _row_tile(t, 512)
    return pl.pallas_call(
        _router_kernel,
        grid=(t // tm,),
        in_specs=[pl.BlockSpec((tm, d), lambda i: (i, 0)), pl.BlockSpec(nw.shape, lambda i: (0, 0)),
                  pl.BlockSpec(wr.shape, lambda i: (0, 0))],
        out_specs=pl.BlockSpec((tm, LANE), lambda i: (i, 0)),
        out_shape=jax.ShapeDtypeStruct((t, LANE), F32),
        compiler_params=_params("parallel"),
        name="moe_router",
    )(x, nw, wr)


def _moe_kernel(x_ref, nw_ref, g_ref, wg_ref, wu_ref, wd_ref, nf_ref, o_ref, hb_ref, acc_ref, *, final_norm):
    e = pl.program_id(1)

    @pl.when(e == 0)
    def _():
        hb_ref[...] = _rms(x_ref[...], nw_ref[...]).astype(BF16)
        acc_ref[...] = jnp.zeros(acc_ref.shape, F32)

    gates = g_ref[...]
    lane = lax.broadcasted_iota(jnp.int32, gates.shape, 1)
    gcol = jnp.sum(jnp.where(lane == e, gates, 0.0), axis=-1, keepdims=True)
    hb = hb_ref[...]
    act = _silu(_dot(hb, wg_ref[0])) * _dot(hb, wu_ref[0]) * gcol
    acc_ref[...] += _dot(act.astype(BF16), wd_ref[0])

    @pl.when(e == pl.num_programs(1) - 1)
    def _():
        y = x_ref[...] + acc_ref[...]
        o_ref[...] = _rms(y, nf_ref[...]) if final_norm else y


def _moe_dense(x, nw, gates, wg, wu, wd, nf, final_norm):
    t, d = x.shape
    tm = _row_tile(t, 1024)
    n_e, _, f = wg.shape
    return pl.pallas_call(
        functools.partial(_moe_kernel, final_norm=final_norm),
        grid=(t // tm, n_e),
        in_specs=[
            pl.BlockSpec((tm, d), lambda i, e: (i, 0)),
            pl.BlockSpec(nw.shape, lambda i, e: (0, 0)),
            pl.BlockSpec((tm, LANE), lambda i, e: (i, 0)),
            pl.BlockSpec((1, d, f), lambda i, e: (e, 0, 0)),
            pl.BlockSpec((1, d, f), lambda i, e: (e, 0, 0)),
            pl.BlockSpec((1, f, d), lambda i, e: (e, 0, 0)),
            pl.BlockSpec(nf.shape, lambda i, e: (0, 0)),
        ],
        out_specs=pl.BlockSpec((tm, d), lambda i, e: (i, 0)),
        out_shape=jax.ShapeDtypeStruct((t, d), F32),
        scratch_shapes=[pltpu.VMEM((tm, d), BF16), pltpu.VMEM((tm, d), F32)],
        compiler_params=_params("parallel", "arbitrary"),
        name="moe_experts",
    )(x, nw, gates, wg, wu, wd, nf)


def _group_rank_kernel(g_ref, grp_ref, rank_ref, cnt_ref, run_ref):
    i = pl.program_id(0)
    tm = g_ref.shape[0]

    @pl.when(i == 0)
    def _():
        run_ref[...] = jnp.zeros(run_ref.shape, F32)

    gates = g_ref[...]
    lane = lax.broadcasted_iota(jnp.int32, gates.shape, 1)
    g_sel = jnp.sum(jnp.where(lane == N_EXPERTS, gates, 0.0), axis=-1, keepdims=True)
    onehot_t = (lane.astype(F32) == g_sel).astype(F32).T[:8]
    ri = lax.broadcasted_iota(jnp.int32, (tm, tm), 0)
    ci = lax.broadcasted_iota(jnp.int32, (tm, tm), 1)
    earlier = _dot(onehot_t.astype(BF16), (ri < ci).astype(BF16))
    run = run_ref[...]
    gidx = lax.broadcasted_iota(jnp.int32, (8, tm), 0).astype(F32)
    rank_ref[0] = jnp.sum(onehot_t * (earlier + run), axis=0, keepdims=True).astype(jnp.int32)
    grp_ref[0] = jnp.sum(onehot_t * gidx, axis=0, keepdims=True).astype(jnp.int32)
    run = run + jnp.sum(onehot_t, axis=-1, keepdims=True)
    run_ref[...] = run
    cnt_ref[...] = jnp.broadcast_to(run, cnt_ref.shape)


def _group_rank(gates, tm):
    t = gates.shape[0]
    nt = t // tm
    row = pl.BlockSpec((1, 1, tm), lambda i: (i, 0, 0))
    return pl.pallas_call(
        _group_rank_kernel,
        grid=(nt,),
        in_specs=[pl.BlockSpec((tm, LANE), lambda i: (i, 0))],
        out_specs=[row, row, pl.BlockSpec((8, LANE), lambda i: (0, 0))],
        out_shape=[jax.ShapeDtypeStruct((nt, 1, tm), jnp.int32), jax.ShapeDtypeStruct((nt, 1, tm), jnp.int32),
                   jax.ShapeDtypeStruct((8, LANE), F32)],
        scratch_shapes=[pltpu.VMEM((8, 1), F32)],
        compiler_params=_params("arbitrary"),
        name="moe_group_rank",
    )(gates)


def _row_copy(src, src_row, dst, dst_row, sem):
    return pltpu.make_async_copy(src.at[pl.ds(src_row, 1)], dst.at[pl.ds(dst_row, 1)], sem)


def _dispatch_kernel(dest_ref, x_ref, g_ref, buf_ref, o_ref, stage_ref, sem):
    del buf_ref
    tm, d = x_ref.shape
    stage_ref[:, :d] = x_ref[...]
    stage_ref[:, d:] = g_ref[...]

    def start(r, _):
        _row_copy(stage_ref, r, o_ref, dest_ref[0, 0, r], sem).start()
        return 0

    def wait(r, _):
        _row_copy(stage_ref, 0, o_ref, 0, sem).wait()
        return 0

    lax.fori_loop(0, tm, start, 0, unroll=8)
    lax.fori_loop(0, tm, wait, 0, unroll=8)


def _dispatch(dest, x, gates, buf, tm):
    t, d = x.shape
    return pl.pallas_call(
        _dispatch_kernel,
        grid=(t // tm,),
        in_specs=[pl.BlockSpec((1, 1, tm), lambda i: (i, 0, 0), memory_space=pltpu.SMEM),
                  pl.BlockSpec((tm, d), lambda i: (i, 0)),
                  pl.BlockSpec((tm, LANE), lambda i: (i, 0)),
                  pl.BlockSpec(memory_space=pl.ANY)],
        out_specs=pl.BlockSpec(memory_space=pl.ANY),
        out_shape=jax.ShapeDtypeStruct(buf.shape, F32),
        scratch_shapes=[pltpu.VMEM((tm, d + LANE), F32), pltpu.SemaphoreType.DMA(())],
        input_output_aliases={3: 0},
        compiler_params=_params("arbitrary"),
        name="moe_dispatch",
    )(dest, x, gates, buf)


def _moe_group_kernel(tg_ref, na_ref, xs_ref, nw_ref, wg_ref, wu_ref, wd_ref, nf_ref, o_ref, *, final_norm):
    i = pl.program_id(0)
    d = o_ref.shape[1]

    @pl.when(i < na_ref[0])
    def _():
        x = xs_ref[:, :d]
        gates = xs_ref[:, d:]
        hb = _rms(x, nw_ref[...]).astype(BF16)
        lane = lax.broadcasted_iota(jnp.int32, gates.shape, 1)
        first = tg_ref[i] * EXPERTS_PER_GROUP
        acc = jnp.zeros(x.shape, F32)
        for e in range(EXPERTS_PER_GROUP):
            gcol = jnp.sum(jnp.where(lane == first + e, gates, 0.0), axis=-1, keepdims=True)
            act = _silu(_dot(hb, wg_ref[0, e])) * _dot(hb, wu_ref[0, e]) * gcol
            acc += _dot(act.astype(BF16), wd_ref[0, e])
        y = x + acc
        o_ref[...] = _rms(y, nf_ref[...]) if final_norm else y

    @pl.when(i >= na_ref[0])
    def _():
        o_ref[...] = jnp.zeros(o_ref.shape, F32)


def _moe_group(tile_group, n_active, xs, nw, wg, wu, wd, nf, tm, final_norm):
    tp, dx = xs.shape
    d = dx - LANE
    wspec = lambda a: pl.BlockSpec((1,) + a.shape[1:], lambda i, tg, na: (tg[i], 0, 0, 0))
    vec = lambda a: pl.BlockSpec(a.shape, lambda i, tg, na: (0, 0))
    grid_spec = pltpu.PrefetchScalarGridSpec(
        num_scalar_prefetch=2,
        grid=(tp // tm,),
        in_specs=[pl.BlockSpec((tm, dx), lambda i, tg, na: (i, 0)), vec(nw), wspec(wg), wspec(wu), wspec(wd), vec(nf)],
        out_specs=pl.BlockSpec((tm, d), lambda i, tg, na: (i, 0)),
    )
    return pl.pallas_call(
        functools.partial(_moe_group_kernel, final_norm=final_norm),
        grid_spec=grid_spec,
        out_shape=jax.ShapeDtypeStruct((tp, d), F32),
        compiler_params=_params("arbitrary"),
        name="moe_group_experts",
    )(tile_group, n_active, xs, nw, wg, wu, wd, nf)


def _combine_kernel(dest_ref, ys_ref, o_ref, sem):
    tm = o_ref.shape[0]

    def start(r, _):
        _row_copy(ys_ref, dest_ref[0, 0, r], o_ref, r, sem).start()
        return 0

    def wait(r, _):
        _row_copy(ys_ref, 0, o_ref, 0, sem).wait()
        return 0

    lax.fori_loop(0, tm, start, 0, unroll=8)
    lax.fori_loop(0, tm, wait, 0, unroll=8)


def _combine(dest, ys, t, tm):
    d = ys.shape[1]
    return pl.pallas_call(
        _combine_kernel,
        grid=(t // tm,),
        in_specs=[pl.BlockSpec((1, 1, tm), lambda i: (i, 0, 0), memory_space=pltpu.SMEM),
                  pl.BlockSpec(memory_space=pl.ANY)],
        out_specs=pl.BlockSpec((tm, d), lambda i: (i, 0)),
        out_shape=jax.ShapeDtypeStruct((t, d), F32),
        scratch_shapes=[pltpu.SemaphoreType.DMA(())],
        compiler_params=_params("arbitrary"),
        name="moe_combine",
    )(dest, ys)


def _moe_sorted(x, nw, gates, wg, wu, wd, nf, final_norm):
    t, d = x.shape
    tm = 512
    nt = t // tm
    grp, rank, cnt = _group_rank(gates, tm)
    counts = cnt[:N_GROUPS, 0].astype(jnp.int32)
    tiles = (counts + tm - 1) // tm
    tile_end = jnp.cumsum(tiles)
    offsets = (tile_end - tiles) * tm
    dest = jnp.take(offsets, grp) + rank
    n_tiles = nt + N_GROUPS
    tile_group = jnp.minimum(jnp.searchsorted(tile_end, jnp.arange(n_tiles), side="right"), N_GROUPS - 1)
    xs = _dispatch(dest, x, gates, jnp.zeros((n_tiles * tm, d + LANE), F32), tm)
    split = lambda w: w.reshape((N_GROUPS, EXPERTS_PER_GROUP) + w.shape[1:])
    ys = _moe_group(tile_group.astype(jnp.int32), tile_end[-1:].astype(jnp.int32), xs, nw,
                    split(wg), split(wu), split(wd), nf, tm, final_norm)
    return _combine(dest, ys, t, tm)


def _odd_kernel(x_ref, nw_ref, win_ref, lnw_ref, lnb_ref, wm_ref, bs_ref, wout_ref, *refs, emit_v):
    if emit_v:
        o_ref, v_ref, gated_ref = refs
    else:
        o_ref, gated_ref = refs
    tm = x_ref.shape[0]
    dc = lnw_ref.shape[1]
    gd = dc // H_C
    x = x_ref[...]
    hb = _rms(x, nw_ref[...]).astype(BF16)
    z = _dot(hb, win_ref[...])
    z = 0.5 * z * (1.0 + lax.erf(z * (2.0 ** -0.5)))
    v = z[:, dc:]
    mu = jnp.mean(v, axis=-1, keepdims=True)
    vc = v - mu
    vn = vc * lax.rsqrt(jnp.mean(vc * vc, axis=-1, keepdims=True) + EPS) * lnw_ref[...] + lnb_ref[...]
    if emit_v:
        v_ref[...] = vn
    vb = vn.astype(BF16)
    for ci in range(tm // CHUNK_C):
        rs = slice(ci * CHUNK_C, (ci + 1) * CHUNK_C)
        for g in range(H_C):
            cs = slice(g * gd, (g + 1) * gd)
            mixed = _dot(wm_ref[g], vb[rs, cs]) + jnp.concatenate([bs_ref[g]] * (gd // LANE), axis=1)
            gated_ref[rs, cs] = (z[rs, cs] * mixed).astype(BF16)
    o_ref[...] = x + _dot(gated_ref[...], wout_ref[...])


def _odd_mixer(x, nw, w_in, ln_w, ln_b, wm, bs, w_out, emit_v):
    t, d = x.shape
    dc = ln_w.shape[1]
    tm = _row_tile(t, 512)
    row = lambda w: pl.BlockSpec((tm, w), lambda i: (i, 0))
    full = lambda a: pl.BlockSpec(a.shape, lambda i: (0,) * a.ndim)
    out_specs = [row(d)] + ([row(dc)] if emit_v else [])
    out_shape = [jax.ShapeDtypeStruct((t, d), F32)] + ([jax.ShapeDtypeStruct((t, dc), F32)] if emit_v else [])
    return pl.pallas_call(
        functools.partial(_odd_kernel, emit_v=emit_v),
        grid=(t // tm,),
        in_specs=[row(d), full(nw), full(w_in), full(ln_w), full(ln_b), full(wm), full(bs), full(w_out)],
        out_specs=out_specs,
        out_shape=out_shape,
        scratch_shapes=[pltpu.VMEM((tm, dc), BF16)],
        compiler_params=_params("parallel"),
        name="odd_mixer",
    )(x, nw, w_in, ln_w, ln_b, wm, bs, w_out)


def _prep_even_weights(w_in, f_bias, a_log, dt_bias):
    d = w_in.shape[0]
    small = jnp.concatenate(
        [w_in[:, 1536:1544], w_in[:, 3080:3088], jnp.zeros((d, LANE - 16), w_in.dtype)], axis=1)
    w = jnp.concatenate([w_in[:, 0:1536], w_in[:, 1544:3080], w_in[:, 3088:3600], small], axis=1)
    bias = jnp.zeros((1, LANE), F32).at[0, 0:8].set(f_bias).at[0, 8:12].set(dt_bias)
    alog = jnp.zeros((1, LANE), F32).at[0, 8:12].set(a_log)
    return w.astype(BF16), bias, alog


def kernel(x_prompt, x_sample, cache_k, cache_v, cache_logf, state_delta, state_conv, page_table, norm_mix, norm_ffn, norm_final, w_in_even, w_out_even, fox_forget_bias, dn_conv_w, dn_a_log, dn_dt_bias, dn_norm_w, w_in_odd, gm_ln_w, gm_ln_b, gm_spatial_w, gm_spatial_b, w_out_odd, moe_router_group, moe_router_expert, moe_w_gate, moe_w_up, moe_w_down):
    n_p, seq, d = x_prompt.shape
    n_s, dec = x_sample.shape[:2]
    page = cache_k.shape[2]
    pool = cache_k.shape[1]
    mix_a = H_A * D_A
    xp = x_prompt.reshape(n_p * seq, d)
    xs = x_sample.reshape(n_s * dec, d)
    row = lambda a: a[None, :]

    w_e, bias_e, alog_e = _prep_even_weights(w_in_even[0], fox_forget_bias[0], dn_a_log[0], dn_dt_bias[0])
    w_out_e = w_out_even[0].astype(BF16)
    conv_w = jnp.zeros((8, dn_conv_w.shape[2]), F32).at[:CONV_W].set(dn_conv_w[0])
    onw = row(dn_norm_w[0])
    nmix0 = row(norm_mix[0])

    def delta(conv3, prefix, small3, gz3, s0, valid, nb):
        small_t = jnp.swapaxes(small3[:, :, :16], 1, 2)
        wv, wk, qd, kd, qk, gl = _delta_prep(conv3, prefix, conv_w, small3, small_t, valid)
        return _delta_scan(wv, wk, qd, kd, qk, gl, s0, gz3, onw, nb)

    q, k, v, c, gz, s = _even_in(xp, nmix0, w_e, bias_e, alog_e)
    logf_p = s[:, :H_A].reshape(n_p, seq, H_A)
    csum = _cumsum_lanes(jnp.swapaxes(logf_p, 1, 2).reshape(n_p * H_A, seq)).reshape(n_p, H_A, seq)
    as3 = lambda a: a.reshape(n_p, seq, a.shape[-1])
    oa_p = _fox_prompt(as3(q), as3(k), as3(v), csum)
    conv_p = as3(c)
    ob_p, sd_p = delta(conv_p, jnp.zeros((n_p, 8, conv_p.shape[-1]), F32), as3(s), as3(gz),
                       jnp.zeros((n_p, H_B, DK_B, DK_B), F32), seq, 4)
    xp = _even_out(xp, oa_p.reshape(n_p * seq, mix_a), ob_p.reshape(n_p * seq, -1), w_out_e)
    k_p = k.reshape(1, n_p, seq, H_A, D_A)
    v_p = v.reshape(1, n_p, seq, H_A, D_A)
    conv_state_p = conv_p[:, seq - (CONV_W - 1):][None]

    q, k, v, c, gz, s = _even_in(xs, nmix0, w_e, bias_e, alog_e)
    as3 = lambda a: a.reshape(n_s, dec, a.shape[-1])
    padrows = lambda a, n: jnp.pad(a, ((0, 0), (0, n - a.shape[1]), (0, 0)))
    logf_s = s[:, :H_A].reshape(n_s, dec, H_A)
    heads = lambda a: a.reshape(n_s, dec * H_A, D_A)
    lfn = padrows(logf_s.reshape(n_s, dec * H_A, 1), LANE).reshape(n_s, 1, LANE)
    oa_s = _fox_sample(page_table, heads(q), padrows(heads(k), LANE), padrows(heads(v), LANE), lfn,
                       cache_k[0], cache_v[0], cache_logf[0])
    conv_s = as3(c)
    lp = DELTA_CHUNK
    prefix_s = jnp.pad(state_conv[0], ((0, 0), (8 - (CONV_W - 1), 0), (0, 0)))
    ob_s, sd_s = delta(padrows(conv_s, lp), prefix_s, padrows(as3(s), lp), padrows(as3(gz), lp),
                       state_delta[0], dec, 8)
    xs = _even_out(xs, oa_s.reshape(n_s * dec, mix_a).astype(BF16), ob_s[:, :dec].reshape(n_s * dec, -1), w_out_e)
    k_s = k.reshape(1, n_s, dec, H_A, D_A)
    v_s = v.reshape(1, n_s, dec, H_A, D_A)
    conv_state_s = jnp.concatenate([state_conv[0], conv_s], axis=1)[:, dec:][None]

    def moe(x, layer, final):
        wr = jnp.concatenate([moe_router_expert[layer], moe_router_group[layer],
                              jnp.zeros((d, LANE - N_EXPERTS - N_GROUPS), F32)], axis=1)
        gates = _router(x, row(norm_ffn[layer]), wr)
        experts = _moe_sorted if x.shape[0] >= 4096 else _moe_dense
        return experts(x, row(norm_ffn[layer]), gates, wg[layer], wu[layer], wd[layer], row(norm_final), final)

    wg, wu, wd = moe_w_gate.astype(BF16), moe_w_up.astype(BF16), moe_w_down.astype(BF16)
    xp = moe(xp, 0, False)
    xs = moe(xs, 0, False)

    idx = jnp.arange(CHUNK_C)
    w_m = jnp.where(idx[:, None] >= idx[None, :], gm_spatial_w[0], 0.0)
    b_s = gm_spatial_b[0]
    lanes = lambda b: jnp.broadcast_to(b[:, :, None], b.shape + (LANE,))
    reps = CHUNK_C // dec
    assert n_s * dec == CHUNK_C, "sample group must fill exactly one 128-row mixing tile"
    w_m_s = (jnp.eye(reps, dtype=F32)[None, :, None, :, None] * w_m[:, None, :dec, None, :dec]).reshape(H_C, CHUNK_C, CHUNK_C)
    b_s_s = jnp.tile(b_s[:, :dec], (1, reps))
    odd_w = (row(norm_mix[1]), w_in_odd[0].astype(BF16), row(gm_ln_w[0]), row(gm_ln_b[0]))
    w_out_o = w_out_odd[0].astype(BF16)
    (xp,) = _odd_mixer(xp, *odd_w, w_m.astype(BF16), lanes(b_s), w_out_o, False)
    xs, v_rows = _odd_mixer(xs, *odd_w, w_m_s.astype(BF16), lanes(b_s_s), w_out_o, True)

    y_p = moe(xp, 1, True).reshape(n_p, seq, d)
    y_s = moe(xs, 1, True).reshape(n_s, dec, d)
    return (y_p, y_s, k_p, v_p, logf_p[None], sd_p[None], conv_state_p,
            k_s, v_s, logf_s[None], sd_s[None], conv_state_s, v_rows.reshape(1, n_s, dec, -1))
```

```python
import functools

import jax
import jax.numpy as jnp
from jax import lax
from jax.experimental import pallas as pl
from jax.experimental.pallas import tpu as pltpu

F32 = jnp.float32
BF16 = jnp.bfloat16
EPS = 1e-6
LANE = 128
VMEM_LIMIT = 56 * 1024 * 1024
HIGHEST = lax.Precision.HIGHEST

H_A, D_A = 8, 64
H_B, DK_B = 4, 128
CONV_W = 4
DELTA_CHUNK = 64
N_GROUPS, EXPERTS_PER_GROUP = 4, 8
N_EXPERTS = N_GROUPS * EXPERTS_PER_GROUP
CHUNK_C = 128
H_C = 8


def _params(*sem):
    return pltpu.CompilerParams(dimension_semantics=sem, vmem_limit_bytes=VMEM_LIMIT)


def _row_tile(t, pref):
    return pref if t % pref == 0 else t


def _rms(x, w):
    return x * lax.rsqrt(jnp.mean(x * x, axis=-1, keepdims=True) + EPS) * w


def _softplus_tail(z):
    return jnp.log1p(jnp.exp(-jnp.abs(z)))


def _even_in_kernel(x_ref, nw_ref, w_ref, bias_ref, alog_ref,
                    q_ref, k_ref, v_ref, c_ref, gz_ref, s_ref):
    hb = _rms(x_ref[...], nw_ref[...]).astype(BF16)

    def proj(lo, hi):
        return jnp.dot(hb, w_ref[:, lo:hi], preferred_element_type=F32)

    q_ref[...] = proj(0, 512)
    k_ref[...] = proj(512, 1024)
    v_ref[...] = proj(1024, 1536)
    c_ref[...] = proj(1536, 3072)
    gz_ref[...] = proj(3072, 3584)
    z = proj(3584, 3712) + bias_ref[...]
    tail = _softplus_tail(z)
    logf = jnp.minimum(z, 0.0) - tail
    g = -jnp.exp(alog_ref[...]) * (jnp.maximum(z, 0.0) + tail)
    beta = 1.0 / (1.0 + jnp.exp(-z))
    lane = lax.broadcasted_iota(jnp.int32, z.shape, 1)
    s_ref[...] = jnp.where(lane < 8, logf, jnp.where(lane < 12, g, jnp.where(lane < 16, beta, 0.0)))


def _even_in(x, nw, w, bias, alog):
    t, d = x.shape
    tm = _row_tile(t, 512)
    widths = (512, 512, 512, 1536, 512, LANE)
    row = lambda wd: pl.BlockSpec((tm, wd), lambda i: (i, 0))
    full = lambda a: pl.BlockSpec(a.shape, lambda i: (0, 0))
    return pl.pallas_call(
        _even_in_kernel,
        grid=(t // tm,),
        in_specs=[row(d), full(nw), full(w), full(bias), full(alog)],
        out_specs=[row(wd) for wd in widths],
        out_shape=[jax.ShapeDtypeStruct((t, wd), F32) for wd in widths],
        compiler_params=_params("parallel"),
        name="even_in_proj",
    )(x, nw, w, bias, alog)


def _cumsum_kernel(x_ref, o_ref):
    rows, length = x_ref.shape
    r = lax.broadcasted_iota(jnp.int32, (LANE, LANE), 0)
    c = lax.broadcasted_iota(jnp.int32, (LANE, LANE), 1)
    upper = (r <= c).astype(F32)
    carry = jnp.zeros((rows, 1), F32)
    for b in range(length // LANE):
        blk = jnp.dot(x_ref[:, b * LANE:(b + 1) * LANE], upper,
                      preferred_element_type=F32, precision=HIGHEST) + carry
        o_ref[:, b * LANE:(b + 1) * LANE] = blk
        carry = blk[:, LANE - 1:LANE]


def _cumsum_lanes(x):
    rows, length = x.shape
    tr = 8
    return pl.pallas_call(
        _cumsum_kernel,
        grid=(rows // tr,),
        in_specs=[pl.BlockSpec((tr, length), lambda i: (i, 0))],
        out_specs=pl.BlockSpec((tr, length), lambda i: (i, 0)),
        out_shape=jax.ShapeDtypeStruct((rows, length), F32),
        compiler_params=_params("parallel"),
        name="logf_cumsum",
    )(x)


def _fox_prompt_kernel(q_ref, k_ref, v_ref, c_ref, o_ref, *, tq):
    qi = pl.program_id(2)
    hp = pl.program_id(1)
    scale = D_A ** -0.5
    lane = lax.broadcasted_iota(jnp.int32, (tq, LANE), 1)
    first = lane < D_A
    q = q_ref[0] * scale
    qs = [jnp.where(first, q, 0.0).astype(BF16), jnp.where(first, 0.0, q).astype(BF16)]
    q0 = pl.multiple_of(qi * tq, tq)

    def crow(h, start):
        return c_ref[0, pl.ds(2 * hp + h, 1), pl.ds(start, tq)]

    cq = [crow(h, q0)[:, 0:1] for h in range(2)]

    def block(j0, carry, mask):
        kb = k_ref[0, pl.ds(j0, tq), :].astype(BF16)
        vb = v_ref[0, pl.ds(j0, tq), :].astype(BF16)
        out = []
        for h in range(2):
            m, l, acc = carry[h]
            s = _dot_nt(qs[h], kb) + (cq[h] - crow(h, j0))
            if mask is not None:
                s = jnp.where(mask, s, -jnp.inf)
            m_new = jnp.maximum(m, jnp.max(s, axis=-1, keepdims=True))
            alpha = jnp.exp(m - m_new)
            p = jnp.exp(s - m_new)
            l = alpha * l + jnp.sum(p, axis=-1, keepdims=True)
            acc = alpha * acc + _dot(p.astype(BF16), vb)
            out.append((m_new, l, acc))
        return tuple(out)

    def body(j, carry):
        return block(pl.multiple_of(j * tq, tq), carry, None)

    init = (jnp.full((tq, 1), -jnp.inf, F32), jnp.zeros((tq, 1), F32), jnp.zeros((tq, LANE), F32))
    carry = lax.fori_loop(0, qi, body, (init, init))
    r = lax.broadcasted_iota(jnp.int32, (tq, tq), 0)
    cidx = lax.broadcasted_iota(jnp.int32, (tq, tq), 1)
    (_, l0, a0), (_, l1, a1) = block(q0, carry, cidx <= r)
    o_ref[0] = jnp.where(first, a0 / l0, a1 / l1).astype(o_ref.dtype)


def _fox_prompt(q, k, v, c):
    n, length, width = q.shape
    tq = 512 if length % 512 == 0 else length
    grid = (n, width // LANE, length // tq)
    return pl.pallas_call(
        functools.partial(_fox_prompt_kernel, tq=tq),
        grid=grid,
        in_specs=[
            pl.BlockSpec((1, tq, LANE), lambda b, h, i: (b, i, h)),
            pl.BlockSpec((1, length, LANE), lambda b, h, i: (b, 0, h)),
            pl.BlockSpec((1, length, LANE), lambda b, h, i: (b, 0, h)),
            pl.BlockSpec((1, H_A, length), lambda b, h, i: (b, 0, 0)),
        ],
        out_specs=pl.BlockSpec((1, tq, LANE), lambda b, h, i: (b, i, h)),
        out_shape=jax.ShapeDtypeStruct((n, length, width), BF16),
        compiler_params=_params("parallel", "parallel", "arbitrary"),
        name="fox_prompt",
    )(q, k, v, c)


def _even_out_kernel(x_ref, a_ref, b_ref, w_ref, o_ref):
    half = a_ref.shape[1]
    acc = jnp.dot(a_ref[...], w_ref[:half, :], preferred_element_type=F32)
    acc += jnp.dot(b_ref[...], w_ref[half:, :], preferred_element_type=F32)
    o_ref[...] = x_ref[...] + acc


def _even_out(x, oa, ob, w):
    t, d = x.shape
    tm = _row_tile(t, 512)
    row = lambda a: pl.BlockSpec((tm, a.shape[1]), lambda i: (i, 0))
    return pl.pallas_call(
        _even_out_kernel,
        grid=(t // tm,),
        in_specs=[row(x), row(oa), row(ob), pl.BlockSpec(w.shape, lambda i: (0, 0))],
        out_specs=row(x),
        out_shape=jax.ShapeDtypeStruct((t, d), F32),
        compiler_params=_params("parallel"),
        name="even_out_proj",
    )(x, oa, ob, w)


def _split3(x):
    hi = x.astype(BF16)
    r1 = x - hi.astype(F32)
    mid = r1.astype(BF16)
    lo = (r1 - mid.astype(F32)).astype(BF16)
    return hi, mid, lo


def _dot(a, b):
    return jnp.dot(a, b, preferred_element_type=F32)


def _dot_nt(a, b):
    return lax.dot_general(a, b, (((1,), (1,)), ((), ())), preferred_element_type=F32)


def _dot_tn(a, b):
    return lax.dot_general(a, b, (((0,), (0,)), ((), ())), preferred_element_type=F32)


def _silu(x):
    return x / (1.0 + jnp.exp(-x))


def _delta_prep_kernel(x_ref, xprev_ref, pre_ref, cw_ref, sm_ref,
                       wv_ref, wk_ref, qd_ref, kd_ref, qk_ref, gl_ref, *, ta, valid):
    i = pl.program_id(1)
    c = DELTA_CHUNK
    dk = DK_B
    rr = H_B * c
    x = x_ref[0]
    prev = jnp.where(i == 0, pre_ref[0], xprev_ref[0])
    xcat = jnp.concatenate([prev, x], axis=0)
    conv = x * cw_ref[CONV_W - 1:CONV_W, :]
    for s in range(1, CONV_W):
        conv += pltpu.roll(xcat, s, axis=0)[8:] * cw_ref[CONV_W - 1 - s:CONV_W - s, :]
    act = _silu(conv)
    sm = sm_ref[0]

    ri = lax.broadcasted_iota(jnp.int32, (rr, rr), 0)
    ci = lax.broadcasted_iota(jnp.int32, (rr, rr), 1)
    same_head = (ri // c) == (ci // c)
    incl = same_head & (ri >= ci)
    strict = same_head & (ri > ci)
    tri = incl.astype(BF16)
    lane = lax.broadcasted_iota(jnp.int32, (c, LANE), 1)

    qn, kn, va = [], [], []
    for h in range(H_B):
        qa = act[:, h * dk:(h + 1) * dk]
        ka = act[:, (H_B + h) * dk:(H_B + h + 1) * dk]
        qn.append(qa * lax.rsqrt(jnp.sum(qa * qa, axis=-1, keepdims=True) + EPS) * (dk ** -0.5))
        kn.append(ka * lax.rsqrt(jnp.sum(ka * ka, axis=-1, keepdims=True) + EPS))
        va.append(act[:, (2 * H_B + h) * dk:(2 * H_B + h + 1) * dk])

    chunks = range(ta // c)
    stack = lambda parts: jnp.concatenate(parts, axis=0)
    q, k, v, b_col, gcb = [], [], [], [], []
    for s in chunks:
        r0 = s * c
        live = lax.broadcasted_iota(jnp.int32, (c, 1), 0) + (i * ta + r0) < valid
        smc = sm[r0:r0 + c]
        pick = lambda ln: jnp.where(live, jnp.sum(jnp.where(lane == ln, smc, 0.0), axis=-1, keepdims=True), 0.0)
        q.append(stack([a[r0:r0 + c] for a in qn]))
        k.append(stack([a[r0:r0 + c] for a in kn]))
        v.append(stack([a[r0:r0 + c] for a in va]))
        b_col.append(stack([pick(12 + h) for h in range(H_B)]))
        g_stack = stack([jnp.broadcast_to(pick(8 + h), (c, LANE)) for h in range(H_B)])
        gcb.append(sum(_dot(tri, p) for p in _split3(g_stack)))

    wide = [jnp.concatenate([g, g], axis=1) for g in gcb]
    decay = [jnp.exp(jnp.where(incl, w - w.T, -jnp.inf)) for w in wide]
    eg = [jnp.exp(g) for g in gcb]
    kb = [a * b for a, b in zip(k, b_col)]
    kbf = [a.astype(BF16) for a in k]
    a_mat = [jnp.where(strict, _dot_nt(a.astype(BF16), b) * d, 0.0) for a, b, d in zip(kb, kbf, decay)]
    qk = [jnp.where(incl, _dot_nt(a.astype(BF16), b) * d, 0.0) for a, b, d in zip(q, kbf, decay)]
    y = [-a for a in a_mat]
    p = list(y)
    for _ in range(5):
        yb = [a.astype(BF16) for a in y]
        y = [_dot(a, a) for a in yb]
        p = [a + b + _dot(a.astype(BF16), b.astype(BF16)) for a, b in zip(p, y)]
    rhs = [jnp.concatenate([a * b, kbb * e], axis=1) for a, b, kbb, e in zip(v, b_col, kb, eg)]
    w = [r + _dot(a.astype(BF16), r.astype(BF16)) for a, r in zip(p, rhs)]

    for s in chunks:
        r0 = s * c
        for h in range(H_B):
            hr = slice(h * c, (h + 1) * c)
            hs = slice(h * dk, (h + 1) * dk)
            g_last = gcb[s][h * c + c - 1:(h + 1) * c, :]
            wv_ref[0, r0:r0 + c, hs] = w[s][hr, :dk]
            wk_ref[0, r0:r0 + c, hs] = w[s][hr, dk:].astype(BF16)
            qd_ref[0, r0:r0 + c, hs] = (q[s][hr] * eg[s][hr]).astype(BF16)
            kd_ref[0, r0:r0 + c, hs] = (k[s][hr] * jnp.exp(g_last - gcb[s][hr])).astype(BF16)
            qk_ref[0, h, r0:r0 + c, :] = qk[s][hr, h * c:(h + 1) * c].astype(BF16)
            gl_ref[0, s, h:h + 1, :] = jnp.exp(g_last)


def _delta_prep(conv_in, prefix, conv_w, small, valid):
    n, lp, cd = conv_in.shape
    ta = LANE if lp % LANE == 0 else lp
    nt = lp // ta
    tb = ta // 8
    f = lambda dt, w: jax.ShapeDtypeStruct((n, lp, w), dt)
    blk = lambda w: pl.BlockSpec((1, ta, w), lambda b, i: (b, i, 0))
    width = H_B * DK_B
    return pl.pallas_call(
        functools.partial(_delta_prep_kernel, ta=ta, valid=valid),
        grid=(n, nt),
        in_specs=[
            blk(cd),
            pl.BlockSpec((1, 8, cd), lambda b, i: (b, jnp.maximum(i * tb - 1, 0), 0)),
            pl.BlockSpec((1, 8, cd), lambda b, i: (b, 0, 0)),
            pl.BlockSpec((8, cd), lambda b, i: (0, 0)),
            blk(LANE),
        ],
        out_specs=[
            blk(width), blk(width), blk(width), blk(width),
            pl.BlockSpec((1, H_B, ta, DELTA_CHUNK), lambda b, i: (b, 0, i, 0)),
            pl.BlockSpec((1, ta // DELTA_CHUNK, H_B, LANE), lambda b, i: (b, i, 0, 0)),
        ],
        out_shape=[
            f(F32, width), f(BF16, width), f(BF16, width), f(BF16, width),
            jax.ShapeDtypeStruct((n, H_B, lp, DELTA_CHUNK), BF16),
            jax.ShapeDtypeStruct((n, lp // DELTA_CHUNK, H_B, LANE), F32),
        ],
        compiler_params=_params("parallel", "parallel"),
        name="delta_prep",
    )(conv_in, conv_in, prefix, conv_w, small)


def _delta_scan_kernel(wv_ref, wk_ref, qd_ref, kd_ref, qk_ref, gl_ref, s0_ref, gz_ref, onw_ref,
                       o_ref, s_ref, *, nb, n_chunks):
    c = DELTA_CHUNK
    dk = DK_B

    @pl.when(pl.program_id(1) == 0)
    def _():
        s_ref[...] = s0_ref[...]

    onw = onw_ref[...]

    def body(ci, _):
        r0 = pl.multiple_of(ci * c, c)
        for b in range(nb):
            for h in range(H_B):
                hs = slice(h * dk, (h + 1) * dk)
                state = s_ref[b, h]
                sb = state.astype(BF16)
                lhs = jnp.concatenate([wk_ref[b, pl.ds(r0, c), hs], qd_ref[b, pl.ds(r0, c), hs]], axis=0)
                prod = _dot(lhs, sb)
                v_new = wv_ref[b, pl.ds(r0, c), hs] - prod[:c]
                vb = v_new.astype(BF16)
                o = prod[c:] + _dot(qk_ref[b, h, pl.ds(r0, c), :], vb)
                gl = gl_ref[b, pl.ds(ci, 1), h, :]
                s_ref[b, h] = state * gl + _dot_tn(kd_ref[b, pl.ds(r0, c), hs], vb)
                on = o * lax.rsqrt(jnp.mean(o * o, axis=-1, keepdims=True) + EPS) * onw
                o_ref[b, pl.ds(r0, c), hs] = (on * _silu(gz_ref[b, pl.ds(r0, c), hs])).astype(o_ref.dtype)
        return 0

    lax.fori_loop(0, n_chunks, body, 0)


def _delta_scan(wv, wk, qd, kd, qk, gl, s0, gz, onw, nb):
    n, lp, width = wv.shape
    tl = _row_tile(lp, 512)
    n_chunks = tl // DELTA_CHUNK
    seq = pl.BlockSpec((nb, tl, width), lambda i, t: (i, t, 0))
    state = pl.BlockSpec((nb,) + s0.shape[1:], lambda i, t: (i, 0, 0, 0))
    return pl.pallas_call(
        functools.partial(_delta_scan_kernel, nb=nb, n_chunks=n_chunks),
        grid=(n // nb, lp // tl),
        in_specs=[seq, seq, seq, seq,
                  pl.BlockSpec((nb, H_B, tl, DELTA_CHUNK), lambda i, t: (i, 0, t, 0)),
                  pl.BlockSpec((nb, n_chunks, H_B, LANE), lambda i, t: (i, t, 0, 0)),
                  state, seq, pl.BlockSpec(onw.shape, lambda i, t: (0, 0))],
        out_specs=[seq, state],
        out_shape=[jax.ShapeDtypeStruct((n, lp, width), BF16), jax.ShapeDtypeStruct(s0.shape, F32)],
        compiler_params=_params("parallel", "arbitrary"),
        name="delta_scan",
    )(wv, wk, qd, kd, qk, gl, s0, gz, onw)


def _fox_sample_kernel(pt_ref, q_ref, kn_ref, vn_ref, lfn_ref, *refs, pages_per_step, n_q):
    del pt_ref
    pps = pages_per_step
    kc_refs, vc_refs, lfc_refs = refs[:pps], refs[pps:2 * pps], refs[2 * pps:3 * pps]
    o_ref, qbd_ref, m_ref, l_ref, acc_ref, carry_ref = refs[3 * pps:]
    j = pl.program_id(1)
    rows = n_q * H_A
    page = lfn_ref.shape[2]
    width = H_A * D_A
    ri = lax.broadcasted_iota(jnp.int32, (page, page), 0)
    ci = lax.broadcasted_iota(jnp.int32, (page, page), 1)

    def attend(blocks):
        qbd = qbd_ref[...]
        ss = []
        for kt, _, bias8, mask in blocks:
            s = _dot(qbd, kt.astype(BF16)) + jnp.concatenate([bias8] * n_q, axis=0)
            ss.append(s if mask is None else jnp.where(mask, s, -jnp.inf))
        m_old = m_ref[...]
        m_new = m_old
        for s in ss:
            m_new = jnp.maximum(m_new, jnp.max(s, axis=-1, keepdims=True))
        alpha = jnp.exp(m_old - m_new)
        l_new = alpha * l_ref[...]
        acc = alpha * acc_ref[...]
        for s, (_, vt, _, _) in zip(ss, blocks):
            p = jnp.exp(s - m_new)
            l_new += jnp.sum(p, axis=-1, keepdims=True)
            acc += _dot_nt(p.astype(BF16), vt.astype(BF16))
        l_ref[...] = l_new
        acc_ref[...] = acc
        m_ref[...] = m_new

    @pl.when(j == 0)
    def _():
        q = q_ref[0] * (D_A ** -0.5)
        qrep = jnp.concatenate([jnp.broadcast_to(q[t:t + 1], (H_A, width)) for t in range(n_q)], axis=0)
        r = lax.broadcasted_iota(jnp.int32, qrep.shape, 0)
        ln = lax.broadcasted_iota(jnp.int32, qrep.shape, 1)
        qbd_ref[...] = jnp.where(ln // D_A == r % H_A, qrep, 0.0).astype(BF16)
        m_ref[...] = jnp.full(m_ref.shape, -jnp.inf, F32)
        l_ref[...] = jnp.zeros(l_ref.shape, F32)
        acc_ref[...] = jnp.zeros(acc_ref.shape, F32)
        carry_ref[...] = jnp.zeros(carry_ref.shape, F32)
        upper = (ri <= ci).astype(BF16)
        csum = sum(_dot(p, upper) for p in _split3(lfn_ref[0]))
        r2 = lax.broadcasted_iota(jnp.int32, (rows, page), 0)
        c2 = lax.broadcasted_iota(jnp.int32, (rows, page), 1)
        attend([(kn_ref[0], vn_ref[0], -csum, c2 <= r2 // H_A)])

    @pl.when(j > 0)
    def _():
        after = (ri > ci).astype(BF16)
        carry = carry_ref[...]
        blocks = []
        for kc_ref, vc_ref, lfc_ref in zip(kc_refs, vc_refs, lfc_refs):
            lf = lfc_ref[0]
            suffix = sum(_dot(p, after) for p in _split3(lf))
            blocks.append((kc_ref[0].reshape(width, page), vc_ref[0].reshape(width, page), carry + suffix, None))
            carry = carry + jnp.sum(lf, axis=-1, keepdims=True)
        attend(blocks)
        carry_ref[...] = carry

    @pl.when(j == pl.num_programs(1) - 1)
    def _():
        out = acc_ref[...] / l_ref[...]
        r = lax.broadcasted_iota(jnp.int32, out.shape, 0)
        ln = lax.broadcasted_iota(jnp.int32, out.shape, 1)
        out = jnp.where(ln // D_A == r % H_A, out, 0.0)
        o_ref[0] = jnp.sum(out.reshape(n_q, H_A, width), axis=1)


def _fox_sample(page_table, q, kn, vn, lfn, cache_k, cache_v, cache_lf):
    b, n_q, width = q.shape
    n_pages = page_table.shape[1]
    page = cache_k.shape[3]
    rows = n_q * H_A
    pps = 8 if n_pages % 8 == 0 else 1
    cur = lambda i, j, pt: (i, 0, 0)

    def past(p, nd):
        def index_map(i, j, pt):
            return (pt[i, n_pages - 1 - ((jnp.maximum(j, 1) - 1) * pps + p)],) + (0,) * nd
        return index_map

    kv_specs = [pl.BlockSpec((1, H_A, D_A, page), past(p, 3)) for p in range(pps)]
    lf_specs = [pl.BlockSpec((1, H_A, page), past(p, 2)) for p in range(pps)]
    grid_spec = pltpu.PrefetchScalarGridSpec(
        num_scalar_prefetch=1,
        grid=(b, n_pages // pps + 1),
        in_specs=[
            pl.BlockSpec((1, n_q, width), cur),
            pl.BlockSpec((1, width, page), cur),
            pl.BlockSpec((1, width, page), cur),
            pl.BlockSpec((1, H_A, page), cur),
        ] + kv_specs + kv_specs + lf_specs,
        out_specs=pl.BlockSpec((1, n_q, width), cur),
        scratch_shapes=[
            pltpu.VMEM((rows, width), BF16),
            pltpu.VMEM((rows, 1), F32),
            pltpu.VMEM((rows, 1), F32),
            pltpu.VMEM((rows, width), F32),
            pltpu.VMEM((H_A, 1), F32),
        ],
    )
    return pl.pallas_call(
        functools.partial(_fox_sample_kernel, pages_per_step=pps, n_q=n_q),
        grid_spec=grid_spec,
        out_shape=jax.ShapeDtypeStruct((b, n_q, width), F32),
        compiler_params=_params("parallel", "arbitrary"),
        name="fox_sample",
    )(page_table, q, kn, vn, lfn, *([cache_k] * pps), *([cache_v] * pps), *([cache_lf] * pps))


def _router_kernel(x_ref, nw_ref, wr_ref, g_ref):
    h = _rms(x_ref[...], nw_ref[...])
    logits = jnp.dot(h, wr_ref[...], preferred_element_type=F32, precision=HIGHEST)
    lane = lax.broadcasted_iota(jnp.int32, logits.shape, 1)
    lanef = lane.astype(F32)
    neg = -jnp.inf
    is_group = (lane >= N_EXPERTS) & (lane < N_EXPERTS + N_GROUPS)
    gl = jnp.where(is_group, logits, neg)
    gmax = jnp.max(gl, axis=-1, keepdims=True)
    g_sel = jnp.min(jnp.where(gl == gmax, lanef - N_EXPERTS, 1e9), axis=-1, keepdims=True)
    p_sel = 1.0 / jnp.sum(jnp.where(is_group, jnp.exp(logits - gmax), 0.0), axis=-1, keepdims=True)
    in_sel = (lane < N_EXPERTS) & ((lane // EXPERTS_PER_GROUP).astype(F32) == g_sel)
    el = jnp.where(in_sel, logits, neg)
    m1 = jnp.max(el, axis=-1, keepdims=True)
    i1 = jnp.min(jnp.where(el == m1, lanef, 1e9), axis=-1, keepdims=True)
    el2 = jnp.where(lanef == i1, neg, el)
    m2 = jnp.max(el2, axis=-1, keepdims=True)
    i2 = jnp.min(jnp.where(el2 == m2, lanef, 1e9), axis=-1, keepdims=True)
    e2 = jnp.exp(m2 - m1)
    w1 = p_sel / (1.0 + e2)
    g_ref[...] = jnp.where(lanef == i1, w1, jnp.where(lanef == i2, w1 * e2, jnp.where(lane == N_EXPERTS, g_sel, 0.0)))


def _router(x, nw, wr):
    t, d = x.shape
    tm = _row_tile(t, 512)
    return pl.pallas_call(
        _router_kernel,
        grid=(t // tm,),
        in_specs=[pl.BlockSpec((tm, d), lambda i: (i, 0)), pl.BlockSpec(nw.shape, lambda i: (0, 0)),
                  pl.BlockSpec(wr.shape, lambda i: (0, 0))],
        out_specs=pl.BlockSpec((tm, LANE), lambda i: (i, 0)),
        out_shape=jax.ShapeDtypeStruct((t, LANE), F32),
        compiler_params=_params("parallel"),
        name="moe_router",
    )(x, nw, wr)


def _moe_kernel(x_ref, nw_ref, g_ref, wg_ref, wu_ref, wd_ref, nf_ref, o_ref, hb_ref, acc_ref, *, final_norm):
    e = pl.program_id(1)

    @pl.when(e == 0)
    def _():
        hb_ref[...] = _rms(x_ref[...], nw_ref[...]).astype(BF16)
        acc_ref[...] = jnp.zeros(acc_ref.shape, F32)

    gates = g_ref[...]
    lane = lax.broadcasted_iota(jnp.int32, gates.shape, 1)
    gcol = jnp.sum(jnp.where(lane == e, gates, 0.0), axis=-1, keepdims=True)
    hb = hb_ref[...]
    act = _silu(_dot(hb, wg_ref[0])) * _dot(hb, wu_ref[0]) * gcol
    acc_ref[...] += _dot(act.astype(BF16), wd_ref[0])

    @pl.when(e == pl.num_programs(1) - 1)
    def _():
        y = x_ref[...] + acc_ref[...]
        o_ref[...] = _rms(y, nf_ref[...]) if final_norm else y


def _moe_dense(x, nw, gates, wg, wu, wd, nf, final_norm):
    t, d = x.shape
    tm = _row_tile(t, 1024)
    n_e, _, f = wg.shape
    return pl.pallas_call(
        functools.partial(_moe_kernel, final_norm=final_norm),
        grid=(t // tm, n_e),
        in_specs=[
            pl.BlockSpec((tm, d), lambda i, e: (i, 0)),
            pl.BlockSpec(nw.shape, lambda i, e: (0, 0)),
            pl.BlockSpec((tm, LANE), lambda i, e: (i, 0)),
            pl.BlockSpec((1, d, f), lambda i, e: (e, 0, 0)),
            pl.BlockSpec((1, d, f), lambda i, e: (e, 0, 0)),
            pl.BlockSpec((1, f, d), lambda i, e: (e, 0, 0)),
            pl.BlockSpec(nf.shape, lambda i, e: (0, 0)),
        ],
        out_specs=pl.BlockSpec((tm, d), lambda i, e: (i, 0)),
        out_shape=jax.ShapeDtypeStruct((t, d), F32),
        scratch_shapes=[pltpu.VMEM((tm, d), BF16), pltpu.VMEM((tm, d), F32)],
        compiler_params=_params("parallel", "arbitrary"),
        name="moe_experts",
    )(x, nw, gates, wg, wu, wd, nf)


def _group_rank_kernel(g_ref, grp_ref, rank_ref, cnt_ref, run_ref):
    i = pl.program_id(0)
    tm = g_ref.shape[0]

    @pl.when(i == 0)
    def _():
        run_ref[...] = jnp.zeros(run_ref.shape, F32)

    gates = g_ref[...]
    lane = lax.broadcasted_iota(jnp.int32, gates.shape, 1)
    g_sel = jnp.sum(jnp.where(lane == N_EXPERTS, gates, 0.0), axis=-1, keepdims=True)
    onehot_t = (lane.astype(F32) == g_sel).astype(F32).T[:8]
    ri = lax.broadcasted_iota(jnp.int32, (tm, tm), 0)
    ci = lax.broadcasted_iota(jnp.int32, (tm, tm), 1)
    earlier = _dot(onehot_t.astype(BF16), (ri < ci).astype(BF16))
    run = run_ref[...]
    gidx = lax.broadcasted_iota(jnp.int32, (8, tm), 0).astype(F32)
    rank_ref[0] = jnp.sum(onehot_t * (earlier + run), axis=0, keepdims=True).astype(jnp.int32)
    grp_ref[0] = jnp.sum(onehot_t * gidx, axis=0, keepdims=True).astype(jnp.int32)
    run = run + jnp.sum(onehot_t, axis=-1, keepdims=True)
    run_ref[...] = run
    cnt_ref[...] = jnp.broadcast_to(run, cnt_ref.shape)


def _group_rank(gates, tm):
    t = gates.shape[0]
    nt = t // tm
    row = pl.BlockSpec((1, 1, tm), lambda i: (i, 0, 0))
    return pl.pallas_call(
        _group_rank_kernel,
        grid=(nt,),
        in_specs=[pl.BlockSpec((tm, LANE), lambda i: (i, 0))],
        out_specs=[row, row, pl.BlockSpec((8, LANE), lambda i: (0, 0))],
        out_shape=[jax.ShapeDtypeStruct((nt, 1, tm), jnp.int32), jax.ShapeDtypeStruct((nt, 1, tm), jnp.int32),
                   jax.ShapeDtypeStruct((8, LANE), F32)],
        scratch_shapes=[pltpu.VMEM((8, 1), F32)],
        compiler_params=_params("arbitrary"),
        name="moe_group_rank",
    )(gates)


def _row_copy(src, src_row, dst, dst_row, sem):
    return pltpu.make_async_copy(src.at[pl.ds(src_row, 1)], dst.at[pl.ds(dst_row, 1)], sem)


def _dispatch_kernel(dest_ref, x_ref, g_ref, buf_ref, o_ref, stage_ref, sem):
    del buf_ref
    tm, d = x_ref.shape
    stage_ref[:, :d] = x_ref[...]
    stage_ref[:, d:] = g_ref[...]

    def start(r, _):
        _row_copy(stage_ref, r, o_ref, dest_ref[0, 0, r], sem).start()
        return 0

    def wait(r, _):
        _row_copy(stage_ref, 0, o_ref, 0, sem).wait()
        return 0

    lax.fori_loop(0, tm, start, 0, unroll=8)
    lax.fori_loop(0, tm, wait, 0, unroll=8)


def _dispatch(dest, x, gates, buf, tm):
    t, d = x.shape
    return pl.pallas_call(
        _dispatch_kernel,
        grid=(t // tm,),
        in_specs=[pl.BlockSpec((1, 1, tm), lambda i: (i, 0, 0), memory_space=pltpu.SMEM),
                  pl.BlockSpec((tm, d), lambda i: (i, 0)),
                  pl.BlockSpec((tm, LANE), lambda i: (i, 0)),
                  pl.BlockSpec(memory_space=pl.ANY)],
        out_specs=pl.BlockSpec(memory_space=pl.ANY),
        out_shape=jax.ShapeDtypeStruct(buf.shape, F32),
        scratch_shapes=[pltpu.VMEM((tm, d + LANE), F32), pltpu.SemaphoreType.DMA(())],
        input_output_aliases={3: 0},
        compiler_params=_params("arbitrary"),
        name="moe_dispatch",
    )(dest, x, gates, buf)


def _moe_group_kernel(tg_ref, na_ref, xs_ref, nw_ref, wg_ref, wu_ref, wd_ref, nf_ref, o_ref, *, final_norm):
    i = pl.program_id(0)
    d = o_ref.shape[1]

    @pl.when(i < na_ref[0])
    def _():
        x = xs_ref[:, :d]
        gates = xs_ref[:, d:]
        hb = _rms(x, nw_ref[...]).astype(BF16)
        lane = lax.broadcasted_iota(jnp.int32, gates.shape, 1)
        first = tg_ref[i] * EXPERTS_PER_GROUP
        acc = jnp.zeros(x.shape, F32)
        for e in range(EXPERTS_PER_GROUP):
            gcol = jnp.sum(jnp.where(lane == first + e, gates, 0.0), axis=-1, keepdims=True)
            act = _silu(_dot(hb, wg_ref[0, e])) * _dot(hb, wu_ref[0, e]) * gcol
            acc += _dot(act.astype(BF16), wd_ref[0, e])
        y = x + acc
        o_ref[...] = _rms(y, nf_ref[...]) if final_norm else y

    @pl.when(i >= na_ref[0])
    def _():
        o_ref[...] = jnp.zeros(o_ref.shape, F32)


def _moe_group(tile_group, n_active, xs, nw, wg, wu, wd, nf, tm, final_norm):
    tp, dx = xs.shape
    d = dx - LANE
    wspec = lambda a: pl.BlockSpec((1,) + a.shape[1:], lambda i, tg, na: (tg[i], 0, 0, 0))
    vec = lambda a: pl.BlockSpec(a.shape, lambda i, tg, na: (0, 0))
    grid_spec = pltpu.PrefetchScalarGridSpec(
        num_scalar_prefetch=2,
        grid=(tp // tm,),
        in_specs=[pl.BlockSpec((tm, dx), lambda i, tg, na: (i, 0)), vec(nw), wspec(wg), wspec(wu), wspec(wd), vec(nf)],
        out_specs=pl.BlockSpec((tm, d), lambda i, tg, na: (i, 0)),
    )
    return pl.pallas_call(
        functools.partial(_moe_group_kernel, final_norm=final_norm),
        grid_spec=grid_spec,
        out_shape=jax.ShapeDtypeStruct((tp, d), F32),
        compiler_params=_params("arbitrary"),
        name="moe_group_experts",
    )(tile_group, n_active, xs, nw, wg, wu, wd, nf)


def _combine_kernel(dest_ref, ys_ref, o_ref, sem):
    tm = o_ref.shape[0]

    def start(r, _):
        _row_copy(ys_ref, dest_ref[0, 0, r], o_ref, r, sem).start()
        return 0

    def wait(r, _):
        _row_copy(ys_ref, 0, o_ref, 0, sem).wait()
        return 0

    lax.fori_loop(0, tm, start, 0, unroll=8)
    lax.fori_loop(0, tm, wait, 0, unroll=8)


def _combine(dest, ys, t, tm):
    d = ys.shape[1]
    return pl.pallas_call(
        _combine_kernel,
        grid=(t // tm,),
        in_specs=[pl.BlockSpec((1, 1, tm), lambda i: (i, 0, 0), memory_space=pltpu.SMEM),
                  pl.BlockSpec(memory_space=pl.ANY)],
        out_specs=pl.BlockSpec((tm, d), lambda i: (i, 0)),
        out_shape=jax.ShapeDtypeStruct((t, d), F32),
        scratch_shapes=[pltpu.SemaphoreType.DMA(())],
        compiler_params=_params("arbitrary"),
        name="moe_combine",
    )(dest, ys)


def _moe_sorted(x, nw, gates, wg, wu, wd, nf, final_norm):
    t, d = x.shape
    tm = 512
    nt = t // tm
    grp, rank, cnt = _group_rank(gates, tm)
    counts = cnt[:N_GROUPS, 0].astype(jnp.int32)
    tiles = (counts + tm - 1) // tm
    tile_end = jnp.cumsum(tiles)
    offsets = (tile_end - tiles) * tm
    dest = jnp.take(offsets, grp) + rank
    n_tiles = nt + N_GROUPS
    tile_group = jnp.minimum(jnp.searchsorted(tile_end, jnp.arange(n_tiles), side="right"), N_GROUPS - 1)
    xs = _dispatch(dest, x, gates, jnp.zeros((n_tiles * tm, d + LANE), F32), tm)
    split = lambda w: w.reshape((N_GROUPS, EXPERTS_PER_GROUP) + w.shape[1:])
    ys = _moe_group(tile_group.astype(jnp.int32), tile_end[-1:].astype(jnp.int32), xs, nw,
                    split(wg), split(wu), split(wd), nf, tm, final_norm)
    return _combine(dest, ys, t, tm)


def _odd_kernel(x_ref, nw_ref, win_ref, lnw_ref, lnb_ref, wm_ref, bs_ref, wout_ref, *refs, emit_v):
    if emit_v:
        o_ref, v_ref, gated_ref = refs
    else:
        o_ref, gated_ref = refs
    tm = x_ref.shape[0]
    dc = lnw_ref.shape[1]
    gd = dc // H_C
    x = x_ref[...]
    hb = _rms(x, nw_ref[...]).astype(BF16)
    z = _dot(hb, win_ref[...])
    z = 0.5 * z * (1.0 + lax.erf(z * (2.0 ** -0.5)))
    v = z[:, dc:]
    mu = jnp.mean(v, axis=-1, keepdims=True)
    vc = v - mu
    vn = vc * lax.rsqrt(jnp.mean(vc * vc, axis=-1, keepdims=True) + EPS) * lnw_ref[...] + lnb_ref[...]
    if emit_v:
        v_ref[...] = vn
    vb = vn.astype(BF16)
    for ci in range(tm // CHUNK_C):
        rs = slice(ci * CHUNK_C, (ci + 1) * CHUNK_C)
        for g in range(H_C):
            cs = slice(g * gd, (g + 1) * gd)
            mixed = _dot(wm_ref[g], vb[rs, cs]) + jnp.concatenate([bs_ref[g]] * (gd // LANE), axis=1)
            gated_ref[rs, cs] = (z[rs, cs] * mixed).astype(BF16)
    o_ref[...] = x + _dot(gated_ref[...], wout_ref[...])


def _odd_mixer(x, nw, w_in, ln_w, ln_b, wm, bs, w_out, emit_v):
    t, d = x.shape
    dc = ln_w.shape[1]
    tm = _row_tile(t, 512)
    row = lambda w: pl.BlockSpec((tm, w), lambda i: (i, 0))
    full = lambda a: pl.BlockSpec(a.shape, lambda i: (0,) * a.ndim)
    out_specs = [row(d)] + ([row(dc)] if emit_v else [])
    out_shape = [jax.ShapeDtypeStruct((t, d), F32)] + ([jax.ShapeDtypeStruct((t, dc), F32)] if emit_v else [])
    return pl.pallas_call(
        functools.partial(_odd_kernel, emit_v=emit_v),
        grid=(t // tm,),
        in_specs=[row(d), full(nw), full(w_in), full(ln_w), full(ln_b), full(wm), full(bs), full(w_out)],
        out_specs=out_specs,
        out_shape=out_shape,
        scratch_shapes=[pltpu.VMEM((tm, dc), BF16)],
        compiler_params=_params("parallel"),
        name="odd_mixer",
    )(x, nw, w_in, ln_w, ln_b, wm, bs, w_out)


def _prep_even_weights(w_in, f_bias, a_log, dt_bias):
    d = w_in.shape[0]
    small = jnp.concatenate(
        [w_in[:, 1536:1544], w_in[:, 3080:3088], jnp.zeros((d, LANE - 16), w_in.dtype)], axis=1)
    w = jnp.concatenate([w_in[:, 0:1536], w_in[:, 1544:3080], w_in[:, 3088:3600], small], axis=1)
    bias = jnp.zeros((1, LANE), F32).at[0, 0:8].set(f_bias).at[0, 8:12].set(dt_bias)
    alog = jnp.zeros((1, LANE), F32).at[0, 8:12].set(a_log)
    return w.astype(BF16), bias, alog


def kernel(x_prompt, x_sample, cache_k, cache_v, cache_logf, state_delta, state_conv, page_table, norm_mix, norm_ffn, norm_final, w_in_even, w_out_even, fox_forget_bias, dn_conv_w, dn_a_log, dn_dt_bias, dn_norm_w, w_in_odd, gm_ln_w, gm_ln_b, gm_spatial_w, gm_spatial_b, w_out_odd, moe_router_group, moe_router_expert, moe_w_gate, moe_w_up, moe_w_down):
    n_p, seq, d = x_prompt.shape
    n_s, dec = x_sample.shape[:2]
    page = cache_k.shape[2]
    pool = cache_k.shape[1]
    mix_a = H_A * D_A
    xp = x_prompt.reshape(n_p * seq, d)
    xs = x_sample.reshape(n_s * dec, d)
    row = lambda a: a[None, :]

    w_e, bias_e, alog_e = _prep_even_weights(w_in_even[0], fox_forget_bias[0], dn_a_log[0], dn_dt_bias[0])
    w_out_e = w_out_even[0].astype(BF16)
    conv_w = jnp.zeros((8, dn_conv_w.shape[2]), F32).at[:CONV_W].set(dn_conv_w[0])
    onw = row(dn_norm_w[0])
    nmix0 = row(norm_mix[0])

    def delta(conv3, prefix, small3, gz3, s0, valid, nb):
        wv, wk, qd, kd, qk, gl = _delta_prep(conv3, prefix, conv_w, small3, valid)
        return _delta_scan(wv, wk, qd, kd, qk, gl, s0, gz3, onw, nb)

    q, k, v, c, gz, s = _even_in(xp, nmix0, w_e, bias_e, alog_e)
    logf_p = s[:, :H_A].reshape(n_p, seq, H_A)
    csum = _cumsum_lanes(jnp.swapaxes(logf_p, 1, 2).reshape(n_p * H_A, seq)).reshape(n_p, H_A, seq)
    as3 = lambda a: a.reshape(n_p, seq, a.shape[-1])
    oa_p = _fox_prompt(as3(q), as3(k), as3(v), csum)
    conv_p = as3(c)
    ob_p, sd_p = delta(conv_p, jnp.zeros((n_p, 8, conv_p.shape[-1]), F32), as3(s), as3(gz),
                       jnp.zeros((n_p, H_B, DK_B, DK_B), F32), seq, 4)
    xp = _even_out(xp, oa_p.reshape(n_p * seq, mix_a), ob_p.reshape(n_p * seq, -1), w_out_e)
    k_p = k.reshape(1, n_p, seq, H_A, D_A)
    v_p = v.reshape(1, n_p, seq, H_A, D_A)
    conv_state_p = conv_p[:, seq - (CONV_W - 1):][None]

    q, k, v, c, gz, s = _even_in(xs, nmix0, w_e, bias_e, alog_e)
    as3 = lambda a: a.reshape(n_s, dec, a.shape[-1])
    padrows = lambda a, n: jnp.pad(a, ((0, 0), (0, n - a.shape[1]), (0, 0)))
    logf_s = s[:, :H_A].reshape(n_s, dec, H_A)
    keys_last = lambda a: jnp.pad(jnp.swapaxes(a, 1, 2), ((0, 0), (0, 0), (0, page - dec)))
    oa_s = _fox_sample(page_table, as3(q), keys_last(as3(k)), keys_last(as3(v)), keys_last(logf_s),
                       jnp.transpose(cache_k[0], (0, 2, 3, 1)), jnp.transpose(cache_v[0], (0, 2, 3, 1)),
                       jnp.swapaxes(cache_logf[0], 1, 2))
    conv_s = as3(c)
    lp = DELTA_CHUNK
    prefix_s = jnp.pad(state_conv[0], ((0, 0), (8 - (CONV_W - 1), 0), (0, 0)))
    ob_s, sd_s = delta(padrows(conv_s, lp), prefix_s, padrows(as3(s), lp), padrows(as3(gz), lp),
                       state_delta[0], dec, 8)
    xs = _even_out(xs, oa_s.reshape(n_s * dec, mix_a).astype(BF16), ob_s[:, :dec].reshape(n_s * dec, -1), w_out_e)
    k_s = k.reshape(1, n_s, dec, H_A, D_A)
    v_s = v.reshape(1, n_s, dec, H_A, D_A)
    conv_state_s = jnp.concatenate([state_conv[0], conv_s], axis=1)[:, dec:][None]

    def moe(x, layer, final):
        wr = jnp.concatenate([moe_router_expert[layer], moe_router_group[layer],
                              jnp.zeros((d, LANE - N_EXPERTS - N_GROUPS), F32)], axis=1)
        gates = _router(x, row(norm_ffn[layer]), wr)
        experts = _moe_sorted if x.shape[0] >= 4096 else _moe_dense
        return experts(x, row(norm_ffn[layer]), gates, wg[layer], wu[layer], wd[layer], row(norm_final), final)

    wg, wu, wd = moe_w_gate.astype(BF16), moe_w_up.astype(BF16), moe_w_down.astype(BF16)
    xp = moe(xp, 0, False)
    xs = moe(xs, 0, False)

    idx = jnp.arange(CHUNK_C)
    w_m = jnp.where(idx[:, None] >= idx[None, :], gm_spatial_w[0], 0.0)
    b_s = gm_spatial_b[0]
    lanes = lambda b: jnp.broadcast_to(b[:, :, None], b.shape + (LANE,))
    reps = CHUNK_C // dec
    assert n_s * dec == CHUNK_C, "sample group must fill exactly one 128-row mixing tile"
    w_m_s = (jnp.eye(reps, dtype=F32)[None, :, None, :, None] * w_m[:, None, :dec, None, :dec]).reshape(H_C, CHUNK_C, CHUNK_C)
    b_s_s = jnp.tile(b_s[:, :dec], (1, reps))
    odd_w = (row(norm_mix[1]), w_in_odd[0].astype(BF16), row(gm_ln_w[0]), row(gm_ln_b[0]))
    w_out_o = w_out_odd[0].astype(BF16)
    (xp,) = _odd_mixer(xp, *odd_w, w_m.astype(BF16), lanes(b_s), w_out_o, False)
    xs, v_rows = _odd_mixer(xs, *odd_w, w_m_s.astype(BF16), lanes(b_s_s), w_out_o, True)

    y_p = moe(xp, 1, True).reshape(n_p, seq, d)
    y_s = moe(xs, 1, True).reshape(n_s, dec, d)
    return (y_p, y_s, k_p, v_p, logf_p[None], sd_p[None], conv_state_p,
            k_s, v_s, logf_s[None], sd_s[None], conv_state_s, v_rows.reshape(1, n_s, dec, -1))
```

```python
import functools

import jax
import jax.numpy as jnp
from jax import lax
from jax.experimental import pallas as pl
from jax.experimental.pallas import tpu as pltpu

F32 = jnp.float32
BF16 = jnp.bfloat16
EPS = 1e-6
LANE = 128
VMEM_LIMIT = 56 * 1024 * 1024
HIGHEST = lax.Precision.HIGHEST

H_A, D_A = 8, 64
H_B, DK_B = 4, 128
CONV_W = 4
DELTA_CHUNK = 64
N_GROUPS, EXPERTS_PER_GROUP = 4, 8
N_EXPERTS = N_GROUPS * EXPERTS_PER_GROUP
CHUNK_C = 128
H_C = 8


def _params(*sem):
    return pltpu.CompilerParams(dimension_semantics=sem, vmem_limit_bytes=VMEM_LIMIT)


def _row_tile(t, pref):
    return pref if t % pref == 0 else t


def _rms(x, w):
    return x * lax.rsqrt(jnp.mean(x * x, axis=-1, keepdims=True) + EPS) * w


def _softplus_tail(z):
    return jnp.log1p(jnp.exp(-jnp.abs(z)))


def _even_in_kernel(x_ref, nw_ref, w_ref, wkv_ref, bias_ref, alog_ref,
                    q_ref, k_ref, v_ref, c_ref, gz_ref, s_ref):
    hb = _rms(x_ref[...], nw_ref[...]).astype(BF16)

    def proj(lo, hi):
        return jnp.dot(hb, w_ref[:, lo:hi], preferred_element_type=F32)

    q_ref[...] = proj(0, 512)
    c_ref[...] = proj(512, 2048)
    gz_ref[...] = proj(2048, 2560)
    k_ref[0] = lax.dot_general(wkv_ref[:512, :], hb, (((1,), (1,)), ((), ())), preferred_element_type=F32)
    v_ref[0] = lax.dot_general(wkv_ref[512:, :], hb, (((1,), (1,)), ((), ())), preferred_element_type=F32)
    z = proj(2560, 2688) + bias_ref[...]
    tail = _softplus_tail(z)
    logf = jnp.minimum(z, 0.0) - tail
    g = -jnp.exp(alog_ref[...]) * (jnp.maximum(z, 0.0) + tail)
    beta = 1.0 / (1.0 + jnp.exp(-z))
    lane = lax.broadcasted_iota(jnp.int32, z.shape, 1)
    s_ref[...] = jnp.where(lane < 8, logf, jnp.where(lane < 12, g, jnp.where(lane < 16, beta, 0.0)))


def _even_in(x, nw, w, wkv, bias, alog, n, length):
    t, d = x.shape
    tm = _row_tile(length, 512)
    per_seq = length // tm
    mix = wkv.shape[0] // 2
    row = lambda wd: pl.BlockSpec((tm, wd), lambda i: (i, 0))
    full = lambda a: pl.BlockSpec(a.shape, lambda i: (0, 0))
    kt = pl.BlockSpec((1, mix, tm), lambda i: (i // per_seq, 0, i % per_seq))
    rows = lambda wd: jax.ShapeDtypeStruct((t, wd), F32)
    kts = jax.ShapeDtypeStruct((n, mix, length), F32)
    return pl.pallas_call(
        _even_in_kernel,
        grid=(t // tm,),
        in_specs=[row(d), full(nw), full(w), full(wkv), full(bias), full(alog)],
        out_specs=[row(mix), kt, kt, row(3 * mix), row(mix), row(LANE)],
        out_shape=[rows(mix), kts, kts, rows(3 * mix), rows(mix), rows(LANE)],
        compiler_params=_params("parallel"),
        name="even_in_proj",
    )(x, nw, w, wkv, bias, alog)


def _cumsum_kernel(x_ref, o_ref):
    rows, length = x_ref.shape
    r = lax.broadcasted_iota(jnp.int32, (LANE, LANE), 0)
    c = lax.broadcasted_iota(jnp.int32, (LANE, LANE), 1)
    upper = (r <= c).astype(F32)
    carry = jnp.zeros((rows, 1), F32)
    for b in range(length // LANE):
        blk = jnp.dot(x_ref[:, b * LANE:(b + 1) * LANE], upper,
                      preferred_element_type=F32, precision=HIGHEST) + carry
        o_ref[:, b * LANE:(b + 1) * LANE] = blk
        carry = blk[:, LANE - 1:LANE]


def _cumsum_lanes(x):
    rows, length = x.shape
    tr = 8
    return pl.pallas_call(
        _cumsum_kernel,
        grid=(rows // tr,),
        in_specs=[pl.BlockSpec((tr, length), lambda i: (i, 0))],
        out_specs=pl.BlockSpec((tr, length), lambda i: (i, 0)),
        out_shape=jax.ShapeDtypeStruct((rows, length), F32),
        compiler_params=_params("parallel"),
        name="logf_cumsum",
    )(x)


def _fox_prompt_kernel(q_ref, k_ref, v_ref, c_ref, o_ref, *, tq):
    qi = pl.program_id(2)
    hp = pl.program_id(1)
    scale = D_A ** -0.5
    lane = lax.broadcasted_iota(jnp.int32, (tq, LANE), 1)
    first = lane < D_A
    q = q_ref[0] * scale
    qs = [jnp.where(first, q, 0.0).astype(BF16), jnp.where(first, 0.0, q).astype(BF16)]
    q0 = pl.multiple_of(qi * tq, tq)

    def crow(h, start):
        return c_ref[0, pl.ds(2 * hp + h, 1), pl.ds(start, tq)]

    cq = [crow(h, q0)[:, 0:1] for h in range(2)]

    def block(j0, carry, mask):
        kb = k_ref[0, :, pl.ds(j0, tq)].astype(BF16)
        vb = v_ref[0, :, pl.ds(j0, tq)].astype(BF16)
        out = []
        for h in range(2):
            m, l, acc = carry[h]
            s = _dot(qs[h], kb) + (cq[h] - crow(h, j0))
            if mask is not None:
                s = jnp.where(mask, s, -jnp.inf)
            m_new = jnp.maximum(m, jnp.max(s, axis=-1, keepdims=True))
            alpha = jnp.exp(m - m_new)
            p = jnp.exp(s - m_new)
            l = alpha * l + jnp.sum(p, axis=-1, keepdims=True)
            acc = alpha * acc + _dot_nt(p.astype(BF16), vb)
            out.append((m_new, l, acc))
        return tuple(out)

    def body(j, carry):
        return block(pl.multiple_of(j * tq, tq), carry, None)

    init = (jnp.full((tq, 1), -jnp.inf, F32), jnp.zeros((tq, 1), F32), jnp.zeros((tq, LANE), F32))
    carry = lax.fori_loop(0, qi, body, (init, init))
    r = lax.broadcasted_iota(jnp.int32, (tq, tq), 0)
    cidx = lax.broadcasted_iota(jnp.int32, (tq, tq), 1)
    (_, l0, a0), (_, l1, a1) = block(q0, carry, cidx <= r)
    o_ref[0] = jnp.where(first, a0 / l0, a1 / l1).astype(o_ref.dtype)


def _fox_prompt(q, k, v, c):
    n, length, width = q.shape
    tq = 512 if length % 512 == 0 else length
    grid = (n, width // LANE, length // tq)
    return pl.pallas_call(
        functools.partial(_fox_prompt_kernel, tq=tq),
        grid=grid,
        in_specs=[
            pl.BlockSpec((1, tq, LANE), lambda b, h, i: (b, i, h)),
            pl.BlockSpec((1, LANE, length), lambda b, h, i: (b, h, 0)),
            pl.BlockSpec((1, LANE, length), lambda b, h, i: (b, h, 0)),
            pl.BlockSpec((1, H_A, length), lambda b, h, i: (b, 0, 0)),
        ],
        out_specs=pl.BlockSpec((1, tq, LANE), lambda b, h, i: (b, i, h)),
        out_shape=jax.ShapeDtypeStruct((n, length, width), BF16),
        compiler_params=_params("parallel", "parallel", "arbitrary"),
        name="fox_prompt",
    )(q, k, v, c)


def _even_out_kernel(x_ref, a_ref, b_ref, w_ref, o_ref):
    half = a_ref.shape[1]
    acc = jnp.dot(a_ref[...], w_ref[:half, :], preferred_element_type=F32)
    acc += jnp.dot(b_ref[...], w_ref[half:, :], preferred_element_type=F32)
    o_ref[...] = x_ref[...] + acc


def _even_out(x, oa, ob, w):
    t, d = x.shape
    tm = _row_tile(t, 512)
    row = lambda a: pl.BlockSpec((tm, a.shape[1]), lambda i: (i, 0))
    return pl.pallas_call(
        _even_out_kernel,
        grid=(t // tm,),
        in_specs=[row(x), row(oa), row(ob), pl.BlockSpec(w.shape, lambda i: (0, 0))],
        out_specs=row(x),
        out_shape=jax.ShapeDtypeStruct((t, d), F32),
        compiler_params=_params("parallel"),
        name="even_out_proj",
    )(x, oa, ob, w)


def _split3(x):
    hi = x.astype(BF16)
    r1 = x - hi.astype(F32)
    mid = r1.astype(BF16)
    lo = (r1 - mid.astype(F32)).astype(BF16)
    return hi, mid, lo


def _dot(a, b):
    return jnp.dot(a, b, preferred_element_type=F32)


def _dot_nt(a, b):
    return lax.dot_general(a, b, (((1,), (1,)), ((), ())), preferred_element_type=F32)


def _dot_tn(a, b):
    return lax.dot_general(a, b, (((0,), (0,)), ((), ())), preferred_element_type=F32)


def _silu(x):
    return x / (1.0 + jnp.exp(-x))


def _delta_prep_kernel(x_ref, xprev_ref, pre_ref, cw_ref, sm_ref,
                       wv_ref, wk_ref, qd_ref, kd_ref, qk_ref, gl_ref, *, ta, valid):
    i = pl.program_id(1)
    c = DELTA_CHUNK
    dk = DK_B
    rr = H_B * c
    x = x_ref[0]
    prev = jnp.where(i == 0, pre_ref[0], xprev_ref[0])
    xcat = jnp.concatenate([prev, x], axis=0)
    conv = x * cw_ref[CONV_W - 1:CONV_W, :]
    for s in range(1, CONV_W):
        conv += pltpu.roll(xcat, s, axis=0)[8:] * cw_ref[CONV_W - 1 - s:CONV_W - s, :]
    act = _silu(conv)
    sm = sm_ref[0]

    ri = lax.broadcasted_iota(jnp.int32, (rr, rr), 0)
    ci = lax.broadcasted_iota(jnp.int32, (rr, rr), 1)
    same_head = (ri // c) == (ci // c)
    incl = same_head & (ri >= ci)
    strict = same_head & (ri > ci)
    tri = incl.astype(BF16)
    lane = lax.broadcasted_iota(jnp.int32, (c, LANE), 1)

    qn, kn, va = [], [], []
    for h in range(H_B):
        qa = act[:, h * dk:(h + 1) * dk]
        ka = act[:, (H_B + h) * dk:(H_B + h + 1) * dk]
        qn.append(qa * lax.rsqrt(jnp.sum(qa * qa, axis=-1, keepdims=True) + EPS) * (dk ** -0.5))
        kn.append(ka * lax.rsqrt(jnp.sum(ka * ka, axis=-1, keepdims=True) + EPS))
        va.append(act[:, (2 * H_B + h) * dk:(2 * H_B + h + 1) * dk])

    chunks = range(ta // c)
    stack = lambda parts: jnp.concatenate(parts, axis=0)
    q, k, v, b_col, gcb = [], [], [], [], []
    for s in chunks:
        r0 = s * c
        live = lax.broadcasted_iota(jnp.int32, (c, 1), 0) + (i * ta + r0) < valid
        smc = sm[r0:r0 + c]
        pick = lambda ln: jnp.where(live, jnp.sum(jnp.where(lane == ln, smc, 0.0), axis=-1, keepdims=True), 0.0)
        q.append(stack([a[r0:r0 + c] for a in qn]))
        k.append(stack([a[r0:r0 + c] for a in kn]))
        v.append(stack([a[r0:r0 + c] for a in va]))
        b_col.append(stack([pick(12 + h) for h in range(H_B)]))
        g_stack = stack([jnp.broadcast_to(pick(8 + h), (c, LANE)) for h in range(H_B)])
        gcb.append(sum(_dot(tri, p) for p in _split3(g_stack)))

    wide = [jnp.concatenate([g, g], axis=1) for g in gcb]
    decay = [jnp.exp(jnp.where(incl, w - w.T, -jnp.inf)) for w in wide]
    eg = [jnp.exp(g) for g in gcb]
    kb = [a * b for a, b in zip(k, b_col)]
    kbf = [a.astype(BF16) for a in k]
    a_mat = [jnp.where(strict, _dot_nt(a.astype(BF16), b) * d, 0.0) for a, b, d in zip(kb, kbf, decay)]
    qk = [jnp.where(incl, _dot_nt(a.astype(BF16), b) * d, 0.0) for a, b, d in zip(q, kbf, decay)]
    y = [-a for a in a_mat]
    p = list(y)
    for _ in range(5):
        yb = [a.astype(BF16) for a in y]
        y = [_dot(a, a) for a in yb]
        p = [a + b + _dot(a.astype(BF16), b.astype(BF16)) for a, b in zip(p, y)]
    rhs = [jnp.concatenate([a * b, kbb * e], axis=1) for a, b, kbb, e in zip(v, b_col, kb, eg)]
    w = [r + _dot(a.astype(BF16), r.astype(BF16)) for a, r in zip(p, rhs)]

    for s in chunks:
        r0 = s * c
        for h in range(H_B):
            hr = slice(h * c, (h + 1) * c)
            hs = slice(h * dk, (h + 1) * dk)
            g_last = gcb[s][h * c + c - 1:(h + 1) * c, :]
            wv_ref[0, r0:r0 + c, hs] = w[s][hr, :dk]
            wk_ref[0, r0:r0 + c, hs] = w[s][hr, dk:].astype(BF16)
            qd_ref[0, r0:r0 + c, hs] = (q[s][hr] * eg[s][hr]).astype(BF16)
            kd_ref[0, r0:r0 + c, hs] = (k[s][hr] * jnp.exp(g_last - gcb[s][hr])).astype(BF16)
            qk_ref[0, h, r0:r0 + c, :] = qk[s][hr, h * c:(h + 1) * c].astype(BF16)
            gl_ref[0, s, h:h + 1, :] = jnp.exp(g_last)


def _delta_prep(conv_in, prefix, conv_w, small, valid):
    n, lp, cd = conv_in.shape
    ta = LANE if lp % LANE == 0 else lp
    nt = lp // ta
    tb = ta // 8
    f = lambda dt, w: jax.ShapeDtypeStruct((n, lp, w), dt)
    blk = lambda w: pl.BlockSpec((1, ta, w), lambda b, i: (b, i, 0))
    width = H_B * DK_B
    return pl.pallas_call(
        functools.partial(_delta_prep_kernel, ta=ta, valid=valid),
        grid=(n, nt),
        in_specs=[
            blk(cd),
            pl.BlockSpec((1, 8, cd), lambda b, i: (b, jnp.maximum(i * tb - 1, 0), 0)),
            pl.BlockSpec((1, 8, cd), lambda b, i: (b, 0, 0)),
            pl.BlockSpec((8, cd), lambda b, i: (0, 0)),
            blk(LANE),
        ],
        out_specs=[
            blk(width), blk(width), blk(width), blk(width),
            pl.BlockSpec((1, H_B, ta, DELTA_CHUNK), lambda b, i: (b, 0, i, 0)),
            pl.BlockSpec((1, ta // DELTA_CHUNK, H_B, LANE), lambda b, i: (b, i, 0, 0)),
        ],
        out_shape=[
            f(F32, width), f(BF16, width), f(BF16, width), f(BF16, width),
            jax.ShapeDtypeStruct((n, H_B, lp, DELTA_CHUNK), BF16),
            jax.ShapeDtypeStruct((n, lp // DELTA_CHUNK, H_B, LANE), F32),
        ],
        compiler_params=_params("parallel", "parallel"),
        name="delta_prep",
    )(conv_in, conv_in, prefix, conv_w, small)


def _delta_scan_kernel(wv_ref, wk_ref, qd_ref, kd_ref, qk_ref, gl_ref, s0_ref, gz_ref, onw_ref,
                       o_ref, s_ref, *, nb, n_chunks):
    c = DELTA_CHUNK
    dk = DK_B

    @pl.when(pl.program_id(1) == 0)
    def _():
        s_ref[...] = s0_ref[...]

    onw = onw_ref[...]

    def body(ci, _):
        r0 = pl.multiple_of(ci * c, c)
        for b in range(nb):
            for h in range(H_B):
                hs = slice(h * dk, (h + 1) * dk)
                state = s_ref[b, h]
                sb = state.astype(BF16)
                lhs = jnp.concatenate([wk_ref[b, pl.ds(r0, c), hs], qd_ref[b, pl.ds(r0, c), hs]], axis=0)
                prod = _dot(lhs, sb)
                v_new = wv_ref[b, pl.ds(r0, c), hs] - prod[:c]
                vb = v_new.astype(BF16)
                o = prod[c:] + _dot(qk_ref[b, h, pl.ds(r0, c), :], vb)
                gl = gl_ref[b, pl.ds(ci, 1), h, :]
                s_ref[b, h] = state * gl + _dot_tn(kd_ref[b, pl.ds(r0, c), hs], vb)
                on = o * lax.rsqrt(jnp.mean(o * o, axis=-1, keepdims=True) + EPS) * onw
                o_ref[b, pl.ds(r0, c), hs] = (on * _silu(gz_ref[b, pl.ds(r0, c), hs])).astype(o_ref.dtype)
        return 0

    lax.fori_loop(0, n_chunks, body, 0)


def _delta_scan(wv, wk, qd, kd, qk, gl, s0, gz, onw, nb):
    n, lp, width = wv.shape
    tl = _row_tile(lp, 512)
    n_chunks = tl // DELTA_CHUNK
    seq = pl.BlockSpec((nb, tl, width), lambda i, t: (i, t, 0))
    state = pl.BlockSpec((nb,) + s0.shape[1:], lambda i, t: (i, 0, 0, 0))
    return pl.pallas_call(
        functools.partial(_delta_scan_kernel, nb=nb, n_chunks=n_chunks),
        grid=(n // nb, lp // tl),
        in_specs=[seq, seq, seq, seq,
                  pl.BlockSpec((nb, H_B, tl, DELTA_CHUNK), lambda i, t: (i, 0, t, 0)),
                  pl.BlockSpec((nb, n_chunks, H_B, LANE), lambda i, t: (i, t, 0, 0)),
                  state, seq, pl.BlockSpec(onw.shape, lambda i, t: (0, 0))],
        out_specs=[seq, state],
        out_shape=[jax.ShapeDtypeStruct((n, lp, width), BF16), jax.ShapeDtypeStruct(s0.shape, F32)],
        compiler_params=_params("parallel", "arbitrary"),
        name="delta_scan",
    )(wv, wk, qd, kd, qk, gl, s0, gz, onw)


def _fox_sample_kernel(pt_ref, q_ref, kn_ref, vn_ref, lfn_ref, *refs, pages_per_step, n_q):
    del pt_ref
    pps = pages_per_step
    kc_refs, vc_refs, lfc_refs = refs[:pps], refs[pps:2 * pps], refs[2 * pps:3 * pps]
    o_ref, qbd_ref, m_ref, l_ref, acc_ref, carry_ref = refs[3 * pps:]
    j = pl.program_id(1)
    rows = n_q * H_A
    page = lfn_ref.shape[2]
    width = H_A * D_A
    ri = lax.broadcasted_iota(jnp.int32, (page, page), 0)
    ci = lax.broadcasted_iota(jnp.int32, (page, page), 1)

    def attend(blocks):
        qbd = qbd_ref[...]
        ss = []
        for kt, _, bias8, mask in blocks:
            s = _dot(qbd, kt.astype(BF16)) + jnp.concatenate([bias8] * n_q, axis=0)
            ss.append(s if mask is None else jnp.where(mask, s, -jnp.inf))
        m_old = m_ref[...]
        m_new = m_old
        for s in ss:
            m_new = jnp.maximum(m_new, jnp.max(s, axis=-1, keepdims=True))
        alpha = jnp.exp(m_old - m_new)
        l_new = alpha * l_ref[...]
        acc = alpha * acc_ref[...]
        for s, (_, vt, _, _) in zip(ss, blocks):
            p = jnp.exp(s - m_new)
            l_new += jnp.sum(p, axis=-1, keepdims=True)
            acc += _dot_nt(p.astype(BF16), vt.astype(BF16))
        l_ref[...] = l_new
        acc_ref[...] = acc
        m_ref[...] = m_new

    @pl.when(j == 0)
    def _():
        q = q_ref[0] * (D_A ** -0.5)
        qrep = jnp.concatenate([jnp.broadcast_to(q[t:t + 1], (H_A, width)) for t in range(n_q)], axis=0)
        r = lax.broadcasted_iota(jnp.int32, qrep.shape, 0)
        ln = lax.broadcasted_iota(jnp.int32, qrep.shape, 1)
        qbd_ref[...] = jnp.where(ln // D_A == r % H_A, qrep, 0.0).astype(BF16)
        m_ref[...] = jnp.full(m_ref.shape, -jnp.inf, F32)
        l_ref[...] = jnp.zeros(l_ref.shape, F32)
        acc_ref[...] = jnp.zeros(acc_ref.shape, F32)
        carry_ref[...] = jnp.zeros(carry_ref.shape, F32)
        upper = (ri <= ci).astype(BF16)
        csum = sum(_dot(p, upper) for p in _split3(lfn_ref[0]))
        r2 = lax.broadcasted_iota(jnp.int32, (rows, page), 0)
        c2 = lax.broadcasted_iota(jnp.int32, (rows, page), 1)
        attend([(kn_ref[0], vn_ref[0], -csum, c2 <= r2 // H_A)])

    @pl.when(j > 0)
    def _():
        after = (ri > ci).astype(BF16)
        carry = carry_ref[...]
        blocks = []
        for kc_ref, vc_ref, lfc_ref in zip(kc_refs, vc_refs, lfc_refs):
            lf = lfc_ref[0]
            suffix = sum(_dot(p, after) for p in _split3(lf))
            blocks.append((kc_ref[0].reshape(width, page), vc_ref[0].reshape(width, page), carry + suffix, None))
            carry = carry + jnp.sum(lf, axis=-1, keepdims=True)
        attend(blocks)
        carry_ref[...] = carry

    @pl.when(j == pl.num_programs(1) - 1)
    def _():
        out = acc_ref[...] / l_ref[...]
        r = lax.broadcasted_iota(jnp.int32, out.shape, 0)
        ln = lax.broadcasted_iota(jnp.int32, out.shape, 1)
        out = jnp.where(ln // D_A == r % H_A, out, 0.0)
        o_ref[0] = jnp.sum(out.reshape(n_q, H_A, width), axis=1)


def _fox_sample(page_table, q, kn, vn, lfn, cache_k, cache_v, cache_lf):
    b, n_q, width = q.shape
    n_pages = page_table.shape[1]
    page = cache_k.shape[3]
    rows = n_q * H_A
    pps = next(p for p in (16, 8, 1) if n_pages % p == 0)
    cur = lambda i, j, pt: (i, 0, 0)

    def past(p, nd):
        def index_map(i, j, pt):
            return (pt[i, n_pages - 1 - ((jnp.maximum(j, 1) - 1) * pps + p)],) + (0,) * nd
        return index_map

    kv_specs = [pl.BlockSpec((1, H_A, D_A, page), past(p, 3)) for p in range(pps)]
    lf_specs = [pl.BlockSpec((1, H_A, page), past(p, 2)) for p in range(pps)]
    grid_spec = pltpu.PrefetchScalarGridSpec(
        num_scalar_prefetch=1,
        grid=(b, n_pages // pps + 1),
        in_specs=[
            pl.BlockSpec((1, n_q, width), cur),
            pl.BlockSpec((1, width, page), cur),
            pl.BlockSpec((1, width, page), cur),
            pl.BlockSpec((1, H_A, page), cur),
        ] + kv_specs + kv_specs + lf_specs,
        out_specs=pl.BlockSpec((1, n_q, width), cur),
        scratch_shapes=[
            pltpu.VMEM((rows, width), BF16),
            pltpu.VMEM((rows, 1), F32),
            pltpu.VMEM((rows, 1), F32),
            pltpu.VMEM((rows, width), F32),
            pltpu.VMEM((H_A, 1), F32),
        ],
    )
    return pl.pallas_call(
        functools.partial(_fox_sample_kernel, pages_per_step=pps, n_q=n_q),
        grid_spec=grid_spec,
        out_shape=jax.ShapeDtypeStruct((b, n_q, width), F32),
        compiler_params=_params("parallel", "arbitrary"),
        name="fox_sample",
    )(page_table, q, kn, vn, lfn, *([cache_k] * pps), *([cache_v] * pps), *([cache_lf] * pps))


def _router_kernel(x_ref, nw_ref, wr_ref, g_ref):
    h = _rms(x_ref[...], nw_ref[...])
    logits = jnp.dot(h, wr_ref[...], preferred_element_type=F32, precision=HIGHEST)
    lane = lax.broadcasted_iota(jnp.int32, logits.shape, 1)
    lanef = lane.astype(F32)
    neg = -jnp.inf
    is_group = (lane >= N_EXPERTS) & (lane < N_EXPERTS + N_GROUPS)
    gl = jnp.where(is_group, logits, neg)
    gmax = jnp.max(gl, axis=-1, keepdims=True)
    g_sel = jnp.min(jnp.where(gl == gmax, lanef - N_EXPERTS, 1e9), axis=-1, keepdims=True)
    p_sel = 1.0 / jnp.sum(jnp.where(is_group, jnp.exp(logits - gmax), 0.0), axis=-1, keepdims=True)
    in_sel = (lane < N_EXPERTS) & ((lane // EXPERTS_PER_GROUP).astype(F32) == g_sel)
    el = jnp.where(in_sel, logits, neg)
    m1 = jnp.max(el, axis=-1, keepdims=True)
    i1 = jnp.min(jnp.where(el == m1, lanef, 1e9), axis=-1, keepdims=True)
    el2 = jnp.where(lanef == i1, neg, el)
    m2 = jnp.max(el2, axis=-1, keepdims=True)
    i2 = jnp.min(jnp.where(el2 == m2, lanef, 1e9), axis=-1, keepdims=True)
    e2 = jnp.exp(m2 - m1)
    w1 = p_sel / (1.0 + e2)
    g_ref[...] = jnp.where(lanef == i1, w1, jnp.where(lanef == i2, w1 * e2, jnp.where(lane == N_EXPERTS, g_sel, 0.0)))


def _router(x, nw, wr):
    t, d = x.shape
    tm = _row_tile(t, 512)
    return pl.pallas_call(
        _router_kernel,
        grid=(t // tm,),
        in_specs=[pl.BlockSpec((tm, d), lambda i: (i, 0)), pl.BlockSpec(nw.shape, lambda i: (0, 0)),
                  pl.BlockSpec(wr.shape, lambda i: (0, 0))],
        out_specs=pl.BlockSpec((tm, LANE), lambda i: (i, 0)),
        out_shape=jax.ShapeDtypeStruct((t, LANE), F32),
        compiler_params=_params("parallel"),
        name="moe_router",
    )(x, nw, wr)


def _moe_kernel(x_ref, nw_ref, g_ref, wg_ref, wu_ref, wd_ref, nf_ref, o_ref, hb_ref, acc_ref, *, final_norm):
    e = pl.program_id(1)

    @pl.when(e == 0)
    def _():
        hb_ref[...] = _rms(x_ref[...], nw_ref[...]).astype(BF16)
        acc_ref[...] = jnp.zeros(acc_ref.shape, F32)

    gates = g_ref[...]
    lane = lax.broadcasted_iota(jnp.int32, gates.shape, 1)
    gcol = jnp.sum(jnp.where(lane == e, gates, 0.0), axis=-1, keepdims=True)
    hb = hb_ref[...]
    act = _silu(_dot(hb, wg_ref[0, 0])) * _dot(hb, wu_ref[0, 0]) * gcol
    acc_ref[...] += _dot(act.astype(BF16), wd_ref[0, 0])

    @pl.when(e == pl.num_programs(1) - 1)
    def _():
        y = x_ref[...] + acc_ref[...]
        o_ref[...] = _rms(y, nf_ref[...]) if final_norm else y


def _moe_dense(x, nw, gates, wg, wu, wd, nf, layer, final_norm):
    t, d = x.shape
    tm = _row_tile(t, 1024)
    _, n_e, _, f = wg.shape
    return pl.pallas_call(
        functools.partial(_moe_kernel, final_norm=final_norm),
        grid=(t // tm, n_e),
        in_specs=[
            pl.BlockSpec((tm, d), lambda i, e: (i, 0)),
            pl.BlockSpec(nw.shape, lambda i, e: (0, 0)),
            pl.BlockSpec((tm, LANE), lambda i, e: (i, 0)),
            pl.BlockSpec((1, 1, d, f), lambda i, e: (layer, e, 0, 0)),
            pl.BlockSpec((1, 1, d, f), lambda i, e: (layer, e, 0, 0)),
            pl.BlockSpec((1, 1, f, d), lambda i, e: (layer, e, 0, 0)),
            pl.BlockSpec(nf.shape, lambda i, e: (0, 0)),
        ],
        out_specs=pl.BlockSpec((tm, d), lambda i, e: (i, 0)),
        out_shape=jax.ShapeDtypeStruct((t, d), F32),
        scratch_shapes=[pltpu.VMEM((tm, d), BF16), pltpu.VMEM((tm, d), F32)],
        compiler_params=_params("parallel", "arbitrary"),
        name="moe_experts",
    )(x, nw, gates, wg, wu, wd, nf)


def _group_rank_kernel(g_ref, grp_ref, rank_ref, cnt_ref, run_ref):
    i = pl.program_id(0)
    tm = g_ref.shape[0]

    @pl.when(i == 0)
    def _():
        run_ref[...] = jnp.zeros(run_ref.shape, F32)

    gates = g_ref[...]
    lane = lax.broadcasted_iota(jnp.int32, gates.shape, 1)
    g_sel = jnp.sum(jnp.where(lane == N_EXPERTS, gates, 0.0), axis=-1, keepdims=True)
    onehot_t = (lane.astype(F32) == g_sel).astype(F32).T[:8]
    ri = lax.broadcasted_iota(jnp.int32, (tm, tm), 0)
    ci = lax.broadcasted_iota(jnp.int32, (tm, tm), 1)
    earlier = _dot(onehot_t.astype(BF16), (ri < ci).astype(BF16))
    run = run_ref[...]
    gidx = lax.broadcasted_iota(jnp.int32, (8, tm), 0).astype(F32)
    rank_ref[0] = jnp.sum(onehot_t * (earlier + run), axis=0, keepdims=True).astype(jnp.int32)
    grp_ref[0] = jnp.sum(onehot_t * gidx, axis=0, keepdims=True).astype(jnp.int32)
    run = run + jnp.sum(onehot_t, axis=-1, keepdims=True)
    run_ref[...] = run
    cnt_ref[...] = jnp.broadcast_to(run, cnt_ref.shape)


def _group_rank(gates, tm):
    t = gates.shape[0]
    nt = t // tm
    row = pl.BlockSpec((1, 1, tm), lambda i: (i, 0, 0))
    return pl.pallas_call(
        _group_rank_kernel,
        grid=(nt,),
        in_specs=[pl.BlockSpec((tm, LANE), lambda i: (i, 0))],
        out_specs=[row, row, pl.BlockSpec((8, LANE), lambda i: (0, 0))],
        out_shape=[jax.ShapeDtypeStruct((nt, 1, tm), jnp.int32), jax.ShapeDtypeStruct((nt, 1, tm), jnp.int32),
                   jax.ShapeDtypeStruct((8, LANE), F32)],
        scratch_shapes=[pltpu.VMEM((8, 1), F32)],
        compiler_params=_params("arbitrary"),
        name="moe_group_rank",
    )(gates)


def _row_copy(src, src_row, dst, dst_row, sem):
    return pltpu.make_async_copy(src.at[pl.ds(src_row, 1)], dst.at[pl.ds(dst_row, 1)], sem)


def _dispatch_kernel(dest_ref, x_ref, g_ref, buf_ref, o_ref, stage_ref, sem):
    del buf_ref
    tm, d = x_ref.shape
    stage_ref[:, :d] = x_ref[...]
    stage_ref[:, d:] = g_ref[...]

    def start(r, _):
        _row_copy(stage_ref, r, o_ref, dest_ref[0, 0, r], sem).start()
        return 0

    def wait(r, _):
        _row_copy(stage_ref, 0, o_ref, 0, sem).wait()
        return 0

    lax.fori_loop(0, tm, start, 0, unroll=8)
    lax.fori_loop(0, tm, wait, 0, unroll=8)


def _dispatch(dest, x, gates, buf, tm):
    t, d = x.shape
    return pl.pallas_call(
        _dispatch_kernel,
        grid=(t // tm,),
        in_specs=[pl.BlockSpec((1, 1, tm), lambda i: (i, 0, 0), memory_space=pltpu.SMEM),
                  pl.BlockSpec((tm, d), lambda i: (i, 0)),
                  pl.BlockSpec((tm, LANE), lambda i: (i, 0)),
                  pl.BlockSpec(memory_space=pl.ANY)],
        out_specs=pl.BlockSpec(memory_space=pl.ANY),
        out_shape=jax.ShapeDtypeStruct(buf.shape, F32),
        scratch_shapes=[pltpu.VMEM((tm, d + LANE), F32), pltpu.SemaphoreType.DMA(())],
        input_output_aliases={3: 0},
        compiler_params=_params("arbitrary"),
        name="moe_dispatch",
    )(dest, x, gates, buf)


def _moe_group_kernel(tg_ref, na_ref, xs_ref, nw_ref, wg_ref, wu_ref, wd_ref, nf_ref, o_ref, *, final_norm):
    i = pl.program_id(0)
    d = o_ref.shape[1]

    @pl.when(i < na_ref[0])
    def _():
        x = xs_ref[:, :d]
        gates = xs_ref[:, d:]
        hb = _rms(x, nw_ref[...]).astype(BF16)
        lane = lax.broadcasted_iota(jnp.int32, gates.shape, 1)
        first = tg_ref[i] * EXPERTS_PER_GROUP
        acc = jnp.zeros(x.shape, F32)
        for e in range(EXPERTS_PER_GROUP):
            gcol = jnp.sum(jnp.where(lane == first + e, gates, 0.0), axis=-1, keepdims=True)
            act = _silu(_dot(hb, wg_ref[0, 0, e])) * _dot(hb, wu_ref[0, 0, e]) * gcol
            acc += _dot(act.astype(BF16), wd_ref[0, 0, e])
        y = x + acc
        o_ref[...] = _rms(y, nf_ref[...]) if final_norm else y

    @pl.when(i >= na_ref[0])
    def _():
        o_ref[...] = jnp.zeros(o_ref.shape, F32)


def _moe_group(tile_group, n_active, xs, nw, wg, wu, wd, nf, tm, layer, final_norm):
    tp, dx = xs.shape
    d = dx - LANE
    wspec = lambda a: pl.BlockSpec((1, 1) + a.shape[2:], lambda i, tg, na: (layer, tg[i], 0, 0, 0))
    vec = lambda a: pl.BlockSpec(a.shape, lambda i, tg, na: (0, 0))
    grid_spec = pltpu.PrefetchScalarGridSpec(
        num_scalar_prefetch=2,
        grid=(tp // tm,),
        in_specs=[pl.BlockSpec((tm, dx), lambda i, tg, na: (i, 0)), vec(nw), wspec(wg), wspec(wu), wspec(wd), vec(nf)],
        out_specs=pl.BlockSpec((tm, d), lambda i, tg, na: (i, 0)),
    )
    return pl.pallas_call(
        functools.partial(_moe_group_kernel, final_norm=final_norm),
        grid_spec=grid_spec,
        out_shape=jax.ShapeDtypeStruct((tp, d), F32),
        compiler_params=_params("arbitrary"),
        name="moe_group_experts",
    )(tile_group, n_active, xs, nw, wg, wu, wd, nf)


def _combine_kernel(dest_ref, ys_ref, o_ref, sem):
    tm = o_ref.shape[0]

    def start(r, _):
        _row_copy(ys_ref, dest_ref[0, 0, r], o_ref, r, sem).start()
        return 0

    def wait(r, _):
        _row_copy(ys_ref, 0, o_ref, 0, sem).wait()
        return 0

    lax.fori_loop(0, tm, start, 0, unroll=8)
    lax.fori_loop(0, tm, wait, 0, unroll=8)


def _combine(dest, ys, t, tm):
    d = ys.shape[1]
    return pl.pallas_call(
        _combine_kernel,
        grid=(t // tm,),
        in_specs=[pl.BlockSpec((1, 1, tm), lambda i: (i, 0, 0), memory_space=pltpu.SMEM),
                  pl.BlockSpec(memory_space=pl.ANY)],
        out_specs=pl.BlockSpec((tm, d), lambda i: (i, 0)),
        out_shape=jax.ShapeDtypeStruct((t, d), F32),
        scratch_shapes=[pltpu.SemaphoreType.DMA(())],
        compiler_params=_params("arbitrary"),
        name="moe_combine",
    )(dest, ys)


def _moe_sorted(x, nw, gates, wg, wu, wd, nf, layer, final_norm):
    t, d = x.shape
    tm = 512
    nt = t // tm
    grp, rank, cnt = _group_rank(gates, tm)
    counts = cnt[:N_GROUPS, 0].astype(jnp.int32)
    tiles = (counts + tm - 1) // tm
    tile_end = jnp.cumsum(tiles)
    offsets = (tile_end - tiles) * tm
    dest = jnp.take(offsets, grp) + rank
    n_tiles = nt + N_GROUPS
    tile_group = jnp.minimum(jnp.searchsorted(tile_end, jnp.arange(n_tiles), side="right"), N_GROUPS - 1)
    xs = _dispatch(dest, x, gates, jnp.zeros((n_tiles * tm, d + LANE), F32), tm)
    split = lambda w: w.reshape((w.shape[0], N_GROUPS, EXPERTS_PER_GROUP) + w.shape[2:])
    ys = _moe_group(tile_group.astype(jnp.int32), tile_end[-1:].astype(jnp.int32), xs, nw,
                    split(wg), split(wu), split(wd), nf, tm, layer, final_norm)
    return _combine(dest, ys, t, tm)


def _odd_kernel(x_ref, nw_ref, win_ref, lnw_ref, lnb_ref, wm_ref, bs_ref, wout_ref, *refs, emit_v):
    if emit_v:
        o_ref, v_ref, gated_ref = refs
    else:
        o_ref, gated_ref = refs
    tm = x_ref.shape[0]
    dc = lnw_ref.shape[1]
    gd = dc // H_C
    x = x_ref[...]
    hb = _rms(x, nw_ref[...]).astype(BF16)
    z = _dot(hb, win_ref[...])
    z = 0.5 * z * (1.0 + lax.erf(z * (2.0 ** -0.5)))
    v = z[:, dc:]
    mu = jnp.mean(v, axis=-1, keepdims=True)
    vc = v - mu
    vn = vc * lax.rsqrt(jnp.mean(vc * vc, axis=-1, keepdims=True) + EPS) * lnw_ref[...] + lnb_ref[...]
    if emit_v:
        v_ref[...] = vn
    vb = vn.astype(BF16)
    for ci in range(tm // CHUNK_C):
        rs = slice(ci * CHUNK_C, (ci + 1) * CHUNK_C)
        for g in range(H_C):
            cs = slice(g * gd, (g + 1) * gd)
            mixed = _dot(wm_ref[g], vb[rs, cs]) + jnp.concatenate([bs_ref[g]] * (gd // LANE), axis=1)
            gated_ref[rs, cs] = (z[rs, cs] * mixed).astype(BF16)
    o_ref[...] = x + _dot(gated_ref[...], wout_ref[...])


def _odd_mixer(x, nw, w_in, ln_w, ln_b, wm, bs, w_out, emit_v):
    t, d = x.shape
    dc = ln_w.shape[1]
    tm = _row_tile(t, 512)
    row = lambda w: pl.BlockSpec((tm, w), lambda i: (i, 0))
    full = lambda a: pl.BlockSpec(a.shape, lambda i: (0,) * a.ndim)
    out_specs = [row(d)] + ([row(dc)] if emit_v else [])
    out_shape = [jax.ShapeDtypeStruct((t, d), F32)] + ([jax.ShapeDtypeStruct((t, dc), F32)] if emit_v else [])
    return pl.pallas_call(
        functools.partial(_odd_kernel, emit_v=emit_v),
        grid=(t // tm,),
        in_specs=[row(d), full(nw), full(w_in), full(ln_w), full(ln_b), full(wm), full(bs), full(w_out)],
        out_specs=out_specs,
        out_shape=out_shape,
        scratch_shapes=[pltpu.VMEM((tm, dc), BF16)],
        compiler_params=_params("parallel"),
        name="odd_mixer",
    )(x, nw, w_in, ln_w, ln_b, wm, bs, w_out)


def _prep_even_weights(w_in, f_bias, a_log, dt_bias):
    d = w_in.shape[0]
    small = jnp.concatenate(
        [w_in[:, 1536:1544], w_in[:, 3080:3088], jnp.zeros((d, LANE - 16), w_in.dtype)], axis=1)
    w = jnp.concatenate([w_in[:, 0:512], w_in[:, 1544:3080], w_in[:, 3088:3600], small], axis=1)
    bias = jnp.zeros((1, LANE), F32).at[0, 0:8].set(f_bias).at[0, 8:12].set(dt_bias)
    alog = jnp.zeros((1, LANE), F32).at[0, 8:12].set(a_log)
    return w.astype(BF16), w_in[:, 512:1536].T.astype(BF16), bias, alog


def kernel(x_prompt, x_sample, cache_k, cache_v, cache_logf, state_delta, state_conv, page_table, norm_mix, norm_ffn, norm_final, w_in_even, w_out_even, fox_forget_bias, dn_conv_w, dn_a_log, dn_dt_bias, dn_norm_w, w_in_odd, gm_ln_w, gm_ln_b, gm_spatial_w, gm_spatial_b, w_out_odd, moe_router_group, moe_router_expert, moe_w_gate, moe_w_up, moe_w_down):
    n_p, seq, d = x_prompt.shape
    n_s, dec = x_sample.shape[:2]
    page = cache_k.shape[2]
    mix_a = H_A * D_A
    xp = x_prompt.reshape(n_p * seq, d)
    xs = x_sample.reshape(n_s * dec, d)
    row = lambda a: a[None, :]

    even_w = (row(norm_mix[0]),) + _prep_even_weights(w_in_even[0], fox_forget_bias[0], dn_a_log[0], dn_dt_bias[0])
    from_kt = lambda a, n, length: jnp.transpose(a.reshape(n, H_A, D_A, length), (0, 3, 1, 2))[None]
    w_out_e = w_out_even[0].astype(BF16)
    conv_w = jnp.zeros((8, dn_conv_w.shape[2]), F32).at[:CONV_W].set(dn_conv_w[0])
    onw = row(dn_norm_w[0])

    def delta(conv3, prefix, small3, gz3, s0, valid, nb):
        wv, wk, qd, kd, qk, gl = _delta_prep(conv3, prefix, conv_w, small3, valid)
        return _delta_scan(wv, wk, qd, kd, qk, gl, s0, gz3, onw, nb)

    q, kt, vt, c, gz, s = _even_in(xp, *even_w, n_p, seq)
    logf_p = s[:, :H_A].reshape(n_p, seq, H_A)
    csum = _cumsum_lanes(jnp.swapaxes(logf_p, 1, 2).reshape(n_p * H_A, seq)).reshape(n_p, H_A, seq)
    as3 = lambda a: a.reshape(n_p, seq, a.shape[-1])
    oa_p = _fox_prompt(as3(q), kt, vt, csum)
    conv_p = as3(c)
    ob_p, sd_p = delta(conv_p, jnp.zeros((n_p, 8, conv_p.shape[-1]), F32), as3(s), as3(gz),
                       jnp.zeros((n_p, H_B, DK_B, DK_B), F32), seq, 4)
    xp = _even_out(xp, oa_p.reshape(n_p * seq, mix_a), ob_p.reshape(n_p * seq, -1), w_out_e)
    k_p = from_kt(kt, n_p, seq)
    v_p = from_kt(vt, n_p, seq)
    conv_state_p = conv_p[:, seq - (CONV_W - 1):][None]

    q, kt, vt, c, gz, s = _even_in(xs, *even_w, 1, n_s * dec)
    as3 = lambda a: a.reshape(n_s, dec, a.shape[-1])
    padrows = lambda a, n: jnp.pad(a, ((0, 0), (0, n - a.shape[1]), (0, 0)))
    logf_s = s[:, :H_A].reshape(n_s, dec, H_A)
    pad_keys = lambda a: jnp.pad(a, ((0, 0), (0, 0), (0, page - dec)))
    per_seq = lambda a: pad_keys(jnp.swapaxes(a[0].reshape(mix_a, n_s, dec), 0, 1))
    oa_s = _fox_sample(page_table, as3(q), per_seq(kt), per_seq(vt), pad_keys(jnp.swapaxes(logf_s, 1, 2)),
                       jnp.transpose(cache_k[0], (0, 2, 3, 1)), jnp.transpose(cache_v[0], (0, 2, 3, 1)),
                       jnp.swapaxes(cache_logf[0], 1, 2))
    conv_s = as3(c)
    lp = DELTA_CHUNK
    prefix_s = jnp.pad(state_conv[0], ((0, 0), (8 - (CONV_W - 1), 0), (0, 0)))
    ob_s, sd_s = delta(padrows(conv_s, lp), prefix_s, padrows(as3(s), lp), padrows(as3(gz), lp),
                       state_delta[0], dec, 8)
    xs = _even_out(xs, oa_s.reshape(n_s * dec, mix_a).astype(BF16), ob_s[:, :dec].reshape(n_s * dec, -1), w_out_e)
    k_s = kt[0].T.reshape(1, n_s, dec, H_A, D_A)
    v_s = vt[0].T.reshape(1, n_s, dec, H_A, D_A)
    conv_state_s = jnp.concatenate([state_conv[0], conv_s], axis=1)[:, dec:][None]

    def moe(x, layer, final):
        wr = jnp.concatenate([moe_router_expert[layer], moe_router_group[layer],
                              jnp.zeros((d, LANE - N_EXPERTS - N_GROUPS), F32)], axis=1)
        gates = _router(x, row(norm_ffn[layer]), wr)
        experts = _moe_sorted if x.shape[0] >= 4096 else _moe_dense
        return experts(x, row(norm_ffn[layer]), gates, wg, wu, wd, row(norm_final), layer, final)

    wg, wu, wd = moe_w_gate.astype(BF16), moe_w_up.astype(BF16), moe_w_down.astype(BF16)
    xp = moe(xp, 0, False)
    xs = moe(xs, 0, False)

    idx = jnp.arange(CHUNK_C)
    w_m = jnp.where(idx[:, None] >= idx[None, :], gm_spatial_w[0], 0.0)
    b_s = gm_spatial_b[0]
    lanes = lambda b: jnp.broadcast_to(b[:, :, None], b.shape + (LANE,))
    reps = CHUNK_C // dec
    assert n_s * dec == CHUNK_C, "sample group must fill exactly one 128-row mixing tile"
    w_m_s = (jnp.eye(reps, dtype=F32)[None, :, None, :, None] * w_m[:, None, :dec, None, :dec]).reshape(H_C, CHUNK_C, CHUNK_C)
    b_s_s = jnp.tile(b_s[:, :dec], (1, reps))
    odd_w = (row(norm_mix[1]), w_in_odd[0].astype(BF16), row(gm_ln_w[0]), row(gm_ln_b[0]))
    w_out_o = w_out_odd[0].astype(BF16)
    (xp,) = _odd_mixer(xp, *odd_w, w_m.astype(BF16), lanes(b_s), w_out_o, False)
    xs, v_rows = _odd_mixer(xs, *odd_w, w_m_s.astype(BF16), lanes(b_s_s), w_out_o, True)

    y_p = moe(xp, 1, True).reshape(n_p, seq, d)
    y_s = moe(xs, 1, True).reshape(n_s, dec, d)
    return (y_p, y_s, k_p, v_p, logf_p[None], sd_p[None], conv_state_p,
            k_s, v_s, logf_s[None], sd_s[None], conv_state_s, v_rows.reshape(1, n_s, dec, -1))
```

```python
import functools

import jax
import jax.numpy as jnp
from jax import lax
from jax.experimental import pallas as pl
from jax.experimental.pallas import tpu as pltpu

F32 = jnp.float32
BF16 = jnp.bfloat16
EPS = 1e-6
LANE = 128
VMEM_LIMIT = 56 * 1024 * 1024
HIGHEST = lax.Precision.HIGHEST

H_A, D_A = 8, 64
H_B, DK_B = 4, 128
CONV_W = 4
DELTA_CHUNK = 64
N_GROUPS, EXPERTS_PER_GROUP = 4, 8
N_EXPERTS = N_GROUPS * EXPERTS_PER_GROUP
CHUNK_C = 128
H_C = 8


def _params(*sem):
    return pltpu.CompilerParams(dimension_semantics=sem, vmem_limit_bytes=VMEM_LIMIT)


def _row_tile(t, pref):
    return pref if t % pref == 0 else t


def _rms(x, w):
    return x * lax.rsqrt(jnp.mean(x * x, axis=-1, keepdims=True) + EPS) * w


def _softplus_tail(z):
    return jnp.log1p(jnp.exp(-jnp.abs(z)))


def _even_in_kernel(x_ref, nw_ref, w_ref, wkv_ref, bias_ref, alog_ref,
                    q_ref, k_ref, v_ref, c_ref, gz_ref, s_ref):
    hb = _rms(x_ref[...], nw_ref[...]).astype(BF16)

    def proj(lo, hi):
        return jnp.dot(hb, w_ref[:, lo:hi], preferred_element_type=F32)

    q_ref[...] = proj(0, 512)
    c_ref[...] = proj(512, 2048)
    gz_ref[...] = proj(2048, 2560)
    k_ref[0] = lax.dot_general(wkv_ref[:512, :], hb, (((1,), (1,)), ((), ())), preferred_element_type=F32)
    v_ref[0] = lax.dot_general(wkv_ref[512:, :], hb, (((1,), (1,)), ((), ())), preferred_element_type=F32)
    z = proj(2560, 2688) + bias_ref[...]
    tail = _softplus_tail(z)
    logf = jnp.minimum(z, 0.0) - tail
    g = -jnp.exp(alog_ref[...]) * (jnp.maximum(z, 0.0) + tail)
    beta = 1.0 / (1.0 + jnp.exp(-z))
    lane = lax.broadcasted_iota(jnp.int32, z.shape, 1)
    s_ref[...] = jnp.where(lane < 8, logf, jnp.where(lane < 12, g, jnp.where(lane < 16, beta, 0.0)))


def _even_in(x, nw, w, wkv, bias, alog, n, length):
    t, d = x.shape
    tm = _row_tile(length, 512)
    per_seq = length // tm
    mix = wkv.shape[0] // 2
    row = lambda wd: pl.BlockSpec((tm, wd), lambda i: (i, 0))
    full = lambda a: pl.BlockSpec(a.shape, lambda i: (0, 0))
    kt = pl.BlockSpec((1, mix, tm), lambda i: (i // per_seq, 0, i % per_seq))
    rows = lambda wd: jax.ShapeDtypeStruct((t, wd), F32)
    kts = jax.ShapeDtypeStruct((n, mix, length), F32)
    return pl.pallas_call(
        _even_in_kernel,
        grid=(t // tm,),
        in_specs=[row(d), full(nw), full(w), full(wkv), full(bias), full(alog)],
        out_specs=[row(mix), kt, kt, row(3 * mix), row(mix), row(LANE)],
        out_shape=[rows(mix), kts, kts, rows(3 * mix), rows(mix), rows(LANE)],
        compiler_params=_params("parallel"),
        name="even_in_proj",
    )(x, nw, w, wkv, bias, alog)


def _cumsum_kernel(x_ref, o_ref):
    rows, length = x_ref.shape
    r = lax.broadcasted_iota(jnp.int32, (LANE, LANE), 0)
    c = lax.broadcasted_iota(jnp.int32, (LANE, LANE), 1)
    upper = (r <= c).astype(F32)
    carry = jnp.zeros((rows, 1), F32)
    for b in range(length // LANE):
        blk = jnp.dot(x_ref[:, b * LANE:(b + 1) * LANE], upper,
                      preferred_element_type=F32, precision=HIGHEST) + carry
        o_ref[:, b * LANE:(b + 1) * LANE] = blk
        carry = blk[:, LANE - 1:LANE]


def _cumsum_lanes(x):
    rows, length = x.shape
    tr = 8
    return pl.pallas_call(
        _cumsum_kernel,
        grid=(rows // tr,),
        in_specs=[pl.BlockSpec((tr, length), lambda i: (i, 0))],
        out_specs=pl.BlockSpec((tr, length), lambda i: (i, 0)),
        out_shape=jax.ShapeDtypeStruct((rows, length), F32),
        compiler_params=_params("parallel"),
        name="logf_cumsum",
    )(x)


def _fox_prompt_kernel(q_ref, k_ref, v_ref, c_ref, o_ref, *, tq):
    qi = pl.program_id(2)
    hp = pl.program_id(1)
    scale = D_A ** -0.5
    lane = lax.broadcasted_iota(jnp.int32, (tq, LANE), 1)
    first = lane < D_A
    q = q_ref[0] * scale
    qs = [jnp.where(first, q, 0.0).astype(BF16), jnp.where(first, 0.0, q).astype(BF16)]
    q0 = pl.multiple_of(qi * tq, tq)

    def crow(h, start):
        return c_ref[0, pl.ds(2 * hp + h, 1), pl.ds(start, tq)]

    cq = [crow(h, q0)[:, 0:1] for h in range(2)]

    def block(j0, carry, mask):
        kb = k_ref[0, :, pl.ds(j0, tq)].astype(BF16)
        vb = v_ref[0, :, pl.ds(j0, tq)].astype(BF16)
        hh = range(2)
        s = [_dot(qs[h], kb) + (cq[h] - crow(h, j0)) for h in hh]
        if mask is not None:
            s = [jnp.where(mask, a, -jnp.inf) for a in s]
        m_new = [jnp.maximum(carry[h][0], jnp.max(s[h], axis=-1, keepdims=True)) for h in hh]
        alpha = [jnp.exp(carry[h][0] - m_new[h]) for h in hh]
        p = [jnp.exp(s[h] - m_new[h]) for h in hh]
        l = [alpha[h] * carry[h][1] + jnp.sum(p[h], axis=-1, keepdims=True) for h in hh]
        pv = [_dot_nt(p[h].astype(BF16), vb) for h in hh]
        acc = [alpha[h] * carry[h][2] + pv[h] for h in hh]
        return tuple((m_new[h], l[h], acc[h]) for h in hh)

    def body(j, carry):
        return block(pl.multiple_of(j * tq, tq), carry, None)

    init = (jnp.full((tq, 1), -jnp.inf, F32), jnp.zeros((tq, 1), F32), jnp.zeros((tq, LANE), F32))
    carry = lax.fori_loop(0, qi, body, (init, init))
    r = lax.broadcasted_iota(jnp.int32, (tq, tq), 0)
    cidx = lax.broadcasted_iota(jnp.int32, (tq, tq), 1)
    (_, l0, a0), (_, l1, a1) = block(q0, carry, cidx <= r)
    o_ref[0] = jnp.where(first, a0 / l0, a1 / l1).astype(o_ref.dtype)


def _fox_prompt(q, k, v, c):
    n, length, width = q.shape
    tq = 512 if length % 512 == 0 else length
    grid = (n, width // LANE, length // tq)
    return pl.pallas_call(
        functools.partial(_fox_prompt_kernel, tq=tq),
        grid=grid,
        in_specs=[
            pl.BlockSpec((1, tq, LANE), lambda b, h, i: (b, i, h)),
            pl.BlockSpec((1, LANE, length), lambda b, h, i: (b, h, 0)),
            pl.BlockSpec((1, LANE, length), lambda b, h, i: (b, h, 0)),
            pl.BlockSpec((1, H_A, length), lambda b, h, i: (b, 0, 0)),
        ],
        out_specs=pl.BlockSpec((1, tq, LANE), lambda b, h, i: (b, i, h)),
        out_shape=jax.ShapeDtypeStruct((n, length, width), BF16),
        compiler_params=_params("parallel", "parallel", "arbitrary"),
        name="fox_prompt",
    )(q, k, v, c)


def _even_out_kernel(x_ref, a_ref, b_ref, w_ref, o_ref):
    half = a_ref.shape[1]
    acc = jnp.dot(a_ref[...], w_ref[:half, :], preferred_element_type=F32)
    acc += jnp.dot(b_ref[...], w_ref[half:, :], preferred_element_type=F32)
    o_ref[...] = x_ref[...] + acc


def _even_out(x, oa, ob, w):
    t, d = x.shape
    tm = _row_tile(t, 512)
    row = lambda a: pl.BlockSpec((tm, a.shape[1]), lambda i: (i, 0))
    return pl.pallas_call(
        _even_out_kernel,
        grid=(t // tm,),
        in_specs=[row(x), row(oa), row(ob), pl.BlockSpec(w.shape, lambda i: (0, 0))],
        out_specs=row(x),
        out_shape=jax.ShapeDtypeStruct((t, d), F32),
        compiler_params=_params("parallel"),
        name="even_out_proj",
    )(x, oa, ob, w)


def _split3(x):
    hi = x.astype(BF16)
    r1 = x - hi.astype(F32)
    mid = r1.astype(BF16)
    lo = (r1 - mid.astype(F32)).astype(BF16)
    return hi, mid, lo


def _dot(a, b):
    return jnp.dot(a, b, preferred_element_type=F32)


def _dot_nt(a, b):
    return lax.dot_general(a, b, (((1,), (1,)), ((), ())), preferred_element_type=F32)


def _dot_tn(a, b):
    return lax.dot_general(a, b, (((0,), (0,)), ((), ())), preferred_element_type=F32)


def _silu(x):
    return x / (1.0 + jnp.exp(-x))


def _delta_prep_kernel(x_ref, xprev_ref, pre_ref, cw_ref, sm_ref,
                       wv_ref, wk_ref, qd_ref, kd_ref, qk_ref, gl_ref, *, ta, valid):
    i = pl.program_id(1)
    c = DELTA_CHUNK
    dk = DK_B
    rr = H_B * c
    x = x_ref[0]
    prev = jnp.where(i == 0, pre_ref[0], xprev_ref[0])
    xcat = jnp.concatenate([prev, x], axis=0)
    conv = x * cw_ref[CONV_W - 1:CONV_W, :]
    for s in range(1, CONV_W):
        conv += pltpu.roll(xcat, s, axis=0)[8:] * cw_ref[CONV_W - 1 - s:CONV_W - s, :]
    act = _silu(conv)
    sm = sm_ref[0]

    ri = lax.broadcasted_iota(jnp.int32, (rr, rr), 0)
    ci = lax.broadcasted_iota(jnp.int32, (rr, rr), 1)
    same_head = (ri // c) == (ci // c)
    incl = same_head & (ri >= ci)
    strict = same_head & (ri > ci)
    tri = incl.astype(BF16)
    lane = lax.broadcasted_iota(jnp.int32, (c, LANE), 1)

    qn, kn, va = [], [], []
    for h in range(H_B):
        qa = act[:, h * dk:(h + 1) * dk]
        ka = act[:, (H_B + h) * dk:(H_B + h + 1) * dk]
        qn.append(qa * lax.rsqrt(jnp.sum(qa * qa, axis=-1, keepdims=True) + EPS) * (dk ** -0.5))
        kn.append(ka * lax.rsqrt(jnp.sum(ka * ka, axis=-1, keepdims=True) + EPS))
        va.append(act[:, (2 * H_B + h) * dk:(2 * H_B + h + 1) * dk])

    chunks = range(ta // c)
    stack = lambda parts: jnp.concatenate(parts, axis=0)
    q, k, v, b_col, gcb = [], [], [], [], []
    for s in chunks:
        r0 = s * c
        live = lax.broadcasted_iota(jnp.int32, (c, 1), 0) + (i * ta + r0) < valid
        smc = sm[r0:r0 + c]
        pick = lambda ln: jnp.where(live, jnp.sum(jnp.where(lane == ln, smc, 0.0), axis=-1, keepdims=True), 0.0)
        q.append(stack([a[r0:r0 + c] for a in qn]))
        k.append(stack([a[r0:r0 + c] for a in kn]))
        v.append(stack([a[r0:r0 + c] for a in va]))
        b_col.append(stack([pick(12 + h) for h in range(H_B)]))
        g_stack = stack([jnp.broadcast_to(pick(8 + h), (c, LANE)) for h in range(H_B)])
        gcb.append(sum(_dot(tri, p) for p in _split3(g_stack)))

    wide = [jnp.concatenate([g, g], axis=1) for g in gcb]
    decay = [jnp.exp(jnp.where(incl, w - w.T, -jnp.inf)) for w in wide]
    eg = [jnp.exp(g) for g in gcb]
    kb = [a * b for a, b in zip(k, b_col)]
    kbf = [a.astype(BF16) for a in k]
    a_mat = [jnp.where(strict, _dot_nt(a.astype(BF16), b) * d, 0.0) for a, b, d in zip(kb, kbf, decay)]
    qk = [jnp.where(incl, _dot_nt(a.astype(BF16), b) * d, 0.0) for a, b, d in zip(q, kbf, decay)]
    y = [-a for a in a_mat]
    p = list(y)
    for _ in range(5):
        yb = [a.astype(BF16) for a in y]
        y = [_dot(a, a) for a in yb]
        p = [a + b + _dot(a.astype(BF16), b.astype(BF16)) for a, b in zip(p, y)]
    rhs = [jnp.concatenate([a * b, kbb * e], axis=1) for a, b, kbb, e in zip(v, b_col, kb, eg)]
    w = [r + _dot(a.astype(BF16), r.astype(BF16)) for a, r in zip(p, rhs)]

    for s in chunks:
        r0 = s * c
        for h in range(H_B):
            hr = slice(h * c, (h + 1) * c)
            hs = slice(h * dk, (h + 1) * dk)
            g_last = gcb[s][h * c + c - 1:(h + 1) * c, :]
            wv_ref[0, r0:r0 + c, hs] = w[s][hr, :dk]
            wk_ref[0, r0:r0 + c, hs] = w[s][hr, dk:].astype(BF16)
            qd_ref[0, r0:r0 + c, hs] = (q[s][hr] * eg[s][hr]).astype(BF16)
            kd_ref[0, r0:r0 + c, hs] = (k[s][hr] * jnp.exp(g_last - gcb[s][hr])).astype(BF16)
            qk_ref[0, h, r0:r0 + c, :] = qk[s][hr, h * c:(h + 1) * c].astype(BF16)
            gl_ref[0, s, h:h + 1, :] = jnp.exp(g_last)


def _delta_prep(conv_in, prefix, conv_w, small, valid):
    n, lp, cd = conv_in.shape
    ta = LANE if lp % LANE == 0 else lp
    nt = lp // ta
    tb = ta // 8
    f = lambda dt, w: jax.ShapeDtypeStruct((n, lp, w), dt)
    blk = lambda w: pl.BlockSpec((1, ta, w), lambda b, i: (b, i, 0))
    width = H_B * DK_B
    return pl.pallas_call(
        functools.partial(_delta_prep_kernel, ta=ta, valid=valid),
        grid=(n, nt),
        in_specs=[
            blk(cd),
            pl.BlockSpec((1, 8, cd), lambda b, i: (b, jnp.maximum(i * tb - 1, 0), 0)),
            pl.BlockSpec((1, 8, cd), lambda b, i: (b, 0, 0)),
            pl.BlockSpec((8, cd), lambda b, i: (0, 0)),
            blk(LANE),
        ],
        out_specs=[
            blk(width), blk(width), blk(width), blk(width),
            pl.BlockSpec((1, H_B, ta, DELTA_CHUNK), lambda b, i: (b, 0, i, 0)),
            pl.BlockSpec((1, ta // DELTA_CHUNK, H_B, LANE), lambda b, i: (b, i, 0, 0)),
        ],
        out_shape=[
            f(F32, width), f(BF16, width), f(BF16, width), f(BF16, width),
            jax.ShapeDtypeStruct((n, H_B, lp, DELTA_CHUNK), BF16),
            jax.ShapeDtypeStruct((n, lp // DELTA_CHUNK, H_B, LANE), F32),
        ],
        compiler_params=_params("parallel", "parallel"),
        name="delta_prep",
    )(conv_in, conv_in, prefix, conv_w, small)


def _delta_scan_kernel(wv_ref, wk_ref, qd_ref, kd_ref, qk_ref, gl_ref, s0_ref, gz_ref, onw_ref,
                       o_ref, s_ref, *, nb, n_chunks):
    c = DELTA_CHUNK
    dk = DK_B

    @pl.when(pl.program_id(1) == 0)
    def _():
        s_ref[...] = s0_ref[...]

    onw = onw_ref[...]

    chains = [(b, h) for b in range(nb) for h in range(H_B)]
    cols = lambda h: slice(h * dk, (h + 1) * dk)

    def body(ci, _):
        r0 = pl.multiple_of(ci * c, c)
        rows = pl.ds(r0, c)
        state = [s_ref[b, h] for b, h in chains]
        prod = [_dot(jnp.concatenate([wk_ref[b, rows, cols(h)], qd_ref[b, rows, cols(h)]], axis=0), s.astype(BF16))
                for (b, h), s in zip(chains, state)]
        vb = [(wv_ref[b, rows, cols(h)] - p[:c]).astype(BF16) for (b, h), p in zip(chains, prod)]
        o = [p[c:] + _dot(qk_ref[b, h, rows, :], v) for (b, h), p, v in zip(chains, prod, vb)]
        new = [s * gl_ref[b, pl.ds(ci, 1), h, :] + _dot_tn(kd_ref[b, rows, cols(h)], v)
               for (b, h), s, v in zip(chains, state, vb)]
        for (b, h), s, oo in zip(chains, new, o):
            s_ref[b, h] = s
            on = oo * lax.rsqrt(jnp.mean(oo * oo, axis=-1, keepdims=True) + EPS) * onw
            o_ref[b, rows, cols(h)] = (on * _silu(gz_ref[b, rows, cols(h)])).astype(o_ref.dtype)
        return 0

    lax.fori_loop(0, n_chunks, body, 0)


def _delta_scan(wv, wk, qd, kd, qk, gl, s0, gz, onw, nb):
    n, lp, width = wv.shape
    tl = _row_tile(lp, 512)
    n_chunks = tl // DELTA_CHUNK
    seq = pl.BlockSpec((nb, tl, width), lambda i, t: (i, t, 0))
    state = pl.BlockSpec((nb,) + s0.shape[1:], lambda i, t: (i, 0, 0, 0))
    return pl.pallas_call(
        functools.partial(_delta_scan_kernel, nb=nb, n_chunks=n_chunks),
        grid=(n // nb, lp // tl),
        in_specs=[seq, seq, seq, seq,
                  pl.BlockSpec((nb, H_B, tl, DELTA_CHUNK), lambda i, t: (i, 0, t, 0)),
                  pl.BlockSpec((nb, n_chunks, H_B, LANE), lambda i, t: (i, t, 0, 0)),
                  state, seq, pl.BlockSpec(onw.shape, lambda i, t: (0, 0))],
        out_specs=[seq, state],
        out_shape=[jax.ShapeDtypeStruct((n, lp, width), BF16), jax.ShapeDtypeStruct(s0.shape, F32)],
        compiler_params=_params("parallel", "arbitrary"),
        name="delta_scan",
    )(wv, wk, qd, kd, qk, gl, s0, gz, onw)


def _fox_sample_kernel(pt_ref, q_ref, kn_ref, vn_ref, lfn_ref, *refs, pages_per_step, n_q):
    del pt_ref
    pps = pages_per_step
    kc_refs, vc_refs, lfc_refs = refs[:pps], refs[pps:2 * pps], refs[2 * pps:3 * pps]
    o_ref, qbd_ref, m_ref, l_ref, acc_ref, carry_ref = refs[3 * pps:]
    j = pl.program_id(1)
    rows = n_q * H_A
    page = lfn_ref.shape[2]
    width = H_A * D_A
    ri = lax.broadcasted_iota(jnp.int32, (page, page), 0)
    ci = lax.broadcasted_iota(jnp.int32, (page, page), 1)

    def attend(blocks):
        qbd = qbd_ref[...]
        ss = []
        for kt, _, bias8, mask in blocks:
            s = _dot(qbd, kt.astype(BF16)) + jnp.concatenate([bias8] * n_q, axis=0)
            ss.append(s if mask is None else jnp.where(mask, s, -jnp.inf))
        m_old = m_ref[...]
        m_new = m_old
        for s in ss:
            m_new = jnp.maximum(m_new, jnp.max(s, axis=-1, keepdims=True))
        alpha = jnp.exp(m_old - m_new)
        ps = [jnp.exp(s - m_new) for s in ss]
        sums = [jnp.sum(p, axis=-1, keepdims=True) for p in ps]
        pvs = [_dot_nt(p.astype(BF16), vt.astype(BF16)) for p, (_, vt, _, _) in zip(ps, blocks)]
        l_ref[...] = alpha * l_ref[...] + sum(sums)
        acc_ref[...] = alpha * acc_ref[...] + sum(pvs)
        m_ref[...] = m_new

    @pl.when(j == 0)
    def _():
        q = q_ref[0] * (D_A ** -0.5)
        qrep = jnp.concatenate([jnp.broadcast_to(q[t:t + 1], (H_A, width)) for t in range(n_q)], axis=0)
        r = lax.broadcasted_iota(jnp.int32, qrep.shape, 0)
        ln = lax.broadcasted_iota(jnp.int32, qrep.shape, 1)
        qbd_ref[...] = jnp.where(ln // D_A == r % H_A, qrep, 0.0).astype(BF16)
        m_ref[...] = jnp.full(m_ref.shape, -jnp.inf, F32)
        l_ref[...] = jnp.zeros(l_ref.shape, F32)
        acc_ref[...] = jnp.zeros(acc_ref.shape, F32)
        carry_ref[...] = jnp.zeros(carry_ref.shape, F32)
        upper = (ri <= ci).astype(BF16)
        csum = sum(_dot(p, upper) for p in _split3(lfn_ref[0]))
        r2 = lax.broadcasted_iota(jnp.int32, (rows, page), 0)
        c2 = lax.broadcasted_iota(jnp.int32, (rows, page), 1)
        attend([(kn_ref[0], vn_ref[0], -csum, c2 <= r2 // H_A)])

    @pl.when(j > 0)
    def _():
        after = (ri > ci).astype(BF16)
        carry = carry_ref[...]
        lfs = [r[0] for r in lfc_refs]
        pieces = [_split3(lf) for lf in lfs]
        suffix = [sum(_dot(p, after) for p in ps) for ps in pieces]
        totals = [jnp.sum(lf, axis=-1, keepdims=True) for lf in lfs]
        blocks = []
        for kc_ref, vc_ref, sfx, tot in zip(kc_refs, vc_refs, suffix, totals):
            blocks.append((kc_ref[0].reshape(width, page), vc_ref[0].reshape(width, page), carry + sfx, None))
            carry = carry + tot
        attend(blocks)
        carry_ref[...] = carry

    @pl.when(j == pl.num_programs(1) - 1)
    def _():
        out = acc_ref[...] / l_ref[...]
        r = lax.broadcasted_iota(jnp.int32, out.shape, 0)
        ln = lax.broadcasted_iota(jnp.int32, out.shape, 1)
        out = jnp.where(ln // D_A == r % H_A, out, 0.0)
        o_ref[0] = jnp.sum(out.reshape(n_q, H_A, width), axis=1)


def _fox_sample(page_table, q, kn, vn, lfn, cache_k, cache_v, cache_lf):
    b, n_q, width = q.shape
    n_pages = page_table.shape[1]
    page = cache_k.shape[3]
    rows = n_q * H_A
    pps = next(p for p in (16, 8, 1) if n_pages % p == 0)
    cur = lambda i, j, pt: (i, 0, 0)

    def past(p, nd):
        def index_map(i, j, pt):
            return (pt[i, n_pages - 1 - ((jnp.maximum(j, 1) - 1) * pps + p)],) + (0,) * nd
        return index_map

    kv_specs = [pl.BlockSpec((1, H_A, D_A, page), past(p, 3)) for p in range(pps)]
    lf_specs = [pl.BlockSpec((1, H_A, page), past(p, 2)) for p in range(pps)]
    grid_spec = pltpu.PrefetchScalarGridSpec(
        num_scalar_prefetch=1,
        grid=(b, n_pages // pps + 1),
        in_specs=[
            pl.BlockSpec((1, n_q, width), cur),
            pl.BlockSpec((1, width, page), cur),
            pl.BlockSpec((1, width, page), cur),
            pl.BlockSpec((1, H_A, page), cur),
        ] + kv_specs + kv_specs + lf_specs,
        out_specs=pl.BlockSpec((1, n_q, width), cur),
        scratch_shapes=[
            pltpu.VMEM((rows, width), BF16),
            pltpu.VMEM((rows, 1), F32),
            pltpu.VMEM((rows, 1), F32),
            pltpu.VMEM((rows, width), F32),
            pltpu.VMEM((H_A, 1), F32),
        ],
    )
    return pl.pallas_call(
        functools.partial(_fox_sample_kernel, pages_per_step=pps, n_q=n_q),
        grid_spec=grid_spec,
        out_shape=jax.ShapeDtypeStruct((b, n_q, width), F32),
        compiler_params=_params("parallel", "arbitrary"),
        name="fox_sample",
    )(page_table, q, kn, vn, lfn, *([cache_k] * pps), *([cache_v] * pps), *([cache_lf] * pps))


def _router_kernel(x_ref, nw_ref, wr_ref, g_ref):
    h = _rms(x_ref[...], nw_ref[...])
    h_hi = h.astype(BF16)
    h_lo = (h - h_hi.astype(F32)).astype(BF16)
    logits = _dot(h_hi, wr_ref[0]) + (_dot(h_lo, wr_ref[0]) + _dot(h_hi, wr_ref[1]))
    lane = lax.broadcasted_iota(jnp.int32, logits.shape, 1)
    lanef = lane.astype(F32)
    neg = -jnp.inf
    is_group = (lane >= N_EXPERTS) & (lane < N_EXPERTS + N_GROUPS)
    gl = jnp.where(is_group, logits, neg)
    gmax = jnp.max(gl, axis=-1, keepdims=True)
    g_sel = jnp.min(jnp.where(gl == gmax, lanef - N_EXPERTS, 1e9), axis=-1, keepdims=True)
    p_sel = 1.0 / jnp.sum(jnp.where(is_group, jnp.exp(logits - gmax), 0.0), axis=-1, keepdims=True)
    in_sel = (lane < N_EXPERTS) & ((lane // EXPERTS_PER_GROUP).astype(F32) == g_sel)
    el = jnp.where(in_sel, logits, neg)
    m1 = jnp.max(el, axis=-1, keepdims=True)
    i1 = jnp.min(jnp.where(el == m1, lanef, 1e9), axis=-1, keepdims=True)
    el2 = jnp.where(lanef == i1, neg, el)
    m2 = jnp.max(el2, axis=-1, keepdims=True)
    i2 = jnp.min(jnp.where(el2 == m2, lanef, 1e9), axis=-1, keepdims=True)
    e2 = jnp.exp(m2 - m1)
    w1 = p_sel / (1.0 + e2)
    g_ref[...] = jnp.where(lanef == i1, w1, jnp.where(lanef == i2, w1 * e2, jnp.where(lane == N_EXPERTS, g_sel, 0.0)))


def _router(x, nw, wr):
    t, d = x.shape
    tm = _row_tile(t, 512)
    return pl.pallas_call(
        _router_kernel,
        grid=(t // tm,),
        in_specs=[pl.BlockSpec((tm, d), lambda i: (i, 0)), pl.BlockSpec(nw.shape, lambda i: (0, 0)),
                  pl.BlockSpec(wr.shape, lambda i: (0, 0, 0))],
        out_specs=pl.BlockSpec((tm, LANE), lambda i: (i, 0)),
        out_shape=jax.ShapeDtypeStruct((t, LANE), F32),
        compiler_params=_params("parallel"),
        name="moe_router",
    )(x, nw, wr)


def _moe_kernel(x_ref, nw_ref, g_ref, wg_ref, wu_ref, wd_ref, nf_ref, o_ref, hb_ref, acc_ref, *, final_norm):
    e = pl.program_id(1)

    @pl.when(e == 0)
    def _():
        hb_ref[...] = _rms(x_ref[...], nw_ref[...]).astype(BF16)
        acc_ref[...] = jnp.zeros(acc_ref.shape, F32)

    gates = g_ref[...]
    lane = lax.broadcasted_iota(jnp.int32, gates.shape, 1)
    gcol = jnp.sum(jnp.where(lane == e, gates, 0.0), axis=-1, keepdims=True)
    hb = hb_ref[...]
    act = _silu(_dot(hb, wg_ref[0, 0])) * _dot(hb, wu_ref[0, 0]) * gcol
    acc_ref[...] += _dot(act.astype(BF16), wd_ref[0, 0])

    @pl.when(e == pl.num_programs(1) - 1)
    def _():
        y = x_ref[...] + acc_ref[...]
        o_ref[...] = _rms(y, nf_ref[...]) if final_norm else y


def _moe_dense(x, nw, gates, wg, wu, wd, nf, layer, final_norm):
    t, d = x.shape
    tm = _row_tile(t, 1024)
    _, n_e, _, f = wg.shape
    return pl.pallas_call(
        functools.partial(_moe_kernel, final_norm=final_norm),
        grid=(t // tm, n_e),
        in_specs=[
            pl.BlockSpec((tm, d), lambda i, e: (i, 0)),
            pl.BlockSpec(nw.shape, lambda i, e: (0, 0)),
            pl.BlockSpec((tm, LANE), lambda i, e: (i, 0)),
            pl.BlockSpec((1, 1, d, f), lambda i, e: (layer, e, 0, 0)),
            pl.BlockSpec((1, 1, d, f), lambda i, e: (layer, e, 0, 0)),
            pl.BlockSpec((1, 1, f, d), lambda i, e: (layer, e, 0, 0)),
            pl.BlockSpec(nf.shape, lambda i, e: (0, 0)),
        ],
        out_specs=pl.BlockSpec((tm, d), lambda i, e: (i, 0)),
        out_shape=jax.ShapeDtypeStruct((t, d), F32),
        scratch_shapes=[pltpu.VMEM((tm, d), BF16), pltpu.VMEM((tm, d), F32)],
        compiler_params=_params("parallel", "arbitrary"),
        name="moe_experts",
    )(x, nw, gates, wg, wu, wd, nf)


def _group_rank_kernel(g_ref, grp_ref, rank_ref, cnt_ref, run_ref):
    i = pl.program_id(0)
    tm = g_ref.shape[0]

    @pl.when(i == 0)
    def _():
        run_ref[...] = jnp.zeros(run_ref.shape, F32)

    gates = g_ref[...]
    lane = lax.broadcasted_iota(jnp.int32, gates.shape, 1)
    g_sel = jnp.sum(jnp.where(lane == N_EXPERTS, gates, 0.0), axis=-1, keepdims=True)
    onehot_t = (lane.astype(F32) == g_sel).astype(F32).T[:8]
    ri = lax.broadcasted_iota(jnp.int32, (tm, tm), 0)
    ci = lax.broadcasted_iota(jnp.int32, (tm, tm), 1)
    earlier = _dot(onehot_t.astype(BF16), (ri < ci).astype(BF16))
    run = run_ref[...]
    gidx = lax.broadcasted_iota(jnp.int32, (8, tm), 0).astype(F32)
    rank_ref[0] = jnp.sum(onehot_t * (earlier + run), axis=0, keepdims=True).astype(jnp.int32)
    grp_ref[0] = jnp.sum(onehot_t * gidx, axis=0, keepdims=True).astype(jnp.int32)
    run = run + jnp.sum(onehot_t, axis=-1, keepdims=True)
    run_ref[...] = run
    cnt_ref[...] = jnp.broadcast_to(run, cnt_ref.shape)


def _group_rank(gates, tm):
    t = gates.shape[0]
    nt = t // tm
    row = pl.BlockSpec((1, 1, tm), lambda i: (i, 0, 0))
    return pl.pallas_call(
        _group_rank_kernel,
        grid=(nt,),
        in_specs=[pl.BlockSpec((tm, LANE), lambda i: (i, 0))],
        out_specs=[row, row, pl.BlockSpec((8, LANE), lambda i: (0, 0))],
        out_shape=[jax.ShapeDtypeStruct((nt, 1, tm), jnp.int32), jax.ShapeDtypeStruct((nt, 1, tm), jnp.int32),
                   jax.ShapeDtypeStruct((8, LANE), F32)],
        scratch_shapes=[pltpu.VMEM((8, 1), F32)],
        compiler_params=_params("arbitrary"),
        name="moe_group_rank",
    )(gates)


def _row_copy(src, src_row, dst, dst_row, sem):
    return pltpu.make_async_copy(src.at[pl.ds(src_row, 1)], dst.at[pl.ds(dst_row, 1)], sem)


def _dispatch_kernel(dest_ref, x_ref, g_ref, buf_ref, o_ref, stage_ref, sem):
    del buf_ref
    tm, d = x_ref.shape
    stage_ref[:, :d] = x_ref[...]
    stage_ref[:, d:] = g_ref[...]

    def start(r, _):
        _row_copy(stage_ref, r, o_ref, dest_ref[0, 0, r], sem).start()
        return 0

    def wait(r, _):
        _row_copy(stage_ref, 0, o_ref, 0, sem).wait()
        return 0

    lax.fori_loop(0, tm, start, 0, unroll=8)
    lax.fori_loop(0, tm, wait, 0, unroll=8)


def _dispatch(dest, x, gates, buf, tm):
    t, d = x.shape
    return pl.pallas_call(
        _dispatch_kernel,
        grid=(t // tm,),
        in_specs=[pl.BlockSpec((1, 1, tm), lambda i: (i, 0, 0), memory_space=pltpu.SMEM),
                  pl.BlockSpec((tm, d), lambda i: (i, 0)),
                  pl.BlockSpec((tm, LANE), lambda i: (i, 0)),
                  pl.BlockSpec(memory_space=pl.ANY)],
        out_specs=pl.BlockSpec(memory_space=pl.ANY),
        out_shape=jax.ShapeDtypeStruct(buf.shape, F32),
        scratch_shapes=[pltpu.VMEM((tm, d + LANE), F32), pltpu.SemaphoreType.DMA(())],
        input_output_aliases={3: 0},
        compiler_params=_params("arbitrary"),
        name="moe_dispatch",
    )(dest, x, gates, buf)


def _moe_group_kernel(tg_ref, na_ref, xs_ref, nw_ref, wg_ref, wu_ref, wd_ref, nf_ref, o_ref, *, final_norm):
    i = pl.program_id(0)
    d = o_ref.shape[1]

    @pl.when(i < na_ref[0])
    def _():
        x = xs_ref[:, :d]
        gates = xs_ref[:, d:]
        hb = _rms(x, nw_ref[...]).astype(BF16)
        lane = lax.broadcasted_iota(jnp.int32, gates.shape, 1)
        first = tg_ref[i] * EXPERTS_PER_GROUP
        acc = jnp.zeros(x.shape, F32)
        for e in range(EXPERTS_PER_GROUP):
            gcol = jnp.sum(jnp.where(lane == first + e, gates, 0.0), axis=-1, keepdims=True)
            act = _silu(_dot(hb, wg_ref[0, 0, e])) * _dot(hb, wu_ref[0, 0, e]) * gcol
            acc += _dot(act.astype(BF16), wd_ref[0, 0, e])
        y = x + acc
        o_ref[...] = _rms(y, nf_ref[...]) if final_norm else y

    @pl.when(i >= na_ref[0])
    def _():
        o_ref[...] = jnp.zeros(o_ref.shape, F32)


def _moe_group(tile_group, n_active, xs, nw, wg, wu, wd, nf, tm, layer, final_norm):
    tp, dx = xs.shape
    d = dx - LANE
    wspec = lambda a: pl.BlockSpec((1, 1) + a.shape[2:], lambda i, tg, na: (layer, tg[i], 0, 0, 0))
    vec = lambda a: pl.BlockSpec(a.shape, lambda i, tg, na: (0, 0))
    grid_spec = pltpu.PrefetchScalarGridSpec(
        num_scalar_prefetch=2,
        grid=(tp // tm,),
        in_specs=[pl.BlockSpec((tm, dx), lambda i, tg, na: (i, 0)), vec(nw), wspec(wg), wspec(wu), wspec(wd), vec(nf)],
        out_specs=pl.BlockSpec((tm, d), lambda i, tg, na: (i, 0)),
    )
    return pl.pallas_call(
        functools.partial(_moe_group_kernel, final_norm=final_norm),
        grid_spec=grid_spec,
        out_shape=jax.ShapeDtypeStruct((tp, d), F32),
        compiler_params=_params("arbitrary"),
        name="moe_group_experts",
    )(tile_group, n_active, xs, nw, wg, wu, wd, nf)


def _combine_kernel(dest_ref, ys_ref, o_ref, sem):
    tm = o_ref.shape[0]

    def start(r, _):
        _row_copy(ys_ref, dest_ref[0, 0, r], o_ref, r, sem).start()
        return 0

    def wait(r, _):
        _row_copy(ys_ref, 0, o_ref, 0, sem).wait()
        return 0

    lax.fori_loop(0, tm, start, 0, unroll=8)
    lax.fori_loop(0, tm, wait, 0, unroll=8)


def _combine(dest, ys, t, tm):
    d = ys.shape[1]
    return pl.pallas_call(
        _combine_kernel,
        grid=(t // tm,),
        in_specs=[pl.BlockSpec((1, 1, tm), lambda i: (i, 0, 0), memory_space=pltpu.SMEM),
                  pl.BlockSpec(memory_space=pl.ANY)],
        out_specs=pl.BlockSpec((tm, d), lambda i: (i, 0)),
        out_shape=jax.ShapeDtypeStruct((t, d), F32),
        scratch_shapes=[pltpu.SemaphoreType.DMA(())],
        compiler_params=_params("arbitrary"),
        name="moe_combine",
    )(dest, ys)


def _moe_sorted(x, nw, gates, wg, wu, wd, nf, layer, final_norm):
    t, d = x.shape
    tm = 512
    nt = t // tm
    grp, rank, cnt = _group_rank(gates, tm)
    counts = cnt[:N_GROUPS, 0].astype(jnp.int32)
    tiles = (counts + tm - 1) // tm
    tile_end = jnp.cumsum(tiles)
    offsets = (tile_end - tiles) * tm
    dest = jnp.take(offsets, grp) + rank
    n_tiles = nt + N_GROUPS
    tile_group = jnp.minimum(jnp.searchsorted(tile_end, jnp.arange(n_tiles), side="right"), N_GROUPS - 1)
    xs = _dispatch(dest, x, gates, jnp.zeros((n_tiles * tm, d + LANE), F32), tm)
    split = lambda w: w.reshape((w.shape[0], N_GROUPS, EXPERTS_PER_GROUP) + w.shape[2:])
    ys = _moe_group(tile_group.astype(jnp.int32), tile_end[-1:].astype(jnp.int32), xs, nw,
                    split(wg), split(wu), split(wd), nf, tm, layer, final_norm)
    return _combine(dest, ys, t, tm)


def _odd_kernel(x_ref, nw_ref, win_ref, lnw_ref, lnb_ref, wm_ref, bs_ref, wout_ref, *refs, emit_v):
    if emit_v:
        o_ref, v_ref, gated_ref = refs
    else:
        o_ref, gated_ref = refs
    tm = x_ref.shape[0]
    dc = lnw_ref.shape[1]
    gd = dc // H_C
    x = x_ref[...]
    hb = _rms(x, nw_ref[...]).astype(BF16)
    z = _dot(hb, win_ref[...])
    z = 0.5 * z * (1.0 + lax.erf(z * (2.0 ** -0.5)))
    v = z[:, dc:]
    mu = jnp.mean(v, axis=-1, keepdims=True)
    vc = v - mu
    vn = vc * lax.rsqrt(jnp.mean(vc * vc, axis=-1, keepdims=True) + EPS) * lnw_ref[...] + lnb_ref[...]
    if emit_v:
        v_ref[...] = vn
    vb = vn.astype(BF16)
    for ci in range(tm // CHUNK_C):
        rs = slice(ci * CHUNK_C, (ci + 1) * CHUNK_C)
        for g in range(H_C):
            cs = slice(g * gd, (g + 1) * gd)
            mixed = _dot(wm_ref[g], vb[rs, cs]) + jnp.concatenate([bs_ref[g]] * (gd // LANE), axis=1)
            gated_ref[rs, cs] = (z[rs, cs] * mixed).astype(BF16)
    o_ref[...] = x + _dot(gated_ref[...], wout_ref[...])


def _odd_mixer(x, nw, w_in, ln_w, ln_b, wm, bs, w_out, emit_v):
    t, d = x.shape
    dc = ln_w.shape[1]
    tm = _row_tile(t, 512)
    row = lambda w: pl.BlockSpec((tm, w), lambda i: (i, 0))
    full = lambda a: pl.BlockSpec(a.shape, lambda i: (0,) * a.ndim)
    out_specs = [row(d)] + ([row(dc)] if emit_v else [])
    out_shape = [jax.ShapeDtypeStruct((t, d), F32)] + ([jax.ShapeDtypeStruct((t, dc), F32)] if emit_v else [])
    return pl.pallas_call(
        functools.partial(_odd_kernel, emit_v=emit_v),
        grid=(t // tm,),
        in_specs=[row(d), full(nw), full(w_in), full(ln_w), full(ln_b), full(wm), full(bs), full(w_out)],
        out_specs=out_specs,
        out_shape=out_shape,
        scratch_shapes=[pltpu.VMEM((tm, dc), BF16)],
        compiler_params=_params("parallel"),
        name="odd_mixer",
    )(x, nw, w_in, ln_w, ln_b, wm, bs, w_out)


def _prep_even_weights(w_in, f_bias, a_log, dt_bias):
    d = w_in.shape[0]
    small = jnp.concatenate(
        [w_in[:, 1536:1544], w_in[:, 3080:3088], jnp.zeros((d, LANE - 16), w_in.dtype)], axis=1)
    w = jnp.concatenate([w_in[:, 0:512], w_in[:, 1544:3080], w_in[:, 3088:3600], small], axis=1)
    bias = jnp.zeros((1, LANE), F32).at[0, 0:8].set(f_bias).at[0, 8:12].set(dt_bias)
    alog = jnp.zeros((1, LANE), F32).at[0, 8:12].set(a_log)
    return w.astype(BF16), w_in[:, 512:1536].T.astype(BF16), bias, alog


def kernel(x_prompt, x_sample, cache_k, cache_v, cache_logf, state_delta, state_conv, page_table, norm_mix, norm_ffn, norm_final, w_in_even, w_out_even, fox_forget_bias, dn_conv_w, dn_a_log, dn_dt_bias, dn_norm_w, w_in_odd, gm_ln_w, gm_ln_b, gm_spatial_w, gm_spatial_b, w_out_odd, moe_router_group, moe_router_expert, moe_w_gate, moe_w_up, moe_w_down):
    n_p, seq, d = x_prompt.shape
    n_s, dec = x_sample.shape[:2]
    page = cache_k.shape[2]
    mix_a = H_A * D_A
    xp = x_prompt.reshape(n_p * seq, d)
    xs = x_sample.reshape(n_s * dec, d)
    row = lambda a: a[None, :]

    even_w = (row(norm_mix[0]),) + _prep_even_weights(w_in_even[0], fox_forget_bias[0], dn_a_log[0], dn_dt_bias[0])
    from_kt = lambda a, n, length: jnp.transpose(a.reshape(n, H_A, D_A, length), (0, 3, 1, 2))[None]
    w_out_e = w_out_even[0].astype(BF16)
    conv_w = jnp.zeros((8, dn_conv_w.shape[2]), F32).at[:CONV_W].set(dn_conv_w[0])
    onw = row(dn_norm_w[0])

    def delta(conv3, prefix, small3, gz3, s0, valid, nb):
        wv, wk, qd, kd, qk, gl = _delta_prep(conv3, prefix, conv_w, small3, valid)
        return _delta_scan(wv, wk, qd, kd, qk, gl, s0, gz3, onw, nb)

    q, kt, vt, c, gz, s = _even_in(xp, *even_w, n_p, seq)
    logf_p = s[:, :H_A].reshape(n_p, seq, H_A)
    csum = _cumsum_lanes(jnp.swapaxes(logf_p, 1, 2).reshape(n_p * H_A, seq)).reshape(n_p, H_A, seq)
    as3 = lambda a: a.reshape(n_p, seq, a.shape[-1])
    oa_p = _fox_prompt(as3(q), kt, vt, csum)
    conv_p = as3(c)
    ob_p, sd_p = delta(conv_p, jnp.zeros((n_p, 8, conv_p.shape[-1]), F32), as3(s), as3(gz),
                       jnp.zeros((n_p, H_B, DK_B, DK_B), F32), seq, 4)
    xp = _even_out(xp, oa_p.reshape(n_p * seq, mix_a), ob_p.reshape(n_p * seq, -1), w_out_e)
    k_p = from_kt(kt, n_p, seq)
    v_p = from_kt(vt, n_p, seq)
    conv_state_p = conv_p[:, seq - (CONV_W - 1):][None]

    q, kt, vt, c, gz, s = _even_in(xs, *even_w, 1, n_s * dec)
    as3 = lambda a: a.reshape(n_s, dec, a.shape[-1])
    padrows = lambda a, n: jnp.pad(a, ((0, 0), (0, n - a.shape[1]), (0, 0)))
    logf_s = s[:, :H_A].reshape(n_s, dec, H_A)
    pad_keys = lambda a: jnp.pad(a, ((0, 0), (0, 0), (0, page - dec)))
    per_seq = lambda a: pad_keys(jnp.swapaxes(a[0].reshape(mix_a, n_s, dec), 0, 1))
    oa_s = _fox_sample(page_table, as3(q), per_seq(kt), per_seq(vt), pad_keys(jnp.swapaxes(logf_s, 1, 2)),
                       jnp.transpose(cache_k[0], (0, 2, 3, 1)), jnp.transpose(cache_v[0], (0, 2, 3, 1)),
                       jnp.swapaxes(cache_logf[0], 1, 2))
    conv_s = as3(c)
    lp = DELTA_CHUNK
    prefix_s = jnp.pad(state_conv[0], ((0, 0), (8 - (CONV_W - 1), 0), (0, 0)))
    ob_s, sd_s = delta(padrows(conv_s, lp), prefix_s, padrows(as3(s), lp), padrows(as3(gz), lp),
                       state_delta[0], dec, 8)
    xs = _even_out(xs, oa_s.reshape(n_s * dec, mix_a).astype(BF16), ob_s[:, :dec].reshape(n_s * dec, -1), w_out_e)
    k_s = kt[0].T.reshape(1, n_s, dec, H_A, D_A)
    v_s = vt[0].T.reshape(1, n_s, dec, H_A, D_A)
    conv_state_s = jnp.concatenate([state_conv[0], conv_s], axis=1)[:, dec:][None]

    def moe(x, layer, final):
        wr = jnp.concatenate([moe_router_expert[layer], moe_router_group[layer],
                              jnp.zeros((d, LANE - N_EXPERTS - N_GROUPS), F32)], axis=1)
        wr_hi = wr.astype(BF16)
        wr = jnp.stack([wr_hi, (wr - wr_hi.astype(F32)).astype(BF16)])
        gates = _router(x, row(norm_ffn[layer]), wr)
        experts = _moe_sorted if x.shape[0] >= 4096 else _moe_dense
        return experts(x, row(norm_ffn[layer]), gates, wg, wu, wd, row(norm_final), layer, final)

    wg, wu, wd = moe_w_gate.astype(BF16), moe_w_up.astype(BF16), moe_w_down.astype(BF16)
    xp = moe(xp, 0, False)
    xs = moe(xs, 0, False)

    idx = jnp.arange(CHUNK_C)
    w_m = jnp.where(idx[:, None] >= idx[None, :], gm_spatial_w[0], 0.0)
    b_s = gm_spatial_b[0]
    lanes = lambda b: jnp.broadcast_to(b[:, :, None], b.shape + (LANE,))
    reps = CHUNK_C // dec
    assert n_s * dec == CHUNK_C, "sample group must fill exactly one 128-row mixing tile"
    w_m_s = (jnp.eye(reps, dtype=F32)[None, :, None, :, None] * w_m[:, None, :dec, None, :dec]).reshape(H_C, CHUNK_C, CHUNK_C)
    b_s_s = jnp.tile(b_s[:, :dec], (1, reps))
    odd_w = (row(norm_mix[1]), w_in_odd[0].astype(BF16), row(gm_ln_w[0]), row(gm_ln_b[0]))
    w_out_o = w_out_odd[0].astype(BF16)
    (xp,) = _odd_mixer(xp, *odd_w, w_m.astype(BF16), lanes(b_s), w_out_o, False)
    xs, v_rows = _odd_mixer(xs, *odd_w, w_m_s.astype(BF16), lanes(b_s_s), w_out_o, True)

    y_p = moe(xp, 1, True).reshape(n_p, seq, d)
    y_s = moe(xs, 1, True).reshape(n_s, dec, d)
    return (y_p, y_s, k_p, v_p, logf_p[None], sd_p[None], conv_state_p,
            k_s, v_s, logf_s[None], sd_s[None], conv_state_s, v_rows.reshape(1, n_s, dec, -1))
```

```python
import functools

import jax
import jax.numpy as jnp
from jax import lax
from jax.experimental import pallas as pl
from jax.experimental.pallas import tpu as pltpu

F32 = jnp.float32
BF16 = jnp.bfloat16
EPS = 1e-6
LANE = 128
VMEM_LIMIT = 56 * 1024 * 1024
HIGHEST = lax.Precision.HIGHEST

H_A, D_A = 8, 64
H_B, DK_B = 4, 128
CONV_W = 4
DELTA_CHUNK = 64
N_GROUPS, EXPERTS_PER_GROUP = 4, 8
N_EXPERTS = N_GROUPS * EXPERTS_PER_GROUP
CHUNK_C = 128
H_C = 8


def _params(*sem):
    return pltpu.CompilerParams(dimension_semantics=sem, vmem_limit_bytes=VMEM_LIMIT)


def _row_tile(t, pref):
    return pref if t % pref == 0 else t


def _rms(x, w):
    return x * lax.rsqrt(jnp.mean(x * x, axis=-1, keepdims=True) + EPS) * w


def _softplus_tail(z):
    return jnp.log1p(jnp.exp(-jnp.abs(z)))


def _even_in_kernel(x_ref, nw_ref, w_ref, wkv_ref, bias_ref, alog_ref,
                    q_ref, k_ref, v_ref, c_ref, gz_ref, s_ref):
    hb = _rms(x_ref[...], nw_ref[...]).astype(BF16)

    def proj(lo, hi):
        return jnp.dot(hb, w_ref[:, lo:hi], preferred_element_type=F32)

    q_ref[...] = proj(0, 512)
    c_ref[...] = proj(512, 2048)
    gz_ref[...] = proj(2048, 2560)
    k_ref[0] = lax.dot_general(wkv_ref[:512, :], hb, (((1,), (1,)), ((), ())), preferred_element_type=F32)
    v_ref[0] = lax.dot_general(wkv_ref[512:, :], hb, (((1,), (1,)), ((), ())), preferred_element_type=F32)
    z = proj(2560, 2688) + bias_ref[...]
    tail = _softplus_tail(z)
    logf = jnp.minimum(z, 0.0) - tail
    g = -jnp.exp(alog_ref[...]) * (jnp.maximum(z, 0.0) + tail)
    beta = 1.0 / (1.0 + jnp.exp(-z))
    lane = lax.broadcasted_iota(jnp.int32, z.shape, 1)
    s_ref[...] = jnp.where(lane < 8, logf, jnp.where(lane < 12, g, jnp.where(lane < 16, beta, 0.0)))


def _even_in(x, nw, w, wkv, bias, alog, n, length):
    t, d = x.shape
    tm = _row_tile(length, 512)
    per_seq = length // tm
    mix = wkv.shape[0] // 2
    row = lambda wd: pl.BlockSpec((tm, wd), lambda i: (i, 0))
    full = lambda a: pl.BlockSpec(a.shape, lambda i: (0, 0))
    kt = pl.BlockSpec((1, mix, tm), lambda i: (i // per_seq, 0, i % per_seq))
    rows = lambda wd: jax.ShapeDtypeStruct((t, wd), F32)
    kts = jax.ShapeDtypeStruct((n, mix, length), F32)
    return pl.pallas_call(
        _even_in_kernel,
        grid=(t // tm,),
        in_specs=[row(d), full(nw), full(w), full(wkv), full(bias), full(alog)],
        out_specs=[row(mix), kt, kt, row(3 * mix), row(mix), row(LANE)],
        out_shape=[rows(mix), kts, kts, rows(3 * mix), rows(mix), rows(LANE)],
        compiler_params=_params("parallel"),
        name="even_in_proj",
    )(x, nw, w, wkv, bias, alog)


def _cumsum_kernel(x_ref, o_ref):
    rows, length = x_ref.shape
    r = lax.broadcasted_iota(jnp.int32, (LANE, LANE), 0)
    c = lax.broadcasted_iota(jnp.int32, (LANE, LANE), 1)
    upper = (r <= c).astype(F32)
    carry = jnp.zeros((rows, 1), F32)
    for b in range(length // LANE):
        blk = jnp.dot(x_ref[:, b * LANE:(b + 1) * LANE], upper,
                      preferred_element_type=F32, precision=HIGHEST) + carry
        o_ref[:, b * LANE:(b + 1) * LANE] = blk
        carry = blk[:, LANE - 1:LANE]


def _cumsum_lanes(x):
    rows, length = x.shape
    tr = 8
    return pl.pallas_call(
        _cumsum_kernel,
        grid=(rows // tr,),
        in_specs=[pl.BlockSpec((tr, length), lambda i: (i, 0))],
        out_specs=pl.BlockSpec((tr, length), lambda i: (i, 0)),
        out_shape=jax.ShapeDtypeStruct((rows, length), F32),
        compiler_params=_params("parallel"),
        name="logf_cumsum",
    )(x)


def _fox_prompt_kernel(q_ref, k_ref, v_ref, c_ref, o_ref, *, tq):
    qi = pl.program_id(2)
    hp = pl.program_id(1)
    scale = D_A ** -0.5
    lane = lax.broadcasted_iota(jnp.int32, (tq, LANE), 1)
    first = lane < D_A
    q = q_ref[0] * scale
    qs = [jnp.where(first, q, 0.0).astype(BF16), jnp.where(first, 0.0, q).astype(BF16)]
    q0 = pl.multiple_of(qi * tq, tq)

    def crow(h, start):
        return c_ref[0, pl.ds(2 * hp + h, 1), pl.ds(start, tq)]

    cq = [crow(h, q0)[:, 0:1] for h in range(2)]

    def block(j0, carry, mask):
        kb = k_ref[0, :, pl.ds(j0, tq)].astype(BF16)
        vb = v_ref[0, :, pl.ds(j0, tq)].astype(BF16)
        hh = range(2)
        s = [_dot(qs[h], kb) + (cq[h] - crow(h, j0)) for h in hh]
        if mask is not None:
            s = [jnp.where(mask, a, -jnp.inf) for a in s]
        m_new = [jnp.maximum(carry[h][0], jnp.max(s[h], axis=-1, keepdims=True)) for h in hh]
        alpha = [jnp.exp(carry[h][0] - m_new[h]) for h in hh]
        p = [jnp.exp(s[h] - m_new[h]) for h in hh]
        l = [alpha[h] * carry[h][1] + jnp.sum(p[h], axis=-1, keepdims=True) for h in hh]
        pv = [_dot_nt(p[h].astype(BF16), vb) for h in hh]
        acc = [alpha[h] * carry[h][2] + pv[h] for h in hh]
        return tuple((m_new[h], l[h], acc[h]) for h in hh)

    def body(j, carry):
        return block(pl.multiple_of(j * tq, tq), carry, None)

    init = (jnp.full((tq, 1), -jnp.inf, F32), jnp.zeros((tq, 1), F32), jnp.zeros((tq, LANE), F32))
    carry = lax.fori_loop(0, qi, body, (init, init))
    r = lax.broadcasted_iota(jnp.int32, (tq, tq), 0)
    cidx = lax.broadcasted_iota(jnp.int32, (tq, tq), 1)
    (_, l0, a0), (_, l1, a1) = block(q0, carry, cidx <= r)
    o_ref[0] = jnp.where(first, a0 / l0, a1 / l1).astype(o_ref.dtype)


def _fox_prompt(q, k, v, c):
    n, length, width = q.shape
    tq = 512 if length % 512 == 0 else length
    grid = (n, width // LANE, length // tq)
    return pl.pallas_call(
        functools.partial(_fox_prompt_kernel, tq=tq),
        grid=grid,
        in_specs=[
            pl.BlockSpec((1, tq, LANE), lambda b, h, i: (b, i, h)),
            pl.BlockSpec((1, LANE, length), lambda b, h, i: (b, h, 0)),
            pl.BlockSpec((1, LANE, length), lambda b, h, i: (b, h, 0)),
            pl.BlockSpec((1, H_A, length), lambda b, h, i: (b, 0, 0)),
        ],
        out_specs=pl.BlockSpec((1, tq, LANE), lambda b, h, i: (b, i, h)),
        out_shape=jax.ShapeDtypeStruct((n, length, width), BF16),
        compiler_params=_params("parallel", "parallel", "arbitrary"),
        name="fox_prompt",
    )(q, k, v, c)


def _even_out_kernel(x_ref, a_ref, b_ref, w_ref, o_ref):
    half = a_ref.shape[1]
    acc = jnp.dot(a_ref[...], w_ref[:half, :], preferred_element_type=F32)
    acc += jnp.dot(b_ref[...], w_ref[half:, :], preferred_element_type=F32)
    o_ref[...] = x_ref[...] + acc


def _even_out(x, oa, ob, w):
    t, d = x.shape
    tm = _row_tile(t, 512)
    row = lambda a: pl.BlockSpec((tm, a.shape[1]), lambda i: (i, 0))
    return pl.pallas_call(
        _even_out_kernel,
        grid=(t // tm,),
        in_specs=[row(x), row(oa), row(ob), pl.BlockSpec(w.shape, lambda i: (0, 0))],
        out_specs=row(x),
        out_shape=jax.ShapeDtypeStruct((t, d), F32),
        compiler_params=_params("parallel"),
        name="even_out_proj",
    )(x, oa, ob, w)


def _split3(x):
    hi = x.astype(BF16)
    r1 = x - hi.astype(F32)
    mid = r1.astype(BF16)
    lo = (r1 - mid.astype(F32)).astype(BF16)
    return hi, mid, lo


def _dot(a, b):
    return jnp.dot(a, b, preferred_element_type=F32)


def _dot_nt(a, b):
    return lax.dot_general(a, b, (((1,), (1,)), ((), ())), preferred_element_type=F32)


def _dot_tn(a, b):
    return lax.dot_general(a, b, (((0,), (0,)), ((), ())), preferred_element_type=F32)


def _silu(x):
    return x / (1.0 + jnp.exp(-x))


def _delta_prep_kernel(x_ref, xprev_ref, pre_ref, cw_ref, sm_ref,
                       wv_ref, wk_ref, qd_ref, kd_ref, qk_ref, gl_ref, *, ta, valid):
    i = pl.program_id(1)
    c = DELTA_CHUNK
    dk = DK_B
    rr = H_B * c
    x = x_ref[0]
    prev = jnp.where(i == 0, pre_ref[0], xprev_ref[0])
    xcat = jnp.concatenate([prev, x], axis=0)
    conv = x * cw_ref[CONV_W - 1:CONV_W, :]
    for s in range(1, CONV_W):
        conv += pltpu.roll(xcat, s, axis=0)[8:] * cw_ref[CONV_W - 1 - s:CONV_W - s, :]
    act = _silu(conv)
    sm = sm_ref[0]

    ri = lax.broadcasted_iota(jnp.int32, (rr, rr), 0)
    ci = lax.broadcasted_iota(jnp.int32, (rr, rr), 1)
    same_head = (ri // c) == (ci // c)
    incl = same_head & (ri >= ci)
    strict = same_head & (ri > ci)
    tri = incl.astype(BF16)
    lane = lax.broadcasted_iota(jnp.int32, (c, LANE), 1)

    qn, kn, va = [], [], []
    for h in range(H_B):
        qa = act[:, h * dk:(h + 1) * dk]
        ka = act[:, (H_B + h) * dk:(H_B + h + 1) * dk]
        qn.append(qa * lax.rsqrt(jnp.sum(qa * qa, axis=-1, keepdims=True) + EPS) * (dk ** -0.5))
        kn.append(ka * lax.rsqrt(jnp.sum(ka * ka, axis=-1, keepdims=True) + EPS))
        va.append(act[:, (2 * H_B + h) * dk:(2 * H_B + h + 1) * dk])

    chunks = range(ta // c)
    stack = lambda parts: jnp.concatenate(parts, axis=0)
    q, k, v, b_col, gcb = [], [], [], [], []
    for s in chunks:
        r0 = s * c
        live = lax.broadcasted_iota(jnp.int32, (c, 1), 0) + (i * ta + r0) < valid
        smc = sm[r0:r0 + c]
        pick = lambda ln: jnp.where(live, jnp.sum(jnp.where(lane == ln, smc, 0.0), axis=-1, keepdims=True), 0.0)
        q.append(stack([a[r0:r0 + c] for a in qn]))
        k.append(stack([a[r0:r0 + c] for a in kn]))
        v.append(stack([a[r0:r0 + c] for a in va]))
        b_col.append(stack([pick(12 + h) for h in range(H_B)]))
        g_stack = stack([jnp.broadcast_to(pick(8 + h), (c, LANE)) for h in range(H_B)])
        gcb.append(sum(_dot(tri, p) for p in _split3(g_stack)))

    wide = [jnp.concatenate([g, g], axis=1) for g in gcb]
    decay = [jnp.exp(jnp.where(incl, w - w.T, -jnp.inf)) for w in wide]
    eg = [jnp.exp(g) for g in gcb]
    kb = [a * b for a, b in zip(k, b_col)]
    kbf = [a.astype(BF16) for a in k]
    a_mat = [jnp.where(strict, _dot_nt(a.astype(BF16), b) * d, 0.0) for a, b, d in zip(kb, kbf, decay)]
    qk = [jnp.where(incl, _dot_nt(a.astype(BF16), b) * d, 0.0) for a, b, d in zip(q, kbf, decay)]
    y = [-a for a in a_mat]
    p = list(y)
    for _ in range(5):
        yb = [a.astype(BF16) for a in y]
        y = [_dot(a, a) for a in yb]
        p = [a + b + _dot(a.astype(BF16), b.astype(BF16)) for a, b in zip(p, y)]
    rhs = [jnp.concatenate([a * b, kbb * e], axis=1) for a, b, kbb, e in zip(v, b_col, kb, eg)]
    w = [r + _dot(a.astype(BF16), r.astype(BF16)) for a, r in zip(p, rhs)]

    for s in chunks:
        r0 = s * c
        for h in range(H_B):
            hr = slice(h * c, (h + 1) * c)
            hs = slice(h * dk, (h + 1) * dk)
            g_last = gcb[s][h * c + c - 1:(h + 1) * c, :]
            wv_ref[0, r0:r0 + c, hs] = w[s][hr, :dk]
            wk_ref[0, r0:r0 + c, hs] = w[s][hr, dk:].astype(BF16)
            qd_ref[0, r0:r0 + c, hs] = (q[s][hr] * eg[s][hr]).astype(BF16)
            kd_ref[0, r0:r0 + c, hs] = (k[s][hr] * jnp.exp(g_last - gcb[s][hr])).astype(BF16)
            qk_ref[0, h, r0:r0 + c, :] = qk[s][hr, h * c:(h + 1) * c].astype(BF16)
            gl_ref[0, s, h:h + 1, :] = jnp.exp(g_last)


def _delta_prep(conv_in, prefix, conv_w, small, valid):
    n, lp, cd = conv_in.shape
    ta = LANE if lp % LANE == 0 else lp
    nt = lp // ta
    tb = ta // 8
    f = lambda dt, w: jax.ShapeDtypeStruct((n, lp, w), dt)
    blk = lambda w: pl.BlockSpec((1, ta, w), lambda b, i: (b, i, 0))
    width = H_B * DK_B
    return pl.pallas_call(
        functools.partial(_delta_prep_kernel, ta=ta, valid=valid),
        grid=(n, nt),
        in_specs=[
            blk(cd),
            pl.BlockSpec((1, 8, cd), lambda b, i: (b, jnp.maximum(i * tb - 1, 0), 0)),
            pl.BlockSpec((1, 8, cd), lambda b, i: (b, 0, 0)),
            pl.BlockSpec((8, cd), lambda b, i: (0, 0)),
            blk(LANE),
        ],
        out_specs=[
            blk(width), blk(width), blk(width), blk(width),
            pl.BlockSpec((1, H_B, ta, DELTA_CHUNK), lambda b, i: (b, 0, i, 0)),
            pl.BlockSpec((1, ta // DELTA_CHUNK, H_B, LANE), lambda b, i: (b, i, 0, 0)),
        ],
        out_shape=[
            f(F32, width), f(BF16, width), f(BF16, width), f(BF16, width),
            jax.ShapeDtypeStruct((n, H_B, lp, DELTA_CHUNK), BF16),
            jax.ShapeDtypeStruct((n, lp // DELTA_CHUNK, H_B, LANE), F32),
        ],
        compiler_params=_params("parallel", "parallel"),
        name="delta_prep",
    )(conv_in, conv_in, prefix, conv_w, small)


def _delta_scan_kernel(wv_ref, wk_ref, qd_ref, kd_ref, qk_ref, gl_ref, s0_ref, gz_ref, onw_ref,
                       o_ref, s_ref, *, nb, n_chunks):
    c = DELTA_CHUNK
    dk = DK_B

    @pl.when(pl.program_id(1) == 0)
    def _():
        s_ref[...] = s0_ref[...]

    onw = onw_ref[...]

    chains = [(b, h) for b in range(nb) for h in range(H_B)]
    cols = lambda h: slice(h * dk, (h + 1) * dk)

    def body(ci, _):
        r0 = pl.multiple_of(ci * c, c)
        rows = pl.ds(r0, c)
        state = [s_ref[b, h] for b, h in chains]
        prod = [_dot(jnp.concatenate([wk_ref[b, rows, cols(h)], qd_ref[b, rows, cols(h)]], axis=0), s.astype(BF16))
                for (b, h), s in zip(chains, state)]
        vb = [(wv_ref[b, rows, cols(h)] - p[:c]).astype(BF16) for (b, h), p in zip(chains, prod)]
        o = [p[c:] + _dot(qk_ref[b, h, rows, :], v) for (b, h), p, v in zip(chains, prod, vb)]
        new = [s * gl_ref[b, pl.ds(ci, 1), h, :] + _dot_tn(kd_ref[b, rows, cols(h)], v)
               for (b, h), s, v in zip(chains, state, vb)]
        for (b, h), s, oo in zip(chains, new, o):
            s_ref[b, h] = s
            on = oo * lax.rsqrt(jnp.mean(oo * oo, axis=-1, keepdims=True) + EPS) * onw
            o_ref[b, rows, cols(h)] = (on * _silu(gz_ref[b, rows, cols(h)])).astype(o_ref.dtype)
        return 0

    lax.fori_loop(0, n_chunks, body, 0)


def _delta_scan(wv, wk, qd, kd, qk, gl, s0, gz, onw, nb):
    n, lp, width = wv.shape
    tl = _row_tile(lp, 512)
    n_chunks = tl // DELTA_CHUNK
    seq = pl.BlockSpec((nb, tl, width), lambda i, t: (i, t, 0))
    state = pl.BlockSpec((nb,) + s0.shape[1:], lambda i, t: (i, 0, 0, 0))
    return pl.pallas_call(
        functools.partial(_delta_scan_kernel, nb=nb, n_chunks=n_chunks),
        grid=(n // nb, lp // tl),
        in_specs=[seq, seq, seq, seq,
                  pl.BlockSpec((nb, H_B, tl, DELTA_CHUNK), lambda i, t: (i, 0, t, 0)),
                  pl.BlockSpec((nb, n_chunks, H_B, LANE), lambda i, t: (i, t, 0, 0)),
                  state, seq, pl.BlockSpec(onw.shape, lambda i, t: (0, 0))],
        out_specs=[seq, state],
        out_shape=[jax.ShapeDtypeStruct((n, lp, width), BF16), jax.ShapeDtypeStruct(s0.shape, F32)],
        compiler_params=_params("parallel", "arbitrary"),
        name="delta_scan",
    )(wv, wk, qd, kd, qk, gl, s0, gz, onw)


def _fox_sample_kernel(pt_ref, q_ref, kn_ref, vn_ref, lfn_ref, *refs, pages_per_step, n_q):
    del pt_ref
    pps = pages_per_step
    kc_refs, vc_refs, lfc_refs = refs[:pps], refs[pps:2 * pps], refs[2 * pps:3 * pps]
    o_ref, qbd_ref, m_ref, l_ref, acc_ref, carry_ref = refs[3 * pps:]
    j = pl.program_id(1)
    rows = n_q * H_A
    page = lfn_ref.shape[2]
    width = H_A * D_A
    ri = lax.broadcasted_iota(jnp.int32, (page, page), 0)
    ci = lax.broadcasted_iota(jnp.int32, (page, page), 1)

    def attend(blocks):
        qbd = qbd_ref[...]
        ss = []
        for kt, _, bias8, mask in blocks:
            s = _dot(qbd, kt.astype(BF16)) + jnp.concatenate([bias8] * n_q, axis=0)
            ss.append(s if mask is None else jnp.where(mask, s, -jnp.inf))
        m_old = m_ref[...]
        m_new = m_old
        for s in ss:
            m_new = jnp.maximum(m_new, jnp.max(s, axis=-1, keepdims=True))
        alpha = jnp.exp(m_old - m_new)
        ps = [jnp.exp(s - m_new) for s in ss]
        sums = [jnp.sum(p, axis=-1, keepdims=True) for p in ps]
        pvs = [_dot_nt(p.astype(BF16), vt.astype(BF16)) for p, (_, vt, _, _) in zip(ps, blocks)]
        l_ref[...] = alpha * l_ref[...] + sum(sums)
        acc_ref[...] = alpha * acc_ref[...] + sum(pvs)
        m_ref[...] = m_new

    @pl.when(j == 0)
    def _():
        q = q_ref[0] * (D_A ** -0.5)
        qrep = jnp.concatenate([jnp.broadcast_to(q[t:t + 1], (H_A, width)) for t in range(n_q)], axis=0)
        r = lax.broadcasted_iota(jnp.int32, qrep.shape, 0)
        ln = lax.broadcasted_iota(jnp.int32, qrep.shape, 1)
        qbd_ref[...] = jnp.where(ln // D_A == r % H_A, qrep, 0.0).astype(BF16)
        m_ref[...] = jnp.full(m_ref.shape, -jnp.inf, F32)
        l_ref[...] = jnp.zeros(l_ref.shape, F32)
        acc_ref[...] = jnp.zeros(acc_ref.shape, F32)
        carry_ref[...] = jnp.zeros(carry_ref.shape, F32)
        upper = (ri <= ci).astype(BF16)
        csum = sum(_dot(p, upper) for p in _split3(lfn_ref[0]))
        r2 = lax.broadcasted_iota(jnp.int32, (rows, page), 0)
        c2 = lax.broadcasted_iota(jnp.int32, (rows, page), 1)
        attend([(kn_ref[0], vn_ref[0], -csum, c2 <= r2 // H_A)])

    @pl.when(j > 0)
    def _():
        after = (ri > ci).astype(BF16)
        carry = carry_ref[...]
        lfs = [r[0] for r in lfc_refs]
        pieces = [_split3(lf) for lf in lfs]
        suffix = [sum(_dot(p, after) for p in ps) for ps in pieces]
        totals = [jnp.sum(lf, axis=-1, keepdims=True) for lf in lfs]
        blocks = []
        for kc_ref, vc_ref, sfx, tot in zip(kc_refs, vc_refs, suffix, totals):
            blocks.append((kc_ref[0].reshape(width, page), vc_ref[0].reshape(width, page), carry + sfx, None))
            carry = carry + tot
        attend(blocks)
        carry_ref[...] = carry

    @pl.when(j == pl.num_programs(1) - 1)
    def _():
        out = acc_ref[...] / l_ref[...]
        r = lax.broadcasted_iota(jnp.int32, out.shape, 0)
        ln = lax.broadcasted_iota(jnp.int32, out.shape, 1)
        out = jnp.where(ln // D_A == r % H_A, out, 0.0)
        o_ref[0] = jnp.sum(out.reshape(n_q, H_A, width), axis=1)


def _fox_sample(page_table, q, kn, vn, lfn, cache_k, cache_v, cache_lf):
    b, n_q, width = q.shape
    n_pages = page_table.shape[1]
    page = cache_k.shape[3]
    rows = n_q * H_A
    pps = next(p for p in (16, 8, 1) if n_pages % p == 0)
    cur = lambda i, j, pt: (i, 0, 0)

    def past(p, nd):
        def index_map(i, j, pt):
            return (pt[i, n_pages - 1 - ((jnp.maximum(j, 1) - 1) * pps + p)],) + (0,) * nd
        return index_map

    kv_specs = [pl.BlockSpec((1, H_A, D_A, page), past(p, 3)) for p in range(pps)]
    lf_specs = [pl.BlockSpec((1, H_A, page), past(p, 2)) for p in range(pps)]
    grid_spec = pltpu.PrefetchScalarGridSpec(
        num_scalar_prefetch=1,
        grid=(b, n_pages // pps + 1),
        in_specs=[
            pl.BlockSpec((1, n_q, width), cur),
            pl.BlockSpec((1, width, page), cur),
            pl.BlockSpec((1, width, page), cur),
            pl.BlockSpec((1, H_A, page), cur),
        ] + kv_specs + kv_specs + lf_specs,
        out_specs=pl.BlockSpec((1, n_q, width), cur),
        scratch_shapes=[
            pltpu.VMEM((rows, width), BF16),
            pltpu.VMEM((rows, 1), F32),
            pltpu.VMEM((rows, 1), F32),
            pltpu.VMEM((rows, width), F32),
            pltpu.VMEM((H_A, 1), F32),
        ],
    )
    return pl.pallas_call(
        functools.partial(_fox_sample_kernel, pages_per_step=pps, n_q=n_q),
        grid_spec=grid_spec,
        out_shape=jax.ShapeDtypeStruct((b, n_q, width), F32),
        compiler_params=_params("parallel", "arbitrary"),
        name="fox_sample",
    )(page_table, q, kn, vn, lfn, *([cache_k] * pps), *([cache_v] * pps), *([cache_lf] * pps))


def _router_kernel(x_ref, nw_ref, wr_ref, g_ref):
    h = _rms(x_ref[...], nw_ref[...])
    h_hi = h.astype(BF16)
    h_lo = (h - h_hi.astype(F32)).astype(BF16)
    logits = _dot(h_hi, wr_ref[0]) + (_dot(h_lo, wr_ref[0]) + _dot(h_hi, wr_ref[1]))
    lane = lax.broadcasted_iota(jnp.int32, logits.shape, 1)
    lanef = lane.astype(F32)
    neg = -jnp.inf
    is_group = (lane >= N_EXPERTS) & (lane < N_EXPERTS + N_GROUPS)
    gl = jnp.where(is_group, logits, neg)
    gmax = jnp.max(gl, axis=-1, keepdims=True)
    g_sel = jnp.min(jnp.where(gl == gmax, lanef - N_EXPERTS, 1e9), axis=-1, keepdims=True)
    p_sel = 1.0 / jnp.sum(jnp.where(is_group, jnp.exp(logits - gmax), 0.0), axis=-1, keepdims=True)
    in_sel = (lane < N_EXPERTS) & ((lane // EXPERTS_PER_GROUP).astype(F32) == g_sel)
    el = jnp.where(in_sel, logits, neg)
    m1 = jnp.max(el, axis=-1, keepdims=True)
    i1 = jnp.min(jnp.where(el == m1, lanef, 1e9), axis=-1, keepdims=True)
    el2 = jnp.where(lanef == i1, neg, el)
    m2 = jnp.max(el2, axis=-1, keepdims=True)
    i2 = jnp.min(jnp.where(el2 == m2, lanef, 1e9), axis=-1, keepdims=True)
    e2 = jnp.exp(m2 - m1)
    w1 = p_sel / (1.0 + e2)
    d = x_ref.shape[1]
    g_ref[:, :d] = x_ref[...]
    g_ref[:, d:] = jnp.where(lanef == i1, w1, jnp.where(lanef == i2, w1 * e2, jnp.where(lane == N_EXPERTS, g_sel, 0.0)))


def _router(x, nw, wr):
    t, d = x.shape
    tm = _row_tile(t, 512)
    return pl.pallas_call(
        _router_kernel,
        grid=(t // tm,),
        in_specs=[pl.BlockSpec((tm, d), lambda i: (i, 0)), pl.BlockSpec(nw.shape, lambda i: (0, 0)),
                  pl.BlockSpec(wr.shape, lambda i: (0, 0, 0))],
        out_specs=pl.BlockSpec((tm, d + LANE), lambda i: (i, 0)),
        out_shape=jax.ShapeDtypeStruct((t, d + LANE), F32),
        compiler_params=_params("parallel"),
        name="moe_router",
    )(x, nw, wr)


def _moe_kernel(x_ref, nw_ref, g_ref, wg_ref, wu_ref, wd_ref, nf_ref, o_ref, hb_ref, acc_ref, *, final_norm):
    e = pl.program_id(1)

    @pl.when(e == 0)
    def _():
        hb_ref[...] = _rms(x_ref[...], nw_ref[...]).astype(BF16)
        acc_ref[...] = jnp.zeros(acc_ref.shape, F32)

    gates = g_ref[...]
    lane = lax.broadcasted_iota(jnp.int32, gates.shape, 1)
    gcol = jnp.sum(jnp.where(lane == e, gates, 0.0), axis=-1, keepdims=True)
    hb = hb_ref[...]
    act = _silu(_dot(hb, wg_ref[0, 0])) * _dot(hb, wu_ref[0, 0]) * gcol
    acc_ref[...] += _dot(act.astype(BF16), wd_ref[0, 0])

    @pl.when(e == pl.num_programs(1) - 1)
    def _():
        y = x_ref[...] + acc_ref[...]
        o_ref[...] = _rms(y, nf_ref[...]) if final_norm else y


def _moe_dense(x, nw, gates, wg, wu, wd, nf, layer, final_norm):
    t = x.shape[0]
    d = x.shape[1] - LANE
    tm = _row_tile(t, 1024)
    _, n_e, _, f = wg.shape
    return pl.pallas_call(
        functools.partial(_moe_kernel, final_norm=final_norm),
        grid=(t // tm, n_e),
        in_specs=[
            pl.BlockSpec((tm, d), lambda i, e: (i, 0)),
            pl.BlockSpec(nw.shape, lambda i, e: (0, 0)),
            pl.BlockSpec((tm, LANE), lambda i, e: (i, d // LANE)),
            pl.BlockSpec((1, 1, d, f), lambda i, e: (layer, e, 0, 0)),
            pl.BlockSpec((1, 1, d, f), lambda i, e: (layer, e, 0, 0)),
            pl.BlockSpec((1, 1, f, d), lambda i, e: (layer, e, 0, 0)),
            pl.BlockSpec(nf.shape, lambda i, e: (0, 0)),
        ],
        out_specs=pl.BlockSpec((tm, d), lambda i, e: (i, 0)),
        out_shape=jax.ShapeDtypeStruct((t, d), F32),
        scratch_shapes=[pltpu.VMEM((tm, d), BF16), pltpu.VMEM((tm, d), F32)],
        compiler_params=_params("parallel", "arbitrary"),
        name="moe_experts",
    )(x, nw, gates, wg, wu, wd, nf)


def _group_rank_kernel(g_ref, grp_ref, rank_ref, cnt_ref, run_ref):
    i = pl.program_id(0)
    tm = g_ref.shape[0]

    @pl.when(i == 0)
    def _():
        run_ref[...] = jnp.zeros(run_ref.shape, F32)

    gates = g_ref[...]
    lane = lax.broadcasted_iota(jnp.int32, gates.shape, 1)
    g_sel = jnp.sum(jnp.where(lane == N_EXPERTS, gates, 0.0), axis=-1, keepdims=True)
    onehot_t = (lane.astype(F32) == g_sel).astype(F32).T[:8]
    ri = lax.broadcasted_iota(jnp.int32, (tm, tm), 0)
    ci = lax.broadcasted_iota(jnp.int32, (tm, tm), 1)
    earlier = _dot(onehot_t.astype(BF16), (ri < ci).astype(BF16))
    run = run_ref[...]
    gidx = lax.broadcasted_iota(jnp.int32, (8, tm), 0).astype(F32)
    rank_ref[0] = jnp.sum(onehot_t * (earlier + run), axis=0, keepdims=True).astype(jnp.int32)
    grp_ref[0] = jnp.sum(onehot_t * gidx, axis=0, keepdims=True).astype(jnp.int32)
    run = run + jnp.sum(onehot_t, axis=-1, keepdims=True)
    run_ref[...] = run
    cnt_ref[...] = jnp.broadcast_to(run, cnt_ref.shape)


def _group_rank(xg, tm):
    t = xg.shape[0]
    nt = t // tm
    row = pl.BlockSpec((1, 1, tm), lambda i: (i, 0, 0))
    gate_block = xg.shape[1] // LANE - 1
    return pl.pallas_call(
        _group_rank_kernel,
        grid=(nt,),
        in_specs=[pl.BlockSpec((tm, LANE), lambda i: (i, gate_block))],
        out_specs=[row, row, pl.BlockSpec((8, LANE), lambda i: (0, 0))],
        out_shape=[jax.ShapeDtypeStruct((nt, 1, tm), jnp.int32), jax.ShapeDtypeStruct((nt, 1, tm), jnp.int32),
                   jax.ShapeDtypeStruct((8, LANE), F32)],
        scratch_shapes=[pltpu.VMEM((8, 1), F32)],
        compiler_params=_params("arbitrary"),
        name="moe_group_rank",
    )(xg)


def _row_copy(src, src_row, dst, dst_row, sem):
    return pltpu.make_async_copy(src.at[pl.ds(src_row, 1)], dst.at[pl.ds(dst_row, 1)], sem)


def _moe_fused_kernel(tg_ref, rows_ref, src_prev_ref, src_cur_ref, src_next_ref, xg_ref,
                      nw_ref, wg_ref, wu_ref, wd_ref, nf_ref, y_ref,
                      xbuf, obuf, gsem, ssem, *, tm, final_norm):
    i = pl.program_id(0)
    nt = pl.num_programs(0)
    d = y_ref.shape[1]
    slot = i % 2
    rows_at = lambda j: jnp.where((j >= 0) & (j < nt), rows_ref[jnp.clip(j, 0, nt - 1)], 0)
    r_cur, r_next, r_prev, r_prev2 = rows_at(i), rows_at(i + 1), rows_at(i - 1), rows_at(i - 2)

    def gather(src_ref, r, s):
        return _row_copy(xg_ref, src_ref[0, 0, r], xbuf.at[s], r, gsem.at[s])

    def scatter(src_ref, r, s):
        return _row_copy(obuf.at[s], r, y_ref, src_ref[0, 0, r], ssem.at[s])

    def each(count, fn):
        def body(r, _):
            fn(r)
            return 0
        lax.fori_loop(0, count, body, 0)

    def experts(between):
        x = xbuf[slot, :, :d]
        gates = xbuf[slot, :, d:]
        hb = _rms(x, nw_ref[...]).astype(BF16)
        lane = lax.broadcasted_iota(jnp.int32, gates.shape, 1)
        first = tg_ref[i] * EXPERTS_PER_GROUP
        acc = jnp.zeros(x.shape, F32)
        for e in range(EXPERTS_PER_GROUP):
            gcol = jnp.sum(jnp.where(lane == first + e, gates, 0.0), axis=-1, keepdims=True)
            hg = _dot(hb, wg_ref[0, 0, e])
            between(3 * e)
            hu = _dot(hb, wu_ref[0, 0, e])
            between(3 * e + 1)
            acc += _dot((_silu(hg) * hu * gcol).astype(BF16), wd_ref[0, 0, e])
            between(3 * e + 2)
        y = x + acc
        obuf[slot] = _rms(y, nf_ref[...]) if final_norm else y

    @pl.when(i == 0)
    def _():
        xbuf[...] = jnp.zeros(xbuf.shape, F32)
        each(r_cur, lambda r: gather(src_cur_ref, r, 0).start())

    each(r_cur, lambda r: gather(src_cur_ref, 0, slot).wait())
    each(r_prev2, lambda r: scatter(src_cur_ref, 0, slot).wait())

    steady = (r_cur == tm) & (r_next == tm) & (r_prev == tm)
    n_gaps = 3 * EXPERTS_PER_GROUP

    @pl.when(steady)
    def _():
        def between(k):
            for r in range(k * tm // n_gaps, (k + 1) * tm // n_gaps):
                gather(src_next_ref, r, 1 - slot).start()
                scatter(src_prev_ref, r, 1 - slot).start()
        experts(between)

    @pl.when(jnp.logical_not(steady))
    def _():
        each(r_next, lambda r: gather(src_next_ref, r, 1 - slot).start())
        each(r_prev, lambda r: scatter(src_prev_ref, r, 1 - slot).start())

        @pl.when(r_cur > 0)
        def _():
            experts(lambda e: None)

    @pl.when(i == nt - 1)
    def _():
        each(r_prev, lambda r: scatter(src_cur_ref, 0, 1 - slot).wait())


def _moe_fused(tile_group, tile_rows, src, xg, nw, wg, wu, wd, nf, tm, layer, final_norm):
    t, dx = xg.shape
    d = dx - LANE
    nt = src.shape[0]
    smem = lambda shift: pl.BlockSpec((1, 1, tm), lambda i, tg, rw: (jnp.clip(i + shift, 0, nt - 1), 0, 0),
                                      memory_space=pltpu.SMEM)
    wspec = lambda a: pl.BlockSpec((1, 1) + a.shape[2:], lambda i, tg, rw: (layer, tg[i], 0, 0, 0))
    vec = lambda a: pl.BlockSpec(a.shape, lambda i, tg, rw: (0, 0))
    grid_spec = pltpu.PrefetchScalarGridSpec(
        num_scalar_prefetch=2,
        grid=(nt,),
        in_specs=[smem(-1), smem(0), smem(1), pl.BlockSpec(memory_space=pl.ANY),
                  vec(nw), wspec(wg), wspec(wu), wspec(wd), vec(nf)],
        out_specs=pl.BlockSpec(memory_space=pl.ANY),
        scratch_shapes=[pltpu.VMEM((2, tm, dx), F32), pltpu.VMEM((2, tm, d), F32),
                        pltpu.SemaphoreType.DMA((2,)), pltpu.SemaphoreType.DMA((2,))],
    )
    return pl.pallas_call(
        functools.partial(_moe_fused_kernel, tm=tm, final_norm=final_norm),
        grid_spec=grid_spec,
        out_shape=jax.ShapeDtypeStruct((t, d), F32),
        compiler_params=_params("arbitrary"),
        name="moe_sorted_experts",
    )(tile_group, tile_rows, src, src, src, xg, nw, wg, wu, wd, nf)


def _moe_sorted(x, nw, gates, wg, wu, wd, nf, layer, final_norm):
    del gates
    t = x.shape[0]
    tm = 512
    grp, rank, cnt = _group_rank(x, tm)
    counts = [cnt[g, 0].astype(jnp.int32) for g in range(N_GROUPS)]
    tiles = [(c + tm - 1) // tm for c in counts]
    tile_end = [sum(tiles[:g + 1]) for g in range(N_GROUPS)]
    tile_start = [e - n for e, n in zip(tile_end, tiles)]
    dest = rank + sum(jnp.where(grp == g, tile_start[g] * tm, 0) for g in range(N_GROUPS))
    nt = t // tm + N_GROUPS
    tile = jnp.arange(nt, dtype=jnp.int32)
    tile_group = jnp.minimum(sum((tile >= e).astype(jnp.int32) for e in tile_end), N_GROUPS - 1)
    tile_rows = sum(jnp.where((tile >= s) & (tile < e), jnp.clip(c - (tile - s) * tm, 0, tm), 0)
                    for s, e, c in zip(tile_start, tile_end, counts))
    src = jnp.zeros((nt * tm,), jnp.int32).at[dest.reshape(-1)].set(jnp.arange(t, dtype=jnp.int32))
    split = lambda w: w.reshape((w.shape[0], N_GROUPS, EXPERTS_PER_GROUP) + w.shape[2:])
    return _moe_fused(tile_group, tile_rows.astype(jnp.int32), src.reshape(nt, 1, tm), x, nw,
                      split(wg), split(wu), split(wd), nf, tm, layer, final_norm)


def _odd_kernel(x_ref, nw_ref, win_ref, lnw_ref, lnb_ref, wm_ref, bs_ref, wout_ref, *refs, emit_v):
    if emit_v:
        o_ref, v_ref, gated_ref = refs
    else:
        o_ref, gated_ref = refs
    tm = x_ref.shape[0]
    dc = lnw_ref.shape[1]
    gd = dc // H_C
    x = x_ref[...]
    hb = _rms(x, nw_ref[...]).astype(BF16)
    z = _dot(hb, win_ref[...])
    z = 0.5 * z * (1.0 + lax.erf(z * (2.0 ** -0.5)))
    v = z[:, dc:]
    mu = jnp.mean(v, axis=-1, keepdims=True)
    vc = v - mu
    vn = vc * lax.rsqrt(jnp.mean(vc * vc, axis=-1, keepdims=True) + EPS) * lnw_ref[...] + lnb_ref[...]
    if emit_v:
        v_ref[...] = vn
    vb = vn.astype(BF16)
    for ci in range(tm // CHUNK_C):
        rs = slice(ci * CHUNK_C, (ci + 1) * CHUNK_C)
        for g in range(H_C):
            cs = slice(g * gd, (g + 1) * gd)
            mixed = _dot(wm_ref[g], vb[rs, cs]) + jnp.concatenate([bs_ref[g]] * (gd // LANE), axis=1)
            gated_ref[rs, cs] = (z[rs, cs] * mixed).astype(BF16)
    o_ref[...] = x + _dot(gated_ref[...], wout_ref[...])


def _odd_mixer(x, nw, w_in, ln_w, ln_b, wm, bs, w_out, emit_v):
    t, d = x.shape
    dc = ln_w.shape[1]
    tm = _row_tile(t, 512)
    row = lambda w: pl.BlockSpec((tm, w), lambda i: (i, 0))
    full = lambda a: pl.BlockSpec(a.shape, lambda i: (0,) * a.ndim)
    out_specs = [row(d)] + ([row(dc)] if emit_v else [])
    out_shape = [jax.ShapeDtypeStruct((t, d), F32)] + ([jax.ShapeDtypeStruct((t, dc), F32)] if emit_v else [])
    return pl.pallas_call(
        functools.partial(_odd_kernel, emit_v=emit_v),
        grid=(t // tm,),
        in_specs=[row(d), full(nw), full(w_in), full(ln_w), full(ln_b), full(wm), full(bs), full(w_out)],
        out_specs=out_specs,
        out_shape=out_shape,
        scratch_shapes=[pltpu.VMEM((tm, dc), BF16)],
        compiler_params=_params("parallel"),
        name="odd_mixer",
    )(x, nw, w_in, ln_w, ln_b, wm, bs, w_out)


def _prep_even_weights(w_in, f_bias, a_log, dt_bias):
    d = w_in.shape[0]
    small = jnp.concatenate(
        [w_in[:, 1536:1544], w_in[:, 3080:3088], jnp.zeros((d, LANE - 16), w_in.dtype)], axis=1)
    w = jnp.concatenate([w_in[:, 0:512], w_in[:, 1544:3080], w_in[:, 3088:3600], small], axis=1)
    bias = jnp.zeros((1, LANE), F32).at[0, 0:8].set(f_bias).at[0, 8:12].set(dt_bias)
    alog = jnp.zeros((1, LANE), F32).at[0, 8:12].set(a_log)
    return w.astype(BF16), w_in[:, 512:1536].T.astype(BF16), bias, alog


def kernel(x_prompt, x_sample, cache_k, cache_v, cache_logf, state_delta, state_conv, page_table, norm_mix, norm_ffn, norm_final, w_in_even, w_out_even, fox_forget_bias, dn_conv_w, dn_a_log, dn_dt_bias, dn_norm_w, w_in_odd, gm_ln_w, gm_ln_b, gm_spatial_w, gm_spatial_b, w_out_odd, moe_router_group, moe_router_expert, moe_w_gate, moe_w_up, moe_w_down):
    n_p, seq, d = x_prompt.shape
    n_s, dec = x_sample.shape[:2]
    page = cache_k.shape[2]
    mix_a = H_A * D_A
    xp = x_prompt.reshape(n_p * seq, d)
    xs = x_sample.reshape(n_s * dec, d)
    row = lambda a: a[None, :]

    even_w = (row(norm_mix[0]),) + _prep_even_weights(w_in_even[0], fox_forget_bias[0], dn_a_log[0], dn_dt_bias[0])
    from_kt = lambda a, n, length: jnp.transpose(a.reshape(n, H_A, D_A, length), (0, 3, 1, 2))[None]
    w_out_e = w_out_even[0].astype(BF16)
    conv_w = jnp.zeros((8, dn_conv_w.shape[2]), F32).at[:CONV_W].set(dn_conv_w[0])
    onw = row(dn_norm_w[0])

    def delta(conv3, prefix, small3, gz3, s0, valid, nb):
        wv, wk, qd, kd, qk, gl = _delta_prep(conv3, prefix, conv_w, small3, valid)
        return _delta_scan(wv, wk, qd, kd, qk, gl, s0, gz3, onw, nb)

    q, kt, vt, c, gz, s = _even_in(xp, *even_w, n_p, seq)
    logf_p = s[:, :H_A].reshape(n_p, seq, H_A)
    csum = _cumsum_lanes(jnp.swapaxes(logf_p, 1, 2).reshape(n_p * H_A, seq)).reshape(n_p, H_A, seq)
    as3 = lambda a: a.reshape(n_p, seq, a.shape[-1])
    oa_p = _fox_prompt(as3(q), kt, vt, csum)
    conv_p = as3(c)
    ob_p, sd_p = delta(conv_p, jnp.zeros((n_p, 8, conv_p.shape[-1]), F32), as3(s), as3(gz),
                       jnp.zeros((n_p, H_B, DK_B, DK_B), F32), seq, 4)
    xp = _even_out(xp, oa_p.reshape(n_p * seq, mix_a), ob_p.reshape(n_p * seq, -1), w_out_e)
    k_p = from_kt(kt, n_p, seq)
    v_p = from_kt(vt, n_p, seq)
    conv_state_p = conv_p[:, seq - (CONV_W - 1):][None]

    q, kt, vt, c, gz, s = _even_in(xs, *even_w, 1, n_s * dec)
    as3 = lambda a: a.reshape(n_s, dec, a.shape[-1])
    padrows = lambda a, n: jnp.pad(a, ((0, 0), (0, n - a.shape[1]), (0, 0)))
    logf_s = s[:, :H_A].reshape(n_s, dec, H_A)
    pad_keys = lambda a: jnp.pad(a, ((0, 0), (0, 0), (0, page - dec)))
    per_seq = lambda a: pad_keys(jnp.swapaxes(a[0].reshape(mix_a, n_s, dec), 0, 1))
    oa_s = _fox_sample(page_table, as3(q), per_seq(kt), per_seq(vt), pad_keys(jnp.swapaxes(logf_s, 1, 2)),
                       jnp.transpose(cache_k[0], (0, 2, 3, 1)), jnp.transpose(cache_v[0], (0, 2, 3, 1)),
                       jnp.swapaxes(cache_logf[0], 1, 2))
    conv_s = as3(c)
    lp = DELTA_CHUNK
    prefix_s = jnp.pad(state_conv[0], ((0, 0), (8 - (CONV_W - 1), 0), (0, 0)))
    ob_s, sd_s = delta(padrows(conv_s, lp), prefix_s, padrows(as3(s), lp), padrows(as3(gz), lp),
                       state_delta[0], dec, 8)
    xs = _even_out(xs, oa_s.reshape(n_s * dec, mix_a).astype(BF16), ob_s[:, :dec].reshape(n_s * dec, -1), w_out_e)
    k_s = kt[0].T.reshape(1, n_s, dec, H_A, D_A)
    v_s = vt[0].T.reshape(1, n_s, dec, H_A, D_A)
    conv_state_s = jnp.concatenate([state_conv[0], conv_s], axis=1)[:, dec:][None]

    def moe(x, layer, final):
        wr = jnp.concatenate([moe_router_expert[layer], moe_router_group[layer],
                              jnp.zeros((d, LANE - N_EXPERTS - N_GROUPS), F32)], axis=1)
        wr_hi = wr.astype(BF16)
        wr = jnp.stack([wr_hi, (wr - wr_hi.astype(F32)).astype(BF16)])
        xg = _router(x, row(norm_ffn[layer]), wr)
        experts = _moe_sorted if x.shape[0] >= 4096 else _moe_dense
        return experts(xg, row(norm_ffn[layer]), xg, wg, wu, wd, row(norm_final), layer, final)

    wg, wu, wd = moe_w_gate.astype(BF16), moe_w_up.astype(BF16), moe_w_down.astype(BF16)
    xp = moe(xp, 0, False)
    xs = moe(xs, 0, False)

    idx = jnp.arange(CHUNK_C)
    w_m = jnp.where(idx[:, None] >= idx[None, :], gm_spatial_w[0], 0.0)
    b_s = gm_spatial_b[0]
    lanes = lambda b: jnp.broadcast_to(b[:, :, None], b.shape + (LANE,))
    reps = CHUNK_C // dec
    assert n_s * dec == CHUNK_C, "sample group must fill exactly one 128-row mixing tile"
    w_m_s = (jnp.eye(reps, dtype=F32)[None, :, None, :, None] * w_m[:, None, :dec, None, :dec]).reshape(H_C, CHUNK_C, CHUNK_C)
    b_s_s = jnp.tile(b_s[:, :dec], (1, reps))
    odd_w = (row(norm_mix[1]), w_in_odd[0].astype(BF16), row(gm_ln_w[0]), row(gm_ln_b[0]))
    w_out_o = w_out_odd[0].astype(BF16)
    (xp,) = _odd_mixer(xp, *odd_w, w_m.astype(BF16), lanes(b_s), w_out_o, False)
    xs, v_rows = _odd_mixer(xs, *odd_w, w_m_s.astype(BF16), lanes(b_s_s), w_out_o, True)

    y_p = moe(xp, 1, True).reshape(n_p, seq, d)
    y_s = moe(xs, 1, True).reshape(n_s, dec, d)
    return (y_p, y_s, k_p, v_p, logf_p[None], sd_p[None], conv_state_p,
            k_s, v_s, logf_s[None], sd_s[None], conv_state_s, v_rows.reshape(1, n_s, dec, -1))
```

```python
import functools

import jax
import jax.numpy as jnp
from jax import lax
from jax.experimental import pallas as pl
from jax.experimental.pallas import tpu as pltpu

F32 = jnp.float32
BF16 = jnp.bfloat16
EPS = 1e-6
LANE = 128
VMEM_LIMIT = 56 * 1024 * 1024
HIGHEST = lax.Precision.HIGHEST

H_A, D_A = 8, 64
H_B, DK_B = 4, 128
CONV_W = 4
DELTA_CHUNK = 64
N_GROUPS, EXPERTS_PER_GROUP = 4, 8
N_EXPERTS = N_GROUPS * EXPERTS_PER_GROUP
CHUNK_C = 128
H_C = 8


def _params(*sem):
    return pltpu.CompilerParams(dimension_semantics=sem, vmem_limit_bytes=VMEM_LIMIT)


def _row_tile(t, pref):
    return pref if t % pref == 0 else t


def _rms(x, w):
    return x * lax.rsqrt(jnp.mean(x * x, axis=-1, keepdims=True) + EPS) * w


def _softplus_tail(z):
    return jnp.log1p(jnp.exp(-jnp.abs(z)))


def _even_in_kernel(x_ref, nw_ref, w_ref, wkv_ref, bias_ref, alog_ref,
                    q_ref, k_ref, v_ref, c_ref, gz_ref, s_ref):
    hb = _rms(x_ref[...], nw_ref[...]).astype(BF16)

    def proj(lo, hi):
        return jnp.dot(hb, w_ref[:, lo:hi], preferred_element_type=F32)

    q_ref[...] = proj(0, 512)
    c_ref[...] = proj(512, 2048)
    gz_ref[...] = proj(2048, 2560)
    k_ref[0] = lax.dot_general(wkv_ref[:512, :], hb, (((1,), (1,)), ((), ())), preferred_element_type=F32)
    v_ref[0] = lax.dot_general(wkv_ref[512:, :], hb, (((1,), (1,)), ((), ())), preferred_element_type=F32)
    z = proj(2560, 2688) + bias_ref[...]
    tail = _softplus_tail(z)
    logf = jnp.minimum(z, 0.0) - tail
    g = -jnp.exp(alog_ref[...]) * (jnp.maximum(z, 0.0) + tail)
    beta = 1.0 / (1.0 + jnp.exp(-z))
    lane = lax.broadcasted_iota(jnp.int32, z.shape, 1)
    s_ref[...] = jnp.where(lane < 8, logf, jnp.where(lane < 12, g, jnp.where(lane < 16, beta, 0.0)))


def _even_in(x, nw, w, wkv, bias, alog, n, length):
    t, d = x.shape
    tm = _row_tile(length, 512)
    per_seq = length // tm
    mix = wkv.shape[0] // 2
    row = lambda wd: pl.BlockSpec((tm, wd), lambda i: (i, 0))
    full = lambda a: pl.BlockSpec(a.shape, lambda i: (0, 0))
    kt = pl.BlockSpec((1, mix, tm), lambda i: (i // per_seq, 0, i % per_seq))
    rows = lambda wd: jax.ShapeDtypeStruct((t, wd), F32)
    kts = jax.ShapeDtypeStruct((n, mix, length), F32)
    return pl.pallas_call(
        _even_in_kernel,
        grid=(t // tm,),
        in_specs=[row(d), full(nw), full(w), full(wkv), full(bias), full(alog)],
        out_specs=[row(mix), kt, kt, row(3 * mix), row(mix), row(LANE)],
        out_shape=[rows(mix), kts, kts, rows(3 * mix), rows(mix), rows(LANE)],
        compiler_params=_params("parallel"),
        name="even_in_proj",
    )(x, nw, w, wkv, bias, alog)


def _cumsum_kernel(x_ref, o_ref):
    rows, length = x_ref.shape
    r = lax.broadcasted_iota(jnp.int32, (LANE, LANE), 0)
    c = lax.broadcasted_iota(jnp.int32, (LANE, LANE), 1)
    upper = (r <= c).astype(F32)
    carry = jnp.zeros((rows, 1), F32)
    for b in range(length // LANE):
        blk = jnp.dot(x_ref[:, b * LANE:(b + 1) * LANE], upper,
                      preferred_element_type=F32, precision=HIGHEST) + carry
        o_ref[:, b * LANE:(b + 1) * LANE] = blk
        carry = blk[:, LANE - 1:LANE]


def _cumsum_lanes(x):
    rows, length = x.shape
    tr = 8
    return pl.pallas_call(
        _cumsum_kernel,
        grid=(rows // tr,),
        in_specs=[pl.BlockSpec((tr, length), lambda i: (i, 0))],
        out_specs=pl.BlockSpec((tr, length), lambda i: (i, 0)),
        out_shape=jax.ShapeDtypeStruct((rows, length), F32),
        compiler_params=_params("parallel"),
        name="logf_cumsum",
    )(x)


def _fox_prompt_kernel(q_ref, k_ref, v_ref, c_ref, o_ref, *, tq):
    qi = pl.program_id(2)
    hp = pl.program_id(1)
    scale = D_A ** -0.5
    lane = lax.broadcasted_iota(jnp.int32, (tq, LANE), 1)
    first = lane < D_A
    q = q_ref[0] * scale
    qs = [jnp.where(first, q, 0.0).astype(BF16), jnp.where(first, 0.0, q).astype(BF16)]
    q0 = pl.multiple_of(qi * tq, tq)

    def crow(h, start):
        return c_ref[0, pl.ds(2 * hp + h, 1), pl.ds(start, tq)]

    cq = [crow(h, q0)[:, 0:1] for h in range(2)]

    def block(j0, carry, mask):
        kb = k_ref[0, :, pl.ds(j0, tq)].astype(BF16)
        vb = v_ref[0, :, pl.ds(j0, tq)].astype(BF16)
        hh = range(2)
        s = [_dot(qs[h], kb) + (cq[h] - crow(h, j0)) for h in hh]
        if mask is not None:
            s = [jnp.where(mask, a, -jnp.inf) for a in s]
        m_new = [jnp.maximum(carry[h][0], jnp.max(s[h], axis=-1, keepdims=True)) for h in hh]
        alpha = [jnp.exp(carry[h][0] - m_new[h]) for h in hh]
        p = [jnp.exp(s[h] - m_new[h]) for h in hh]
        l = [alpha[h] * carry[h][1] + jnp.sum(p[h], axis=-1, keepdims=True) for h in hh]
        pv = [_dot_nt(p[h].astype(BF16), vb) for h in hh]
        acc = [alpha[h] * carry[h][2] + pv[h] for h in hh]
        return tuple((m_new[h], l[h], acc[h]) for h in hh)

    def body(j, carry):
        return block(pl.multiple_of(j * tq, tq), carry, None)

    init = (jnp.full((tq, 1), -jnp.inf, F32), jnp.zeros((tq, 1), F32), jnp.zeros((tq, LANE), F32))
    carry = lax.fori_loop(0, qi, body, (init, init))
    r = lax.broadcasted_iota(jnp.int32, (tq, tq), 0)
    cidx = lax.broadcasted_iota(jnp.int32, (tq, tq), 1)
    (_, l0, a0), (_, l1, a1) = block(q0, carry, cidx <= r)
    o_ref[0] = jnp.where(first, a0 / l0, a1 / l1).astype(o_ref.dtype)


def _fox_prompt(q, k, v, c):
    n, length, width = q.shape
    tq = 512 if length % 512 == 0 else length
    grid = (n, width // LANE, length // tq)
    return pl.pallas_call(
        functools.partial(_fox_prompt_kernel, tq=tq),
        grid=grid,
        in_specs=[
            pl.BlockSpec((1, tq, LANE), lambda b, h, i: (b, i, h)),
            pl.BlockSpec((1, LANE, length), lambda b, h, i: (b, h, 0)),
            pl.BlockSpec((1, LANE, length), lambda b, h, i: (b, h, 0)),
            pl.BlockSpec((1, H_A, length), lambda b, h, i: (b, 0, 0)),
        ],
        out_specs=pl.BlockSpec((1, tq, LANE), lambda b, h, i: (b, i, h)),
        out_shape=jax.ShapeDtypeStruct((n, length, width), BF16),
        compiler_params=_params("parallel", "parallel", "arbitrary"),
        name="fox_prompt",
    )(q, k, v, c)


def _even_out_kernel(x_ref, a_ref, b_ref, w_ref, o_ref):
    half = a_ref.shape[1]
    acc = jnp.dot(a_ref[...], w_ref[:half, :], preferred_element_type=F32)
    acc += jnp.dot(b_ref[...], w_ref[half:, :], preferred_element_type=F32)
    o_ref[...] = x_ref[...] + acc


def _even_out(x, oa, ob, w):
    t, d = x.shape
    tm = _row_tile(t, 512)
    row = lambda a: pl.BlockSpec((tm, a.shape[1]), lambda i: (i, 0))
    return pl.pallas_call(
        _even_out_kernel,
        grid=(t // tm,),
        in_specs=[row(x), row(oa), row(ob), pl.BlockSpec(w.shape, lambda i: (0, 0))],
        out_specs=row(x),
        out_shape=jax.ShapeDtypeStruct((t, d), F32),
        compiler_params=_params("parallel"),
        name="even_out_proj",
    )(x, oa, ob, w)


def _split3(x):
    hi = x.astype(BF16)
    r1 = x - hi.astype(F32)
    mid = r1.astype(BF16)
    lo = (r1 - mid.astype(F32)).astype(BF16)
    return hi, mid, lo


def _dot(a, b):
    return jnp.dot(a, b, preferred_element_type=F32)


def _dot_nt(a, b):
    return lax.dot_general(a, b, (((1,), (1,)), ((), ())), preferred_element_type=F32)


def _dot_tn(a, b):
    return lax.dot_general(a, b, (((0,), (0,)), ((), ())), preferred_element_type=F32)


def _silu(x):
    return x / (1.0 + jnp.exp(-x))


def _delta_prep_kernel(x_ref, xprev_ref, pre_ref, cw_ref, sm_ref,
                       wv_ref, wk_ref, qd_ref, kd_ref, qk_ref, gl_ref, *, ta, valid):
    i = pl.program_id(1)
    c = DELTA_CHUNK
    dk = DK_B
    rr = H_B * c
    x = x_ref[0]
    prev = jnp.where(i == 0, pre_ref[0], xprev_ref[0])
    xcat = jnp.concatenate([prev, x], axis=0)
    conv = x * cw_ref[CONV_W - 1:CONV_W, :]
    for s in range(1, CONV_W):
        conv += pltpu.roll(xcat, s, axis=0)[8:] * cw_ref[CONV_W - 1 - s:CONV_W - s, :]
    act = _silu(conv)
    sm = sm_ref[0]

    ri = lax.broadcasted_iota(jnp.int32, (rr, rr), 0)
    ci = lax.broadcasted_iota(jnp.int32, (rr, rr), 1)
    same_head = (ri // c) == (ci // c)
    same_bf = same_head.astype(BF16)
    tri = (same_head & (ri >= ci)).astype(BF16)
    lane = lax.broadcasted_iota(jnp.int32, (c, LANE), 1)
    wrow = lax.broadcasted_iota(jnp.int32, (c, rr), 0)
    wlane = lax.broadcasted_iota(jnp.int32, (c, rr), 1)
    whead = wlane // c
    wcol = wlane % c
    incl = wrow >= wcol
    strict = wrow > wcol
    r64 = lax.broadcasted_iota(jnp.int32, (c, c), 0)
    c64 = lax.broadcasted_iota(jnp.int32, (c, c), 1)
    tri64 = (r64 >= c64).astype(BF16)
    ones64 = jnp.ones((c, c), BF16)

    def block_diag(wide_bf):
        return jnp.concatenate([wide_bf] * H_B, axis=0) * same_bf

    def fold(bd):
        return sum(jnp.where(whead == h, bd[h * c:(h + 1) * c], 0.0) for h in range(H_B))

    qn, kn, va = [], [], []
    for h in range(H_B):
        qa = act[:, h * dk:(h + 1) * dk]
        ka = act[:, (H_B + h) * dk:(H_B + h + 1) * dk]
        qn.append(qa * lax.rsqrt(jnp.sum(qa * qa, axis=-1, keepdims=True) + EPS) * (dk ** -0.5))
        kn.append(ka * lax.rsqrt(jnp.sum(ka * ka, axis=-1, keepdims=True) + EPS))
        va.append(act[:, (2 * H_B + h) * dk:(2 * H_B + h + 1) * dk])

    chunks = range(ta // c)
    stack = lambda parts: jnp.concatenate(parts, axis=0)
    q, k, v, b_col, gcb, g_wide = [], [], [], [], [], []
    for s in chunks:
        r0 = s * c
        live = lax.broadcasted_iota(jnp.int32, (c, 1), 0) + (i * ta + r0) < valid
        smc = sm[r0:r0 + c]
        pick = lambda ln: jnp.where(live, jnp.sum(jnp.where(lane == ln, smc, 0.0), axis=-1, keepdims=True), 0.0)
        q.append(stack([a[r0:r0 + c] for a in qn]))
        k.append(stack([a[r0:r0 + c] for a in kn]))
        v.append(stack([a[r0:r0 + c] for a in va]))
        b_col.append(stack([pick(12 + h) for h in range(H_B)]))
        g_cols = [pick(8 + h) for h in range(H_B)]
        g_stack = stack([jnp.broadcast_to(g, (c, LANE)) for g in g_cols])
        gcb.append(sum(_dot(tri, p) for p in _split3(g_stack)))
        g_wide.append(sum(jnp.where(whead == h, g_cols[h], 0.0) for h in range(H_B)))

    gc_i = [sum(_dot(tri64, p) for p in _split3(g)) for g in g_wide]
    gc_j = [sum(_dot(ones64, p) for p in _split3(jnp.where(wrow <= wcol, g, 0.0))) for g in g_wide]
    decay = [jnp.exp(jnp.where(incl, a - b, -jnp.inf)) for a, b in zip(gc_i, gc_j)]
    eg = [jnp.exp(g) for g in gcb]
    kb = [a * b for a, b in zip(k, b_col)]
    kbf = [a.astype(BF16) for a in k]
    a_mat = [jnp.where(strict, fold(_dot_nt(a.astype(BF16), b)) * d, 0.0) for a, b, d in zip(kb, kbf, decay)]
    qk = [jnp.where(incl, fold(_dot_nt(a.astype(BF16), b)) * d, 0.0) for a, b, d in zip(q, kbf, decay)]
    y = [-a for a in a_mat]
    p = list(y)
    ybd = [block_diag(a.astype(BF16)) for a in y]
    for _ in range(5):
        y = [_dot(a.astype(BF16), bd) for a, bd in zip(y, ybd)]
        ybd = [block_diag(a.astype(BF16)) for a in y]
        p = [a + b + _dot(a.astype(BF16), bd) for a, b, bd in zip(p, y, ybd)]
    rhs = [jnp.concatenate([a * b, kbb * e], axis=1) for a, b, kbb, e in zip(v, b_col, kb, eg)]
    w = [r + _dot(block_diag(a.astype(BF16)), r.astype(BF16)) for a, r in zip(p, rhs)]

    for s in chunks:
        r0 = s * c
        for h in range(H_B):
            hr = slice(h * c, (h + 1) * c)
            hs = slice(h * dk, (h + 1) * dk)
            g_last = gcb[s][h * c + c - 1:(h + 1) * c, :]
            wv_ref[0, r0:r0 + c, hs] = w[s][hr, :dk]
            wk_ref[0, r0:r0 + c, hs] = w[s][hr, dk:].astype(BF16)
            qd_ref[0, r0:r0 + c, hs] = (q[s][hr] * eg[s][hr]).astype(BF16)
            kd_ref[0, r0:r0 + c, hs] = (k[s][hr] * jnp.exp(g_last - gcb[s][hr])).astype(BF16)
            qk_ref[0, h, r0:r0 + c, :] = qk[s][:, h * c:(h + 1) * c].astype(BF16)
            gl_ref[0, s, h:h + 1, :] = jnp.exp(g_last)


def _delta_prep(conv_in, prefix, conv_w, small, valid):
    n, lp, cd = conv_in.shape
    ta = LANE if lp % LANE == 0 else lp
    nt = lp // ta
    tb = ta // 8
    f = lambda dt, w: jax.ShapeDtypeStruct((n, lp, w), dt)
    blk = lambda w: pl.BlockSpec((1, ta, w), lambda b, i: (b, i, 0))
    width = H_B * DK_B
    return pl.pallas_call(
        functools.partial(_delta_prep_kernel, ta=ta, valid=valid),
        grid=(n, nt),
        in_specs=[
            blk(cd),
            pl.BlockSpec((1, 8, cd), lambda b, i: (b, jnp.maximum(i * tb - 1, 0), 0)),
            pl.BlockSpec((1, 8, cd), lambda b, i: (b, 0, 0)),
            pl.BlockSpec((8, cd), lambda b, i: (0, 0)),
            blk(LANE),
        ],
        out_specs=[
            blk(width), blk(width), blk(width), blk(width),
            pl.BlockSpec((1, H_B, ta, DELTA_CHUNK), lambda b, i: (b, 0, i, 0)),
            pl.BlockSpec((1, ta // DELTA_CHUNK, H_B, LANE), lambda b, i: (b, i, 0, 0)),
        ],
        out_shape=[
            f(F32, width), f(BF16, width), f(BF16, width), f(BF16, width),
            jax.ShapeDtypeStruct((n, H_B, lp, DELTA_CHUNK), BF16),
            jax.ShapeDtypeStruct((n, lp // DELTA_CHUNK, H_B, LANE), F32),
        ],
        compiler_params=_params("parallel", "parallel"),
        name="delta_prep",
    )(conv_in, conv_in, prefix, conv_w, small)


def _delta_scan_kernel(wv_ref, wk_ref, qd_ref, kd_ref, qk_ref, gl_ref, s0_ref, gz_ref, onw_ref,
                       o_ref, s_ref, *, nb, n_chunks):
    c = DELTA_CHUNK
    dk = DK_B

    @pl.when(pl.program_id(1) == 0)
    def _():
        s_ref[...] = s0_ref[...]

    onw = onw_ref[...]

    chains = [(b, h) for b in range(nb) for h in range(H_B)]
    cols = lambda h: slice(h * dk, (h + 1) * dk)

    def body(ci, _):
        r0 = pl.multiple_of(ci * c, c)
        rows = pl.ds(r0, c)
        state = [s_ref[b, h] for b, h in chains]
        prod = [_dot(jnp.concatenate([wk_ref[b, rows, cols(h)], qd_ref[b, rows, cols(h)]], axis=0), s.astype(BF16))
                for (b, h), s in zip(chains, state)]
        vb = [(wv_ref[b, rows, cols(h)] - p[:c]).astype(BF16) for (b, h), p in zip(chains, prod)]
        o = [p[c:] + _dot(qk_ref[b, h, rows, :], v) for (b, h), p, v in zip(chains, prod, vb)]
        new = [s * gl_ref[b, pl.ds(ci, 1), h, :] + _dot_tn(kd_ref[b, rows, cols(h)], v)
               for (b, h), s, v in zip(chains, state, vb)]
        for (b, h), s, oo in zip(chains, new, o):
            s_ref[b, h] = s
            on = oo * lax.rsqrt(jnp.mean(oo * oo, axis=-1, keepdims=True) + EPS) * onw
            o_ref[b, rows, cols(h)] = (on * _silu(gz_ref[b, rows, cols(h)])).astype(o_ref.dtype)
        return 0

    lax.fori_loop(0, n_chunks, body, 0)


def _delta_scan(wv, wk, qd, kd, qk, gl, s0, gz, onw, nb):
    n, lp, width = wv.shape
    tl = _row_tile(lp, 512)
    n_chunks = tl // DELTA_CHUNK
    seq = pl.BlockSpec((nb, tl, width), lambda i, t: (i, t, 0))
    state = pl.BlockSpec((nb,) + s0.shape[1:], lambda i, t: (i, 0, 0, 0))
    return pl.pallas_call(
        functools.partial(_delta_scan_kernel, nb=nb, n_chunks=n_chunks),
        grid=(n // nb, lp // tl),
        in_specs=[seq, seq, seq, seq,
                  pl.BlockSpec((nb, H_B, tl, DELTA_CHUNK), lambda i, t: (i, 0, t, 0)),
                  pl.BlockSpec((nb, n_chunks, H_B, LANE), lambda i, t: (i, t, 0, 0)),
                  state, seq, pl.BlockSpec(onw.shape, lambda i, t: (0, 0))],
        out_specs=[seq, state],
        out_shape=[jax.ShapeDtypeStruct((n, lp, width), BF16), jax.ShapeDtypeStruct(s0.shape, F32)],
        compiler_params=_params("parallel", "arbitrary"),
        name="delta_scan",
    )(wv, wk, qd, kd, qk, gl, s0, gz, onw)


def _fox_sample_kernel(pt_ref, q_ref, kn_ref, vn_ref, lfn_ref, kc_hbm, vc_hbm, lfc_hbm, o_ref,
                       qbd_ref, m_ref, l_ref, acc_ref, carry_ref, kbuf, vbuf, lfbuf, sems,
                       *, pages_per_step, n_q):
    pps = pages_per_step
    b = pl.program_id(0)
    j = pl.program_id(1)
    n_b = pl.num_programs(0)
    n_steps = pl.num_programs(1)
    n_pages = pt_ref.shape[1]
    rows = n_q * H_A
    page = lfn_ref.shape[2]
    width = H_A * D_A
    ri = lax.broadcasted_iota(jnp.int32, (page, page), 0)
    ci = lax.broadcasted_iota(jnp.int32, (page, page), 1)

    def slot_of(bb, jj):
        return (bb * (n_steps - 1) + jj - 1) % 2

    def copies(bb, jj, p):
        src = pt_ref[bb, n_pages - 1 - ((jj - 1) * pps + p)]
        s = slot_of(bb, jj)
        return (pltpu.make_async_copy(kc_hbm.at[src], kbuf.at[s, p], sems.at[0, s]),
                pltpu.make_async_copy(vc_hbm.at[src], vbuf.at[s, p], sems.at[1, s]),
                pltpu.make_async_copy(lfc_hbm.at[src], lfbuf.at[s, p], sems.at[2, s]))

    def fetch(bb, jj):
        for p in range(pps):
            for cp in copies(bb, jj, p):
                cp.start()

    @pl.when((b == 0) & (j == 0))
    def _():
        fetch(0, 1)

    @pl.when(j > 0)
    def _():
        for p in range(pps):
            for cp in copies(b, j, p):
                cp.wait()

        @pl.when(j + 1 < n_steps)
        def _():
            fetch(b, j + 1)

        @pl.when((j + 1 == n_steps) & (b + 1 < n_b))
        def _():
            fetch(b + 1, 1)

    def attend(blocks):
        qbd = qbd_ref[...]
        ss = []
        for kt, _, bias8, mask in blocks:
            s = _dot(qbd, kt.astype(BF16)) + jnp.concatenate([bias8] * n_q, axis=0)
            ss.append(s if mask is None else jnp.where(mask, s, -jnp.inf))
        m_old = m_ref[...]
        m_new = m_old
        for s in ss:
            m_new = jnp.maximum(m_new, jnp.max(s, axis=-1, keepdims=True))
        alpha = jnp.exp(m_old - m_new)
        ps = [jnp.exp(s - m_new) for s in ss]
        sums = [jnp.sum(p, axis=-1, keepdims=True) for p in ps]
        pvs = [_dot_nt(p.astype(BF16), vt.astype(BF16)) for p, (_, vt, _, _) in zip(ps, blocks)]
        l_ref[...] = alpha * l_ref[...] + sum(sums)
        acc_ref[...] = alpha * acc_ref[...] + sum(pvs)
        m_ref[...] = m_new

    @pl.when(j == 0)
    def _():
        q = q_ref[0] * (D_A ** -0.5)
        qrep = jnp.concatenate([jnp.broadcast_to(q[t:t + 1], (H_A, width)) for t in range(n_q)], axis=0)
        r = lax.broadcasted_iota(jnp.int32, qrep.shape, 0)
        ln = lax.broadcasted_iota(jnp.int32, qrep.shape, 1)
        qbd_ref[...] = jnp.where(ln // D_A == r % H_A, qrep, 0.0).astype(BF16)
        m_ref[...] = jnp.full(m_ref.shape, -jnp.inf, F32)
        l_ref[...] = jnp.zeros(l_ref.shape, F32)
        acc_ref[...] = jnp.zeros(acc_ref.shape, F32)
        carry_ref[...] = jnp.zeros(carry_ref.shape, F32)
        upper = (ri <= ci).astype(BF16)
        csum = sum(_dot(p, upper) for p in _split3(lfn_ref[0]))
        r2 = lax.broadcasted_iota(jnp.int32, (rows, page), 0)
        c2 = lax.broadcasted_iota(jnp.int32, (rows, page), 1)
        attend([(kn_ref[0], vn_ref[0], -csum, c2 <= r2 // H_A)])

    @pl.when(j > 0)
    def _():
        after = (ri > ci).astype(BF16)
        carry = carry_ref[...]
        slot = slot_of(b, j)
        lfs = [lfbuf[slot, p] for p in range(pps)]
        pieces = [_split3(lf) for lf in lfs]
        suffix = [sum(_dot(p, after) for p in ps) for ps in pieces]
        totals = [jnp.sum(lf, axis=-1, keepdims=True) for lf in lfs]
        blocks = []
        for p, (sfx, tot) in enumerate(zip(suffix, totals)):
            blocks.append((kbuf[slot, p].reshape(width, page), vbuf[slot, p].reshape(width, page), carry + sfx, None))
            carry = carry + tot
        attend(blocks)
        carry_ref[...] = carry

    @pl.when(j == pl.num_programs(1) - 1)
    def _():
        out = acc_ref[...] / l_ref[...]
        r = lax.broadcasted_iota(jnp.int32, out.shape, 0)
        ln = lax.broadcasted_iota(jnp.int32, out.shape, 1)
        out = jnp.where(ln // D_A == r % H_A, out, 0.0)
        o_ref[0] = jnp.sum(out.reshape(n_q, H_A, width), axis=1)


def _fox_sample(page_table, q, kn, vn, lfn, cache_k, cache_v, cache_lf):
    b, n_q, width = q.shape
    n_pages = page_table.shape[1]
    page = cache_k.shape[3]
    rows = n_q * H_A
    pps = next(p for p in (16, 8, 1) if n_pages % p == 0)
    cur = lambda i, j, pt: (i, 0, 0)
    hbm = pl.BlockSpec(memory_space=pl.ANY)
    grid_spec = pltpu.PrefetchScalarGridSpec(
        num_scalar_prefetch=1,
        grid=(b, n_pages // pps + 1),
        in_specs=[
            pl.BlockSpec((1, n_q, width), cur),
            pl.BlockSpec((1, width, page), cur),
            pl.BlockSpec((1, width, page), cur),
            pl.BlockSpec((1, H_A, page), cur),
            hbm, hbm, hbm,
        ],
        out_specs=pl.BlockSpec((1, n_q, width), cur),
        scratch_shapes=[
            pltpu.VMEM((rows, width), BF16),
            pltpu.VMEM((rows, 1), F32),
            pltpu.VMEM((rows, 1), F32),
            pltpu.VMEM((rows, width), F32),
            pltpu.VMEM((H_A, 1), F32),
            pltpu.VMEM((2, pps, H_A, D_A, page), F32),
            pltpu.VMEM((2, pps, H_A, D_A, page), F32),
            pltpu.VMEM((2, pps, H_A, page), F32),
            pltpu.SemaphoreType.DMA((3, 2)),
        ],
    )
    return pl.pallas_call(
        functools.partial(_fox_sample_kernel, pages_per_step=pps, n_q=n_q),
        grid_spec=grid_spec,
        out_shape=jax.ShapeDtypeStruct((b, n_q, width), F32),
        compiler_params=_params("arbitrary", "arbitrary"),
        name="fox_sample",
    )(page_table, q, kn, vn, lfn, cache_k, cache_v, cache_lf)


def _router_kernel(x_ref, nw_ref, wr_ref, g_ref):
    h = _rms(x_ref[...], nw_ref[...])
    h_hi = h.astype(BF16)
    h_lo = (h - h_hi.astype(F32)).astype(BF16)
    logits = _dot(h_hi, wr_ref[0]) + (_dot(h_lo, wr_ref[0]) + _dot(h_hi, wr_ref[1]))
    lane = lax.broadcasted_iota(jnp.int32, logits.shape, 1)
    lanef = lane.astype(F32)
    neg = -jnp.inf
    is_group = (lane >= N_EXPERTS) & (lane < N_EXPERTS + N_GROUPS)
    gl = jnp.where(is_group, logits, neg)
    gmax = jnp.max(gl, axis=-1, keepdims=True)
    g_sel = jnp.min(jnp.where(gl == gmax, lanef - N_EXPERTS, 1e9), axis=-1, keepdims=True)
    p_sel = 1.0 / jnp.sum(jnp.where(is_group, jnp.exp(logits - gmax), 0.0), axis=-1, keepdims=True)
    in_sel = (lane < N_EXPERTS) & ((lane // EXPERTS_PER_GROUP).astype(F32) == g_sel)
    el = jnp.where(in_sel, logits, neg)
    m1 = jnp.max(el, axis=-1, keepdims=True)
    i1 = jnp.min(jnp.where(el == m1, lanef, 1e9), axis=-1, keepdims=True)
    el2 = jnp.where(lanef == i1, neg, el)
    m2 = jnp.max(el2, axis=-1, keepdims=True)
    i2 = jnp.min(jnp.where(el2 == m2, lanef, 1e9), axis=-1, keepdims=True)
    e2 = jnp.exp(m2 - m1)
    w1 = p_sel / (1.0 + e2)
    g_ref[...] = jnp.where(lanef == i1, w1, jnp.where(lanef == i2, w1 * e2, jnp.where(lane == N_EXPERTS, g_sel, 0.0)))


def _router(x, nw, wr):
    t, d = x.shape
    tm = _row_tile(t, 512)
    return pl.pallas_call(
        _router_kernel,
        grid=(t // tm,),
        in_specs=[pl.BlockSpec((tm, d), lambda i: (i, 0)), pl.BlockSpec(nw.shape, lambda i: (0, 0)),
                  pl.BlockSpec(wr.shape, lambda i: (0, 0, 0))],
        out_specs=pl.BlockSpec((tm, LANE), lambda i: (i, 0)),
        out_shape=jax.ShapeDtypeStruct((t, LANE), F32),
        compiler_params=_params("parallel"),
        name="moe_router",
    )(x, nw, wr)


def _moe_kernel(x_ref, nw_ref, g_ref, wg_ref, wu_ref, wd_ref, nf_ref, o_ref, hb_ref, acc_ref, *, final_norm):
    e = pl.program_id(1)

    @pl.when(e == 0)
    def _():
        hb_ref[...] = _rms(x_ref[...], nw_ref[...]).astype(BF16)
        acc_ref[...] = jnp.zeros(acc_ref.shape, F32)

    gates = g_ref[...]
    lane = lax.broadcasted_iota(jnp.int32, gates.shape, 1)
    gcol = jnp.sum(jnp.where(lane == e, gates, 0.0), axis=-1, keepdims=True)
    hb = hb_ref[...]
    act = _silu(_dot(hb, wg_ref[0, 0])) * _dot(hb, wu_ref[0, 0]) * gcol
    acc_ref[...] += _dot(act.astype(BF16), wd_ref[0, 0])

    @pl.when(e == pl.num_programs(1) - 1)
    def _():
        y = x_ref[...] + acc_ref[...]
        o_ref[...] = _rms(y, nf_ref[...]) if final_norm else y


def _moe_dense(x, nw, gates, wg, wu, wd, nf, layer, final_norm):
    t, d = x.shape
    tm = _row_tile(t, 1024)
    _, n_e, _, f = wg.shape
    return pl.pallas_call(
        functools.partial(_moe_kernel, final_norm=final_norm),
        grid=(t // tm, n_e),
        in_specs=[
            pl.BlockSpec((tm, d), lambda i, e: (i, 0)),
            pl.BlockSpec(nw.shape, lambda i, e: (0, 0)),
            pl.BlockSpec((tm, LANE), lambda i, e: (i, 0)),
            pl.BlockSpec((1, 1, d, f), lambda i, e: (layer, e, 0, 0)),
            pl.BlockSpec((1, 1, d, f), lambda i, e: (layer, e, 0, 0)),
            pl.BlockSpec((1, 1, f, d), lambda i, e: (layer, e, 0, 0)),
            pl.BlockSpec(nf.shape, lambda i, e: (0, 0)),
        ],
        out_specs=pl.BlockSpec((tm, d), lambda i, e: (i, 0)),
        out_shape=jax.ShapeDtypeStruct((t, d), F32),
        scratch_shapes=[pltpu.VMEM((tm, d), BF16), pltpu.VMEM((tm, d), F32)],
        compiler_params=_params("parallel", "arbitrary"),
        name="moe_experts",
    )(x, nw, gates, wg, wu, wd, nf)


def _group_rank_kernel(g_ref, grp_ref, rank_ref, cnt_ref, run_ref):
    i = pl.program_id(0)
    tm = g_ref.shape[0]

    @pl.when(i == 0)
    def _():
        run_ref[...] = jnp.zeros(run_ref.shape, F32)

    gates = g_ref[...]
    lane = lax.broadcasted_iota(jnp.int32, gates.shape, 1)
    g_sel = jnp.sum(jnp.where(lane == N_EXPERTS, gates, 0.0), axis=-1, keepdims=True)
    onehot_t = (lane.astype(F32) == g_sel).astype(F32).T[:8]
    ri = lax.broadcasted_iota(jnp.int32, (tm, tm), 0)
    ci = lax.broadcasted_iota(jnp.int32, (tm, tm), 1)
    earlier = _dot(onehot_t.astype(BF16), (ri < ci).astype(BF16))
    run = run_ref[...]
    gidx = lax.broadcasted_iota(jnp.int32, (8, tm), 0).astype(F32)
    rank_ref[0] = jnp.sum(onehot_t * (earlier + run), axis=0, keepdims=True).astype(jnp.int32)
    grp_ref[0] = jnp.sum(onehot_t * gidx, axis=0, keepdims=True).astype(jnp.int32)
    run = run + jnp.sum(onehot_t, axis=-1, keepdims=True)
    run_ref[...] = run
    cnt_ref[...] = jnp.broadcast_to(run, cnt_ref.shape)


def _group_rank(gates, tm):
    t = gates.shape[0]
    nt = t // tm
    row = pl.BlockSpec((1, 1, tm), lambda i: (i, 0, 0))
    return pl.pallas_call(
        _group_rank_kernel,
        grid=(nt,),
        in_specs=[pl.BlockSpec((tm, LANE), lambda i: (i, 0))],
        out_specs=[row, row, pl.BlockSpec((8, LANE), lambda i: (0, 0))],
        out_shape=[jax.ShapeDtypeStruct((nt, 1, tm), jnp.int32), jax.ShapeDtypeStruct((nt, 1, tm), jnp.int32),
                   jax.ShapeDtypeStruct((8, LANE), F32)],
        scratch_shapes=[pltpu.VMEM((8, 1), F32)],
        compiler_params=_params("arbitrary"),
        name="moe_group_rank",
    )(gates)


def _row_copy(src, src_row, dst, dst_row, sem):
    return pltpu.make_async_copy(src.at[pl.ds(src_row, 1)], dst.at[pl.ds(dst_row, 1)], sem)


def _dispatch_kernel(dest_ref, x_ref, g_ref, buf_ref, o_ref, stage_ref, sem):
    del buf_ref
    tm, d = x_ref.shape
    stage_ref[:, :d] = x_ref[...]
    stage_ref[:, d:] = g_ref[...]

    def start(r, _):
        _row_copy(stage_ref, r, o_ref, dest_ref[0, 0, r], sem).start()
        return 0

    def wait(r, _):
        _row_copy(stage_ref, 0, o_ref, 0, sem).wait()
        return 0

    lax.fori_loop(0, tm, start, 0, unroll=8)
    lax.fori_loop(0, tm, wait, 0, unroll=8)


def _dispatch(dest, x, gates, buf, tm):
    t, d = x.shape
    return pl.pallas_call(
        _dispatch_kernel,
        grid=(t // tm,),
        in_specs=[pl.BlockSpec((1, 1, tm), lambda i: (i, 0, 0), memory_space=pltpu.SMEM),
                  pl.BlockSpec((tm, d), lambda i: (i, 0)),
                  pl.BlockSpec((tm, LANE), lambda i: (i, 0)),
                  pl.BlockSpec(memory_space=pl.ANY)],
        out_specs=pl.BlockSpec(memory_space=pl.ANY),
        out_shape=jax.ShapeDtypeStruct(buf.shape, F32),
        scratch_shapes=[pltpu.VMEM((tm, d + LANE), F32), pltpu.SemaphoreType.DMA(())],
        input_output_aliases={3: 0},
        compiler_params=_params("arbitrary"),
        name="moe_dispatch",
    )(dest, x, gates, buf)


def _moe_group_kernel(tg_ref, na_ref, xs_ref, nw_ref, wg_ref, wu_ref, wd_ref, nf_ref, o_ref, *, final_norm):
    i = pl.program_id(0)
    d = o_ref.shape[1]

    @pl.when(i < na_ref[0])
    def _():
        x = xs_ref[:, :d]
        gates = xs_ref[:, d:]
        hb = _rms(x, nw_ref[...]).astype(BF16)
        lane = lax.broadcasted_iota(jnp.int32, gates.shape, 1)
        first = tg_ref[i] * EXPERTS_PER_GROUP
        acc = jnp.zeros(x.shape, F32)
        for e in range(EXPERTS_PER_GROUP):
            gcol = jnp.sum(jnp.where(lane == first + e, gates, 0.0), axis=-1, keepdims=True)
            act = _silu(_dot(hb, wg_ref[0, 0, e])) * _dot(hb, wu_ref[0, 0, e]) * gcol
            acc += _dot(act.astype(BF16), wd_ref[0, 0, e])
        y = x + acc
        o_ref[...] = _rms(y, nf_ref[...]) if final_norm else y

    @pl.when(i >= na_ref[0])
    def _():
        o_ref[...] = jnp.zeros(o_ref.shape, F32)


def _moe_group(tile_group, n_active, xs, nw, wg, wu, wd, nf, tm, layer, final_norm):
    tp, dx = xs.shape
    d = dx - LANE
    wspec = lambda a: pl.BlockSpec((1, 1) + a.shape[2:], lambda i, tg, na: (layer, tg[i], 0, 0, 0))
    vec = lambda a: pl.BlockSpec(a.shape, lambda i, tg, na: (0, 0))
    grid_spec = pltpu.PrefetchScalarGridSpec(
        num_scalar_prefetch=2,
        grid=(tp // tm,),
        in_specs=[pl.BlockSpec((tm, dx), lambda i, tg, na: (i, 0)), vec(nw), wspec(wg), wspec(wu), wspec(wd), vec(nf)],
        out_specs=pl.BlockSpec((tm, d), lambda i, tg, na: (i, 0)),
    )
    return pl.pallas_call(
        functools.partial(_moe_group_kernel, final_norm=final_norm),
        grid_spec=grid_spec,
        out_shape=jax.ShapeDtypeStruct((tp, d), F32),
        compiler_params=_params("arbitrary"),
        name="moe_group_experts",
    )(tile_group, n_active, xs, nw, wg, wu, wd, nf)


def _combine_kernel(dest_ref, ys_ref, o_ref, sem):
    tm = o_ref.shape[0]

    def start(r, _):
        _row_copy(ys_ref, dest_ref[0, 0, r], o_ref, r, sem).start()
        return 0

    def wait(r, _):
        _row_copy(ys_ref, 0, o_ref, 0, sem).wait()
        return 0

    lax.fori_loop(0, tm, start, 0, unroll=8)
    lax.fori_loop(0, tm, wait, 0, unroll=8)


def _combine(dest, ys, t, tm):
    d = ys.shape[1]
    return pl.pallas_call(
        _combine_kernel,
        grid=(t // tm,),
        in_specs=[pl.BlockSpec((1, 1, tm), lambda i: (i, 0, 0), memory_space=pltpu.SMEM),
                  pl.BlockSpec(memory_space=pl.ANY)],
        out_specs=pl.BlockSpec((tm, d), lambda i: (i, 0)),
        out_shape=jax.ShapeDtypeStruct((t, d), F32),
        scratch_shapes=[pltpu.SemaphoreType.DMA(())],
        compiler_params=_params("arbitrary"),
        name="moe_combine",
    )(dest, ys)


def _moe_sorted(x, nw, gates, wg, wu, wd, nf, layer, final_norm):
    t, d = x.shape
    tm = 512
    grp, rank, cnt = _group_rank(gates, tm)
    counts = [cnt[g, 0].astype(jnp.int32) for g in range(N_GROUPS)]
    tiles = [(c + tm - 1) // tm for c in counts]
    tile_end = [sum(tiles[:g + 1]) for g in range(N_GROUPS)]
    dest = rank + sum(jnp.where(grp == g, (tile_end[g] - tiles[g]) * tm, 0) for g in range(N_GROUPS))
    n_tiles = t // tm + N_GROUPS
    tile = jnp.arange(n_tiles, dtype=jnp.int32)
    tile_group = jnp.minimum(sum((tile >= e).astype(jnp.int32) for e in tile_end), N_GROUPS - 1)
    xs = _dispatch(dest, x, gates, jnp.zeros((n_tiles * tm, d + LANE), F32), tm)
    split = lambda w: w.reshape((w.shape[0], N_GROUPS, EXPERTS_PER_GROUP) + w.shape[2:])
    ys = _moe_group(tile_group, tile_end[-1].reshape(1), xs, nw,
                    split(wg), split(wu), split(wd), nf, tm, layer, final_norm)
    return _combine(dest, ys, t, tm)


def _odd_kernel(x_ref, nw_ref, win_ref, lnw_ref, lnb_ref, wm_ref, bs_ref, wout_ref, *refs, emit_v):
    if emit_v:
        o_ref, v_ref, gated_ref = refs
    else:
        o_ref, gated_ref = refs
    tm = x_ref.shape[0]
    dc = lnw_ref.shape[1]
    gd = dc // H_C
    x = x_ref[...]
    hb = _rms(x, nw_ref[...]).astype(BF16)
    z = _dot(hb, win_ref[...])
    z = 0.5 * z * (1.0 + lax.erf(z * (2.0 ** -0.5)))
    v = z[:, dc:]
    mu = jnp.mean(v, axis=-1, keepdims=True)
    vc = v - mu
    vn = vc * lax.rsqrt(jnp.mean(vc * vc, axis=-1, keepdims=True) + EPS) * lnw_ref[...] + lnb_ref[...]
    if emit_v:
        v_ref[...] = vn
    vb = vn.astype(BF16)
    for ci in range(tm // CHUNK_C):
        rs = slice(ci * CHUNK_C, (ci + 1) * CHUNK_C)
        for g in range(H_C):
            cs = slice(g * gd, (g + 1) * gd)
            mixed = _dot(wm_ref[g], vb[rs, cs]) + jnp.concatenate([bs_ref[g]] * (gd // LANE), axis=1)
            gated_ref[rs, cs] = (z[rs, cs] * mixed).astype(BF16)
    o_ref[...] = x + _dot(gated_ref[...], wout_ref[...])


def _odd_mixer(x, nw, w_in, ln_w, ln_b, wm, bs, w_out, emit_v):
    t, d = x.shape
    dc = ln_w.shape[1]
    tm = _row_tile(t, 512)
    row = lambda w: pl.BlockSpec((tm, w), lambda i: (i, 0))
    full = lambda a: pl.BlockSpec(a.shape, lambda i: (0,) * a.ndim)
    out_specs = [row(d)] + ([row(dc)] if emit_v else [])
    out_shape = [jax.ShapeDtypeStruct((t, d), F32)] + ([jax.ShapeDtypeStruct((t, dc), F32)] if emit_v else [])
    return pl.pallas_call(
        functools.partial(_odd_kernel, emit_v=emit_v),
        grid=(t // tm,),
        in_specs=[row(d), full(nw), full(w_in), full(ln_w), full(ln_b), full(wm), full(bs), full(w_out)],
        out_specs=out_specs,
        out_shape=out_shape,
        scratch_shapes=[pltpu.VMEM((tm, dc), BF16)],
        compiler_params=_params("parallel"),
        name="odd_mixer",
    )(x, nw, w_in, ln_w, ln_b, wm, bs, w_out)


def _prep_even_weights(w_in, f_bias, a_log, dt_bias):
    d = w_in.shape[0]
    small = jnp.concatenate(
        [w_in[:, 1536:1544], w_in[:, 3080:3088], jnp.zeros((d, LANE - 16), w_in.dtype)], axis=1)
    w = jnp.concatenate([w_in[:, 0:512], w_in[:, 1544:3080], w_in[:, 3088:3600], small], axis=1)
    bias = jnp.zeros((1, LANE), F32).at[0, 0:8].set(f_bias).at[0, 8:12].set(dt_bias)
    alog = jnp.zeros((1, LANE), F32).at[0, 8:12].set(a_log)
    return w.astype(BF16), w_in[:, 512:1536].T.astype(BF16), bias, alog


def kernel(x_prompt, x_sample, cache_k, cache_v, cache_logf, state_delta, state_conv, page_table, norm_mix, norm_ffn, norm_final, w_in_even, w_out_even, fox_forget_bias, dn_conv_w, dn_a_log, dn_dt_bias, dn_norm_w, w_in_odd, gm_ln_w, gm_ln_b, gm_spatial_w, gm_spatial_b, w_out_odd, moe_router_group, moe_router_expert, moe_w_gate, moe_w_up, moe_w_down):
    n_p, seq, d = x_prompt.shape
    n_s, dec = x_sample.shape[:2]
    page = cache_k.shape[2]
    mix_a = H_A * D_A
    xp = x_prompt.reshape(n_p * seq, d)
    xs = x_sample.reshape(n_s * dec, d)
    row = lambda a: a[None, :]

    even_w = (row(norm_mix[0]),) + _prep_even_weights(w_in_even[0], fox_forget_bias[0], dn_a_log[0], dn_dt_bias[0])
    from_kt = lambda a, n, length: jnp.transpose(a.reshape(n, H_A, D_A, length), (0, 3, 1, 2))[None]
    w_out_e = w_out_even[0].astype(BF16)
    conv_w = jnp.zeros((8, dn_conv_w.shape[2]), F32).at[:CONV_W].set(dn_conv_w[0])
    onw = row(dn_norm_w[0])

    def delta(conv3, prefix, small3, gz3, s0, valid, nb):
        wv, wk, qd, kd, qk, gl = _delta_prep(conv3, prefix, conv_w, small3, valid)
        return _delta_scan(wv, wk, qd, kd, qk, gl, s0, gz3, onw, nb)

    q, kt, vt, c, gz, s = _even_in(xp, *even_w, n_p, seq)
    logf_p = s[:, :H_A].reshape(n_p, seq, H_A)
    csum = _cumsum_lanes(jnp.swapaxes(logf_p, 1, 2).reshape(n_p * H_A, seq)).reshape(n_p, H_A, seq)
    as3 = lambda a: a.reshape(n_p, seq, a.shape[-1])
    oa_p = _fox_prompt(as3(q), kt, vt, csum)
    conv_p = as3(c)
    ob_p, sd_p = delta(conv_p, jnp.zeros((n_p, 8, conv_p.shape[-1]), F32), as3(s), as3(gz),
                       jnp.zeros((n_p, H_B, DK_B, DK_B), F32), seq, 4)
    xp = _even_out(xp, oa_p.reshape(n_p * seq, mix_a), ob_p.reshape(n_p * seq, -1), w_out_e)
    k_p = from_kt(kt, n_p, seq)
    v_p = from_kt(vt, n_p, seq)
    conv_state_p = conv_p[:, seq - (CONV_W - 1):][None]

    q, kt, vt, c, gz, s = _even_in(xs, *even_w, 1, n_s * dec)
    as3 = lambda a: a.reshape(n_s, dec, a.shape[-1])
    padrows = lambda a, n: jnp.pad(a, ((0, 0), (0, n - a.shape[1]), (0, 0)))
    logf_s = s[:, :H_A].reshape(n_s, dec, H_A)
    pad_keys = lambda a: jnp.pad(a, ((0, 0), (0, 0), (0, page - dec)))
    per_seq = lambda a: pad_keys(jnp.swapaxes(a[0].reshape(mix_a, n_s, dec), 0, 1))
    oa_s = _fox_sample(page_table, as3(q), per_seq(kt), per_seq(vt), pad_keys(jnp.swapaxes(logf_s, 1, 2)),
                       jnp.transpose(cache_k[0], (0, 2, 3, 1)), jnp.transpose(cache_v[0], (0, 2, 3, 1)),
                       jnp.swapaxes(cache_logf[0], 1, 2))
    conv_s = as3(c)
    lp = DELTA_CHUNK
    prefix_s = jnp.pad(state_conv[0], ((0, 0), (8 - (CONV_W - 1), 0), (0, 0)))
    ob_s, sd_s = delta(padrows(conv_s, lp), prefix_s, padrows(as3(s), lp), padrows(as3(gz), lp),
                       state_delta[0], dec, 8)
    xs = _even_out(xs, oa_s.reshape(n_s * dec, mix_a).astype(BF16), ob_s[:, :dec].reshape(n_s * dec, -1), w_out_e)
    k_s = kt[0].T.reshape(1, n_s, dec, H_A, D_A)
    v_s = vt[0].T.reshape(1, n_s, dec, H_A, D_A)
    conv_state_s = jnp.concatenate([state_conv[0], conv_s], axis=1)[:, dec:][None]

    def moe(x, layer, final):
        wr = jnp.concatenate([moe_router_expert[layer], moe_router_group[layer],
                              jnp.zeros((d, LANE - N_EXPERTS - N_GROUPS), F32)], axis=1)
        wr_hi = wr.astype(BF16)
        wr = jnp.stack([wr_hi, (wr - wr_hi.astype(F32)).astype(BF16)])
        gates = _router(x, row(norm_ffn[layer]), wr)
        experts = _moe_sorted if x.shape[0] >= 4096 else _moe_dense
        return experts(x, row(norm_ffn[layer]), gates, wg, wu, wd, row(norm_final), layer, final)

    wg, wu, wd = moe_w_gate.astype(BF16), moe_w_up.astype(BF16), moe_w_down.astype(BF16)
    xp = moe(xp, 0, False)
    xs = moe(xs, 0, False)

    idx = jnp.arange(CHUNK_C)
    w_m = jnp.where(idx[:, None] >= idx[None, :], gm_spatial_w[0], 0.0)
    b_s = gm_spatial_b[0]
    lanes = lambda b: jnp.broadcast_to(b[:, :, None], b.shape + (LANE,))
    reps = CHUNK_C // dec
    assert n_s * dec == CHUNK_C, "sample group must fill exactly one 128-row mixing tile"
    w_m_s = (jnp.eye(reps, dtype=F32)[None, :, None, :, None] * w_m[:, None, :dec, None, :dec]).reshape(H_C, CHUNK_C, CHUNK_C)
    b_s_s = jnp.tile(b_s[:, :dec], (1, reps))
    odd_w = (row(norm_mix[1]), w_in_odd[0].astype(BF16), row(gm_ln_w[0]), row(gm_ln_b[0]))
    w_out_o = w_out_odd[0].astype(BF16)
    (xp,) = _odd_mixer(xp, *odd_w, w_m.astype(BF16), lanes(b_s), w_out_o, False)
    xs, v_rows = _odd_mixer(xs, *odd_w, w_m_s.astype(BF16), lanes(b_s_s), w_out_o, True)

    y_p = moe(xp, 1, True).reshape(n_p, seq, d)
    y_s = moe(xs, 1, True).reshape(n_s, dec, d)
    return (y_p, y_s, k_p, v_p, logf_p[None], sd_p[None], conv_state_p,
            k_s, v_s, logf_s[None], sd_s[None], conv_state_s, v_rows.reshape(1, n_s, dec, -1))
```

```python
import functools

import jax
import jax.numpy as jnp
from jax import lax
from jax.experimental import pallas as pl
from jax.experimental.pallas import tpu as pltpu

F32 = jnp.float32
BF16 = jnp.bfloat16
EPS = 1e-6
LANE = 128
VMEM_LIMIT = 56 * 1024 * 1024
HIGHEST = lax.Precision.HIGHEST

H_A, D_A = 8, 64
H_B, DK_B = 4, 128
CONV_W = 4
DELTA_CHUNK = 64
N_GROUPS, EXPERTS_PER_GROUP = 4, 8
N_EXPERTS = N_GROUPS * EXPERTS_PER_GROUP
CHUNK_C = 128
H_C = 8


def _params(*sem):
    return pltpu.CompilerParams(dimension_semantics=sem, vmem_limit_bytes=VMEM_LIMIT)


def _row_tile(t, pref):
    return pref if t % pref == 0 else t


def _rms(x, w):
    return x * lax.rsqrt(jnp.mean(x * x, axis=-1, keepdims=True) + EPS) * w


def _softplus_tail(z):
    return jnp.log1p(jnp.exp(-jnp.abs(z)))


def _even_in_kernel(x_ref, nw_ref, w_ref, wkv_ref, bias_ref, alog_ref,
                    q_ref, k_ref, v_ref, c_ref, gz_ref, s_ref):
    hb = _rms(x_ref[...], nw_ref[...]).astype(BF16)

    def proj(lo, hi):
        return jnp.dot(hb, w_ref[:, lo:hi], preferred_element_type=F32)

    q_ref[...] = proj(0, 512)
    c_ref[...] = proj(512, 2048)
    gz_ref[...] = proj(2048, 2560)
    k_ref[0] = lax.dot_general(wkv_ref[:512, :], hb, (((1,), (1,)), ((), ())), preferred_element_type=F32)
    v_ref[0] = lax.dot_general(wkv_ref[512:, :], hb, (((1,), (1,)), ((), ())), preferred_element_type=F32)
    z = proj(2560, 2688) + bias_ref[...]
    tail = _softplus_tail(z)
    logf = jnp.minimum(z, 0.0) - tail
    g = -jnp.exp(alog_ref[...]) * (jnp.maximum(z, 0.0) + tail)
    beta = 1.0 / (1.0 + jnp.exp(-z))
    lane = lax.broadcasted_iota(jnp.int32, z.shape, 1)
    s_ref[...] = jnp.where(lane < 8, logf, jnp.where(lane < 12, g, jnp.where(lane < 16, beta, 0.0)))


def _even_in(x, nw, w, wkv, bias, alog, n, length):
    t, d = x.shape
    tm = _row_tile(length, 512)
    per_seq = length // tm
    mix = wkv.shape[0] // 2
    row = lambda wd: pl.BlockSpec((tm, wd), lambda i: (i, 0))
    full = lambda a: pl.BlockSpec(a.shape, lambda i: (0, 0))
    kt = pl.BlockSpec((1, mix, tm), lambda i: (i // per_seq, 0, i % per_seq))
    rows = lambda wd: jax.ShapeDtypeStruct((t, wd), F32)
    kts = jax.ShapeDtypeStruct((n, mix, length), F32)
    return pl.pallas_call(
        _even_in_kernel,
        grid=(t // tm,),
        in_specs=[row(d), full(nw), full(w), full(wkv), full(bias), full(alog)],
        out_specs=[row(mix), kt, kt, row(3 * mix), row(mix), row(LANE)],
        out_shape=[rows(mix), kts, kts, rows(3 * mix), rows(mix), rows(LANE)],
        compiler_params=_params("parallel"),
        name="even_in_proj",
    )(x, nw, w, wkv, bias, alog)


def _cumsum_kernel(x_ref, o_ref):
    rows, length = x_ref.shape
    r = lax.broadcasted_iota(jnp.int32, (LANE, LANE), 0)
    c = lax.broadcasted_iota(jnp.int32, (LANE, LANE), 1)
    upper = (r <= c).astype(F32)
    carry = jnp.zeros((rows, 1), F32)
    for b in range(length // LANE):
        blk = jnp.dot(x_ref[:, b * LANE:(b + 1) * LANE], upper,
                      preferred_element_type=F32, precision=HIGHEST) + carry
        o_ref[:, b * LANE:(b + 1) * LANE] = blk
        carry = blk[:, LANE - 1:LANE]


def _cumsum_lanes(x):
    rows, length = x.shape
    tr = 8
    return pl.pallas_call(
        _cumsum_kernel,
        grid=(rows // tr,),
        in_specs=[pl.BlockSpec((tr, length), lambda i: (i, 0))],
        out_specs=pl.BlockSpec((tr, length), lambda i: (i, 0)),
        out_shape=jax.ShapeDtypeStruct((rows, length), F32),
        compiler_params=_params("parallel"),
        name="logf_cumsum",
    )(x)


def _fox_prompt_kernel(q_ref, k_ref, v_ref, c_ref, o_ref, *, tq):
    qi = pl.program_id(2)
    hp = pl.program_id(1)
    scale = D_A ** -0.5
    lane = lax.broadcasted_iota(jnp.int32, (tq, LANE), 1)
    first = lane < D_A
    q = q_ref[0] * scale
    qs = [jnp.where(first, q, 0.0).astype(BF16), jnp.where(first, 0.0, q).astype(BF16)]
    q0 = pl.multiple_of(qi * tq, tq)

    def crow(h, start):
        return c_ref[0, pl.ds(2 * hp + h, 1), pl.ds(start, tq)]

    cq = [crow(h, q0)[:, 0:1] for h in range(2)]

    def block(j0, carry, mask):
        kb = k_ref[0, :, pl.ds(j0, tq)].astype(BF16)
        vb = v_ref[0, :, pl.ds(j0, tq)].astype(BF16)
        hh = range(2)
        s = [_dot(qs[h], kb) + (cq[h] - crow(h, j0)) for h in hh]
        if mask is not None:
            s = [jnp.where(mask, a, -jnp.inf) for a in s]
        m_new = [jnp.maximum(carry[h][0], jnp.max(s[h], axis=-1, keepdims=True)) for h in hh]
        alpha = [jnp.exp(carry[h][0] - m_new[h]) for h in hh]
        p = [jnp.exp(s[h] - m_new[h]) for h in hh]
        l = [alpha[h] * carry[h][1] + jnp.sum(p[h], axis=-1, keepdims=True) for h in hh]
        pv = [_dot_nt(p[h].astype(BF16), vb) for h in hh]
        acc = [alpha[h] * carry[h][2] + pv[h] for h in hh]
        return tuple((m_new[h], l[h], acc[h]) for h in hh)

    def body(j, carry):
        return block(pl.multiple_of(j * tq, tq), carry, None)

    init = (jnp.full((tq, 1), -jnp.inf, F32), jnp.zeros((tq, 1), F32), jnp.zeros((tq, LANE), F32))
    carry = lax.fori_loop(0, qi, body, (init, init))
    r = lax.broadcasted_iota(jnp.int32, (tq, tq), 0)
    cidx = lax.broadcasted_iota(jnp.int32, (tq, tq), 1)
    (_, l0, a0), (_, l1, a1) = block(q0, carry, cidx <= r)
    o_ref[0] = jnp.where(first, a0 / l0, a1 / l1).astype(o_ref.dtype)


def _fox_prompt(q, k, v, c):
    n, length, width = q.shape
    tq = 512 if length % 512 == 0 else length
    grid = (n, width // LANE, length // tq)
    return pl.pallas_call(
        functools.partial(_fox_prompt_kernel, tq=tq),
        grid=grid,
        in_specs=[
            pl.BlockSpec((1, tq, LANE), lambda b, h, i: (b, i, h)),
            pl.BlockSpec((1, LANE, length), lambda b, h, i: (b, h, 0)),
            pl.BlockSpec((1, LANE, length), lambda b, h, i: (b, h, 0)),
            pl.BlockSpec((1, H_A, length), lambda b, h, i: (b, 0, 0)),
        ],
        out_specs=pl.BlockSpec((1, tq, LANE), lambda b, h, i: (b, i, h)),
        out_shape=jax.ShapeDtypeStruct((n, length, width), BF16),
        compiler_params=_params("parallel", "parallel", "arbitrary"),
        name="fox_prompt",
    )(q, k, v, c)


def _even_out_kernel(x_ref, a_ref, b_ref, w_ref, o_ref):
    half = a_ref.shape[1]
    acc = jnp.dot(a_ref[...], w_ref[:half, :], preferred_element_type=F32)
    acc += jnp.dot(b_ref[...], w_ref[half:, :], preferred_element_type=F32)
    o_ref[...] = x_ref[...] + acc


def _even_out(x, oa, ob, w):
    t, d = x.shape
    tm = _row_tile(t, 512)
    row = lambda a: pl.BlockSpec((tm, a.shape[1]), lambda i: (i, 0))
    return pl.pallas_call(
        _even_out_kernel,
        grid=(t // tm,),
        in_specs=[row(x), row(oa), row(ob), pl.BlockSpec(w.shape, lambda i: (0, 0))],
        out_specs=row(x),
        out_shape=jax.ShapeDtypeStruct((t, d), F32),
        compiler_params=_params("parallel"),
        name="even_out_proj",
    )(x, oa, ob, w)


def _split3(x):
    hi = x.astype(BF16)
    r1 = x - hi.astype(F32)
    mid = r1.astype(BF16)
    lo = (r1 - mid.astype(F32)).astype(BF16)
    return hi, mid, lo


def _dot(a, b):
    return jnp.dot(a, b, preferred_element_type=F32)


def _dot_nt(a, b):
    return lax.dot_general(a, b, (((1,), (1,)), ((), ())), preferred_element_type=F32)


def _dot_tn(a, b):
    return lax.dot_general(a, b, (((0,), (0,)), ((), ())), preferred_element_type=F32)


def _silu(x):
    return x / (1.0 + jnp.exp(-x))


def _delta_prep_kernel(x_ref, xprev_ref, pre_ref, cw_ref, sm_ref,
                       wv_ref, wk_ref, qd_ref, kd_ref, qk_ref, gl_ref, *, ta, valid):
    i = pl.program_id(1)
    c = DELTA_CHUNK
    dk = DK_B
    rr = H_B * c
    x = x_ref[0]
    prev = jnp.where(i == 0, pre_ref[0], xprev_ref[0])
    xcat = jnp.concatenate([prev, x], axis=0)
    conv = x * cw_ref[CONV_W - 1:CONV_W, :]
    for s in range(1, CONV_W):
        conv += pltpu.roll(xcat, s, axis=0)[8:] * cw_ref[CONV_W - 1 - s:CONV_W - s, :]
    act = _silu(conv)
    sm = sm_ref[0]

    ri = lax.broadcasted_iota(jnp.int32, (rr, rr), 0)
    ci = lax.broadcasted_iota(jnp.int32, (rr, rr), 1)
    same_head = (ri // c) == (ci // c)
    same_bf = same_head.astype(BF16)
    tri = (same_head & (ri >= ci)).astype(BF16)
    lane = lax.broadcasted_iota(jnp.int32, (c, LANE), 1)
    wrow = lax.broadcasted_iota(jnp.int32, (c, rr), 0)
    wlane = lax.broadcasted_iota(jnp.int32, (c, rr), 1)
    whead = wlane // c
    wcol = wlane % c
    incl = wrow >= wcol
    strict = wrow > wcol
    r64 = lax.broadcasted_iota(jnp.int32, (c, c), 0)
    c64 = lax.broadcasted_iota(jnp.int32, (c, c), 1)
    tri64 = (r64 >= c64).astype(BF16)
    ones64 = jnp.ones((c, c), BF16)

    def block_diag(wide_bf):
        return jnp.concatenate([wide_bf] * H_B, axis=0) * same_bf

    def fold(bd):
        return sum(jnp.where(whead == h, bd[h * c:(h + 1) * c], 0.0) for h in range(H_B))

    qn, kn, va = [], [], []
    for h in range(H_B):
        qa = act[:, h * dk:(h + 1) * dk]
        ka = act[:, (H_B + h) * dk:(H_B + h + 1) * dk]
        qn.append(qa * lax.rsqrt(jnp.sum(qa * qa, axis=-1, keepdims=True) + EPS) * (dk ** -0.5))
        kn.append(ka * lax.rsqrt(jnp.sum(ka * ka, axis=-1, keepdims=True) + EPS))
        va.append(act[:, (2 * H_B + h) * dk:(2 * H_B + h + 1) * dk])

    chunks = range(ta // c)
    stack = lambda parts: jnp.concatenate(parts, axis=0)
    q, k, v, b_col, gcb, g_wide = [], [], [], [], [], []
    for s in chunks:
        r0 = s * c
        live = lax.broadcasted_iota(jnp.int32, (c, 1), 0) + (i * ta + r0) < valid
        smc = sm[r0:r0 + c]
        pick = lambda ln: jnp.where(live, jnp.sum(jnp.where(lane == ln, smc, 0.0), axis=-1, keepdims=True), 0.0)
        q.append(stack([a[r0:r0 + c] for a in qn]))
        k.append(stack([a[r0:r0 + c] for a in kn]))
        v.append(stack([a[r0:r0 + c] for a in va]))
        b_col.append(stack([pick(12 + h) for h in range(H_B)]))
        g_cols = [pick(8 + h) for h in range(H_B)]
        g_stack = stack([jnp.broadcast_to(g, (c, LANE)) for g in g_cols])
        gcb.append(sum(_dot(tri, p) for p in _split3(g_stack)))
        g_wide.append(sum(jnp.where(whead == h, g_cols[h], 0.0) for h in range(H_B)))

    gc_i = [sum(_dot(tri64, p) for p in _split3(g)) for g in g_wide]
    gc_j = [sum(_dot(ones64, p) for p in _split3(jnp.where(wrow <= wcol, g, 0.0))) for g in g_wide]
    decay = [jnp.exp(jnp.where(incl, a - b, -jnp.inf)) for a, b in zip(gc_i, gc_j)]
    eg = [jnp.exp(g) for g in gcb]
    kb = [a * b for a, b in zip(k, b_col)]
    kbf = [a.astype(BF16) for a in k]
    a_mat = [jnp.where(strict, fold(_dot_nt(a.astype(BF16), b)) * d, 0.0) for a, b, d in zip(kb, kbf, decay)]
    qk = [jnp.where(incl, fold(_dot_nt(a.astype(BF16), b)) * d, 0.0) for a, b, d in zip(q, kbf, decay)]
    y = [-a for a in a_mat]
    p = list(y)
    ybd = [block_diag(a.astype(BF16)) for a in y]
    for _ in range(5):
        y = [_dot(a.astype(BF16), bd) for a, bd in zip(y, ybd)]
        ybd = [block_diag(a.astype(BF16)) for a in y]
        p = [a + b + _dot(a.astype(BF16), bd) for a, b, bd in zip(p, y, ybd)]
    rhs = [jnp.concatenate([a * b, kbb * e], axis=1) for a, b, kbb, e in zip(v, b_col, kb, eg)]
    w = [r + _dot(block_diag(a.astype(BF16)), r.astype(BF16)) for a, r in zip(p, rhs)]

    for s in chunks:
        r0 = s * c
        for h in range(H_B):
            hr = slice(h * c, (h + 1) * c)
            hs = slice(h * dk, (h + 1) * dk)
            g_last = gcb[s][h * c + c - 1:(h + 1) * c, :]
            wv_ref[0, r0:r0 + c, hs] = w[s][hr, :dk]
            wk_ref[0, r0:r0 + c, hs] = w[s][hr, dk:].astype(BF16)
            qd_ref[0, r0:r0 + c, hs] = (q[s][hr] * eg[s][hr]).astype(BF16)
            kd_ref[0, r0:r0 + c, hs] = (k[s][hr] * jnp.exp(g_last - gcb[s][hr])).astype(BF16)
            qk_ref[0, h, r0:r0 + c, :] = qk[s][:, h * c:(h + 1) * c].astype(BF16)
            gl_ref[0, s, h:h + 1, :] = jnp.exp(g_last)


def _delta_prep(conv_in, prefix, conv_w, small, valid):
    n, lp, cd = conv_in.shape
    ta = LANE if lp % LANE == 0 else lp
    nt = lp // ta
    tb = ta // 8
    f = lambda dt, w: jax.ShapeDtypeStruct((n, lp, w), dt)
    blk = lambda w: pl.BlockSpec((1, ta, w), lambda b, i: (b, i, 0))
    width = H_B * DK_B
    return pl.pallas_call(
        functools.partial(_delta_prep_kernel, ta=ta, valid=valid),
        grid=(n, nt),
        in_specs=[
            blk(cd),
            pl.BlockSpec((1, 8, cd), lambda b, i: (b, jnp.maximum(i * tb - 1, 0), 0)),
            pl.BlockSpec((1, 8, cd), lambda b, i: (b, 0, 0)),
            pl.BlockSpec((8, cd), lambda b, i: (0, 0)),
            blk(LANE),
        ],
        out_specs=[
            blk(width), blk(width), blk(width), blk(width),
            pl.BlockSpec((1, H_B, ta, DELTA_CHUNK), lambda b, i: (b, 0, i, 0)),
            pl.BlockSpec((1, ta // DELTA_CHUNK, H_B, LANE), lambda b, i: (b, i, 0, 0)),
        ],
        out_shape=[
            f(F32, width), f(BF16, width), f(BF16, width), f(BF16, width),
            jax.ShapeDtypeStruct((n, H_B, lp, DELTA_CHUNK), BF16),
            jax.ShapeDtypeStruct((n, lp // DELTA_CHUNK, H_B, LANE), F32),
        ],
        compiler_params=_params("parallel", "parallel"),
        name="delta_prep",
    )(conv_in, conv_in, prefix, conv_w, small)


def _delta_scan_kernel(wv_ref, wk_ref, qd_ref, kd_ref, qk_ref, gl_ref, s0_ref, gz_ref, onw_ref,
                       o_ref, s_ref, *, nb, n_chunks):
    c = DELTA_CHUNK
    dk = DK_B

    @pl.when(pl.program_id(1) == 0)
    def _():
        s_ref[...] = s0_ref[...]

    onw = onw_ref[...]

    chains = [(b, h) for b in range(nb) for h in range(H_B)]
    cols = lambda h: slice(h * dk, (h + 1) * dk)

    def body(ci, _):
        r0 = pl.multiple_of(ci * c, c)
        rows = pl.ds(r0, c)
        state = [s_ref[b, h] for b, h in chains]
        prod = [_dot(jnp.concatenate([wk_ref[b, rows, cols(h)], qd_ref[b, rows, cols(h)]], axis=0), s.astype(BF16))
                for (b, h), s in zip(chains, state)]
        vb = [(wv_ref[b, rows, cols(h)] - p[:c]).astype(BF16) for (b, h), p in zip(chains, prod)]
        o = [p[c:] + _dot(qk_ref[b, h, rows, :], v) for (b, h), p, v in zip(chains, prod, vb)]
        new = [s * gl_ref[b, pl.ds(ci, 1), h, :] + _dot_tn(kd_ref[b, rows, cols(h)], v)
               for (b, h), s, v in zip(chains, state, vb)]
        for (b, h), s, oo in zip(chains, new, o):
            s_ref[b, h] = s
            on = oo * lax.rsqrt(jnp.mean(oo * oo, axis=-1, keepdims=True) + EPS) * onw
            o_ref[b, rows, cols(h)] = (on * _silu(gz_ref[b, rows, cols(h)])).astype(o_ref.dtype)
        return 0

    lax.fori_loop(0, n_chunks, body, 0)


def _delta_scan(wv, wk, qd, kd, qk, gl, s0, gz, onw, nb):
    n, lp, width = wv.shape
    tl = _row_tile(lp, 512)
    n_chunks = tl // DELTA_CHUNK
    seq = pl.BlockSpec((nb, tl, width), lambda i, t: (i, t, 0))
    state = pl.BlockSpec((nb,) + s0.shape[1:], lambda i, t: (i, 0, 0, 0))
    return pl.pallas_call(
        functools.partial(_delta_scan_kernel, nb=nb, n_chunks=n_chunks),
        grid=(n // nb, lp // tl),
        in_specs=[seq, seq, seq, seq,
                  pl.BlockSpec((nb, H_B, tl, DELTA_CHUNK), lambda i, t: (i, 0, t, 0)),
                  pl.BlockSpec((nb, n_chunks, H_B, LANE), lambda i, t: (i, t, 0, 0)),
                  state, seq, pl.BlockSpec(onw.shape, lambda i, t: (0, 0))],
        out_specs=[seq, state],
        out_shape=[jax.ShapeDtypeStruct((n, lp, width), BF16), jax.ShapeDtypeStruct(s0.shape, F32)],
        compiler_params=_params("parallel", "arbitrary"),
        name="delta_scan",
    )(wv, wk, qd, kd, qk, gl, s0, gz, onw)


def _fox_sample_kernel(pt_ref, q_ref, kn_ref, vn_ref, lfn_ref, kc_hbm, vc_hbm, lfc_hbm, o_ref,
                       qbd_ref, m_ref, l_ref, acc_ref, carry_ref, kbuf, vbuf, lfbuf, sems,
                       *, pages_per_step, n_q):
    pps = pages_per_step
    b = pl.program_id(0)
    j = pl.program_id(1)
    n_b = pl.num_programs(0)
    n_steps = pl.num_programs(1)
    n_pages = pt_ref.shape[1]
    rows = n_q * H_A
    page = lfn_ref.shape[2]
    width = H_A * D_A
    ri = lax.broadcasted_iota(jnp.int32, (page, page), 0)
    ci = lax.broadcasted_iota(jnp.int32, (page, page), 1)

    per_seq = n_steps - 1
    n_slots = kbuf.shape[0]
    g = b * per_seq + j - 1

    def copies(gg, p):
        src = pt_ref[gg // per_seq, n_pages - 1 - ((gg % per_seq) * pps + p)]
        s = gg % n_slots
        return (pltpu.make_async_copy(kc_hbm.at[src], kbuf.at[s, p], sems.at[0, s]),
                pltpu.make_async_copy(vc_hbm.at[src], vbuf.at[s, p], sems.at[1, s]),
                pltpu.make_async_copy(lfc_hbm.at[src], lfbuf.at[s, p], sems.at[2, s]))

    def fetch(gg):
        @pl.when(gg < n_b * per_seq)
        def _():
            for p in range(pps):
                for cp in copies(gg, p):
                    cp.start()

    @pl.when((b == 0) & (j == 0))
    def _():
        for ahead in range(n_slots - 1):
            fetch(jnp.int32(ahead))

    @pl.when(j > 0)
    def _():
        for p in range(pps):
            for cp in copies(g, p):
                cp.wait()
        fetch(g + n_slots - 1)

    def attend(blocks):
        qbd = qbd_ref[...]
        ss = []
        for kt, _, bias8, mask in blocks:
            s = _dot(qbd, kt.astype(BF16)) + jnp.concatenate([bias8] * n_q, axis=0)
            ss.append(s if mask is None else jnp.where(mask, s, -jnp.inf))
        m_old = m_ref[...]
        m_new = m_old
        for s in ss:
            m_new = jnp.maximum(m_new, jnp.max(s, axis=-1, keepdims=True))
        alpha = jnp.exp(m_old - m_new)
        ps = [jnp.exp(s - m_new) for s in ss]
        sums = [jnp.sum(p, axis=-1, keepdims=True) for p in ps]
        pvs = [_dot_nt(p.astype(BF16), vt.astype(BF16)) for p, (_, vt, _, _) in zip(ps, blocks)]
        l_ref[...] = alpha * l_ref[...] + sum(sums)
        acc_ref[...] = alpha * acc_ref[...] + sum(pvs)
        m_ref[...] = m_new

    @pl.when(j == 0)
    def _():
        q = q_ref[0] * (D_A ** -0.5)
        qrep = jnp.concatenate([jnp.broadcast_to(q[t:t + 1], (H_A, width)) for t in range(n_q)], axis=0)
        r = lax.broadcasted_iota(jnp.int32, qrep.shape, 0)
        ln = lax.broadcasted_iota(jnp.int32, qrep.shape, 1)
        qbd_ref[...] = jnp.where(ln // D_A == r % H_A, qrep, 0.0).astype(BF16)
        m_ref[...] = jnp.full(m_ref.shape, -jnp.inf, F32)
        l_ref[...] = jnp.zeros(l_ref.shape, F32)
        acc_ref[...] = jnp.zeros(acc_ref.shape, F32)
        carry_ref[...] = jnp.zeros(carry_ref.shape, F32)
        upper = (ri <= ci).astype(BF16)
        csum = sum(_dot(p, upper) for p in _split3(lfn_ref[0]))
        r2 = lax.broadcasted_iota(jnp.int32, (rows, page), 0)
        c2 = lax.broadcasted_iota(jnp.int32, (rows, page), 1)
        attend([(kn_ref[0], vn_ref[0], -csum, c2 <= r2 // H_A)])

    @pl.when(j > 0)
    def _():
        after = (ri > ci).astype(BF16)
        carry = carry_ref[...]
        slot = g % n_slots
        lfs = [lfbuf[slot, p] for p in range(pps)]
        pieces = [_split3(lf) for lf in lfs]
        suffix = [sum(_dot(p, after) for p in ps) for ps in pieces]
        totals = [jnp.sum(lf, axis=-1, keepdims=True) for lf in lfs]
        blocks = []
        for p, (sfx, tot) in enumerate(zip(suffix, totals)):
            blocks.append((kbuf[slot, p].reshape(width, page), vbuf[slot, p].reshape(width, page), carry + sfx, None))
            carry = carry + tot
        attend(blocks)
        carry_ref[...] = carry

    @pl.when(j == pl.num_programs(1) - 1)
    def _():
        out = acc_ref[...] / l_ref[...]
        r = lax.broadcasted_iota(jnp.int32, out.shape, 0)
        ln = lax.broadcasted_iota(jnp.int32, out.shape, 1)
        out = jnp.where(ln // D_A == r % H_A, out, 0.0)
        o_ref[0] = jnp.sum(out.reshape(n_q, H_A, width), axis=1)


def _fox_sample(page_table, q, kn, vn, lfn, cache_k, cache_v, cache_lf):
    b, n_q, width = q.shape
    n_pages = page_table.shape[1]
    page = cache_k.shape[3]
    rows = n_q * H_A
    pps = next(p for p in (16, 8, 1) if n_pages % p == 0)
    n_slots = 3
    cur = lambda i, j, pt: (i, 0, 0)
    hbm = pl.BlockSpec(memory_space=pl.ANY)
    grid_spec = pltpu.PrefetchScalarGridSpec(
        num_scalar_prefetch=1,
        grid=(b, n_pages // pps + 1),
        in_specs=[
            pl.BlockSpec((1, n_q, width), cur),
            pl.BlockSpec((1, width, page), cur),
            pl.BlockSpec((1, width, page), cur),
            pl.BlockSpec((1, H_A, page), cur),
            hbm, hbm, hbm,
        ],
        out_specs=pl.BlockSpec((1, n_q, width), cur),
        scratch_shapes=[
            pltpu.VMEM((rows, width), BF16),
            pltpu.VMEM((rows, 1), F32),
            pltpu.VMEM((rows, 1), F32),
            pltpu.VMEM((rows, width), F32),
            pltpu.VMEM((H_A, 1), F32),
            pltpu.VMEM((n_slots, pps, H_A, D_A, page), F32),
            pltpu.VMEM((n_slots, pps, H_A, D_A, page), F32),
            pltpu.VMEM((n_slots, pps, H_A, page), F32),
            pltpu.SemaphoreType.DMA((3, n_slots)),
        ],
    )
    return pl.pallas_call(
        functools.partial(_fox_sample_kernel, pages_per_step=pps, n_q=n_q),
        grid_spec=grid_spec,
        out_shape=jax.ShapeDtypeStruct((b, n_q, width), F32),
        compiler_params=_params("arbitrary", "arbitrary"),
        name="fox_sample",
    )(page_table, q, kn, vn, lfn, cache_k, cache_v, cache_lf)


def _router_kernel(x_ref, nw_ref, wr_ref, g_ref):
    h = _rms(x_ref[...], nw_ref[...])
    h_hi = h.astype(BF16)
    h_lo = (h - h_hi.astype(F32)).astype(BF16)
    logits = _dot(h_hi, wr_ref[0]) + (_dot(h_lo, wr_ref[0]) + _dot(h_hi, wr_ref[1]))
    lane = lax.broadcasted_iota(jnp.int32, logits.shape, 1)
    lanef = lane.astype(F32)
    neg = -jnp.inf
    is_group = (lane >= N_EXPERTS) & (lane < N_EXPERTS + N_GROUPS)
    gl = jnp.where(is_group, logits, neg)
    gmax = jnp.max(gl, axis=-1, keepdims=True)
    g_sel = jnp.min(jnp.where(gl == gmax, lanef - N_EXPERTS, 1e9), axis=-1, keepdims=True)
    p_sel = 1.0 / jnp.sum(jnp.where(is_group, jnp.exp(logits - gmax), 0.0), axis=-1, keepdims=True)
    in_sel = (lane < N_EXPERTS) & ((lane // EXPERTS_PER_GROUP).astype(F32) == g_sel)
    el = jnp.where(in_sel, logits, neg)
    m1 = jnp.max(el, axis=-1, keepdims=True)
    i1 = jnp.min(jnp.where(el == m1, lanef, 1e9), axis=-1, keepdims=True)
    el2 = jnp.where(lanef == i1, neg, el)
    m2 = jnp.max(el2, axis=-1, keepdims=True)
    i2 = jnp.min(jnp.where(el2 == m2, lanef, 1e9), axis=-1, keepdims=True)
    e2 = jnp.exp(m2 - m1)
    w1 = p_sel / (1.0 + e2)
    g_ref[...] = jnp.where(lanef == i1, w1, jnp.where(lanef == i2, w1 * e2, jnp.where(lane == N_EXPERTS, g_sel, 0.0)))


def _router(x, nw, wr):
    t, d = x.shape
    tm = _row_tile(t, 512)
    return pl.pallas_call(
        _router_kernel,
        grid=(t // tm,),
        in_specs=[pl.BlockSpec((tm, d), lambda i: (i, 0)), pl.BlockSpec(nw.shape, lambda i: (0, 0)),
                  pl.BlockSpec(wr.shape, lambda i: (0, 0, 0))],
        out_specs=pl.BlockSpec((tm, LANE), lambda i: (i, 0)),
        out_shape=jax.ShapeDtypeStruct((t, LANE), F32),
        compiler_params=_params("parallel"),
        name="moe_router",
    )(x, nw, wr)


def _moe_kernel(x_ref, nw_ref, g_ref, wg_ref, wu_ref, wd_ref, nf_ref, o_ref, hb_ref, acc_ref, *, final_norm):
    e = pl.program_id(1)

    @pl.when(e == 0)
    def _():
        hb_ref[...] = _rms(x_ref[...], nw_ref[...]).astype(BF16)
        acc_ref[...] = jnp.zeros(acc_ref.shape, F32)

    gates = g_ref[...]
    lane = lax.broadcasted_iota(jnp.int32, gates.shape, 1)
    gcol = jnp.sum(jnp.where(lane == e, gates, 0.0), axis=-1, keepdims=True)
    hb = hb_ref[...]
    act = _silu(_dot(hb, wg_ref[0, 0])) * _dot(hb, wu_ref[0, 0]) * gcol
    acc_ref[...] += _dot(act.astype(BF16), wd_ref[0, 0])

    @pl.when(e == pl.num_programs(1) - 1)
    def _():
        y = x_ref[...] + acc_ref[...]
        o_ref[...] = _rms(y, nf_ref[...]) if final_norm else y


def _moe_dense(x, nw, gates, wg, wu, wd, nf, layer, final_norm):
    t, d = x.shape
    tm = _row_tile(t, 1024)
    _, n_e, _, f = wg.shape
    return pl.pallas_call(
        functools.partial(_moe_kernel, final_norm=final_norm),
        grid=(t // tm, n_e),
        in_specs=[
            pl.BlockSpec((tm, d), lambda i, e: (i, 0)),
            pl.BlockSpec(nw.shape, lambda i, e: (0, 0)),
            pl.BlockSpec((tm, LANE), lambda i, e: (i, 0)),
            pl.BlockSpec((1, 1, d, f), lambda i, e: (layer, e, 0, 0)),
            pl.BlockSpec((1, 1, d, f), lambda i, e: (layer, e, 0, 0)),
            pl.BlockSpec((1, 1, f, d), lambda i, e: (layer, e, 0, 0)),
            pl.BlockSpec(nf.shape, lambda i, e: (0, 0)),
        ],
        out_specs=pl.BlockSpec((tm, d), lambda i, e: (i, 0)),
        out_shape=jax.ShapeDtypeStruct((t, d), F32),
        scratch_shapes=[pltpu.VMEM((tm, d), BF16), pltpu.VMEM((tm, d), F32)],
        compiler_params=_params("parallel", "arbitrary"),
        name="moe_experts",
    )(x, nw, gates, wg, wu, wd, nf)


def _group_rank_kernel(g_ref, grp_ref, rank_ref, cnt_ref, run_ref):
    i = pl.program_id(0)
    tm = g_ref.shape[0]

    @pl.when(i == 0)
    def _():
        run_ref[...] = jnp.zeros(run_ref.shape, F32)

    gates = g_ref[...]
    lane = lax.broadcasted_iota(jnp.int32, gates.shape, 1)
    g_sel = jnp.sum(jnp.where(lane == N_EXPERTS, gates, 0.0), axis=-1, keepdims=True)
    onehot_t = (lane.astype(F32) == g_sel).astype(F32).T[:8]
    ri = lax.broadcasted_iota(jnp.int32, (tm, tm), 0)
    ci = lax.broadcasted_iota(jnp.int32, (tm, tm), 1)
    earlier = _dot(onehot_t.astype(BF16), (ri < ci).astype(BF16))
    run = run_ref[...]
    gidx = lax.broadcasted_iota(jnp.int32, (8, tm), 0).astype(F32)
    rank_ref[0] = jnp.sum(onehot_t * (earlier + run), axis=0, keepdims=True).astype(jnp.int32)
    grp_ref[0] = jnp.sum(onehot_t * gidx, axis=0, keepdims=True).astype(jnp.int32)
    run = run + jnp.sum(onehot_t, axis=-1, keepdims=True)
    run_ref[...] = run
    cnt_ref[...] = jnp.broadcast_to(run, cnt_ref.shape)


def _group_rank(gates, tm):
    t = gates.shape[0]
    nt = t // tm
    row = pl.BlockSpec((1, 1, tm), lambda i: (i, 0, 0))
    return pl.pallas_call(
        _group_rank_kernel,
        grid=(nt,),
        in_specs=[pl.BlockSpec((tm, LANE), lambda i: (i, 0))],
        out_specs=[row, row, pl.BlockSpec((8, LANE), lambda i: (0, 0))],
        out_shape=[jax.ShapeDtypeStruct((nt, 1, tm), jnp.int32), jax.ShapeDtypeStruct((nt, 1, tm), jnp.int32),
                   jax.ShapeDtypeStruct((8, LANE), F32)],
        scratch_shapes=[pltpu.VMEM((8, 1), F32)],
        compiler_params=_params("arbitrary"),
        name="moe_group_rank",
    )(gates)


def _row_copy(src, src_row, dst, dst_row, sem):
    return pltpu.make_async_copy(src.at[pl.ds(src_row, 1)], dst.at[pl.ds(dst_row, 1)], sem)


def _dispatch_kernel(dest_ref, x_ref, g_ref, buf_ref, o_ref, stage_ref, sem):
    del buf_ref
    tm, d = x_ref.shape
    stage_ref[:, :d] = x_ref[...]
    stage_ref[:, d:] = g_ref[...]

    def start(r2, _):
        for pr in range(2):
            r = 2 * r2 + pr
            _row_copy(stage_ref, r, o_ref, dest_ref[0, 0, r], sem).start(priority=pr)
        return 0

    def wait(r, _):
        _row_copy(stage_ref, 0, o_ref, 0, sem).wait()
        return 0

    lax.fori_loop(0, tm // 2, start, 0, unroll=4)
    lax.fori_loop(0, tm, wait, 0, unroll=8)


def _dispatch(dest, x, gates, buf, tm):
    t, d = x.shape
    return pl.pallas_call(
        _dispatch_kernel,
        grid=(t // tm,),
        in_specs=[pl.BlockSpec((1, 1, tm), lambda i: (i, 0, 0), memory_space=pltpu.SMEM),
                  pl.BlockSpec((tm, d), lambda i: (i, 0)),
                  pl.BlockSpec((tm, LANE), lambda i: (i, 0)),
                  pl.BlockSpec(memory_space=pl.ANY)],
        out_specs=pl.BlockSpec(memory_space=pl.ANY),
        out_shape=jax.ShapeDtypeStruct(buf.shape, F32),
        scratch_shapes=[pltpu.VMEM((tm, d + LANE), F32), pltpu.SemaphoreType.DMA(())],
        input_output_aliases={3: 0},
        compiler_params=_params("arbitrary"),
        name="moe_dispatch",
    )(dest, x, gates, buf)


def _moe_group_kernel(tg_ref, na_ref, xs_ref, nw_ref, wg_ref, wu_ref, wd_ref, nf_ref, o_ref, *, final_norm):
    i = pl.program_id(0)
    d = o_ref.shape[1]

    @pl.when(i < na_ref[0])
    def _():
        x = xs_ref[:, :d]
        gates = xs_ref[:, d:]
        hb = _rms(x, nw_ref[...]).astype(BF16)
        lane = lax.broadcasted_iota(jnp.int32, gates.shape, 1)
        first = tg_ref[i] * EXPERTS_PER_GROUP
        acc = jnp.zeros(x.shape, F32)
        for e in range(EXPERTS_PER_GROUP):
            gcol = jnp.sum(jnp.where(lane == first + e, gates, 0.0), axis=-1, keepdims=True)
            act = _silu(_dot(hb, wg_ref[0, 0, e])) * _dot(hb, wu_ref[0, 0, e]) * gcol
            acc += _dot(act.astype(BF16), wd_ref[0, 0, e])
        y = x + acc
        o_ref[...] = _rms(y, nf_ref[...]) if final_norm else y

    @pl.when(i >= na_ref[0])
    def _():
        o_ref[...] = jnp.zeros(o_ref.shape, F32)


def _moe_group(tile_group, n_active, xs, nw, wg, wu, wd, nf, tm, layer, final_norm):
    tp, dx = xs.shape
    d = dx - LANE
    wspec = lambda a: pl.BlockSpec((1, 1) + a.shape[2:], lambda i, tg, na: (layer, tg[i], 0, 0, 0))
    vec = lambda a: pl.BlockSpec(a.shape, lambda i, tg, na: (0, 0))
    grid_spec = pltpu.PrefetchScalarGridSpec(
        num_scalar_prefetch=2,
        grid=(tp // tm,),
        in_specs=[pl.BlockSpec((tm, dx), lambda i, tg, na: (i, 0)), vec(nw), wspec(wg), wspec(wu), wspec(wd), vec(nf)],
        out_specs=pl.BlockSpec((tm, d), lambda i, tg, na: (i, 0)),
    )
    return pl.pallas_call(
        functools.partial(_moe_group_kernel, final_norm=final_norm),
        grid_spec=grid_spec,
        out_shape=jax.ShapeDtypeStruct((tp, d), F32),
        compiler_params=_params("arbitrary"),
        name="moe_group_experts",
    )(tile_group, n_active, xs, nw, wg, wu, wd, nf)


def _combine_kernel(dest_ref, ys_ref, o_ref, sem):
    tm = o_ref.shape[0]

    def start(r2, _):
        for pr in range(2):
            r = 2 * r2 + pr
            _row_copy(ys_ref, dest_ref[0, 0, r], o_ref, r, sem).start(priority=pr)
        return 0

    def wait(r, _):
        _row_copy(ys_ref, 0, o_ref, 0, sem).wait()
        return 0

    lax.fori_loop(0, tm // 2, start, 0, unroll=4)
    lax.fori_loop(0, tm, wait, 0, unroll=8)


def _combine(dest, ys, t, tm):
    d = ys.shape[1]
    return pl.pallas_call(
        _combine_kernel,
        grid=(t // tm,),
        in_specs=[pl.BlockSpec((1, 1, tm), lambda i: (i, 0, 0), memory_space=pltpu.SMEM),
                  pl.BlockSpec(memory_space=pl.ANY)],
        out_specs=pl.BlockSpec((tm, d), lambda i: (i, 0)),
        out_shape=jax.ShapeDtypeStruct((t, d), F32),
        scratch_shapes=[pltpu.SemaphoreType.DMA(())],
        compiler_params=_params("arbitrary"),
        name="moe_combine",
    )(dest, ys)


def _moe_sorted(x, nw, gates, wg, wu, wd, nf, layer, final_norm):
    t, d = x.shape
    tm = 512
    grp, rank, cnt = _group_rank(gates, tm)
    counts = [cnt[g, 0].astype(jnp.int32) for g in range(N_GROUPS)]
    tiles = [(c + tm - 1) // tm for c in counts]
    tile_end = [sum(tiles[:g + 1]) for g in range(N_GROUPS)]
    dest = rank + sum(jnp.where(grp == g, (tile_end[g] - tiles[g]) * tm, 0) for g in range(N_GROUPS))
    n_tiles = t // tm + N_GROUPS
    tile = jnp.arange(n_tiles, dtype=jnp.int32)
    tile_group = jnp.minimum(sum((tile >= e).astype(jnp.int32) for e in tile_end), N_GROUPS - 1)
    xs = _dispatch(dest, x, gates, jnp.zeros((n_tiles * tm, d + LANE), F32), tm)
    split = lambda w: w.reshape((w.shape[0], N_GROUPS, EXPERTS_PER_GROUP) + w.shape[2:])
    ys = _moe_group(tile_group, tile_end[-1].reshape(1), xs, nw,
                    split(wg), split(wu), split(wd), nf, tm, layer, final_norm)
    return _combine(dest, ys, t, tm)


def _odd_kernel(x_ref, nw_ref, win_ref, lnw_ref, lnb_ref, wm_ref, bs_ref, wout_ref, *refs, emit_v):
    if emit_v:
        o_ref, v_ref, gated_ref = refs
    else:
        o_ref, gated_ref = refs
    tm = x_ref.shape[0]
    dc = lnw_ref.shape[1]
    gd = dc // H_C
    x = x_ref[...]
    hb = _rms(x, nw_ref[...]).astype(BF16)
    z = _dot(hb, win_ref[...])
    z = 0.5 * z * (1.0 + lax.erf(z * (2.0 ** -0.5)))
    v = z[:, dc:]
    mu = jnp.mean(v, axis=-1, keepdims=True)
    vc = v - mu
    vn = vc * lax.rsqrt(jnp.mean(vc * vc, axis=-1, keepdims=True) + EPS) * lnw_ref[...] + lnb_ref[...]
    if emit_v:
        v_ref[...] = vn
    vb = vn.astype(BF16)
    for ci in range(tm // CHUNK_C):
        rs = slice(ci * CHUNK_C, (ci + 1) * CHUNK_C)
        for g in range(H_C):
            cs = slice(g * gd, (g + 1) * gd)
            mixed = _dot(wm_ref[g], vb[rs, cs]) + jnp.concatenate([bs_ref[g]] * (gd // LANE), axis=1)
            gated_ref[rs, cs] = (z[rs, cs] * mixed).astype(BF16)
    o_ref[...] = x + _dot(gated_ref[...], wout_ref[...])


def _odd_mixer(x, nw, w_in, ln_w, ln_b, wm, bs, w_out, emit_v):
    t, d = x.shape
    dc = ln_w.shape[1]
    tm = _row_tile(t, 512)
    row = lambda w: pl.BlockSpec((tm, w), lambda i: (i, 0))
    full = lambda a: pl.BlockSpec(a.shape, lambda i: (0,) * a.ndim)
    out_specs = [row(d)] + ([row(dc)] if emit_v else [])
    out_shape = [jax.ShapeDtypeStruct((t, d), F32)] + ([jax.ShapeDtypeStruct((t, dc), F32)] if emit_v else [])
    return pl.pallas_call(
        functools.partial(_odd_kernel, emit_v=emit_v),
        grid=(t // tm,),
        in_specs=[row(d), full(nw), full(w_in), full(ln_w), full(ln_b), full(wm), full(bs), full(w_out)],
        out_specs=out_specs,
        out_shape=out_shape,
        scratch_shapes=[pltpu.VMEM((tm, dc), BF16)],
        compiler_params=_params("parallel"),
        name="odd_mixer",
    )(x, nw, w_in, ln_w, ln_b, wm, bs, w_out)


def _prep_even_weights(w_in, f_bias, a_log, dt_bias):
    d = w_in.shape[0]
    small = jnp.concatenate(
        [w_in[:, 1536:1544], w_in[:, 3080:3088], jnp.zeros((d, LANE - 16), w_in.dtype)], axis=1)
    w = jnp.concatenate([w_in[:, 0:512], w_in[:, 1544:3080], w_in[:, 3088:3600], small], axis=1)
    bias = jnp.zeros((1, LANE), F32).at[0, 0:8].set(f_bias).at[0, 8:12].set(dt_bias)
    alog = jnp.zeros((1, LANE), F32).at[0, 8:12].set(a_log)
    return w.astype(BF16), w_in[:, 512:1536].T.astype(BF16), bias, alog


def kernel(x_prompt, x_sample, cache_k, cache_v, cache_logf, state_delta, state_conv, page_table, norm_mix, norm_ffn, norm_final, w_in_even, w_out_even, fox_forget_bias, dn_conv_w, dn_a_log, dn_dt_bias, dn_norm_w, w_in_odd, gm_ln_w, gm_ln_b, gm_spatial_w, gm_spatial_b, w_out_odd, moe_router_group, moe_router_expert, moe_w_gate, moe_w_up, moe_w_down):
    n_p, seq, d = x_prompt.shape
    n_s, dec = x_sample.shape[:2]
    page = cache_k.shape[2]
    mix_a = H_A * D_A
    xp = x_prompt.reshape(n_p * seq, d)
    xs = x_sample.reshape(n_s * dec, d)
    row = lambda a: a[None, :]

    even_w = (row(norm_mix[0]),) + _prep_even_weights(w_in_even[0], fox_forget_bias[0], dn_a_log[0], dn_dt_bias[0])
    from_kt = lambda a, n, length: jnp.transpose(a.reshape(n, H_A, D_A, length), (0, 3, 1, 2))[None]
    w_out_e = w_out_even[0].astype(BF16)
    conv_w = jnp.zeros((8, dn_conv_w.shape[2]), F32).at[:CONV_W].set(dn_conv_w[0])
    onw = row(dn_norm_w[0])

    def delta(conv3, prefix, small3, gz3, s0, valid, nb):
        wv, wk, qd, kd, qk, gl = _delta_prep(conv3, prefix, conv_w, small3, valid)
        return _delta_scan(wv, wk, qd, kd, qk, gl, s0, gz3, onw, nb)

    q, kt, vt, c, gz, s = _even_in(xp, *even_w, n_p, seq)
    logf_p = s[:, :H_A].reshape(n_p, seq, H_A)
    csum = _cumsum_lanes(jnp.swapaxes(logf_p, 1, 2).reshape(n_p * H_A, seq)).reshape(n_p, H_A, seq)
    as3 = lambda a: a.reshape(n_p, seq, a.shape[-1])
    oa_p = _fox_prompt(as3(q), kt, vt, csum)
    conv_p = as3(c)
    ob_p, sd_p = delta(conv_p, jnp.zeros((n_p, 8, conv_p.shape[-1]), F32), as3(s), as3(gz),
                       jnp.zeros((n_p, H_B, DK_B, DK_B), F32), seq, 4)
    xp = _even_out(xp, oa_p.reshape(n_p * seq, mix_a), ob_p.reshape(n_p * seq, -1), w_out_e)
    k_p = from_kt(kt, n_p, seq)
    v_p = from_kt(vt, n_p, seq)
    conv_state_p = conv_p[:, seq - (CONV_W - 1):][None]

    q, kt, vt, c, gz, s = _even_in(xs, *even_w, 1, n_s * dec)
    as3 = lambda a: a.reshape(n_s, dec, a.shape[-1])
    padrows = lambda a, n: jnp.pad(a, ((0, 0), (0, n - a.shape[1]), (0, 0)))
    logf_s = s[:, :H_A].reshape(n_s, dec, H_A)
    pad_keys = lambda a: jnp.pad(a, ((0, 0), (0, 0), (0, page - dec)))
    per_seq = lambda a: pad_keys(jnp.swapaxes(a[0].reshape(mix_a, n_s, dec), 0, 1))
    oa_s = _fox_sample(page_table, as3(q), per_seq(kt), per_seq(vt), pad_keys(jnp.swapaxes(logf_s, 1, 2)),
                       jnp.transpose(cache_k[0], (0, 2, 3, 1)), jnp.transpose(cache_v[0], (0, 2, 3, 1)),
                       jnp.swapaxes(cache_logf[0], 1, 2))
    conv_s = as3(c)
    lp = DELTA_CHUNK
    prefix_s = jnp.pad(state_conv[0], ((0, 0), (8 - (CONV_W - 1), 0), (0, 0)))
    ob_s, sd_s = delta(padrows(conv_s, lp), prefix_s, padrows(as3(s), lp), padrows(as3(gz), lp),
                       state_delta[0], dec, 8)
    xs = _even_out(xs, oa_s.reshape(n_s * dec, mix_a).astype(BF16), ob_s[:, :dec].reshape(n_s * dec, -1), w_out_e)
    k_s = kt[0].T.reshape(1, n_s, dec, H_A, D_A)
    v_s = vt[0].T.reshape(1, n_s, dec, H_A, D_A)
    conv_state_s = jnp.concatenate([state_conv[0], conv_s], axis=1)[:, dec:][None]

    def moe(x, layer, final):
        wr = jnp.concatenate([moe_router_expert[layer], moe_router_group[layer],
                              jnp.zeros((d, LANE - N_EXPERTS - N_GROUPS), F32)], axis=1)
        wr_hi = wr.astype(BF16)
        wr = jnp.stack([wr_hi, (wr - wr_hi.astype(F32)).astype(BF16)])
        gates = _router(x, row(norm_ffn[layer]), wr)
        experts = _moe_sorted if x.shape[0] >= 4096 else _moe_dense
        return experts(x, row(norm_ffn[layer]), gates, wg, wu, wd, row(norm_final), layer, final)

    wg, wu, wd = moe_w_gate.astype(BF16), moe_w_up.astype(BF16), moe_w_down.astype(BF16)
    xp = moe(xp, 0, False)
    xs = moe(xs, 0, False)

    idx = jnp.arange(CHUNK_C)
    w_m = jnp.where(idx[:, None] >= idx[None, :], gm_spatial_w[0], 0.0)
    b_s = gm_spatial_b[0]
    lanes = lambda b: jnp.broadcast_to(b[:, :, None], b.shape + (LANE,))
    reps = CHUNK_C // dec
    assert n_s * dec == CHUNK_C, "sample group must fill exactly one 128-row mixing tile"
    w_m_s = (jnp.eye(reps, dtype=F32)[None, :, None, :, None] * w_m[:, None, :dec, None, :dec]).reshape(H_C, CHUNK_C, CHUNK_C)
    b_s_s = jnp.tile(b_s[:, :dec], (1, reps))
    odd_w = (row(norm_mix[1]), w_in_odd[0].astype(BF16), row(gm_ln_w[0]), row(gm_ln_b[0]))
    w_out_o = w_out_odd[0].astype(BF16)
    (xp,) = _odd_mixer(xp, *odd_w, w_m.astype(BF16), lanes(b_s), w_out_o, False)
    xs, v_rows = _odd_mixer(xs, *odd_w, w_m_s.astype(BF16), lanes(b_s_s), w_out_o, True)

    y_p = moe(xp, 1, True).reshape(n_p, seq, d)
    y_s = moe(xs, 1, True).reshape(n_s, dec, d)
    return (y_p, y_s, k_p, v_p, logf_p[None], sd_p[None], conv_state_p,
            k_s, v_s, logf_s[None], sd_s[None], conv_state_s, v_rows.reshape(1, n_s, dec, -1))
```

```python
import functools

import jax
import jax.numpy as jnp
from jax import lax
from jax.experimental import pallas as pl
from jax.experimental.pallas import tpu as pltpu

F32 = jnp.float32
BF16 = jnp.bfloat16
EPS = 1e-6
LANE = 128
VMEM_LIMIT = 56 * 1024 * 1024
HIGHEST = lax.Precision.HIGHEST

H_A, D_A = 8, 64
H_B, DK_B = 4, 128
CONV_W = 4
DELTA_CHUNK = 64
N_GROUPS, EXPERTS_PER_GROUP = 4, 8
N_EXPERTS = N_GROUPS * EXPERTS_PER_GROUP
CHUNK_C = 128
H_C = 8


def _params(*sem):
    return pltpu.CompilerParams(dimension_semantics=sem, vmem_limit_bytes=VMEM_LIMIT)


def _row_tile(t, pref):
    return pref if t % pref == 0 else t


def _rms(x, w):
    return x * lax.rsqrt(jnp.mean(x * x, axis=-1, keepdims=True) + EPS) * w


def _softplus_tail(z):
    return jnp.log1p(jnp.exp(-jnp.abs(z)))


def _even_in_kernel(x_ref, nw_ref, w_ref, wkv_ref, bias_ref, alog_ref,
                    q_ref, k_ref, v_ref, c_ref, gz_ref, s_ref):
    hb = _rms(x_ref[...], nw_ref[...]).astype(BF16)

    def proj(lo, hi):
        return jnp.dot(hb, w_ref[:, lo:hi], preferred_element_type=F32)

    q_ref[...] = proj(0, 512)
    c_ref[...] = proj(512, 2048)
    gz_ref[...] = proj(2048, 2560)
    k_ref[0] = lax.dot_general(wkv_ref[:512, :], hb, (((1,), (1,)), ((), ())), preferred_element_type=F32)
    v_ref[0] = lax.dot_general(wkv_ref[512:, :], hb, (((1,), (1,)), ((), ())), preferred_element_type=F32)
    z = proj(2560, 2688) + bias_ref[...]
    tail = _softplus_tail(z)
    logf = jnp.minimum(z, 0.0) - tail
    g = -jnp.exp(alog_ref[...]) * (jnp.maximum(z, 0.0) + tail)
    beta = 1.0 / (1.0 + jnp.exp(-z))
    lane = lax.broadcasted_iota(jnp.int32, z.shape, 1)
    s_ref[...] = jnp.where(lane < 8, logf, jnp.where(lane < 12, g, jnp.where(lane < 16, beta, 0.0)))


def _even_in(x, nw, w, wkv, bias, alog, n, length):
    t, d = x.shape
    tm = _row_tile(length, 512)
    per_seq = length // tm
    mix = wkv.shape[0] // 2
    row = lambda wd: pl.BlockSpec((tm, wd), lambda i: (i, 0))
    full = lambda a: pl.BlockSpec(a.shape, lambda i: (0, 0))
    kt = pl.BlockSpec((1, mix, tm), lambda i: (i // per_seq, 0, i % per_seq))
    rows = lambda wd: jax.ShapeDtypeStruct((t, wd), F32)
    kts = jax.ShapeDtypeStruct((n, mix, length), F32)
    return pl.pallas_call(
        _even_in_kernel,
        grid=(t // tm,),
        in_specs=[row(d), full(nw), full(w), full(wkv), full(bias), full(alog)],
        out_specs=[row(mix), kt, kt, row(3 * mix), row(mix), row(LANE)],
        out_shape=[rows(mix), kts, kts, rows(3 * mix), rows(mix), rows(LANE)],
        compiler_params=_params("parallel"),
        name="even_in_proj",
    )(x, nw, w, wkv, bias, alog)


def _cumsum_kernel(x_ref, o_ref):
    rows, length = x_ref.shape
    r = lax.broadcasted_iota(jnp.int32, (LANE, LANE), 0)
    c = lax.broadcasted_iota(jnp.int32, (LANE, LANE), 1)
    upper = (r <= c).astype(F32)
    carry = jnp.zeros((rows, 1), F32)
    for b in range(length // LANE):
        blk = jnp.dot(x_ref[:, b * LANE:(b + 1) * LANE], upper,
                      preferred_element_type=F32, precision=HIGHEST) + carry
        o_ref[:, b * LANE:(b + 1) * LANE] = blk
        carry = blk[:, LANE - 1:LANE]


def _cumsum_lanes(x):
    rows, length = x.shape
    tr = 8
    return pl.pallas_call(
        _cumsum_kernel,
        grid=(rows // tr,),
        in_specs=[pl.BlockSpec((tr, length), lambda i: (i, 0))],
        out_specs=pl.BlockSpec((tr, length), lambda i: (i, 0)),
        out_shape=jax.ShapeDtypeStruct((rows, length), F32),
        compiler_params=_params("parallel"),
        name="logf_cumsum",
    )(x)


def _fox_prompt_kernel(q_ref, k_ref, v_ref, c_ref, o_ref, *, tq):
    qi = pl.program_id(2)
    hp = pl.program_id(1)
    scale = D_A ** -0.5
    lane = lax.broadcasted_iota(jnp.int32, (tq, LANE), 1)
    first = lane < D_A
    q = q_ref[0] * scale
    qs = [jnp.where(first, q, 0.0).astype(BF16), jnp.where(first, 0.0, q).astype(BF16)]
    q0 = pl.multiple_of(qi * tq, tq)

    def crow(h, start):
        return c_ref[0, pl.ds(2 * hp + h, 1), pl.ds(start, tq)]

    cq = [crow(h, q0)[:, 0:1] for h in range(2)]

    def block(j0, carry, mask):
        kb = k_ref[0, :, pl.ds(j0, tq)].astype(BF16)
        vb = v_ref[0, :, pl.ds(j0, tq)].astype(BF16)
        hh = range(2)
        s = [_dot(qs[h], kb) + (cq[h] - crow(h, j0)) for h in hh]
        if mask is not None:
            s = [jnp.where(mask, a, -jnp.inf) for a in s]
        m_new = [jnp.maximum(carry[h][0], jnp.max(s[h], axis=-1, keepdims=True)) for h in hh]
        alpha = [jnp.exp(carry[h][0] - m_new[h]) for h in hh]
        p = [jnp.exp(s[h] - m_new[h]) for h in hh]
        l = [alpha[h] * carry[h][1] + jnp.sum(p[h], axis=-1, keepdims=True) for h in hh]
        pv = [_dot_nt(p[h].astype(BF16), vb) for h in hh]
        acc = [alpha[h] * carry[h][2] + pv[h] for h in hh]
        return tuple((m_new[h], l[h], acc[h]) for h in hh)

    def body(j, carry):
        return block(pl.multiple_of(j * tq, tq), carry, None)

    init = (jnp.full((tq, 1), -jnp.inf, F32), jnp.zeros((tq, 1), F32), jnp.zeros((tq, LANE), F32))
    carry = lax.fori_loop(0, qi, body, (init, init))
    r = lax.broadcasted_iota(jnp.int32, (tq, tq), 0)
    cidx = lax.broadcasted_iota(jnp.int32, (tq, tq), 1)
    (_, l0, a0), (_, l1, a1) = block(q0, carry, cidx <= r)
    o_ref[0] = jnp.where(first, a0 / l0, a1 / l1).astype(o_ref.dtype)


def _fox_prompt(q, k, v, c):
    n, length, width = q.shape
    tq = 512 if length % 512 == 0 else length
    grid = (n, width // LANE, length // tq)
    return pl.pallas_call(
        functools.partial(_fox_prompt_kernel, tq=tq),
        grid=grid,
        in_specs=[
            pl.BlockSpec((1, tq, LANE), lambda b, h, i: (b, i, h)),
            pl.BlockSpec((1, LANE, length), lambda b, h, i: (b, h, 0)),
            pl.BlockSpec((1, LANE, length), lambda b, h, i: (b, h, 0)),
            pl.BlockSpec((1, H_A, length), lambda b, h, i: (b, 0, 0)),
        ],
        out_specs=pl.BlockSpec((1, tq, LANE), lambda b, h, i: (b, i, h)),
        out_shape=jax.ShapeDtypeStruct((n, length, width), BF16),
        compiler_params=_params("parallel", "parallel", "arbitrary"),
        name="fox_prompt",
    )(q, k, v, c)


def _even_out_kernel(x_ref, a_ref, b_ref, w_ref, o_ref):
    half = a_ref.shape[1]
    acc = jnp.dot(a_ref[...], w_ref[:half, :], preferred_element_type=F32)
    acc += jnp.dot(b_ref[...], w_ref[half:, :], preferred_element_type=F32)
    o_ref[...] = x_ref[...] + acc


def _even_out(x, oa, ob, w):
    t, d = x.shape
    tm = _row_tile(t, 512)
    row = lambda a: pl.BlockSpec((tm, a.shape[1]), lambda i: (i, 0))
    return pl.pallas_call(
        _even_out_kernel,
        grid=(t // tm,),
        in_specs=[row(x), row(oa), row(ob), pl.BlockSpec(w.shape, lambda i: (0, 0))],
        out_specs=row(x),
        out_shape=jax.ShapeDtypeStruct((t, d), F32),
        compiler_params=_params("parallel"),
        name="even_out_proj",
    )(x, oa, ob, w)


def _split3(x):
    hi = x.astype(BF16)
    r1 = x - hi.astype(F32)
    mid = r1.astype(BF16)
    lo = (r1 - mid.astype(F32)).astype(BF16)
    return hi, mid, lo


def _dot(a, b):
    return jnp.dot(a, b, preferred_element_type=F32)


def _dot_nt(a, b):
    return lax.dot_general(a, b, (((1,), (1,)), ((), ())), preferred_element_type=F32)


def _dot_tn(a, b):
    return lax.dot_general(a, b, (((0,), (0,)), ((), ())), preferred_element_type=F32)


def _silu(x):
    return x / (1.0 + jnp.exp(-x))


def _delta_prep_kernel(x_ref, xprev_ref, pre_ref, cw_ref, sm_ref,
                       wv_ref, wk_ref, qd_ref, kd_ref, qk_ref, gl_ref, *, ta, valid):
    i = pl.program_id(1)
    c = DELTA_CHUNK
    dk = DK_B
    rr = H_B * c
    x = x_ref[0]
    prev = jnp.where(i == 0, pre_ref[0], xprev_ref[0])
    xcat = jnp.concatenate([prev, x], axis=0)
    conv = x * cw_ref[CONV_W - 1:CONV_W, :]
    for s in range(1, CONV_W):
        conv += pltpu.roll(xcat, s, axis=0)[8:] * cw_ref[CONV_W - 1 - s:CONV_W - s, :]
    act = _silu(conv)
    sm = sm_ref[0]

    ri = lax.broadcasted_iota(jnp.int32, (rr, rr), 0)
    ci = lax.broadcasted_iota(jnp.int32, (rr, rr), 1)
    same_head = (ri // c) == (ci // c)
    same_bf = same_head.astype(BF16)
    tri = (same_head & (ri >= ci)).astype(BF16)
    lane = lax.broadcasted_iota(jnp.int32, (c, LANE), 1)
    wrow = lax.broadcasted_iota(jnp.int32, (c, rr), 0)
    wlane = lax.broadcasted_iota(jnp.int32, (c, rr), 1)
    whead = wlane // c
    wcol = wlane % c
    incl = wrow >= wcol
    strict = wrow > wcol
    r64 = lax.broadcasted_iota(jnp.int32, (c, c), 0)
    c64 = lax.broadcasted_iota(jnp.int32, (c, c), 1)
    tri64 = (r64 >= c64).astype(BF16)
    ones64 = jnp.ones((c, c), BF16)

    def block_diag(wide_bf):
        return jnp.concatenate([wide_bf] * H_B, axis=0) * same_bf

    def fold(bd):
        return sum(jnp.where(whead == h, bd[h * c:(h + 1) * c], 0.0) for h in range(H_B))

    qn, kn, va = [], [], []
    for h in range(H_B):
        qa = act[:, h * dk:(h + 1) * dk]
        ka = act[:, (H_B + h) * dk:(H_B + h + 1) * dk]
        qn.append(qa * lax.rsqrt(jnp.sum(qa * qa, axis=-1, keepdims=True) + EPS) * (dk ** -0.5))
        kn.append(ka * lax.rsqrt(jnp.sum(ka * ka, axis=-1, keepdims=True) + EPS))
        va.append(act[:, (2 * H_B + h) * dk:(2 * H_B + h + 1) * dk])

    chunks = range(ta // c)
    stack = lambda parts: jnp.concatenate(parts, axis=0)
    q, k, v, b_col, gcb, g_wide = [], [], [], [], [], []
    for s in chunks:
        r0 = s * c
        live = lax.broadcasted_iota(jnp.int32, (c, 1), 0) + (i * ta + r0) < valid
        smc = sm[r0:r0 + c]
        pick = lambda ln: jnp.where(live, jnp.sum(jnp.where(lane == ln, smc, 0.0), axis=-1, keepdims=True), 0.0)
        q.append(stack([a[r0:r0 + c] for a in qn]))
        k.append(stack([a[r0:r0 + c] for a in kn]))
        v.append(stack([a[r0:r0 + c] for a in va]))
        b_col.append(stack([pick(12 + h) for h in range(H_B)]))
        g_cols = [pick(8 + h) for h in range(H_B)]
        g_stack = stack([jnp.broadcast_to(g, (c, LANE)) for g in g_cols])
        gcb.append(sum(_dot(tri, p) for p in _split3(g_stack)))
        g_wide.append(sum(jnp.where(whead == h, g_cols[h], 0.0) for h in range(H_B)))

    gc_i = [sum(_dot(tri64, p) for p in _split3(g)) for g in g_wide]
    gc_j = [sum(_dot(ones64, p) for p in _split3(jnp.where(wrow <= wcol, g, 0.0))) for g in g_wide]
    decay = [jnp.exp(jnp.where(incl, a - b, -jnp.inf)) for a, b in zip(gc_i, gc_j)]
    eg = [jnp.exp(g) for g in gcb]
    kb = [a * b for a, b in zip(k, b_col)]
    kbf = [a.astype(BF16) for a in k]
    a_mat = [jnp.where(strict, fold(_dot_nt(a.astype(BF16), b)) * d, 0.0) for a, b, d in zip(kb, kbf, decay)]
    qk = [jnp.where(incl, fold(_dot_nt(a.astype(BF16), b)) * d, 0.0) for a, b, d in zip(q, kbf, decay)]
    y = [-a for a in a_mat]
    p = list(y)
    ybd = [block_diag(a.astype(BF16)) for a in y]
    for _ in range(5):
        y = [_dot(a.astype(BF16), bd) for a, bd in zip(y, ybd)]
        ybd = [block_diag(a.astype(BF16)) for a in y]
        p = [a + b + _dot(a.astype(BF16), bd) for a, b, bd in zip(p, y, ybd)]
    rhs = [jnp.concatenate([a * b, kbb * e], axis=1) for a, b, kbb, e in zip(v, b_col, kb, eg)]
    w = [r + _dot(block_diag(a.astype(BF16)), r.astype(BF16)) for a, r in zip(p, rhs)]

    for s in chunks:
        r0 = s * c
        for h in range(H_B):
            hr = slice(h * c, (h + 1) * c)
            hs = slice(h * dk, (h + 1) * dk)
            g_last = gcb[s][h * c + c - 1:(h + 1) * c, :]
            wv_ref[0, r0:r0 + c, hs] = w[s][hr, :dk]
            wk_ref[0, r0:r0 + c, hs] = w[s][hr, dk:].astype(BF16)
            qd_ref[0, r0:r0 + c, hs] = (q[s][hr] * eg[s][hr]).astype(BF16)
            kd_ref[0, r0:r0 + c, hs] = (k[s][hr] * jnp.exp(g_last - gcb[s][hr])).astype(BF16)
            qk_ref[0, h, r0:r0 + c, :] = qk[s][:, h * c:(h + 1) * c].astype(BF16)
            gl_ref[0, s, h:h + 1, :] = jnp.exp(g_last)


def _delta_prep(conv_in, prefix, conv_w, small, valid):
    n, lp, cd = conv_in.shape
    ta = LANE if lp % LANE == 0 else lp
    nt = lp // ta
    tb = ta // 8
    f = lambda dt, w: jax.ShapeDtypeStruct((n, lp, w), dt)
    blk = lambda w: pl.BlockSpec((1, ta, w), lambda b, i: (b, i, 0))
    width = H_B * DK_B
    return pl.pallas_call(
        functools.partial(_delta_prep_kernel, ta=ta, valid=valid),
        grid=(n, nt),
        in_specs=[
            blk(cd),
            pl.BlockSpec((1, 8, cd), lambda b, i: (b, jnp.maximum(i * tb - 1, 0), 0)),
            pl.BlockSpec((1, 8, cd), lambda b, i: (b, 0, 0)),
            pl.BlockSpec((8, cd), lambda b, i: (0, 0)),
            blk(LANE),
        ],
        out_specs=[
            blk(width), blk(width), blk(width), blk(width),
            pl.BlockSpec((1, H_B, ta, DELTA_CHUNK), lambda b, i: (b, 0, i, 0)),
            pl.BlockSpec((1, ta // DELTA_CHUNK, H_B, LANE), lambda b, i: (b, i, 0, 0)),
        ],
        out_shape=[
            f(F32, width), f(BF16, width), f(BF16, width), f(BF16, width),
            jax.ShapeDtypeStruct((n, H_B, lp, DELTA_CHUNK), BF16),
            jax.ShapeDtypeStruct((n, lp // DELTA_CHUNK, H_B, LANE), F32),
        ],
        compiler_params=_params("parallel", "parallel"),
        name="delta_prep",
    )(conv_in, conv_in, prefix, conv_w, small)


def _delta_scan_kernel(wv_ref, wk_ref, qd_ref, kd_ref, qk_ref, gl_ref, s0_ref, gz_ref, onw_ref,
                       o_ref, s_ref, *, nb, n_chunks):
    c = DELTA_CHUNK
    dk = DK_B

    @pl.when(pl.program_id(1) == 0)
    def _():
        s_ref[...] = s0_ref[...]

    onw = onw_ref[...]

    chains = [(b, h) for b in range(nb) for h in range(H_B)]
    cols = lambda h: slice(h * dk, (h + 1) * dk)

    def body(ci, _):
        r0 = pl.multiple_of(ci * c, c)
        rows = pl.ds(r0, c)
        state = [s_ref[b, h] for b, h in chains]
        prod = [_dot(jnp.concatenate([wk_ref[b, rows, cols(h)], qd_ref[b, rows, cols(h)]], axis=0), s.astype(BF16))
                for (b, h), s in zip(chains, state)]
        vb = [(wv_ref[b, rows, cols(h)] - p[:c]).astype(BF16) for (b, h), p in zip(chains, prod)]
        o = [p[c:] + _dot(qk_ref[b, h, rows, :], v) for (b, h), p, v in zip(chains, prod, vb)]
        new = [s * gl_ref[b, pl.ds(ci, 1), h, :] + _dot_tn(kd_ref[b, rows, cols(h)], v)
               for (b, h), s, v in zip(chains, state, vb)]
        for (b, h), s, oo in zip(chains, new, o):
            s_ref[b, h] = s
            on = oo * lax.rsqrt(jnp.mean(oo * oo, axis=-1, keepdims=True) + EPS) * onw
            o_ref[b, rows, cols(h)] = (on * _silu(gz_ref[b, rows, cols(h)])).astype(o_ref.dtype)
        return 0

    lax.fori_loop(0, n_chunks, body, 0)


def _delta_scan(wv, wk, qd, kd, qk, gl, s0, gz, onw, nb):
    n, lp, width = wv.shape
    tl = _row_tile(lp, 512)
    n_chunks = tl // DELTA_CHUNK
    seq = pl.BlockSpec((nb, tl, width), lambda i, t: (i, t, 0))
    state = pl.BlockSpec((nb,) + s0.shape[1:], lambda i, t: (i, 0, 0, 0))
    return pl.pallas_call(
        functools.partial(_delta_scan_kernel, nb=nb, n_chunks=n_chunks),
        grid=(n // nb, lp // tl),
        in_specs=[seq, seq, seq, seq,
                  pl.BlockSpec((nb, H_B, tl, DELTA_CHUNK), lambda i, t: (i, 0, t, 0)),
                  pl.BlockSpec((nb, n_chunks, H_B, LANE), lambda i, t: (i, t, 0, 0)),
                  state, seq, pl.BlockSpec(onw.shape, lambda i, t: (0, 0))],
        out_specs=[seq, state],
        out_shape=[jax.ShapeDtypeStruct((n, lp, width), BF16), jax.ShapeDtypeStruct(s0.shape, F32)],
        compiler_params=_params("parallel", "arbitrary"),
        name="delta_scan",
    )(wv, wk, qd, kd, qk, gl, s0, gz, onw)


def _fox_sample_kernel(pt_ref, q_ref, kn_ref, vn_ref, lfn_ref, kc_hbm, vc_hbm, lfc_hbm, o_ref,
                       qbd_ref, m_ref, l_ref, acc_ref, carry_ref, kbuf, vbuf, lfbuf, sems,
                       *, pages_per_step, n_q):
    pps = pages_per_step
    b = pl.program_id(0)
    j = pl.program_id(1)
    n_b = pl.num_programs(0)
    n_steps = pl.num_programs(1)
    n_pages = pt_ref.shape[1]
    rows = n_q * H_A
    page = lfn_ref.shape[2]
    width = H_A * D_A
    ri = lax.broadcasted_iota(jnp.int32, (page, page), 0)
    ci = lax.broadcasted_iota(jnp.int32, (page, page), 1)

    per_seq = n_steps - 1
    n_slots = kbuf.shape[0]
    g = b * per_seq + j - 1

    def copies(gg, p):
        src = pt_ref[gg // per_seq, n_pages - 1 - ((gg % per_seq) * pps + p)]
        s = gg % n_slots
        return (pltpu.make_async_copy(kc_hbm.at[src], kbuf.at[s, p], sems.at[0, s]),
                pltpu.make_async_copy(vc_hbm.at[src], vbuf.at[s, p], sems.at[1, s]),
                pltpu.make_async_copy(lfc_hbm.at[src], lfbuf.at[s, p], sems.at[2, s]))

    def fetch(gg):
        @pl.when(gg < n_b * per_seq)
        def _():
            for p in range(pps):
                for cp in copies(gg, p):
                    cp.start()

    @pl.when((b == 0) & (j == 0))
    def _():
        for ahead in range(n_slots - 1):
            fetch(jnp.int32(ahead))

    @pl.when(j > 0)
    def _():
        for p in range(pps):
            for cp in copies(g, p):
                cp.wait()
        fetch(g + n_slots - 1)

    def attend(blocks):
        qbd = qbd_ref[...]
        ss = []
        for kt, _, bias8, mask in blocks:
            s = _dot(qbd, kt.astype(BF16)) + jnp.concatenate([bias8] * n_q, axis=0)
            ss.append(s if mask is None else jnp.where(mask, s, -jnp.inf))
        m_old = m_ref[...]
        m_new = m_old
        for s in ss:
            m_new = jnp.maximum(m_new, jnp.max(s, axis=-1, keepdims=True))
        alpha = jnp.exp(m_old - m_new)
        ps = [jnp.exp(s - m_new) for s in ss]
        sums = [jnp.sum(p, axis=-1, keepdims=True) for p in ps]
        pvs = [_dot_nt(p.astype(BF16), vt.astype(BF16)) for p, (_, vt, _, _) in zip(ps, blocks)]
        l_ref[...] = alpha * l_ref[...] + sum(sums)
        acc_ref[...] = alpha * acc_ref[...] + sum(pvs)
        m_ref[...] = m_new

    @pl.when(j == 0)
    def _():
        q = q_ref[0] * (D_A ** -0.5)
        qrep = jnp.concatenate([jnp.broadcast_to(q[t:t + 1], (H_A, width)) for t in range(n_q)], axis=0)
        r = lax.broadcasted_iota(jnp.int32, qrep.shape, 0)
        ln = lax.broadcasted_iota(jnp.int32, qrep.shape, 1)
        qbd_ref[...] = jnp.where(ln // D_A == r % H_A, qrep, 0.0).astype(BF16)
        m_ref[...] = jnp.full(m_ref.shape, -jnp.inf, F32)
        l_ref[...] = jnp.zeros(l_ref.shape, F32)
        acc_ref[...] = jnp.zeros(acc_ref.shape, F32)
        carry_ref[...] = jnp.zeros(carry_ref.shape, F32)
        upper = (ri <= ci).astype(BF16)
        csum = sum(_dot(p, upper) for p in _split3(lfn_ref[0]))
        r2 = lax.broadcasted_iota(jnp.int32, (rows, page), 0)
        c2 = lax.broadcasted_iota(jnp.int32, (rows, page), 1)
        attend([(kn_ref[0], vn_ref[0], -csum, c2 <= r2 // H_A)])

    @pl.when(j > 0)
    def _():
        after = (ri > ci).astype(BF16)
        carry = carry_ref[...]
        slot = g % n_slots
        lfs = [lfbuf[slot, p] for p in range(pps)]
        pieces = [_split3(lf) for lf in lfs]
        suffix = [sum(_dot(p, after) for p in ps) for ps in pieces]
        totals = [jnp.sum(lf, axis=-1, keepdims=True) for lf in lfs]
        blocks = []
        for p, (sfx, tot) in enumerate(zip(suffix, totals)):
            blocks.append((kbuf[slot, p].reshape(width, page), vbuf[slot, p].reshape(width, page), carry + sfx, None))
            carry = carry + tot
        attend(blocks)
        carry_ref[...] = carry

    @pl.when(j == pl.num_programs(1) - 1)
    def _():
        out = acc_ref[...] / l_ref[...]
        r = lax.broadcasted_iota(jnp.int32, out.shape, 0)
        ln = lax.broadcasted_iota(jnp.int32, out.shape, 1)
        out = jnp.where(ln // D_A == r % H_A, out, 0.0)
        o_ref[0] = jnp.sum(out.reshape(n_q, H_A, width), axis=1)


def _fox_sample(page_table, q, kn, vn, lfn, cache_k, cache_v, cache_lf):
    b, n_q, width = q.shape
    n_pages = page_table.shape[1]
    page = cache_k.shape[3]
    rows = n_q * H_A
    pps = next(p for p in (16, 8, 1) if n_pages % p == 0)
    n_slots = 3
    cur = lambda i, j, pt: (i, 0, 0)
    hbm = pl.BlockSpec(memory_space=pl.ANY)
    grid_spec = pltpu.PrefetchScalarGridSpec(
        num_scalar_prefetch=1,
        grid=(b, n_pages // pps + 1),
        in_specs=[
            pl.BlockSpec((1, n_q, width), cur),
            pl.BlockSpec((1, width, page), cur),
            pl.BlockSpec((1, width, page), cur),
            pl.BlockSpec((1, H_A, page), cur),
            hbm, hbm, hbm,
        ],
        out_specs=pl.BlockSpec((1, n_q, width), cur),
        scratch_shapes=[
            pltpu.VMEM((rows, width), BF16),
            pltpu.VMEM((rows, 1), F32),
            pltpu.VMEM((rows, 1), F32),
            pltpu.VMEM((rows, width), F32),
            pltpu.VMEM((H_A, 1), F32),
            pltpu.VMEM((n_slots, pps, H_A, D_A, page), F32),
            pltpu.VMEM((n_slots, pps, H_A, D_A, page), F32),
            pltpu.VMEM((n_slots, pps, H_A, page), F32),
            pltpu.SemaphoreType.DMA((3, n_slots)),
        ],
    )
    return pl.pallas_call(
        functools.partial(_fox_sample_kernel, pages_per_step=pps, n_q=n_q),
        grid_spec=grid_spec,
        out_shape=jax.ShapeDtypeStruct((b, n_q, width), F32),
        compiler_params=_params("arbitrary", "arbitrary"),
        name="fox_sample",
    )(page_table, q, kn, vn, lfn, cache_k, cache_v, cache_lf)


def _router_kernel(x_ref, nw_ref, wr_ref, g_ref, grp_ref, rank_ref, cnt_ref, run_ref):
    step = pl.program_id(0)
    tm = x_ref.shape[0]

    @pl.when(step == 0)
    def _():
        run_ref[...] = jnp.zeros(run_ref.shape, F32)

    h = _rms(x_ref[...], nw_ref[...])
    h_hi = h.astype(BF16)
    h_lo = (h - h_hi.astype(F32)).astype(BF16)
    logits = _dot(h_hi, wr_ref[0]) + (_dot(h_lo, wr_ref[0]) + _dot(h_hi, wr_ref[1]))
    lane = lax.broadcasted_iota(jnp.int32, logits.shape, 1)
    lanef = lane.astype(F32)
    neg = -jnp.inf
    is_group = (lane >= N_EXPERTS) & (lane < N_EXPERTS + N_GROUPS)
    gl = jnp.where(is_group, logits, neg)
    gmax = jnp.max(gl, axis=-1, keepdims=True)
    g_sel = jnp.min(jnp.where(gl == gmax, lanef - N_EXPERTS, 1e9), axis=-1, keepdims=True)
    p_sel = 1.0 / jnp.sum(jnp.where(is_group, jnp.exp(logits - gmax), 0.0), axis=-1, keepdims=True)
    in_sel = (lane < N_EXPERTS) & ((lane // EXPERTS_PER_GROUP).astype(F32) == g_sel)
    el = jnp.where(in_sel, logits, neg)
    m1 = jnp.max(el, axis=-1, keepdims=True)
    i1 = jnp.min(jnp.where(el == m1, lanef, 1e9), axis=-1, keepdims=True)
    el2 = jnp.where(lanef == i1, neg, el)
    m2 = jnp.max(el2, axis=-1, keepdims=True)
    i2 = jnp.min(jnp.where(el2 == m2, lanef, 1e9), axis=-1, keepdims=True)
    e2 = jnp.exp(m2 - m1)
    w1 = p_sel / (1.0 + e2)
    g_ref[...] = jnp.where(lanef == i1, w1, jnp.where(lanef == i2, w1 * e2, jnp.where(lane == N_EXPERTS, g_sel, 0.0)))

    onehot_t = (lanef == g_sel).astype(F32).T[:8]
    ri = lax.broadcasted_iota(jnp.int32, (tm, tm), 0)
    ci = lax.broadcasted_iota(jnp.int32, (tm, tm), 1)
    earlier = _dot(onehot_t.astype(BF16), (ri < ci).astype(BF16))
    run = run_ref[...]
    gidx = lax.broadcasted_iota(jnp.int32, (8, tm), 0).astype(F32)
    rank_ref[0] = jnp.sum(onehot_t * (earlier + run), axis=0, keepdims=True).astype(jnp.int32)
    grp_ref[0] = jnp.sum(onehot_t * gidx, axis=0, keepdims=True).astype(jnp.int32)
    run = run + jnp.sum(onehot_t, axis=-1, keepdims=True)
    run_ref[...] = run
    cnt_ref[...] = jnp.broadcast_to(run, cnt_ref.shape)


def _router(x, nw, wr):
    t, d = x.shape
    tm = _row_tile(t, 512)
    nt = t // tm
    row = pl.BlockSpec((1, 1, tm), lambda i: (i, 0, 0))
    return pl.pallas_call(
        _router_kernel,
        grid=(nt,),
        in_specs=[pl.BlockSpec((tm, d), lambda i: (i, 0)), pl.BlockSpec(nw.shape, lambda i: (0, 0)),
                  pl.BlockSpec(wr.shape, lambda i: (0, 0, 0))],
        out_specs=[pl.BlockSpec((tm, LANE), lambda i: (i, 0)), row, row, pl.BlockSpec((8, LANE), lambda i: (0, 0))],
        out_shape=[jax.ShapeDtypeStruct((t, LANE), F32), jax.ShapeDtypeStruct((nt, 1, tm), jnp.int32),
                   jax.ShapeDtypeStruct((nt, 1, tm), jnp.int32), jax.ShapeDtypeStruct((8, LANE), F32)],
        scratch_shapes=[pltpu.VMEM((8, 1), F32)],
        compiler_params=_params("arbitrary"),
        name="moe_router",
    )(x, nw, wr)


def _moe_kernel(x_ref, nw_ref, g_ref, wg_ref, wu_ref, wd_ref, nf_ref, o_ref, hb_ref, acc_ref, *, final_norm):
    e = pl.program_id(1)

    @pl.when(e == 0)
    def _():
        hb_ref[...] = _rms(x_ref[...], nw_ref[...]).astype(BF16)
        acc_ref[...] = jnp.zeros(acc_ref.shape, F32)

    gates = g_ref[...]
    lane = lax.broadcasted_iota(jnp.int32, gates.shape, 1)
    gcol = jnp.sum(jnp.where(lane == e, gates, 0.0), axis=-1, keepdims=True)
    hb = hb_ref[...]
    act = _silu(_dot(hb, wg_ref[0, 0])) * _dot(hb, wu_ref[0, 0]) * gcol
    acc_ref[...] += _dot(act.astype(BF16), wd_ref[0, 0])

    @pl.when(e == pl.num_programs(1) - 1)
    def _():
        y = x_ref[...] + acc_ref[...]
        o_ref[...] = _rms(y, nf_ref[...]) if final_norm else y


def _moe_dense(x, nw, gates, wg, wu, wd, nf, layer, final_norm):
    t, d = x.shape
    tm = _row_tile(t, 1024)
    _, n_e, _, f = wg.shape
    return pl.pallas_call(
        functools.partial(_moe_kernel, final_norm=final_norm),
        grid=(t // tm, n_e),
        in_specs=[
            pl.BlockSpec((tm, d), lambda i, e: (i, 0)),
            pl.BlockSpec(nw.shape, lambda i, e: (0, 0)),
            pl.BlockSpec((tm, LANE), lambda i, e: (i, 0)),
            pl.BlockSpec((1, 1, d, f), lambda i, e: (layer, e, 0, 0)),
            pl.BlockSpec((1, 1, d, f), lambda i, e: (layer, e, 0, 0)),
            pl.BlockSpec((1, 1, f, d), lambda i, e: (layer, e, 0, 0)),
            pl.BlockSpec(nf.shape, lambda i, e: (0, 0)),
        ],
        out_specs=pl.BlockSpec((tm, d), lambda i, e: (i, 0)),
        out_shape=jax.ShapeDtypeStruct((t, d), F32),
        scratch_shapes=[pltpu.VMEM((tm, d), BF16), pltpu.VMEM((tm, d), F32)],
        compiler_params=_params("parallel", "arbitrary"),
        name="moe_experts",
    )(x, nw, gates, wg, wu, wd, nf)


def _row_copy(src, src_row, dst, dst_row, sem):
    return pltpu.make_async_copy(src.at[pl.ds(src_row, 1)], dst.at[pl.ds(dst_row, 1)], sem)


def _dispatch_kernel(zt_ref, dest_ref, x_ref, g_ref, o_ref, stage_ref, sem):
    tm, d = x_ref.shape

    @pl.when(pl.program_id(0) == 0)
    def _():
        stage_ref[...] = jnp.zeros(stage_ref.shape, F32)
        for k in range(zt_ref.shape[0]):
            cp = pltpu.make_async_copy(stage_ref, o_ref.at[pl.ds(pl.multiple_of(zt_ref[k] * tm, tm), tm)], sem)
            cp.start()
            cp.wait()

    stage_ref[:, :d] = x_ref[...]
    stage_ref[:, d:] = g_ref[...]

    def start(r2, _):
        for pr in range(2):
            r = 2 * r2 + pr
            _row_copy(stage_ref, r, o_ref, dest_ref[0, 0, r], sem).start(priority=pr)
        return 0

    def wait(r, _):
        _row_copy(stage_ref, 0, o_ref, 0, sem).wait()
        return 0

    lax.fori_loop(0, tm // 2, start, 0, unroll=4)
    lax.fori_loop(0, tm, wait, 0, unroll=8)


def _dispatch(zero_tiles, dest, x, gates, n_tiles, tm):
    t, d = x.shape
    grid_spec = pltpu.PrefetchScalarGridSpec(
        num_scalar_prefetch=1,
        grid=(t // tm,),
        in_specs=[pl.BlockSpec((1, 1, tm), lambda i, zt: (i, 0, 0), memory_space=pltpu.SMEM),
                  pl.BlockSpec((tm, d), lambda i, zt: (i, 0)),
                  pl.BlockSpec((tm, LANE), lambda i, zt: (i, 0))],
        out_specs=pl.BlockSpec(memory_space=pl.ANY),
        scratch_shapes=[pltpu.VMEM((tm, d + LANE), F32), pltpu.SemaphoreType.DMA(())],
    )
    return pl.pallas_call(
        _dispatch_kernel,
        grid_spec=grid_spec,
        out_shape=jax.ShapeDtypeStruct((n_tiles * tm, d + LANE), F32),
        compiler_params=_params("arbitrary"),
        name="moe_dispatch",
    )(zero_tiles, dest, x, gates)


def _moe_group_kernel(tg_ref, na_ref, xs_ref, nw_ref, wg_ref, wu_ref, wd_ref, nf_ref, o_ref, *, final_norm):
    i = pl.program_id(0)
    d = o_ref.shape[1]

    @pl.when(i < na_ref[0])
    def _():
        x = xs_ref[:, :d]
        gates = xs_ref[:, d:]
        hb = _rms(x, nw_ref[...]).astype(BF16)
        lane = lax.broadcasted_iota(jnp.int32, gates.shape, 1)
        first = tg_ref[i] * EXPERTS_PER_GROUP
        acc = jnp.zeros(x.shape, F32)
        for e in range(EXPERTS_PER_GROUP):
            gcol = jnp.sum(jnp.where(lane == first + e, gates, 0.0), axis=-1, keepdims=True)
            act = _silu(_dot(hb, wg_ref[0, 0, e])) * _dot(hb, wu_ref[0, 0, e]) * gcol
            acc += _dot(act.astype(BF16), wd_ref[0, 0, e])
        y = x + acc
        o_ref[...] = _rms(y, nf_ref[...]) if final_norm else y

    @pl.when(i >= na_ref[0])
    def _():
        o_ref[...] = jnp.zeros(o_ref.shape, F32)


def _moe_group(tile_group, n_active, xs, nw, wg, wu, wd, nf, tm, layer, final_norm):
    tp, dx = xs.shape
    d = dx - LANE
    wspec = lambda a: pl.BlockSpec((1, 1) + a.shape[2:], lambda i, tg, na: (layer, tg[i], 0, 0, 0))
    vec = lambda a: pl.BlockSpec(a.shape, lambda i, tg, na: (0, 0))
    grid_spec = pltpu.PrefetchScalarGridSpec(
        num_scalar_prefetch=2,
        grid=(tp // tm,),
        in_specs=[pl.BlockSpec((tm, dx), lambda i, tg, na: (i, 0)), vec(nw), wspec(wg), wspec(wu), wspec(wd), vec(nf)],
        out_specs=pl.BlockSpec((tm, d), lambda i, tg, na: (i, 0)),
    )
    return pl.pallas_call(
        functools.partial(_moe_group_kernel, final_norm=final_norm),
        grid_spec=grid_spec,
        out_shape=jax.ShapeDtypeStruct((tp, d), F32),
        compiler_params=_params("arbitrary"),
        name="moe_group_experts",
    )(tile_group, n_active, xs, nw, wg, wu, wd, nf)


def _combine_kernel(dest_ref, ys_ref, o_ref, sem):
    tm = o_ref.shape[0]

    def start(r2, _):
        for pr in range(2):
            r = 2 * r2 + pr
            _row_copy(ys_ref, dest_ref[0, 0, r], o_ref, r, sem).start(priority=pr)
        return 0

    def wait(r, _):
        _row_copy(ys_ref, 0, o_ref, 0, sem).wait()
        return 0

    lax.fori_loop(0, tm // 2, start, 0, unroll=4)
    lax.fori_loop(0, tm, wait, 0, unroll=8)


def _combine(dest, ys, t, tm):
    d = ys.shape[1]
    return pl.pallas_call(
        _combine_kernel,
        grid=(t // tm,),
        in_specs=[pl.BlockSpec((1, 1, tm), lambda i: (i, 0, 0), memory_space=pltpu.SMEM),
                  pl.BlockSpec(memory_space=pl.ANY)],
        out_specs=pl.BlockSpec((tm, d), lambda i: (i, 0)),
        out_shape=jax.ShapeDtypeStruct((t, d), F32),
        scratch_shapes=[pltpu.SemaphoreType.DMA(())],
        compiler_params=_params("arbitrary"),
        name="moe_combine",
    )(dest, ys)


def _moe_sorted(x, nw, routed, wg, wu, wd, nf, layer, final_norm):
    t, d = x.shape
    gates, grp, rank, cnt = routed
    tm = grp.shape[2]
    counts = [cnt[g, 0].astype(jnp.int32) for g in range(N_GROUPS)]
    tiles = [(c + tm - 1) // tm for c in counts]
    tile_end = [sum(tiles[:g + 1]) for g in range(N_GROUPS)]
    dest = rank + sum(jnp.where(grp == g, (tile_end[g] - tiles[g]) * tm, 0) for g in range(N_GROUPS))
    n_tiles = t // tm + N_GROUPS
    tile = jnp.arange(n_tiles, dtype=jnp.int32)
    tile_group = jnp.minimum(sum((tile >= e).astype(jnp.int32) for e in tile_end), N_GROUPS - 1)
    last = n_tiles - 1
    zero_tiles = jnp.stack([jnp.where(tiles[g] > 0, tile_end[g] - 1, last) for g in range(N_GROUPS)]
                           + [jnp.minimum(tile_end[-1] + k, last) for k in range(N_GROUPS)]).astype(jnp.int32)
    xs = _dispatch(zero_tiles, dest, x, gates, n_tiles, tm)
    split = lambda w: w.reshape((w.shape[0], N_GROUPS, EXPERTS_PER_GROUP) + w.shape[2:])
    ys = _moe_group(tile_group, tile_end[-1].reshape(1), xs, nw,
                    split(wg), split(wu), split(wd), nf, tm, layer, final_norm)
    return _combine(dest, ys, t, tm)


def _odd_kernel(x_ref, nw_ref, win_ref, lnw_ref, lnb_ref, wm_ref, bs_ref, wout_ref, *refs, emit_v):
    if emit_v:
        o_ref, v_ref, gated_ref = refs
    else:
        o_ref, gated_ref = refs
    tm = x_ref.shape[0]
    dc = lnw_ref.shape[1]
    gd = dc // H_C
    x = x_ref[...]
    hb = _rms(x, nw_ref[...]).astype(BF16)
    z = _dot(hb, win_ref[...])
    z = 0.5 * z * (1.0 + lax.erf(z * (2.0 ** -0.5)))
    v = z[:, dc:]
    mu = jnp.mean(v, axis=-1, keepdims=True)
    vc = v - mu
    vn = vc * lax.rsqrt(jnp.mean(vc * vc, axis=-1, keepdims=True) + EPS) * lnw_ref[...] + lnb_ref[...]
    if emit_v:
        v_ref[...] = vn
    vb = vn.astype(BF16)
    for ci in range(tm // CHUNK_C):
        rs = slice(ci * CHUNK_C, (ci + 1) * CHUNK_C)
        for g in range(H_C):
            cs = slice(g * gd, (g + 1) * gd)
            mixed = _dot(wm_ref[g], vb[rs, cs]) + jnp.concatenate([bs_ref[g]] * (gd // LANE), axis=1)
            gated_ref[rs, cs] = (z[rs, cs] * mixed).astype(BF16)
    o_ref[...] = x + _dot(gated_ref[...], wout_ref[...])


def _odd_mixer(x, nw, w_in, ln_w, ln_b, wm, bs, w_out, emit_v):
    t, d = x.shape
    dc = ln_w.shape[1]
    tm = _row_tile(t, 512)
    row = lambda w: pl.BlockSpec((tm, w), lambda i: (i, 0))
    full = lambda a: pl.BlockSpec(a.shape, lambda i: (0,) * a.ndim)
    out_specs = [row(d)] + ([row(dc)] if emit_v else [])
    out_shape = [jax.ShapeDtypeStruct((t, d), F32)] + ([jax.ShapeDtypeStruct((t, dc), F32)] if emit_v else [])
    return pl.pallas_call(
        functools.partial(_odd_kernel, emit_v=emit_v),
        grid=(t // tm,),
        in_specs=[row(d), full(nw), full(w_in), full(ln_w), full(ln_b), full(wm), full(bs), full(w_out)],
        out_specs=out_specs,
        out_shape=out_shape,
        scratch_shapes=[pltpu.VMEM((tm, dc), BF16)],
        compiler_params=_params("parallel"),
        name="odd_mixer",
    )(x, nw, w_in, ln_w, ln_b, wm, bs, w_out)


def _prep_even_weights(w_in, f_bias, a_log, dt_bias):
    d = w_in.shape[0]
    small = jnp.concatenate(
        [w_in[:, 1536:1544], w_in[:, 3080:3088], jnp.zeros((d, LANE - 16), w_in.dtype)], axis=1)
    w = jnp.concatenate([w_in[:, 0:512], w_in[:, 1544:3080], w_in[:, 3088:3600], small], axis=1)
    bias = jnp.zeros((1, LANE), F32).at[0, 0:8].set(f_bias).at[0, 8:12].set(dt_bias)
    alog = jnp.zeros((1, LANE), F32).at[0, 8:12].set(a_log)
    return w.astype(BF16), w_in[:, 512:1536].T.astype(BF16), bias, alog


def kernel(x_prompt, x_sample, cache_k, cache_v, cache_logf, state_delta, state_conv, page_table, norm_mix, norm_ffn, norm_final, w_in_even, w_out_even, fox_forget_bias, dn_conv_w, dn_a_log, dn_dt_bias, dn_norm_w, w_in_odd, gm_ln_w, gm_ln_b, gm_spatial_w, gm_spatial_b, w_out_odd, moe_router_group, moe_router_expert, moe_w_gate, moe_w_up, moe_w_down):
    n_p, seq, d = x_prompt.shape
    n_s, dec = x_sample.shape[:2]
    page = cache_k.shape[2]
    mix_a = H_A * D_A
    xp = x_prompt.reshape(n_p * seq, d)
    xs = x_sample.reshape(n_s * dec, d)
    row = lambda a: a[None, :]

    even_w = (row(norm_mix[0]),) + _prep_even_weights(w_in_even[0], fox_forget_bias[0], dn_a_log[0], dn_dt_bias[0])
    from_kt = lambda a, n, length: jnp.transpose(a.reshape(n, H_A, D_A, length), (0, 3, 1, 2))[None]
    w_out_e = w_out_even[0].astype(BF16)
    conv_w = jnp.zeros((8, dn_conv_w.shape[2]), F32).at[:CONV_W].set(dn_conv_w[0])
    onw = row(dn_norm_w[0])

    def delta(conv3, prefix, small3, gz3, s0, valid, nb):
        wv, wk, qd, kd, qk, gl = _delta_prep(conv3, prefix, conv_w, small3, valid)
        return _delta_scan(wv, wk, qd, kd, qk, gl, s0, gz3, onw, nb)

    q, kt, vt, c, gz, s = _even_in(xp, *even_w, n_p, seq)
    logf_p = s[:, :H_A].reshape(n_p, seq, H_A)
    csum = _cumsum_lanes(jnp.swapaxes(logf_p, 1, 2).reshape(n_p * H_A, seq)).reshape(n_p, H_A, seq)
    as3 = lambda a: a.reshape(n_p, seq, a.shape[-1])
    oa_p = _fox_prompt(as3(q), kt, vt, csum)
    conv_p = as3(c)
    ob_p, sd_p = delta(conv_p, jnp.zeros((n_p, 8, conv_p.shape[-1]), F32), as3(s), as3(gz),
                       jnp.zeros((n_p, H_B, DK_B, DK_B), F32), seq, 4)
    xp = _even_out(xp, oa_p.reshape(n_p * seq, mix_a), ob_p.reshape(n_p * seq, -1), w_out_e)
    k_p = from_kt(kt, n_p, seq)
    v_p = from_kt(vt, n_p, seq)
    conv_state_p = conv_p[:, seq - (CONV_W - 1):][None]

    q, kt, vt, c, gz, s = _even_in(xs, *even_w, 1, n_s * dec)
    as3 = lambda a: a.reshape(n_s, dec, a.shape[-1])
    padrows = lambda a, n: jnp.pad(a, ((0, 0), (0, n - a.shape[1]), (0, 0)))
    logf_s = s[:, :H_A].reshape(n_s, dec, H_A)
    pad_keys = lambda a: jnp.pad(a, ((0, 0), (0, 0), (0, page - dec)))
    per_seq = lambda a: pad_keys(jnp.swapaxes(a[0].reshape(mix_a, n_s, dec), 0, 1))
    oa_s = _fox_sample(page_table, as3(q), per_seq(kt), per_seq(vt), pad_keys(jnp.swapaxes(logf_s, 1, 2)),
                       jnp.transpose(cache_k[0], (0, 2, 3, 1)), jnp.transpose(cache_v[0], (0, 2, 3, 1)),
                       jnp.swapaxes(cache_logf[0], 1, 2))
    conv_s = as3(c)
    lp = DELTA_CHUNK
    prefix_s = jnp.pad(state_conv[0], ((0, 0), (8 - (CONV_W - 1), 0), (0, 0)))
    ob_s, sd_s = delta(padrows(conv_s, lp), prefix_s, padrows(as3(s), lp), padrows(as3(gz), lp),
                       state_delta[0], dec, 8)
    xs = _even_out(xs, oa_s.reshape(n_s * dec, mix_a).astype(BF16), ob_s[:, :dec].reshape(n_s * dec, -1), w_out_e)
    k_s = kt[0].T.reshape(1, n_s, dec, H_A, D_A)
    v_s = vt[0].T.reshape(1, n_s, dec, H_A, D_A)
    conv_state_s = jnp.concatenate([state_conv[0], conv_s], axis=1)[:, dec:][None]

    def moe(x, layer, final):
        wr = jnp.concatenate([moe_router_expert[layer], moe_router_group[layer],
                              jnp.zeros((d, LANE - N_EXPERTS - N_GROUPS), F32)], axis=1)
        wr_hi = wr.astype(BF16)
        wr = jnp.stack([wr_hi, (wr - wr_hi.astype(F32)).astype(BF16)])
        routed = _router(x, row(norm_ffn[layer]), wr)
        if x.shape[0] >= 4096:
            return _moe_sorted(x, row(norm_ffn[layer]), routed, wg, wu, wd, row(norm_final), layer, final)
        return _moe_dense(x, row(norm_ffn[layer]), routed[0], wg, wu, wd, row(norm_final), layer, final)

    wg, wu, wd = moe_w_gate.astype(BF16), moe_w_up.astype(BF16), moe_w_down.astype(BF16)
    xp = moe(xp, 0, False)
    xs = moe(xs, 0, False)

    idx = jnp.arange(CHUNK_C)
    w_m = jnp.where(idx[:, None] >= idx[None, :], gm_spatial_w[0], 0.0)
    b_s = gm_spatial_b[0]
    lanes = lambda b: jnp.broadcast_to(b[:, :, None], b.shape + (LANE,))
    reps = CHUNK_C // dec
    assert n_s * dec == CHUNK_C, "sample group must fill exactly one 128-row mixing tile"
    w_m_s = (jnp.eye(reps, dtype=F32)[None, :, None, :, None] * w_m[:, None, :dec, None, :dec]).reshape(H_C, CHUNK_C, CHUNK_C)
    b_s_s = jnp.tile(b_s[:, :dec], (1, reps))
    odd_w = (row(norm_mix[1]), w_in_odd[0].astype(BF16), row(gm_ln_w[0]), row(gm_ln_b[0]))
    w_out_o = w_out_odd[0].astype(BF16)
    (xp,) = _odd_mixer(xp, *odd_w, w_m.astype(BF16), lanes(b_s), w_out_o, False)
    xs, v_rows = _odd_mixer(xs, *odd_w, w_m_s.astype(BF16), lanes(b_s_s), w_out_o, True)

    y_p = moe(xp, 1, True).reshape(n_p, seq, d)
    y_s = moe(xs, 1, True).reshape(n_s, dec, d)
    return (y_p, y_s, k_p, v_p, logf_p[None], sd_p[None], conv_state_p,
            k_s, v_s, logf_s[None], sd_s[None], conv_state_s, v_rows.reshape(1, n_s, dec, -1))
```

```python
import functools

import jax
import jax.numpy as jnp
from jax import lax
from jax.experimental import pallas as pl
from jax.experimental.pallas import tpu as pltpu

F32 = jnp.float32
BF16 = jnp.bfloat16
EPS = 1e-6
LANE = 128
VMEM_LIMIT = 56 * 1024 * 1024
HIGHEST = lax.Precision.HIGHEST

H_A, D_A = 8, 64
H_B, DK_B = 4, 128
CONV_W = 4
DELTA_CHUNK = 64
N_GROUPS, EXPERTS_PER_GROUP = 4, 8
N_EXPERTS = N_GROUPS * EXPERTS_PER_GROUP
CHUNK_C = 128
H_C = 8


def _params(*sem):
    return pltpu.CompilerParams(dimension_semantics=sem, vmem_limit_bytes=VMEM_LIMIT)


def _row_tile(t, pref):
    return pref if t % pref == 0 else t


def _rms(x, w):
    return x * lax.rsqrt(jnp.mean(x * x, axis=-1, keepdims=True) + EPS) * w


def _softplus_tail(z):
    return jnp.log1p(jnp.exp(-jnp.abs(z)))


def _even_in_kernel(x_ref, nw_ref, w_ref, wkv_ref, bias_ref, alog_ref,
                    q_ref, k_ref, v_ref, c_ref, gz_ref, s_ref):
    hb = _rms(x_ref[...], nw_ref[...]).astype(BF16)

    def proj(lo, hi):
        return jnp.dot(hb, w_ref[:, lo:hi], preferred_element_type=F32)

    q_ref[...] = proj(0, 512)
    c_ref[...] = proj(512, 2048)
    gz_ref[...] = proj(2048, 2560)
    k_ref[0] = lax.dot_general(wkv_ref[:512, :], hb, (((1,), (1,)), ((), ())), preferred_element_type=F32)
    v_ref[0] = lax.dot_general(wkv_ref[512:, :], hb, (((1,), (1,)), ((), ())), preferred_element_type=F32)
    z = proj(2560, 2688) + bias_ref[...]
    tail = _softplus_tail(z)
    logf = jnp.minimum(z, 0.0) - tail
    g = -jnp.exp(alog_ref[...]) * (jnp.maximum(z, 0.0) + tail)
    beta = 1.0 / (1.0 + jnp.exp(-z))
    lane = lax.broadcasted_iota(jnp.int32, z.shape, 1)
    s_ref[...] = jnp.where(lane < 8, logf, jnp.where(lane < 12, g, jnp.where(lane < 16, beta, 0.0)))


def _even_in(x, nw, w, wkv, bias, alog, n, length):
    t, d = x.shape
    tm = _row_tile(length, 512)
    per_seq = length // tm
    mix = wkv.shape[0] // 2
    row = lambda wd: pl.BlockSpec((tm, wd), lambda i: (i, 0))
    full = lambda a: pl.BlockSpec(a.shape, lambda i: (0, 0))
    kt = pl.BlockSpec((1, mix, tm), lambda i: (i // per_seq, 0, i % per_seq))
    rows = lambda wd: jax.ShapeDtypeStruct((t, wd), F32)
    kts = jax.ShapeDtypeStruct((n, mix, length), F32)
    return pl.pallas_call(
        _even_in_kernel,
        grid=(t // tm,),
        in_specs=[row(d), full(nw), full(w), full(wkv), full(bias), full(alog)],
        out_specs=[row(mix), kt, kt, row(3 * mix), row(mix), row(LANE)],
        out_shape=[rows(mix), kts, kts, rows(3 * mix), rows(mix), rows(LANE)],
        compiler_params=_params("parallel"),
        name="even_in_proj",
    )(x, nw, w, wkv, bias, alog)


def _cumsum_kernel(x_ref, o_ref):
    rows, length = x_ref.shape
    r = lax.broadcasted_iota(jnp.int32, (LANE, LANE), 0)
    c = lax.broadcasted_iota(jnp.int32, (LANE, LANE), 1)
    upper = (r <= c).astype(F32)
    carry = jnp.zeros((rows, 1), F32)
    for b in range(length // LANE):
        blk = jnp.dot(x_ref[:, b * LANE:(b + 1) * LANE], upper,
                      preferred_element_type=F32, precision=HIGHEST) + carry
        o_ref[:, b * LANE:(b + 1) * LANE] = blk
        carry = blk[:, LANE - 1:LANE]


def _cumsum_lanes(x):
    rows, length = x.shape
    tr = 8
    return pl.pallas_call(
        _cumsum_kernel,
        grid=(rows // tr,),
        in_specs=[pl.BlockSpec((tr, length), lambda i: (i, 0))],
        out_specs=pl.BlockSpec((tr, length), lambda i: (i, 0)),
        out_shape=jax.ShapeDtypeStruct((rows, length), F32),
        compiler_params=_params("parallel"),
        name="logf_cumsum",
    )(x)


def _fox_prompt_kernel(q_ref, k_ref, v_ref, c_ref, o_ref, *, tq):
    qi = pl.program_id(2)
    hp = pl.program_id(1)
    scale = D_A ** -0.5
    lane = lax.broadcasted_iota(jnp.int32, (tq, LANE), 1)
    first = lane < D_A
    q = q_ref[0] * scale
    qs = [jnp.where(first, q, 0.0).astype(BF16), jnp.where(first, 0.0, q).astype(BF16)]
    q0 = pl.multiple_of(qi * tq, tq)

    def crow(h, start):
        return c_ref[0, pl.ds(2 * hp + h, 1), pl.ds(start, tq)]

    cq = [crow(h, q0)[:, 0:1] for h in range(2)]

    def block(j0, carry, mask):
        kb = k_ref[0, :, pl.ds(j0, tq)].astype(BF16)
        vb = v_ref[0, :, pl.ds(j0, tq)].astype(BF16)
        hh = range(2)
        s = [_dot(qs[h], kb) + (cq[h] - crow(h, j0)) for h in hh]
        if mask is not None:
            s = [jnp.where(mask, a, -jnp.inf) for a in s]
        m_new = [jnp.maximum(carry[h][0], jnp.max(s[h], axis=-1, keepdims=True)) for h in hh]
        alpha = [jnp.exp(carry[h][0] - m_new[h]) for h in hh]
        p = [jnp.exp(s[h] - m_new[h]) for h in hh]
        l = [alpha[h] * carry[h][1] + jnp.sum(p[h], axis=-1, keepdims=True) for h in hh]
        pv = [_dot_nt(p[h].astype(BF16), vb) for h in hh]
        acc = [alpha[h] * carry[h][2] + pv[h] for h in hh]
        return tuple((m_new[h], l[h], acc[h]) for h in hh)

    def body(j, carry):
        return block(pl.multiple_of(j * tq, tq), carry, None)

    init = (jnp.full((tq, 1), -jnp.inf, F32), jnp.zeros((tq, 1), F32), jnp.zeros((tq, LANE), F32))
    carry = lax.fori_loop(0, qi, body, (init, init))
    r = lax.broadcasted_iota(jnp.int32, (tq, tq), 0)
    cidx = lax.broadcasted_iota(jnp.int32, (tq, tq), 1)
    (_, l0, a0), (_, l1, a1) = block(q0, carry, cidx <= r)
    o_ref[0] = jnp.where(first, a0 / l0, a1 / l1).astype(o_ref.dtype)


def _fox_prompt(q, k, v, c):
    n, length, width = q.shape
    tq = 512 if length % 512 == 0 else length
    grid = (n, width // LANE, length // tq)
    return pl.pallas_call(
        functools.partial(_fox_prompt_kernel, tq=tq),
        grid=grid,
        in_specs=[
            pl.BlockSpec((1, tq, LANE), lambda b, h, i: (b, i, h)),
            pl.BlockSpec((1, LANE, length), lambda b, h, i: (b, h, 0)),
            pl.BlockSpec((1, LANE, length), lambda b, h, i: (b, h, 0)),
            pl.BlockSpec((1, H_A, length), lambda b, h, i: (b, 0, 0)),
        ],
        out_specs=pl.BlockSpec((1, tq, LANE), lambda b, h, i: (b, i, h)),
        out_shape=jax.ShapeDtypeStruct((n, length, width), BF16),
        compiler_params=_params("parallel", "parallel", "arbitrary"),
        name="fox_prompt",
    )(q, k, v, c)


def _even_out_kernel(x_ref, a_ref, b_ref, w_ref, o_ref):
    half = a_ref.shape[1]
    acc = jnp.dot(a_ref[...], w_ref[:half, :], preferred_element_type=F32)
    acc += jnp.dot(b_ref[...], w_ref[half:, :], preferred_element_type=F32)
    o_ref[...] = x_ref[...] + acc


def _even_out(x, oa, ob, w):
    t, d = x.shape
    tm = _row_tile(t, 512)
    row = lambda a: pl.BlockSpec((tm, a.shape[1]), lambda i: (i, 0))
    return pl.pallas_call(
        _even_out_kernel,
        grid=(t // tm,),
        in_specs=[row(x), row(oa), row(ob), pl.BlockSpec(w.shape, lambda i: (0, 0))],
        out_specs=row(x),
        out_shape=jax.ShapeDtypeStruct((t, d), F32),
        compiler_params=_params("parallel"),
        name="even_out_proj",
    )(x, oa, ob, w)


def _split3(x):
    hi = x.astype(BF16)
    r1 = x - hi.astype(F32)
    mid = r1.astype(BF16)
    lo = (r1 - mid.astype(F32)).astype(BF16)
    return hi, mid, lo


def _dot(a, b):
    return jnp.dot(a, b, preferred_element_type=F32)


def _dot_nt(a, b):
    return lax.dot_general(a, b, (((1,), (1,)), ((), ())), preferred_element_type=F32)


def _dot_tn(a, b):
    return lax.dot_general(a, b, (((0,), (0,)), ((), ())), preferred_element_type=F32)


def _silu(x):
    half = 0.5 * x
    return half + half * jnp.tanh(half)


def _delta_prep_kernel(x_ref, xprev_ref, pre_ref, cw_ref, sm_ref,
                       wv_ref, wk_ref, qd_ref, kd_ref, qk_ref, gl_ref, *, ta, valid):
    i = pl.program_id(1)
    c = DELTA_CHUNK
    dk = DK_B
    rr = H_B * c
    x = x_ref[0]
    prev = jnp.where(i == 0, pre_ref[0], xprev_ref[0])
    xcat = jnp.concatenate([prev, x], axis=0)
    conv = x * cw_ref[CONV_W - 1:CONV_W, :]
    for s in range(1, CONV_W):
        conv += pltpu.roll(xcat, s, axis=0)[8:] * cw_ref[CONV_W - 1 - s:CONV_W - s, :]
    act = _silu(conv)
    sm = sm_ref[0]

    ri = lax.broadcasted_iota(jnp.int32, (rr, rr), 0)
    ci = lax.broadcasted_iota(jnp.int32, (rr, rr), 1)
    same_head = (ri // c) == (ci // c)
    same_bf = same_head.astype(BF16)
    tri = (same_head & (ri >= ci)).astype(BF16)
    lane = lax.broadcasted_iota(jnp.int32, (c, LANE), 1)
    wrow = lax.broadcasted_iota(jnp.int32, (c, rr), 0)
    wlane = lax.broadcasted_iota(jnp.int32, (c, rr), 1)
    whead = wlane // c
    wcol = wlane % c
    incl = wrow >= wcol
    strict = wrow > wcol
    r64 = lax.broadcasted_iota(jnp.int32, (c, c), 0)
    c64 = lax.broadcasted_iota(jnp.int32, (c, c), 1)
    tri64 = (r64 >= c64).astype(BF16)
    ones64 = jnp.ones((c, c), BF16)

    def block_diag(wide_bf):
        return jnp.concatenate([wide_bf] * H_B, axis=0) * same_bf

    def fold(bd):
        return sum(jnp.where(whead == h, bd[h * c:(h + 1) * c], 0.0) for h in range(H_B))

    qn, kn, va = [], [], []
    for h in range(H_B):
        qa = act[:, h * dk:(h + 1) * dk]
        ka = act[:, (H_B + h) * dk:(H_B + h + 1) * dk]
        qn.append(qa * lax.rsqrt(jnp.sum(qa * qa, axis=-1, keepdims=True) + EPS) * (dk ** -0.5))
        kn.append(ka * lax.rsqrt(jnp.sum(ka * ka, axis=-1, keepdims=True) + EPS))
        va.append(act[:, (2 * H_B + h) * dk:(2 * H_B + h + 1) * dk])

    chunks = range(ta // c)
    stack = lambda parts: jnp.concatenate(parts, axis=0)
    q, k, v, b_col, gcb, g_wide = [], [], [], [], [], []
    for s in chunks:
        r0 = s * c
        live = lax.broadcasted_iota(jnp.int32, (c, 1), 0) + (i * ta + r0) < valid
        smc = sm[r0:r0 + c]
        pick = lambda ln: jnp.where(live, jnp.sum(jnp.where(lane == ln, smc, 0.0), axis=-1, keepdims=True), 0.0)
        q.append(stack([a[r0:r0 + c] for a in qn]))
        k.append(stack([a[r0:r0 + c] for a in kn]))
        v.append(stack([a[r0:r0 + c] for a in va]))
        b_col.append(stack([pick(12 + h) for h in range(H_B)]))
        g_cols = [pick(8 + h) for h in range(H_B)]
        g_stack = stack([jnp.broadcast_to(g, (c, LANE)) for g in g_cols])
        gcb.append(sum(_dot(tri, p) for p in _split3(g_stack)))
        g_wide.append(sum(jnp.where(whead == h, g_cols[h], 0.0) for h in range(H_B)))

    gc_i = [sum(_dot(tri64, p) for p in _split3(g)) for g in g_wide]
    gc_j = [sum(_dot(ones64, p) for p in _split3(jnp.where(wrow <= wcol, g, 0.0))) for g in g_wide]
    decay = [jnp.exp(jnp.where(incl, a - b, -jnp.inf)) for a, b in zip(gc_i, gc_j)]
    eg = [jnp.exp(g) for g in gcb]
    kb = [a * b for a, b in zip(k, b_col)]
    kbf = [a.astype(BF16) for a in k]
    a_mat = [jnp.where(strict, fold(_dot_nt(a.astype(BF16), b)) * d, 0.0) for a, b, d in zip(kb, kbf, decay)]
    qk = [jnp.where(incl, fold(_dot_nt(a.astype(BF16), b)) * d, 0.0) for a, b, d in zip(q, kbf, decay)]
    y = [-a for a in a_mat]
    p = list(y)
    ybd = [block_diag(a.astype(BF16)) for a in y]
    for _ in range(5):
        y = [_dot(a.astype(BF16), bd) for a, bd in zip(y, ybd)]
        ybd = [block_diag(a.astype(BF16)) for a in y]
        p = [a + b + _dot(a.astype(BF16), bd) for a, b, bd in zip(p, y, ybd)]
    rhs = [jnp.concatenate([a * b, kbb * e], axis=1) for a, b, kbb, e in zip(v, b_col, kb, eg)]
    w = [r + _dot(block_diag(a.astype(BF16)), r.astype(BF16)) for a, r in zip(p, rhs)]

    for s in chunks:
        r0 = s * c
        for h in range(H_B):
            hr = slice(h * c, (h + 1) * c)
            hs = slice(h * dk, (h + 1) * dk)
            g_last = gcb[s][h * c + c - 1:(h + 1) * c, :]
            wv_ref[0, r0:r0 + c, hs] = w[s][hr, :dk]
            wk_ref[0, r0:r0 + c, hs] = w[s][hr, dk:].astype(BF16)
            qd_ref[0, r0:r0 + c, hs] = (q[s][hr] * eg[s][hr]).astype(BF16)
            kd_ref[0, r0:r0 + c, hs] = (k[s][hr] * jnp.exp(g_last - gcb[s][hr])).astype(BF16)
            qk_ref[0, h, r0:r0 + c, :] = qk[s][:, h * c:(h + 1) * c].astype(BF16)
            gl_ref[0, s, h:h + 1, :] = jnp.exp(g_last)


def _delta_prep(conv_in, prefix, conv_w, small, valid):
    n, lp, cd = conv_in.shape
    ta = next((r for r in (4 * LANE, 2 * LANE, LANE) if lp % r == 0), lp)
    nt = lp // ta
    tb = ta // 8
    f = lambda dt, w: jax.ShapeDtypeStruct((n, lp, w), dt)
    blk = lambda w: pl.BlockSpec((1, ta, w), lambda b, i: (b, i, 0))
    width = H_B * DK_B
    return pl.pallas_call(
        functools.partial(_delta_prep_kernel, ta=ta, valid=valid),
        grid=(n, nt),
        in_specs=[
            blk(cd),
            pl.BlockSpec((1, 8, cd), lambda b, i: (b, jnp.maximum(i * tb - 1, 0), 0)),
            pl.BlockSpec((1, 8, cd), lambda b, i: (b, 0, 0)),
            pl.BlockSpec((8, cd), lambda b, i: (0, 0)),
            blk(LANE),
        ],
        out_specs=[
            blk(width), blk(width), blk(width), blk(width),
            pl.BlockSpec((1, H_B, ta, DELTA_CHUNK), lambda b, i: (b, 0, i, 0)),
            pl.BlockSpec((1, ta // DELTA_CHUNK, H_B, LANE), lambda b, i: (b, i, 0, 0)),
        ],
        out_shape=[
            f(F32, width), f(BF16, width), f(BF16, width), f(BF16, width),
            jax.ShapeDtypeStruct((n, H_B, lp, DELTA_CHUNK), BF16),
            jax.ShapeDtypeStruct((n, lp // DELTA_CHUNK, H_B, LANE), F32),
        ],
        compiler_params=_params("parallel", "parallel"),
        name="delta_prep",
    )(conv_in, conv_in, prefix, conv_w, small)


def _delta_scan_kernel(wv_ref, wk_ref, qd_ref, kd_ref, qk_ref, gl_ref, s0_ref, gz_ref, onw_ref,
                       o_ref, s_ref, *, nb, n_chunks):
    c = DELTA_CHUNK
    dk = DK_B

    @pl.when(pl.program_id(1) == 0)
    def _():
        s_ref[...] = s0_ref[...]

    onw = onw_ref[...]

    chains = [(b, h) for b in range(nb) for h in range(H_B)]
    cols = lambda h: slice(h * dk, (h + 1) * dk)

    def body(ci, _):
        r0 = pl.multiple_of(ci * c, c)
        rows = pl.ds(r0, c)
        state = [s_ref[b, h] for b, h in chains]
        prod = [_dot(jnp.concatenate([wk_ref[b, rows, cols(h)], qd_ref[b, rows, cols(h)]], axis=0), s.astype(BF16))
                for (b, h), s in zip(chains, state)]
        vb = [(wv_ref[b, rows, cols(h)] - p[:c]).astype(BF16) for (b, h), p in zip(chains, prod)]
        o = [p[c:] + _dot(qk_ref[b, h, rows, :], v) for (b, h), p, v in zip(chains, prod, vb)]
        new = [s * gl_ref[b, pl.ds(ci, 1), h, :] + _dot_tn(kd_ref[b, rows, cols(h)], v)
               for (b, h), s, v in zip(chains, state, vb)]
        for (b, h), s, oo in zip(chains, new, o):
            s_ref[b, h] = s
            on = oo * lax.rsqrt(jnp.mean(oo * oo, axis=-1, keepdims=True) + EPS) * onw
            o_ref[b, rows, cols(h)] = (on * _silu(gz_ref[b, rows, cols(h)])).astype(o_ref.dtype)
        return 0

    lax.fori_loop(0, n_chunks, body, 0)


def _delta_scan(wv, wk, qd, kd, qk, gl, s0, gz, onw, nb):
    n, lp, width = wv.shape
    tl = _row_tile(lp, 512)
    n_chunks = tl // DELTA_CHUNK
    seq = pl.BlockSpec((nb, tl, width), lambda i, t: (i, t, 0))
    state = pl.BlockSpec((nb,) + s0.shape[1:], lambda i, t: (i, 0, 0, 0))
    return pl.pallas_call(
        functools.partial(_delta_scan_kernel, nb=nb, n_chunks=n_chunks),
        grid=(n // nb, lp // tl),
        in_specs=[seq, seq, seq, seq,
                  pl.BlockSpec((nb, H_B, tl, DELTA_CHUNK), lambda i, t: (i, 0, t, 0)),
                  pl.BlockSpec((nb, n_chunks, H_B, LANE), lambda i, t: (i, t, 0, 0)),
                  state, seq, pl.BlockSpec(onw.shape, lambda i, t: (0, 0))],
        out_specs=[seq, state],
        out_shape=[jax.ShapeDtypeStruct((n, lp, width), BF16), jax.ShapeDtypeStruct(s0.shape, F32)],
        compiler_params=_params("parallel", "arbitrary"),
        name="delta_scan",
    )(wv, wk, qd, kd, qk, gl, s0, gz, onw)


def _fox_sample_kernel(pt_ref, q_ref, kn_ref, vn_ref, lfn_ref, kc_hbm, vc_hbm, lfc_hbm, o_ref,
                       qbd_ref, m_ref, l_ref, acc_ref, carry_ref, kbuf, vbuf, lfbuf, sems,
                       *, pages_per_step, n_q):
    pps = pages_per_step
    b = pl.program_id(0)
    j = pl.program_id(1)
    n_b = pl.num_programs(0)
    n_steps = pl.num_programs(1)
    n_pages = pt_ref.shape[1]
    rows = n_q * H_A
    page = lfn_ref.shape[2]
    width = H_A * D_A
    ri = lax.broadcasted_iota(jnp.int32, (page, page), 0)
    ci = lax.broadcasted_iota(jnp.int32, (page, page), 1)

    per_seq = n_steps - 1
    n_slots = kbuf.shape[0]
    g = b * per_seq + j - 1

    def copies(gg, p):
        src = pt_ref[gg // per_seq, n_pages - 1 - ((gg % per_seq) * pps + p)]
        s = gg % n_slots
        return (pltpu.make_async_copy(kc_hbm.at[src], kbuf.at[s, p], sems.at[0, s]),
                pltpu.make_async_copy(vc_hbm.at[src], vbuf.at[s, p], sems.at[1, s]),
                pltpu.make_async_copy(lfc_hbm.at[src], lfbuf.at[s, p], sems.at[2, s]))

    def fetch(gg):
        @pl.when(gg < n_b * per_seq)
        def _():
            for p in range(pps):
                for cp in copies(gg, p):
                    cp.start()

    @pl.when((b == 0) & (j == 0))
    def _():
        for ahead in range(n_slots - 1):
            fetch(jnp.int32(ahead))

    @pl.when(j > 0)
    def _():
        for p in range(pps):
            for cp in copies(g, p):
                cp.wait()
        fetch(g + n_slots - 1)

    def attend(blocks):
        qbd = qbd_ref[...]
        ss = []
        for kt, _, bias8, mask in blocks:
            s = _dot(qbd, kt.astype(BF16)) + jnp.concatenate([bias8] * n_q, axis=0)
            ss.append(s if mask is None else jnp.where(mask, s, -jnp.inf))
        m_old = m_ref[...]
        m_new = m_old
        for s in ss:
            m_new = jnp.maximum(m_new, jnp.max(s, axis=-1, keepdims=True))
        alpha = jnp.exp(m_old - m_new)
        ps = [jnp.exp(s - m_new) for s in ss]
        sums = [jnp.sum(p, axis=-1, keepdims=True) for p in ps]
        pvs = [_dot_nt(p.astype(BF16), vt.astype(BF16)) for p, (_, vt, _, _) in zip(ps, blocks)]
        l_ref[...] = alpha * l_ref[...] + sum(sums)
        acc_ref[...] = alpha * acc_ref[...] + sum(pvs)
        m_ref[...] = m_new

    @pl.when(j == 0)
    def _():
        q = q_ref[0] * (D_A ** -0.5)
        qrep = jnp.concatenate([jnp.broadcast_to(q[t:t + 1], (H_A, width)) for t in range(n_q)], axis=0)
        r = lax.broadcasted_iota(jnp.int32, qrep.shape, 0)
        ln = lax.broadcasted_iota(jnp.int32, qrep.shape, 1)
        qbd_ref[...] = jnp.where(ln // D_A == r % H_A, qrep, 0.0).astype(BF16)
        m_ref[...] = jnp.full(m_ref.shape, -jnp.inf, F32)
        l_ref[...] = jnp.zeros(l_ref.shape, F32)
        acc_ref[...] = jnp.zeros(acc_ref.shape, F32)
        carry_ref[...] = jnp.zeros(carry_ref.shape, F32)
        upper = (ri <= ci).astype(BF16)
        csum = sum(_dot(p, upper) for p in _split3(lfn_ref[0]))
        r2 = lax.broadcasted_iota(jnp.int32, (rows, page), 0)
        c2 = lax.broadcasted_iota(jnp.int32, (rows, page), 1)
        attend([(kn_ref[0], vn_ref[0], -csum, c2 <= r2 // H_A)])

    @pl.when(j > 0)
    def _():
        after = (ri > ci).astype(BF16)
        carry = carry_ref[...]
        slot = g % n_slots
        lfs = [lfbuf[slot, p] for p in range(pps)]
        pieces = [_split3(lf) for lf in lfs]
        suffix = [sum(_dot(p, after) for p in ps) for ps in pieces]
        totals = [jnp.sum(lf, axis=-1, keepdims=True) for lf in lfs]
        blocks = []
        for p, (sfx, tot) in enumerate(zip(suffix, totals)):
            blocks.append((kbuf[slot, p].reshape(width, page), vbuf[slot, p].reshape(width, page), carry + sfx, None))
            carry = carry + tot
        attend(blocks)
        carry_ref[...] = carry

    @pl.when(j == pl.num_programs(1) - 1)
    def _():
        out = acc_ref[...] / l_ref[...]
        r = lax.broadcasted_iota(jnp.int32, out.shape, 0)
        ln = lax.broadcasted_iota(jnp.int32, out.shape, 1)
        out = jnp.where(ln // D_A == r % H_A, out, 0.0)
        o_ref[0] = jnp.sum(out.reshape(n_q, H_A, width), axis=1)


def _fox_sample(page_table, q, kn, vn, lfn, cache_k, cache_v, cache_lf):
    b, n_q, width = q.shape
    n_pages = page_table.shape[1]
    page = cache_k.shape[3]
    rows = n_q * H_A
    pps = next(p for p in (16, 8, 1) if n_pages % p == 0)
    n_slots = 3
    cur = lambda i, j, pt: (i, 0, 0)
    hbm = pl.BlockSpec(memory_space=pl.ANY)
    grid_spec = pltpu.PrefetchScalarGridSpec(
        num_scalar_prefetch=1,
        grid=(b, n_pages // pps + 1),
        in_specs=[
            pl.BlockSpec((1, n_q, width), cur),
            pl.BlockSpec((1, width, page), cur),
            pl.BlockSpec((1, width, page), cur),
            pl.BlockSpec((1, H_A, page), cur),
            hbm, hbm, hbm,
        ],
        out_specs=pl.BlockSpec((1, n_q, width), cur),
        scratch_shapes=[
            pltpu.VMEM((rows, width), BF16),
            pltpu.VMEM((rows, 1), F32),
            pltpu.VMEM((rows, 1), F32),
            pltpu.VMEM((rows, width), F32),
            pltpu.VMEM((H_A, 1), F32),
            pltpu.VMEM((n_slots, pps, H_A, D_A, page), F32),
            pltpu.VMEM((n_slots, pps, H_A, D_A, page), F32),
            pltpu.VMEM((n_slots, pps, H_A, page), F32),
            pltpu.SemaphoreType.DMA((3, n_slots)),
        ],
    )
    return pl.pallas_call(
        functools.partial(_fox_sample_kernel, pages_per_step=pps, n_q=n_q),
        grid_spec=grid_spec,
        out_shape=jax.ShapeDtypeStruct((b, n_q, width), F32),
        compiler_params=_params("arbitrary", "arbitrary"),
        name="fox_sample",
    )(page_table, q, kn, vn, lfn, cache_k, cache_v, cache_lf)


def _router_kernel(x_ref, nw_ref, wr_ref, g_ref, grp_ref, rank_ref, cnt_ref, run_ref):
    step = pl.program_id(0)
    tm = x_ref.shape[0]

    @pl.when(step == 0)
    def _():
        run_ref[...] = jnp.zeros(run_ref.shape, F32)

    h = _rms(x_ref[...], nw_ref[...])
    h_hi = h.astype(BF16)
    h_lo = (h - h_hi.astype(F32)).astype(BF16)
    logits = _dot(h_hi, wr_ref[0]) + (_dot(h_lo, wr_ref[0]) + _dot(h_hi, wr_ref[1]))
    lane = lax.broadcasted_iota(jnp.int32, logits.shape, 1)
    lanef = lane.astype(F32)
    neg = -jnp.inf
    is_group = (lane >= N_EXPERTS) & (lane < N_EXPERTS + N_GROUPS)
    gl = jnp.where(is_group, logits, neg)
    gmax = jnp.max(gl, axis=-1, keepdims=True)
    g_sel = jnp.min(jnp.where(gl == gmax, lanef - N_EXPERTS, 1e9), axis=-1, keepdims=True)
    p_sel = 1.0 / jnp.sum(jnp.where(is_group, jnp.exp(logits - gmax), 0.0), axis=-1, keepdims=True)
    in_sel = (lane < N_EXPERTS) & ((lane // EXPERTS_PER_GROUP).astype(F32) == g_sel)
    el = jnp.where(in_sel, logits, neg)
    m1 = jnp.max(el, axis=-1, keepdims=True)
    i1 = jnp.min(jnp.where(el == m1, lanef, 1e9), axis=-1, keepdims=True)
    el2 = jnp.where(lanef == i1, neg, el)
    m2 = jnp.max(el2, axis=-1, keepdims=True)
    i2 = jnp.min(jnp.where(el2 == m2, lanef, 1e9), axis=-1, keepdims=True)
    e2 = jnp.exp(m2 - m1)
    w1 = p_sel / (1.0 + e2)
    g_ref[...] = jnp.where(lanef == i1, w1, jnp.where(lanef == i2, w1 * e2, jnp.where(lane == N_EXPERTS, g_sel, 0.0)))

    onehot_t = (lanef == g_sel).astype(F32).T[:8]
    ri = lax.broadcasted_iota(jnp.int32, (tm, tm), 0)
    ci = lax.broadcasted_iota(jnp.int32, (tm, tm), 1)
    earlier = _dot(onehot_t.astype(BF16), (ri < ci).astype(BF16))
    run = run_ref[...]
    gidx = lax.broadcasted_iota(jnp.int32, (8, tm), 0).astype(F32)
    rank_ref[0] = jnp.sum(onehot_t * (earlier + run), axis=0, keepdims=True).astype(jnp.int32)
    grp_ref[0] = jnp.sum(onehot_t * gidx, axis=0, keepdims=True).astype(jnp.int32)
    run = run + jnp.sum(onehot_t, axis=-1, keepdims=True)
    run_ref[...] = run
    cnt_ref[...] = jnp.broadcast_to(run, cnt_ref.shape)


def _router(x, nw, wr):
    t, d = x.shape
    tm = _row_tile(t, 512)
    nt = t // tm
    row = pl.BlockSpec((1, 1, tm), lambda i: (i, 0, 0))
    return pl.pallas_call(
        _router_kernel,
        grid=(nt,),
        in_specs=[pl.BlockSpec((tm, d), lambda i: (i, 0)), pl.BlockSpec(nw.shape, lambda i: (0, 0)),
                  pl.BlockSpec(wr.shape, lambda i: (0, 0, 0))],
        out_specs=[pl.BlockSpec((tm, LANE), lambda i: (i, 0)), row, row, pl.BlockSpec((8, LANE), lambda i: (0, 0))],
        out_shape=[jax.ShapeDtypeStruct((t, LANE), F32), jax.ShapeDtypeStruct((nt, 1, tm), jnp.int32),
                   jax.ShapeDtypeStruct((nt, 1, tm), jnp.int32), jax.ShapeDtypeStruct((8, LANE), F32)],
        scratch_shapes=[pltpu.VMEM((8, 1), F32)],
        compiler_params=_params("arbitrary"),
        name="moe_router",
    )(x, nw, wr)


def _moe_kernel(x_ref, nw_ref, g_ref, wg_ref, wu_ref, wd_ref, nf_ref, o_ref, hb_ref, acc_ref, *, final_norm):
    e = pl.program_id(1)

    @pl.when(e == 0)
    def _():
        hb_ref[...] = _rms(x_ref[...], nw_ref[...]).astype(BF16)
        acc_ref[...] = jnp.zeros(acc_ref.shape, F32)

    gates = g_ref[...]
    lane = lax.broadcasted_iota(jnp.int32, gates.shape, 1)
    gcol = jnp.sum(jnp.where(lane == e, gates, 0.0), axis=-1, keepdims=True)
    hb = hb_ref[...]
    act = _silu(_dot(hb, wg_ref[0, 0])) * _dot(hb, wu_ref[0, 0]) * gcol
    acc_ref[...] += _dot(act.astype(BF16), wd_ref[0, 0])

    @pl.when(e == pl.num_programs(1) - 1)
    def _():
        y = x_ref[...] + acc_ref[...]
        o_ref[...] = _rms(y, nf_ref[...]) if final_norm else y


def _moe_dense(x, nw, gates, wg, wu, wd, nf, layer, final_norm):
    t, d = x.shape
    tm = _row_tile(t, 1024)
    _, n_e, _, f = wg.shape
    return pl.pallas_call(
        functools.partial(_moe_kernel, final_norm=final_norm),
        grid=(t // tm, n_e),
        in_specs=[
            pl.BlockSpec((tm, d), lambda i, e: (i, 0)),
            pl.BlockSpec(nw.shape, lambda i, e: (0, 0)),
            pl.BlockSpec((tm, LANE), lambda i, e: (i, 0)),
            pl.BlockSpec((1, 1, d, f), lambda i, e: (layer, e, 0, 0)),
            pl.BlockSpec((1, 1, d, f), lambda i, e: (layer, e, 0, 0)),
            pl.BlockSpec((1, 1, f, d), lambda i, e: (layer, e, 0, 0)),
            pl.BlockSpec(nf.shape, lambda i, e: (0, 0)),
        ],
        out_specs=pl.BlockSpec((tm, d), lambda i, e: (i, 0)),
        out_shape=jax.ShapeDtypeStruct((t, d), F32),
        scratch_shapes=[pltpu.VMEM((tm, d), BF16), pltpu.VMEM((tm, d), F32)],
        compiler_params=_params("parallel", "arbitrary"),
        name="moe_experts",
    )(x, nw, gates, wg, wu, wd, nf)


def _row_copy(src, src_row, dst, dst_row, sem):
    return pltpu.make_async_copy(src.at[pl.ds(src_row, 1)], dst.at[pl.ds(dst_row, 1)], sem)


def _dispatch_kernel(zt_ref, dest_ref, x_ref, g_ref, o_ref, stage_ref, sem):
    tm, d = x_ref.shape

    @pl.when(pl.program_id(0) == 0)
    def _():
        stage_ref[...] = jnp.zeros(stage_ref.shape, F32)
        for k in range(zt_ref.shape[0]):
            cp = pltpu.make_async_copy(stage_ref, o_ref.at[pl.ds(pl.multiple_of(zt_ref[k] * tm, tm), tm)], sem)
            cp.start()
            cp.wait()

    stage_ref[:, :d] = x_ref[...]
    stage_ref[:, d:] = g_ref[...]

    def start(r2, _):
        for pr in range(2):
            r = 2 * r2 + pr
            _row_copy(stage_ref, r, o_ref, dest_ref[0, 0, r], sem).start(priority=pr)
        return 0

    def wait(r, _):
        _row_copy(stage_ref, 0, o_ref, 0, sem).wait()
        return 0

    lax.fori_loop(0, tm // 2, start, 0, unroll=4)
    lax.fori_loop(0, tm, wait, 0, unroll=8)


def _dispatch(zero_tiles, dest, x, gates, n_tiles, tm):
    t, d = x.shape
    grid_spec = pltpu.PrefetchScalarGridSpec(
        num_scalar_prefetch=1,
        grid=(t // tm,),
        in_specs=[pl.BlockSpec((1, 1, tm), lambda i, zt: (i, 0, 0), memory_space=pltpu.SMEM),
                  pl.BlockSpec((tm, d), lambda i, zt: (i, 0)),
                  pl.BlockSpec((tm, LANE), lambda i, zt: (i, 0))],
        out_specs=pl.BlockSpec(memory_space=pl.ANY),
        scratch_shapes=[pltpu.VMEM((tm, d + LANE), F32), pltpu.SemaphoreType.DMA(())],
    )
    return pl.pallas_call(
        _dispatch_kernel,
        grid_spec=grid_spec,
        out_shape=jax.ShapeDtypeStruct((n_tiles * tm, d + LANE), F32),
        compiler_params=_params("arbitrary"),
        name="moe_dispatch",
    )(zero_tiles, dest, x, gates)


def _moe_group_kernel(tg_ref, na_ref, xs_ref, nw_ref, wg_ref, wu_ref, wd_ref, nf_ref, o_ref, *, final_norm):
    i = pl.program_id(0)
    d = o_ref.shape[1]

    @pl.when(i < na_ref[0])
    def _():
        x = xs_ref[:, :d]
        gates = xs_ref[:, d:]
        hb = _rms(x, nw_ref[...]).astype(BF16)
        lane = lax.broadcasted_iota(jnp.int32, gates.shape, 1)
        first = tg_ref[i] * EXPERTS_PER_GROUP
        acc = jnp.zeros(x.shape, F32)
        for e in range(EXPERTS_PER_GROUP):
            gcol = jnp.sum(jnp.where(lane == first + e, gates, 0.0), axis=-1, keepdims=True)
            act = _silu(_dot(hb, wg_ref[0, 0, e])) * _dot(hb, wu_ref[0, 0, e]) * gcol
            acc += _dot(act.astype(BF16), wd_ref[0, 0, e])
        y = x + acc
        o_ref[...] = _rms(y, nf_ref[...]) if final_norm else y

    @pl.when(i >= na_ref[0])
    def _():
        o_ref[...] = jnp.zeros(o_ref.shape, F32)


def _moe_group(tile_group, n_active, xs, nw, wg, wu, wd, nf, tm, layer, final_norm):
    tp, dx = xs.shape
    d = dx - LANE
    wspec = lambda a: pl.BlockSpec((1, 1) + a.shape[2:], lambda i, tg, na: (layer, tg[i], 0, 0, 0))
    vec = lambda a: pl.BlockSpec(a.shape, lambda i, tg, na: (0, 0))
    grid_spec = pltpu.PrefetchScalarGridSpec(
        num_scalar_prefetch=2,
        grid=(tp // tm,),
        in_specs=[pl.BlockSpec((tm, dx), lambda i, tg, na: (i, 0)), vec(nw), wspec(wg), wspec(wu), wspec(wd), vec(nf)],
        out_specs=pl.BlockSpec((tm, d), lambda i, tg, na: (i, 0)),
    )
    return pl.pallas_call(
        functools.partial(_moe_group_kernel, final_norm=final_norm),
        grid_spec=grid_spec,
        out_shape=jax.ShapeDtypeStruct((tp, d), F32),
        compiler_params=_params("arbitrary"),
        name="moe_group_experts",
    )(tile_group, n_active, xs, nw, wg, wu, wd, nf)


def _combine_kernel(dest_ref, ys_ref, o_ref, sem):
    tm = o_ref.shape[0]

    def start(r2, _):
        for pr in range(2):
            r = 2 * r2 + pr
            _row_copy(ys_ref, dest_ref[0, 0, r], o_ref, r, sem).start(priority=pr)
        return 0

    def wait(r, _):
        _row_copy(ys_ref, 0, o_ref, 0, sem).wait()
        return 0

    lax.fori_loop(0, tm // 2, start, 0, unroll=4)
    lax.fori_loop(0, tm, wait, 0, unroll=8)


def _combine(dest, ys, t, tm):
    d = ys.shape[1]
    return pl.pallas_call(
        _combine_kernel,
        grid=(t // tm,),
        in_specs=[pl.BlockSpec((1, 1, tm), lambda i: (i, 0, 0), memory_space=pltpu.SMEM),
                  pl.BlockSpec(memory_space=pl.ANY)],
        out_specs=pl.BlockSpec((tm, d), lambda i: (i, 0)),
        out_shape=jax.ShapeDtypeStruct((t, d), F32),
        scratch_shapes=[pltpu.SemaphoreType.DMA(())],
        compiler_params=_params("arbitrary"),
        name="moe_combine",
    )(dest, ys)


def _moe_sorted(x, nw, routed, wg, wu, wd, nf, layer, final_norm):
    t, d = x.shape
    gates, grp, rank, cnt = routed
    tm = grp.shape[2]
    counts = [cnt[g, 0].astype(jnp.int32) for g in range(N_GROUPS)]
    tiles = [(c + tm - 1) // tm for c in counts]
    tile_end = [sum(tiles[:g + 1]) for g in range(N_GROUPS)]
    dest = rank + sum(jnp.where(grp == g, (tile_end[g] - tiles[g]) * tm, 0) for g in range(N_GROUPS))
    n_tiles = t // tm + N_GROUPS
    tile = jnp.arange(n_tiles, dtype=jnp.int32)
    tile_group = jnp.minimum(sum((tile >= e).astype(jnp.int32) for e in tile_end), N_GROUPS - 1)
    last = n_tiles - 1
    zero_tiles = jnp.stack([jnp.where(tiles[g] > 0, tile_end[g] - 1, last) for g in range(N_GROUPS)]
                           + [jnp.minimum(tile_end[-1] + k, last) for k in range(N_GROUPS)]).astype(jnp.int32)
    xs = _dispatch(zero_tiles, dest, x, gates, n_tiles, tm)
    split = lambda w: w.reshape((w.shape[0], N_GROUPS, EXPERTS_PER_GROUP) + w.shape[2:])
    ys = _moe_group(tile_group, tile_end[-1].reshape(1), xs, nw,
                    split(wg), split(wu), split(wd), nf, tm, layer, final_norm)
    return _combine(dest, ys, t, tm)


def _odd_kernel(x_ref, nw_ref, win_ref, lnw_ref, lnb_ref, wm_ref, bs_ref, wout_ref, *refs, emit_v):
    if emit_v:
        o_ref, v_ref, gated_ref = refs
    else:
        o_ref, gated_ref = refs
    tm = x_ref.shape[0]
    dc = lnw_ref.shape[1]
    gd = dc // H_C
    x = x_ref[...]
    hb = _rms(x, nw_ref[...]).astype(BF16)
    z = _dot(hb, win_ref[...])
    z = 0.5 * z * (1.0 + lax.erf(z * (2.0 ** -0.5)))
    v = z[:, dc:]
    mu = jnp.mean(v, axis=-1, keepdims=True)
    vc = v - mu
    vn = vc * lax.rsqrt(jnp.mean(vc * vc, axis=-1, keepdims=True) + EPS) * lnw_ref[...] + lnb_ref[...]
    if emit_v:
        v_ref[...] = vn
    vb = vn.astype(BF16)
    for ci in range(tm // CHUNK_C):
        rs = slice(ci * CHUNK_C, (ci + 1) * CHUNK_C)
        for g in range(H_C):
            cs = slice(g * gd, (g + 1) * gd)
            mixed = _dot(wm_ref[g], vb[rs, cs]) + jnp.concatenate([bs_ref[g]] * (gd // LANE), axis=1)
            gated_ref[rs, cs] = (z[rs, cs] * mixed).astype(BF16)
    o_ref[...] = x + _dot(gated_ref[...], wout_ref[...])


def _odd_mixer(x, nw, w_in, ln_w, ln_b, wm, bs, w_out, emit_v):
    t, d = x.shape
    dc = ln_w.shape[1]
    tm = _row_tile(t, 512)
    row = lambda w: pl.BlockSpec((tm, w), lambda i: (i, 0))
    full = lambda a: pl.BlockSpec(a.shape, lambda i: (0,) * a.ndim)
    out_specs = [row(d)] + ([row(dc)] if emit_v else [])
    out_shape = [jax.ShapeDtypeStruct((t, d), F32)] + ([jax.ShapeDtypeStruct((t, dc), F32)] if emit_v else [])
    return pl.pallas_call(
        functools.partial(_odd_kernel, emit_v=emit_v),
        grid=(t // tm,),
        in_specs=[row(d), full(nw), full(w_in), full(ln_w), full(ln_b), full(wm), full(bs), full(w_out)],
        out_specs=out_specs,
        out_shape=out_shape,
        scratch_shapes=[pltpu.VMEM((tm, dc), BF16)],
        compiler_params=_params("parallel"),
        name="odd_mixer",
    )(x, nw, w_in, ln_w, ln_b, wm, bs, w_out)


def _prep_even_weights(w_in, f_bias, a_log, dt_bias):
    d = w_in.shape[0]
    small = jnp.concatenate(
        [w_in[:, 1536:1544], w_in[:, 3080:3088], jnp.zeros((d, LANE - 16), w_in.dtype)], axis=1)
    w = jnp.concatenate([w_in[:, 0:512], w_in[:, 1544:3080], w_in[:, 3088:3600], small], axis=1)
    bias = jnp.zeros((1, LANE), F32).at[0, 0:8].set(f_bias).at[0, 8:12].set(dt_bias)
    alog = jnp.zeros((1, LANE), F32).at[0, 8:12].set(a_log)
    return w.astype(BF16), w_in[:, 512:1536].T.astype(BF16), bias, alog


def kernel(x_prompt, x_sample, cache_k, cache_v, cache_logf, state_delta, state_conv, page_table, norm_mix, norm_ffn, norm_final, w_in_even, w_out_even, fox_forget_bias, dn_conv_w, dn_a_log, dn_dt_bias, dn_norm_w, w_in_odd, gm_ln_w, gm_ln_b, gm_spatial_w, gm_spatial_b, w_out_odd, moe_router_group, moe_router_expert, moe_w_gate, moe_w_up, moe_w_down):
    n_p, seq, d = x_prompt.shape
    n_s, dec = x_sample.shape[:2]
    page = cache_k.shape[2]
    mix_a = H_A * D_A
    xp = x_prompt.reshape(n_p * seq, d)
    xs = x_sample.reshape(n_s * dec, d)
    row = lambda a: a[None, :]

    even_w = (row(norm_mix[0]),) + _prep_even_weights(w_in_even[0], fox_forget_bias[0], dn_a_log[0], dn_dt_bias[0])
    from_kt = lambda a, n, length: jnp.transpose(a.reshape(n, H_A, D_A, length), (0, 3, 1, 2))[None]
    w_out_e = w_out_even[0].astype(BF16)
    conv_w = jnp.zeros((8, dn_conv_w.shape[2]), F32).at[:CONV_W].set(dn_conv_w[0])
    onw = row(dn_norm_w[0])

    def delta(conv3, prefix, small3, gz3, s0, valid, nb):
        wv, wk, qd, kd, qk, gl = _delta_prep(conv3, prefix, conv_w, small3, valid)
        return _delta_scan(wv, wk, qd, kd, qk, gl, s0, gz3, onw, nb)

    q, kt, vt, c, gz, s = _even_in(xp, *even_w, n_p, seq)
    logf_p = s[:, :H_A].reshape(n_p, seq, H_A)
    csum = _cumsum_lanes(jnp.swapaxes(logf_p, 1, 2).reshape(n_p * H_A, seq)).reshape(n_p, H_A, seq)
    as3 = lambda a: a.reshape(n_p, seq, a.shape[-1])
    oa_p = _fox_prompt(as3(q), kt, vt, csum)
    conv_p = as3(c)
    ob_p, sd_p = delta(conv_p, jnp.zeros((n_p, 8, conv_p.shape[-1]), F32), as3(s), as3(gz),
                       jnp.zeros((n_p, H_B, DK_B, DK_B), F32), seq, 4)
    xp = _even_out(xp, oa_p.reshape(n_p * seq, mix_a), ob_p.reshape(n_p * seq, -1), w_out_e)
    k_p = from_kt(kt, n_p, seq)
    v_p = from_kt(vt, n_p, seq)
    conv_state_p = conv_p[:, seq - (CONV_W - 1):][None]

    q, kt, vt, c, gz, s = _even_in(xs, *even_w, 1, n_s * dec)
    as3 = lambda a: a.reshape(n_s, dec, a.shape[-1])
    padrows = lambda a, n: jnp.pad(a, ((0, 0), (0, n - a.shape[1]), (0, 0)))
    logf_s = s[:, :H_A].reshape(n_s, dec, H_A)
    pad_keys = lambda a: jnp.pad(a, ((0, 0), (0, 0), (0, page - dec)))
    per_seq = lambda a: pad_keys(jnp.swapaxes(a[0].reshape(mix_a, n_s, dec), 0, 1))
    oa_s = _fox_sample(page_table, as3(q), per_seq(kt), per_seq(vt), pad_keys(jnp.swapaxes(logf_s, 1, 2)),
                       jnp.transpose(cache_k[0], (0, 2, 3, 1)), jnp.transpose(cache_v[0], (0, 2, 3, 1)),
                       jnp.swapaxes(cache_logf[0], 1, 2))
    conv_s = as3(c)
    lp = DELTA_CHUNK
    prefix_s = jnp.pad(state_conv[0], ((0, 0), (8 - (CONV_W - 1), 0), (0, 0)))
    ob_s, sd_s = delta(padrows(conv_s, lp), prefix_s, padrows(as3(s), lp), padrows(as3(gz), lp),
                       state_delta[0], dec, 8)
    xs = _even_out(xs, oa_s.reshape(n_s * dec, mix_a).astype(BF16), ob_s[:, :dec].reshape(n_s * dec, -1), w_out_e)
    k_s = kt[0].T.reshape(1, n_s, dec, H_A, D_A)
    v_s = vt[0].T.reshape(1, n_s, dec, H_A, D_A)
    conv_state_s = jnp.concatenate([state_conv[0], conv_s], axis=1)[:, dec:][None]

    def moe(x, layer, final):
        wr = jnp.concatenate([moe_router_expert[layer], moe_router_group[layer],
                              jnp.zeros((d, LANE - N_EXPERTS - N_GROUPS), F32)], axis=1)
        wr_hi = wr.astype(BF16)
        wr = jnp.stack([wr_hi, (wr - wr_hi.astype(F32)).astype(BF16)])
        routed = _router(x, row(norm_ffn[layer]), wr)
        if x.shape[0] >= 4096:
            return _moe_sorted(x, row(norm_ffn[layer]), routed, wg, wu, wd, row(norm_final), layer, final)
        return _moe_dense(x, row(norm_ffn[layer]), routed[0], wg, wu, wd, row(norm_final), layer, final)

    wg, wu, wd = moe_w_gate.astype(BF16), moe_w_up.astype(BF16), moe_w_down.astype(BF16)
    xp = moe(xp, 0, False)
    xs = moe(xs, 0, False)

    idx = jnp.arange(CHUNK_C)
    w_m = jnp.where(idx[:, None] >= idx[None, :], gm_spatial_w[0], 0.0)
    b_s = gm_spatial_b[0]
    lanes = lambda b: jnp.broadcast_to(b[:, :, None], b.shape + (LANE,))
    reps = CHUNK_C // dec
    assert n_s * dec == CHUNK_C, "sample group must fill exactly one 128-row mixing tile"
    w_m_s = (jnp.eye(reps, dtype=F32)[None, :, None, :, None] * w_m[:, None, :dec, None, :dec]).reshape(H_C, CHUNK_C, CHUNK_C)
    b_s_s = jnp.tile(b_s[:, :dec], (1, reps))
    odd_w = (row(norm_mix[1]), w_in_odd[0].astype(BF16), row(gm_ln_w[0]), row(gm_ln_b[0]))
    w_out_o = w_out_odd[0].astype(BF16)
    (xp,) = _odd_mixer(xp, *odd_w, w_m.astype(BF16), lanes(b_s), w_out_o, False)
    xs, v_rows = _odd_mixer(xs, *odd_w, w_m_s.astype(BF16), lanes(b_s_s), w_out_o, True)

    y_p = moe(xp, 1, True).reshape(n_p, seq, d)
    y_s = moe(xs, 1, True).reshape(n_s, dec, d)
    return (y_p, y_s, k_p, v_p, logf_p[None], sd_p[None], conv_state_p,
            k_s, v_s, logf_s[None], sd_s[None], conv_state_s, v_rows.reshape(1, n_s, dec, -1))
```

```python
import functools

import jax
import jax.numpy as jnp
from jax import lax
from jax.experimental import pallas as pl
from jax.experimental.pallas import tpu as pltpu

F32 = jnp.float32
BF16 = jnp.bfloat16
EPS = 1e-6
LANE = 128
VMEM_LIMIT = 56 * 1024 * 1024
HIGHEST = lax.Precision.HIGHEST

H_A, D_A = 8, 64
H_B, DK_B = 4, 128
CONV_W = 4
DELTA_CHUNK = 64
N_GROUPS, EXPERTS_PER_GROUP = 4, 8
N_EXPERTS = N_GROUPS * EXPERTS_PER_GROUP
CHUNK_C = 128
H_C = 8


def _params(*sem):
    return pltpu.CompilerParams(dimension_semantics=sem, vmem_limit_bytes=VMEM_LIMIT)


def _row_tile(t, pref):
    return pref if t % pref == 0 else t


def _rms(x, w):
    return x * lax.rsqrt(jnp.mean(x * x, axis=-1, keepdims=True) + EPS) * w


def _softplus_tail(z):
    return jnp.log1p(jnp.exp(-jnp.abs(z)))


def _even_in_kernel(x_ref, nw_ref, w_ref, wkv_ref, bias_ref, alog_ref,
                    q_ref, k_ref, v_ref, c_ref, gz_ref, s_ref):
    hb = _rms(x_ref[...], nw_ref[...]).astype(BF16)

    def proj(lo, hi):
        return jnp.dot(hb, w_ref[:, lo:hi], preferred_element_type=F32)

    q_ref[...] = proj(0, 512)
    c_ref[...] = proj(512, 2048)
    gz_ref[...] = proj(2048, 2560)
    k_ref[0] = lax.dot_general(wkv_ref[:512, :], hb, (((1,), (1,)), ((), ())), preferred_element_type=F32)
    v_ref[0] = lax.dot_general(wkv_ref[512:, :], hb, (((1,), (1,)), ((), ())), preferred_element_type=F32)
    z = proj(2560, 2688) + bias_ref[...]
    tail = _softplus_tail(z)
    logf = jnp.minimum(z, 0.0) - tail
    g = -jnp.exp(alog_ref[...]) * (jnp.maximum(z, 0.0) + tail)
    beta = 1.0 / (1.0 + jnp.exp(-z))
    lane = lax.broadcasted_iota(jnp.int32, z.shape, 1)
    s_ref[...] = jnp.where(lane < 8, logf, jnp.where(lane < 12, g, jnp.where(lane < 16, beta, 0.0)))


def _even_in(x, nw, w, wkv, bias, alog, n, length):
    t, d = x.shape
    tm = _row_tile(length, 512)
    per_seq = length // tm
    mix = wkv.shape[0] // 2
    row = lambda wd: pl.BlockSpec((tm, wd), lambda i: (i, 0))
    full = lambda a: pl.BlockSpec(a.shape, lambda i: (0, 0))
    kt = pl.BlockSpec((1, mix, tm), lambda i: (i // per_seq, 0, i % per_seq))
    rows = lambda wd: jax.ShapeDtypeStruct((t, wd), F32)
    kts = jax.ShapeDtypeStruct((n, mix, length), F32)
    return pl.pallas_call(
        _even_in_kernel,
        grid=(t // tm,),
        in_specs=[row(d), full(nw), full(w), full(wkv), full(bias), full(alog)],
        out_specs=[row(mix), kt, kt, row(3 * mix), row(mix), row(LANE)],
        out_shape=[rows(mix), kts, kts, rows(3 * mix), rows(mix), rows(LANE)],
        compiler_params=_params("parallel"),
        name="even_in_proj",
    )(x, nw, w, wkv, bias, alog)


def _cumsum_kernel(x_ref, o_ref):
    rows, length = x_ref.shape
    r = lax.broadcasted_iota(jnp.int32, (LANE, LANE), 0)
    c = lax.broadcasted_iota(jnp.int32, (LANE, LANE), 1)
    upper = (r <= c).astype(F32)
    carry = jnp.zeros((rows, 1), F32)
    for b in range(length // LANE):
        blk = jnp.dot(x_ref[:, b * LANE:(b + 1) * LANE], upper,
                      preferred_element_type=F32, precision=HIGHEST) + carry
        o_ref[:, b * LANE:(b + 1) * LANE] = blk
        carry = blk[:, LANE - 1:LANE]


def _cumsum_lanes(x):
    rows, length = x.shape
    tr = 8
    return pl.pallas_call(
        _cumsum_kernel,
        grid=(rows // tr,),
        in_specs=[pl.BlockSpec((tr, length), lambda i: (i, 0))],
        out_specs=pl.BlockSpec((tr, length), lambda i: (i, 0)),
        out_shape=jax.ShapeDtypeStruct((rows, length), F32),
        compiler_params=_params("parallel"),
        name="logf_cumsum",
    )(x)


def _fox_prompt_kernel(q_ref, k_ref, v_ref, c_ref, o_ref, *, tq):
    qi = pl.program_id(2)
    hp = pl.program_id(1)
    scale = D_A ** -0.5
    lane = lax.broadcasted_iota(jnp.int32, (tq, LANE), 1)
    first = lane < D_A
    q = q_ref[0] * scale
    qs = [jnp.where(first, q, 0.0).astype(BF16), jnp.where(first, 0.0, q).astype(BF16)]
    q0 = pl.multiple_of(qi * tq, tq)

    def crow(h, start):
        return c_ref[0, pl.ds(2 * hp + h, 1), pl.ds(start, tq)]

    cq = [crow(h, q0)[:, 0:1] for h in range(2)]

    def block(j0, carry, mask):
        kb = k_ref[0, :, pl.ds(j0, tq)].astype(BF16)
        vb = v_ref[0, :, pl.ds(j0, tq)].astype(BF16)
        hh = range(2)
        s = [_dot(qs[h], kb) + (cq[h] - crow(h, j0)) for h in hh]
        if mask is not None:
            s = [jnp.where(mask, a, -jnp.inf) for a in s]
        m_new = [jnp.maximum(carry[h][0], jnp.max(s[h], axis=-1, keepdims=True)) for h in hh]
        alpha = [jnp.exp(carry[h][0] - m_new[h]) for h in hh]
        p = [jnp.exp(s[h] - m_new[h]) for h in hh]
        l = [alpha[h] * carry[h][1] + jnp.sum(p[h], axis=-1, keepdims=True) for h in hh]
        pv = [_dot_nt(p[h].astype(BF16), vb) for h in hh]
        acc = [alpha[h] * carry[h][2] + pv[h] for h in hh]
        return tuple((m_new[h], l[h], acc[h]) for h in hh)

    def body(j, carry):
        return block(pl.multiple_of(j * tq, tq), carry, None)

    init = (jnp.full((tq, 1), -jnp.inf, F32), jnp.zeros((tq, 1), F32), jnp.zeros((tq, LANE), F32))
    carry = lax.fori_loop(0, qi, body, (init, init))
    r = lax.broadcasted_iota(jnp.int32, (tq, tq), 0)
    cidx = lax.broadcasted_iota(jnp.int32, (tq, tq), 1)
    (_, l0, a0), (_, l1, a1) = block(q0, carry, cidx <= r)
    o_ref[0] = jnp.where(first, a0 / l0, a1 / l1).astype(o_ref.dtype)


def _fox_prompt(q, k, v, c):
    n, length, width = q.shape
    tq = next((r for r in (1024, 512) if length % r == 0), length)
    grid = (n, width // LANE, length // tq)
    return pl.pallas_call(
        functools.partial(_fox_prompt_kernel, tq=tq),
        grid=grid,
        in_specs=[
            pl.BlockSpec((1, tq, LANE), lambda b, h, i: (b, i, h)),
            pl.BlockSpec((1, LANE, length), lambda b, h, i: (b, h, 0)),
            pl.BlockSpec((1, LANE, length), lambda b, h, i: (b, h, 0)),
            pl.BlockSpec((1, H_A, length), lambda b, h, i: (b, 0, 0)),
        ],
        out_specs=pl.BlockSpec((1, tq, LANE), lambda b, h, i: (b, i, h)),
        out_shape=jax.ShapeDtypeStruct((n, length, width), BF16),
        compiler_params=_params("parallel", "parallel", "arbitrary"),
        name="fox_prompt",
    )(q, k, v, c)


def _even_out_kernel(x_ref, a_ref, b_ref, w_ref, o_ref):
    half = a_ref.shape[1]
    acc = jnp.dot(a_ref[...], w_ref[:half, :], preferred_element_type=F32)
    acc += jnp.dot(b_ref[...], w_ref[half:, :], preferred_element_type=F32)
    o_ref[...] = x_ref[...] + acc


def _even_out(x, oa, ob, w):
    t, d = x.shape
    tm = _row_tile(t, 512)
    row = lambda a: pl.BlockSpec((tm, a.shape[1]), lambda i: (i, 0))
    return pl.pallas_call(
        _even_out_kernel,
        grid=(t // tm,),
        in_specs=[row(x), row(oa), row(ob), pl.BlockSpec(w.shape, lambda i: (0, 0))],
        out_specs=row(x),
        out_shape=jax.ShapeDtypeStruct((t, d), F32),
        compiler_params=_params("parallel"),
        name="even_out_proj",
    )(x, oa, ob, w)


def _split3(x):
    hi = x.astype(BF16)
    r1 = x - hi.astype(F32)
    mid = r1.astype(BF16)
    lo = (r1 - mid.astype(F32)).astype(BF16)
    return hi, mid, lo


def _dot(a, b):
    return jnp.dot(a, b, preferred_element_type=F32)


def _dot_nt(a, b):
    return lax.dot_general(a, b, (((1,), (1,)), ((), ())), preferred_element_type=F32)


def _dot_tn(a, b):
    return lax.dot_general(a, b, (((0,), (0,)), ((), ())), preferred_element_type=F32)


def _silu(x):
    half = 0.5 * x
    return half + half * jnp.tanh(half)


def _delta_prep_kernel(x_ref, xprev_ref, pre_ref, cw_ref, sm_ref,
                       wv_ref, wk_ref, qd_ref, kd_ref, qk_ref, gl_ref, *, ta, valid):
    i = pl.program_id(1)
    c = DELTA_CHUNK
    dk = DK_B
    rr = H_B * c
    x = x_ref[0]
    prev = jnp.where(i == 0, pre_ref[0], xprev_ref[0])
    xcat = jnp.concatenate([prev, x], axis=0)
    conv = x * cw_ref[CONV_W - 1:CONV_W, :]
    for s in range(1, CONV_W):
        conv += pltpu.roll(xcat, s, axis=0)[8:] * cw_ref[CONV_W - 1 - s:CONV_W - s, :]
    act = _silu(conv)
    sm = sm_ref[0]

    ri = lax.broadcasted_iota(jnp.int32, (rr, rr), 0)
    ci = lax.broadcasted_iota(jnp.int32, (rr, rr), 1)
    same_head = (ri // c) == (ci // c)
    same_bf = same_head.astype(BF16)
    tri = (same_head & (ri >= ci)).astype(BF16)
    lane = lax.broadcasted_iota(jnp.int32, (c, LANE), 1)
    wrow = lax.broadcasted_iota(jnp.int32, (c, rr), 0)
    wlane = lax.broadcasted_iota(jnp.int32, (c, rr), 1)
    whead = wlane // c
    wcol = wlane % c
    incl = wrow >= wcol
    strict = wrow > wcol
    r64 = lax.broadcasted_iota(jnp.int32, (c, c), 0)
    c64 = lax.broadcasted_iota(jnp.int32, (c, c), 1)
    tri64 = (r64 >= c64).astype(BF16)
    ones64 = jnp.ones((c, c), BF16)

    def block_diag(wide_bf):
        return jnp.concatenate([wide_bf] * H_B, axis=0) * same_bf

    def fold(bd):
        return sum(jnp.where(whead == h, bd[h * c:(h + 1) * c], 0.0) for h in range(H_B))

    qn, kn, va = [], [], []
    for h in range(H_B):
        qa = act[:, h * dk:(h + 1) * dk]
        ka = act[:, (H_B + h) * dk:(H_B + h + 1) * dk]
        qn.append(qa * lax.rsqrt(jnp.sum(qa * qa, axis=-1, keepdims=True) + EPS) * (dk ** -0.5))
        kn.append(ka * lax.rsqrt(jnp.sum(ka * ka, axis=-1, keepdims=True) + EPS))
        va.append(act[:, (2 * H_B + h) * dk:(2 * H_B + h + 1) * dk])

    chunks = range(ta // c)
    stack = lambda parts: jnp.concatenate(parts, axis=0)
    q, k, v, b_col, gcb, g_wide = [], [], [], [], [], []
    for s in chunks:
        r0 = s * c
        live = lax.broadcasted_iota(jnp.int32, (c, 1), 0) + (i * ta + r0) < valid
        smc = sm[r0:r0 + c]
        pick = lambda ln: jnp.where(live, jnp.sum(jnp.where(lane == ln, smc, 0.0), axis=-1, keepdims=True), 0.0)
        q.append(stack([a[r0:r0 + c] for a in qn]))
        k.append(stack([a[r0:r0 + c] for a in kn]))
        v.append(stack([a[r0:r0 + c] for a in va]))
        b_col.append(stack([pick(12 + h) for h in range(H_B)]))
        g_cols = [pick(8 + h) for h in range(H_B)]
        g_stack = stack([jnp.broadcast_to(g, (c, LANE)) for g in g_cols])
        gcb.append(sum(_dot(tri, p) for p in _split3(g_stack)))
        g_wide.append(sum(jnp.where(whead == h, g_cols[h], 0.0) for h in range(H_B)))

    gc_i = [sum(_dot(tri64, p) for p in _split3(g)) for g in g_wide]
    gc_j = [sum(_dot(ones64, p) for p in _split3(jnp.where(wrow <= wcol, g, 0.0))) for g in g_wide]
    decay = [jnp.exp(jnp.where(incl, a - b, -jnp.inf)) for a, b in zip(gc_i, gc_j)]
    eg = [jnp.exp(g) for g in gcb]
    kb = [a * b for a, b in zip(k, b_col)]
    kbf = [a.astype(BF16) for a in k]
    a_mat = [jnp.where(strict, fold(_dot_nt(a.astype(BF16), b)) * d, 0.0) for a, b, d in zip(kb, kbf, decay)]
    qk = [jnp.where(incl, fold(_dot_nt(a.astype(BF16), b)) * d, 0.0) for a, b, d in zip(q, kbf, decay)]
    y = [-a for a in a_mat]
    p = list(y)
    ybd = [block_diag(a.astype(BF16)) for a in y]
    for _ in range(5):
        y = [_dot(a.astype(BF16), bd) for a, bd in zip(y, ybd)]
        ybd = [block_diag(a.astype(BF16)) for a in y]
        p = [a + b + _dot(a.astype(BF16), bd) for a, b, bd in zip(p, y, ybd)]
    rhs = [jnp.concatenate([a * b, kbb * e], axis=1) for a, b, kbb, e in zip(v, b_col, kb, eg)]
    w = [r + _dot(block_diag(a.astype(BF16)), r.astype(BF16)) for a, r in zip(p, rhs)]

    for s in chunks:
        r0 = s * c
        for h in range(H_B):
            hr = slice(h * c, (h + 1) * c)
            hs = slice(h * dk, (h + 1) * dk)
            g_last = gcb[s][h * c + c - 1:(h + 1) * c, :]
            wv_ref[0, r0:r0 + c, hs] = w[s][hr, :dk]
            wk_ref[0, r0:r0 + c, hs] = w[s][hr, dk:].astype(BF16)
            qd_ref[0, r0:r0 + c, hs] = (q[s][hr] * eg[s][hr]).astype(BF16)
            kd_ref[0, r0:r0 + c, hs] = (k[s][hr] * jnp.exp(g_last - gcb[s][hr])).astype(BF16)
            qk_ref[0, h, r0:r0 + c, :] = qk[s][:, h * c:(h + 1) * c].astype(BF16)
            gl_ref[0, s, h:h + 1, :] = jnp.exp(g_last)


def _delta_prep(conv_in, prefix, conv_w, small, valid):
    n, lp, cd = conv_in.shape
    ta = next((r for r in (4 * LANE, 2 * LANE, LANE) if lp % r == 0), lp)
    nt = lp // ta
    tb = ta // 8
    f = lambda dt, w: jax.ShapeDtypeStruct((n, lp, w), dt)
    blk = lambda w: pl.BlockSpec((1, ta, w), lambda b, i: (b, i, 0))
    width = H_B * DK_B
    return pl.pallas_call(
        functools.partial(_delta_prep_kernel, ta=ta, valid=valid),
        grid=(n, nt),
        in_specs=[
            blk(cd),
            pl.BlockSpec((1, 8, cd), lambda b, i: (b, jnp.maximum(i * tb - 1, 0), 0)),
            pl.BlockSpec((1, 8, cd), lambda b, i: (b, 0, 0)),
            pl.BlockSpec((8, cd), lambda b, i: (0, 0)),
            blk(LANE),
        ],
        out_specs=[
            blk(width), blk(width), blk(width), blk(width),
            pl.BlockSpec((1, H_B, ta, DELTA_CHUNK), lambda b, i: (b, 0, i, 0)),
            pl.BlockSpec((1, ta // DELTA_CHUNK, H_B, LANE), lambda b, i: (b, i, 0, 0)),
        ],
        out_shape=[
            f(F32, width), f(BF16, width), f(BF16, width), f(BF16, width),
            jax.ShapeDtypeStruct((n, H_B, lp, DELTA_CHUNK), BF16),
            jax.ShapeDtypeStruct((n, lp // DELTA_CHUNK, H_B, LANE), F32),
        ],
        compiler_params=_params("parallel", "parallel"),
        name="delta_prep",
    )(conv_in, conv_in, prefix, conv_w, small)


def _delta_scan_kernel(wv_ref, wk_ref, qd_ref, kd_ref, qk_ref, gl_ref, s0_ref, gz_ref, onw_ref,
                       o_ref, s_ref, *, nb, n_chunks):
    c = DELTA_CHUNK
    dk = DK_B

    @pl.when(pl.program_id(1) == 0)
    def _():
        s_ref[...] = s0_ref[...]

    onw = onw_ref[...]

    chains = [(b, h) for b in range(nb) for h in range(H_B)]
    cols = lambda h: slice(h * dk, (h + 1) * dk)

    def body(ci, _):
        r0 = pl.multiple_of(ci * c, c)
        rows = pl.ds(r0, c)
        state = [s_ref[b, h] for b, h in chains]
        prod = [_dot(jnp.concatenate([wk_ref[b, rows, cols(h)], qd_ref[b, rows, cols(h)]], axis=0), s.astype(BF16))
                for (b, h), s in zip(chains, state)]
        vb = [(wv_ref[b, rows, cols(h)] - p[:c]).astype(BF16) for (b, h), p in zip(chains, prod)]
        o = [p[c:] + _dot(qk_ref[b, h, rows, :], v) for (b, h), p, v in zip(chains, prod, vb)]
        new = [s * gl_ref[b, pl.ds(ci, 1), h, :] + _dot_tn(kd_ref[b, rows, cols(h)], v)
               for (b, h), s, v in zip(chains, state, vb)]
        for (b, h), s, oo in zip(chains, new, o):
            s_ref[b, h] = s
            on = oo * lax.rsqrt(jnp.mean(oo * oo, axis=-1, keepdims=True) + EPS) * onw
            o_ref[b, rows, cols(h)] = (on * _silu(gz_ref[b, rows, cols(h)])).astype(o_ref.dtype)
        return 0

    lax.fori_loop(0, n_chunks, body, 0)


def _delta_scan(wv, wk, qd, kd, qk, gl, s0, gz, onw, nb):
    n, lp, width = wv.shape
    tl = _row_tile(lp, 512)
    n_chunks = tl // DELTA_CHUNK
    seq = pl.BlockSpec((nb, tl, width), lambda i, t: (i, t, 0))
    state = pl.BlockSpec((nb,) + s0.shape[1:], lambda i, t: (i, 0, 0, 0))
    return pl.pallas_call(
        functools.partial(_delta_scan_kernel, nb=nb, n_chunks=n_chunks),
        grid=(n // nb, lp // tl),
        in_specs=[seq, seq, seq, seq,
                  pl.BlockSpec((nb, H_B, tl, DELTA_CHUNK), lambda i, t: (i, 0, t, 0)),
                  pl.BlockSpec((nb, n_chunks, H_B, LANE), lambda i, t: (i, t, 0, 0)),
                  state, seq, pl.BlockSpec(onw.shape, lambda i, t: (0, 0))],
        out_specs=[seq, state],
        out_shape=[jax.ShapeDtypeStruct((n, lp, width), BF16), jax.ShapeDtypeStruct(s0.shape, F32)],
        compiler_params=_params("parallel", "arbitrary"),
        name="delta_scan",
    )(wv, wk, qd, kd, qk, gl, s0, gz, onw)


def _fox_sample_kernel(pt_ref, q_ref, kn_ref, vn_ref, lfn_ref, kc_hbm, vc_hbm, lfc_hbm, o_ref,
                       qbd_ref, m_ref, l_ref, acc_ref, carry_ref, kbuf, vbuf, lfbuf, sems,
                       *, pages_per_step, n_q):
    pps = pages_per_step
    b = pl.program_id(0)
    j = pl.program_id(1)
    n_b = pl.num_programs(0)
    n_steps = pl.num_programs(1)
    n_pages = pt_ref.shape[1]
    rows = n_q * H_A
    page = lfn_ref.shape[2]
    width = H_A * D_A
    ri = lax.broadcasted_iota(jnp.int32, (page, page), 0)
    ci = lax.broadcasted_iota(jnp.int32, (page, page), 1)

    per_seq = n_steps - 1
    n_slots = kbuf.shape[0]
    g = b * per_seq + j - 1

    def copies(gg, p):
        src = pt_ref[gg // per_seq, n_pages - 1 - ((gg % per_seq) * pps + p)]
        s = gg % n_slots
        return (pltpu.make_async_copy(kc_hbm.at[src], kbuf.at[s, p], sems.at[0, s]),
                pltpu.make_async_copy(vc_hbm.at[src], vbuf.at[s, p], sems.at[1, s]),
                pltpu.make_async_copy(lfc_hbm.at[src], lfbuf.at[s, p], sems.at[2, s]))

    def fetch(gg):
        @pl.when(gg < n_b * per_seq)
        def _():
            for p in range(pps):
                for cp in copies(gg, p):
                    cp.start()

    @pl.when((b == 0) & (j == 0))
    def _():
        for ahead in range(n_slots - 1):
            fetch(jnp.int32(ahead))

    @pl.when(j > 0)
    def _():
        for p in range(pps):
            for cp in copies(g, p):
                cp.wait()
        fetch(g + n_slots - 1)

    def attend(blocks):
        qbd = qbd_ref[...]
        ss = []
        for kt, _, bias8, mask in blocks:
            s = _dot(qbd, kt.astype(BF16)) + jnp.concatenate([bias8] * n_q, axis=0)
            ss.append(s if mask is None else jnp.where(mask, s, -jnp.inf))
        m_old = m_ref[...]
        m_new = m_old
        for s in ss:
            m_new = jnp.maximum(m_new, jnp.max(s, axis=-1, keepdims=True))
        alpha = jnp.exp(m_old - m_new)
        ps = [jnp.exp(s - m_new) for s in ss]
        sums = [jnp.sum(p, axis=-1, keepdims=True) for p in ps]
        pvs = [_dot_nt(p.astype(BF16), vt.astype(BF16)) for p, (_, vt, _, _) in zip(ps, blocks)]
        l_ref[...] = alpha * l_ref[...] + sum(sums)
        acc_ref[...] = alpha * acc_ref[...] + sum(pvs)
        m_ref[...] = m_new

    @pl.when(j == 0)
    def _():
        q = q_ref[0] * (D_A ** -0.5)
        qrep = jnp.concatenate([jnp.broadcast_to(q[t:t + 1], (H_A, width)) for t in range(n_q)], axis=0)
        r = lax.broadcasted_iota(jnp.int32, qrep.shape, 0)
        ln = lax.broadcasted_iota(jnp.int32, qrep.shape, 1)
        qbd_ref[...] = jnp.where(ln // D_A == r % H_A, qrep, 0.0).astype(BF16)
        m_ref[...] = jnp.full(m_ref.shape, -jnp.inf, F32)
        l_ref[...] = jnp.zeros(l_ref.shape, F32)
        acc_ref[...] = jnp.zeros(acc_ref.shape, F32)
        carry_ref[...] = jnp.zeros(carry_ref.shape, F32)
        upper = (ri <= ci).astype(BF16)
        csum = sum(_dot(p, upper) for p in _split3(lfn_ref[0]))
        r2 = lax.broadcasted_iota(jnp.int32, (rows, page), 0)
        c2 = lax.broadcasted_iota(jnp.int32, (rows, page), 1)
        attend([(kn_ref[0], vn_ref[0], -csum, c2 <= r2 // H_A)])

    @pl.when(j > 0)
    def _():
        after = (ri > ci).astype(BF16)
        carry = carry_ref[...]
        slot = g % n_slots
        lfs = [lfbuf[slot, p] for p in range(pps)]
        pieces = [_split3(lf) for lf in lfs]
        suffix = [sum(_dot(p, after) for p in ps) for ps in pieces]
        totals = [jnp.sum(lf, axis=-1, keepdims=True) for lf in lfs]
        blocks = []
        for p, (sfx, tot) in enumerate(zip(suffix, totals)):
            blocks.append((kbuf[slot, p].reshape(width, page), vbuf[slot, p].reshape(width, page), carry + sfx, None))
            carry = carry + tot
        attend(blocks)
        carry_ref[...] = carry

    @pl.when(j == pl.num_programs(1) - 1)
    def _():
        out = acc_ref[...] / l_ref[...]
        r = lax.broadcasted_iota(jnp.int32, out.shape, 0)
        ln = lax.broadcasted_iota(jnp.int32, out.shape, 1)
        out = jnp.where(ln // D_A == r % H_A, out, 0.0)
        o_ref[0] = jnp.sum(out.reshape(n_q, H_A, width), axis=1)


def _fox_sample(page_table, q, kn, vn, lfn, cache_k, cache_v, cache_lf):
    b, n_q, width = q.shape
    n_pages = page_table.shape[1]
    page = cache_k.shape[3]
    rows = n_q * H_A
    pps = next(p for p in (16, 8, 1) if n_pages % p == 0)
    n_slots = 3
    cur = lambda i, j, pt: (i, 0, 0)
    hbm = pl.BlockSpec(memory_space=pl.ANY)
    grid_spec = pltpu.PrefetchScalarGridSpec(
        num_scalar_prefetch=1,
        grid=(b, n_pages // pps + 1),
        in_specs=[
            pl.BlockSpec((1, n_q, width), cur),
            pl.BlockSpec((1, width, page), cur),
            pl.BlockSpec((1, width, page), cur),
            pl.BlockSpec((1, H_A, page), cur),
            hbm, hbm, hbm,
        ],
        out_specs=pl.BlockSpec((1, n_q, width), cur),
        scratch_shapes=[
            pltpu.VMEM((rows, width), BF16),
            pltpu.VMEM((rows, 1), F32),
            pltpu.VMEM((rows, 1), F32),
            pltpu.VMEM((rows, width), F32),
            pltpu.VMEM((H_A, 1), F32),
            pltpu.VMEM((n_slots, pps, H_A, D_A, page), F32),
            pltpu.VMEM((n_slots, pps, H_A, D_A, page), F32),
            pltpu.VMEM((n_slots, pps, H_A, page), F32),
            pltpu.SemaphoreType.DMA((3, n_slots)),
        ],
    )
    return pl.pallas_call(
        functools.partial(_fox_sample_kernel, pages_per_step=pps, n_q=n_q),
        grid_spec=grid_spec,
        out_shape=jax.ShapeDtypeStruct((b, n_q, width), F32),
        compiler_params=_params("arbitrary", "arbitrary"),
        name="fox_sample",
    )(page_table, q, kn, vn, lfn, cache_k, cache_v, cache_lf)


def _router_kernel(x_ref, nw_ref, wr_ref, g_ref, grp_ref, rank_ref, cnt_ref, run_ref):
    step = pl.program_id(0)
    tm = x_ref.shape[0]

    @pl.when(step == 0)
    def _():
        run_ref[...] = jnp.zeros(run_ref.shape, F32)

    h = _rms(x_ref[...], nw_ref[...])
    h_hi = h.astype(BF16)
    h_lo = (h - h_hi.astype(F32)).astype(BF16)
    logits = _dot(h_hi, wr_ref[0]) + (_dot(h_lo, wr_ref[0]) + _dot(h_hi, wr_ref[1]))
    lane = lax.broadcasted_iota(jnp.int32, logits.shape, 1)
    lanef = lane.astype(F32)
    neg = -jnp.inf
    is_group = (lane >= N_EXPERTS) & (lane < N_EXPERTS + N_GROUPS)
    gl = jnp.where(is_group, logits, neg)
    gmax = jnp.max(gl, axis=-1, keepdims=True)
    g_sel = jnp.min(jnp.where(gl == gmax, lanef - N_EXPERTS, 1e9), axis=-1, keepdims=True)
    p_sel = 1.0 / jnp.sum(jnp.where(is_group, jnp.exp(logits - gmax), 0.0), axis=-1, keepdims=True)
    in_sel = (lane < N_EXPERTS) & ((lane // EXPERTS_PER_GROUP).astype(F32) == g_sel)
    el = jnp.where(in_sel, logits, neg)
    m1 = jnp.max(el, axis=-1, keepdims=True)
    i1 = jnp.min(jnp.where(el == m1, lanef, 1e9), axis=-1, keepdims=True)
    el2 = jnp.where(lanef == i1, neg, el)
    m2 = jnp.max(el2, axis=-1, keepdims=True)
    i2 = jnp.min(jnp.where(el2 == m2, lanef, 1e9), axis=-1, keepdims=True)
    e2 = jnp.exp(m2 - m1)
    w1 = p_sel / (1.0 + e2)
    g_ref[...] = jnp.where(lanef == i1, w1, jnp.where(lanef == i2, w1 * e2, jnp.where(lane == N_EXPERTS, g_sel, 0.0)))

    onehot_t = (lanef == g_sel).astype(F32).T[:8]
    ri = lax.broadcasted_iota(jnp.int32, (tm, tm), 0)
    ci = lax.broadcasted_iota(jnp.int32, (tm, tm), 1)
    earlier = _dot(onehot_t.astype(BF16), (ri < ci).astype(BF16))
    run = run_ref[...]
    gidx = lax.broadcasted_iota(jnp.int32, (8, tm), 0).astype(F32)
    rank_ref[0] = jnp.sum(onehot_t * (earlier + run), axis=0, keepdims=True).astype(jnp.int32)
    grp_ref[0] = jnp.sum(onehot_t * gidx, axis=0, keepdims=True).astype(jnp.int32)
    run = run + jnp.sum(onehot_t, axis=-1, keepdims=True)
    run_ref[...] = run
    cnt_ref[...] = jnp.broadcast_to(run, cnt_ref.shape)


def _router(x, nw, wr):
    t, d = x.shape
    tm = _row_tile(t, 512)
    nt = t // tm
    row = pl.BlockSpec((1, 1, tm), lambda i: (i, 0, 0))
    return pl.pallas_call(
        _router_kernel,
        grid=(nt,),
        in_specs=[pl.BlockSpec((tm, d), lambda i: (i, 0)), pl.BlockSpec(nw.shape, lambda i: (0, 0)),
                  pl.BlockSpec(wr.shape, lambda i: (0, 0, 0))],
        out_specs=[pl.BlockSpec((tm, LANE), lambda i: (i, 0)), row, row, pl.BlockSpec((8, LANE), lambda i: (0, 0))],
        out_shape=[jax.ShapeDtypeStruct((t, LANE), F32), jax.ShapeDtypeStruct((nt, 1, tm), jnp.int32),
                   jax.ShapeDtypeStruct((nt, 1, tm), jnp.int32), jax.ShapeDtypeStruct((8, LANE), F32)],
        scratch_shapes=[pltpu.VMEM((8, 1), F32)],
        compiler_params=_params("arbitrary"),
        name="moe_router",
    )(x, nw, wr)


def _moe_kernel(x_ref, nw_ref, g_ref, wg_ref, wu_ref, wd_ref, nf_ref, o_ref, hb_ref, acc_ref, *, final_norm):
    e = pl.program_id(1)

    @pl.when(e == 0)
    def _():
        hb_ref[...] = _rms(x_ref[...], nw_ref[...]).astype(BF16)
        acc_ref[...] = jnp.zeros(acc_ref.shape, F32)

    gates = g_ref[...]
    lane = lax.broadcasted_iota(jnp.int32, gates.shape, 1)
    gcol = jnp.sum(jnp.where(lane == e, gates, 0.0), axis=-1, keepdims=True)
    hb = hb_ref[...]
    act = _silu(_dot(hb, wg_ref[0, 0])) * _dot(hb, wu_ref[0, 0]) * gcol
    acc_ref[...] += _dot(act.astype(BF16), wd_ref[0, 0])

    @pl.when(e == pl.num_programs(1) - 1)
    def _():
        y = x_ref[...] + acc_ref[...]
        o_ref[...] = _rms(y, nf_ref[...]) if final_norm else y


def _moe_dense(x, nw, gates, wg, wu, wd, nf, layer, final_norm):
    t, d = x.shape
    tm = _row_tile(t, 1024)
    _, n_e, _, f = wg.shape
    return pl.pallas_call(
        functools.partial(_moe_kernel, final_norm=final_norm),
        grid=(t // tm, n_e),
        in_specs=[
            pl.BlockSpec((tm, d), lambda i, e: (i, 0)),
            pl.BlockSpec(nw.shape, lambda i, e: (0, 0)),
            pl.BlockSpec((tm, LANE), lambda i, e: (i, 0)),
            pl.BlockSpec((1, 1, d, f), lambda i, e: (layer, e, 0, 0)),
            pl.BlockSpec((1, 1, d, f), lambda i, e: (layer, e, 0, 0)),
            pl.BlockSpec((1, 1, f, d), lambda i, e: (layer, e, 0, 0)),
            pl.BlockSpec(nf.shape, lambda i, e: (0, 0)),
        ],
        out_specs=pl.BlockSpec((tm, d), lambda i, e: (i, 0)),
        out_shape=jax.ShapeDtypeStruct((t, d), F32),
        scratch_shapes=[pltpu.VMEM((tm, d), BF16), pltpu.VMEM((tm, d), F32)],
        compiler_params=_params("parallel", "arbitrary"),
        name="moe_experts",
    )(x, nw, gates, wg, wu, wd, nf)


def _row_copy(src, src_row, dst, dst_row, sem):
    return pltpu.make_async_copy(src.at[pl.ds(src_row, 1)], dst.at[pl.ds(dst_row, 1)], sem)


def _dispatch_kernel(zt_ref, dest_ref, x_ref, g_ref, o_ref, stage_ref, sem):
    tm, d = x_ref.shape

    @pl.when(pl.program_id(0) == 0)
    def _():
        stage_ref[...] = jnp.zeros(stage_ref.shape, F32)
        for k in range(zt_ref.shape[0]):
            cp = pltpu.make_async_copy(stage_ref, o_ref.at[pl.ds(pl.multiple_of(zt_ref[k] * tm, tm), tm)], sem)
            cp.start()
            cp.wait()

    stage_ref[:, :d] = x_ref[...]
    stage_ref[:, d:] = g_ref[...]

    def start(r2, _):
        for pr in range(2):
            r = 2 * r2 + pr
            _row_copy(stage_ref, r, o_ref, dest_ref[0, 0, r], sem).start(priority=pr)
        return 0

    def wait(r, _):
        _row_copy(stage_ref, 0, o_ref, 0, sem).wait()
        return 0

    lax.fori_loop(0, tm // 2, start, 0, unroll=4)
    lax.fori_loop(0, tm, wait, 0, unroll=8)


def _dispatch(zero_tiles, dest, x, gates, n_tiles, tm):
    t, d = x.shape
    grid_spec = pltpu.PrefetchScalarGridSpec(
        num_scalar_prefetch=1,
        grid=(t // tm,),
        in_specs=[pl.BlockSpec((1, 1, tm), lambda i, zt: (i, 0, 0), memory_space=pltpu.SMEM),
                  pl.BlockSpec((tm, d), lambda i, zt: (i, 0)),
                  pl.BlockSpec((tm, LANE), lambda i, zt: (i, 0))],
        out_specs=pl.BlockSpec(memory_space=pl.ANY),
        scratch_shapes=[pltpu.VMEM((tm, d + LANE), F32), pltpu.SemaphoreType.DMA(())],
    )
    return pl.pallas_call(
        _dispatch_kernel,
        grid_spec=grid_spec,
        out_shape=jax.ShapeDtypeStruct((n_tiles * tm, d + LANE), F32),
        compiler_params=_params("arbitrary"),
        name="moe_dispatch",
    )(zero_tiles, dest, x, gates)


def _moe_group_kernel(tg_ref, na_ref, xs_ref, nw_ref, wg_ref, wu_ref, wd_ref, nf_ref, o_ref, *, final_norm):
    i = pl.program_id(0)
    d = o_ref.shape[1]

    @pl.when(i < na_ref[0])
    def _():
        x = xs_ref[:, :d]
        gates = xs_ref[:, d:]
        hb = _rms(x, nw_ref[...]).astype(BF16)
        lane = lax.broadcasted_iota(jnp.int32, gates.shape, 1)
        first = tg_ref[i] * EXPERTS_PER_GROUP
        acc = jnp.zeros(x.shape, F32)
        for e in range(EXPERTS_PER_GROUP):
            gcol = jnp.sum(jnp.where(lane == first + e, gates, 0.0), axis=-1, keepdims=True)
            act = _silu(_dot(hb, wg_ref[0, 0, e])) * _dot(hb, wu_ref[0, 0, e]) * gcol
            acc += _dot(act.astype(BF16), wd_ref[0, 0, e])
        y = x + acc
        o_ref[...] = _rms(y, nf_ref[...]) if final_norm else y

    @pl.when(i >= na_ref[0])
    def _():
        o_ref[...] = jnp.zeros(o_ref.shape, F32)


def _moe_group(tile_group, n_active, xs, nw, wg, wu, wd, nf, tm, layer, final_norm):
    tp, dx = xs.shape
    d = dx - LANE
    wspec = lambda a: pl.BlockSpec((1, 1) + a.shape[2:], lambda i, tg, na: (layer, tg[i], 0, 0, 0))
    vec = lambda a: pl.BlockSpec(a.shape, lambda i, tg, na: (0, 0))
    grid_spec = pltpu.PrefetchScalarGridSpec(
        num_scalar_prefetch=2,
        grid=(tp // tm,),
        in_specs=[pl.BlockSpec((tm, dx), lambda i, tg, na: (i, 0)), vec(nw), wspec(wg), wspec(wu), wspec(wd), vec(nf)],
        out_specs=pl.BlockSpec((tm, d), lambda i, tg, na: (i, 0)),
    )
    return pl.pallas_call(
        functools.partial(_moe_group_kernel, final_norm=final_norm),
        grid_spec=grid_spec,
        out_shape=jax.ShapeDtypeStruct((tp, d), F32),
        compiler_params=_params("arbitrary"),
        name="moe_group_experts",
    )(tile_group, n_active, xs, nw, wg, wu, wd, nf)


def _combine_kernel(dest_ref, ys_ref, o_ref, sem):
    tm = o_ref.shape[0]

    def start(r2, _):
        for pr in range(2):
            r = 2 * r2 + pr
            _row_copy(ys_ref, dest_ref[0, 0, r], o_ref, r, sem).start(priority=pr)
        return 0

    def wait(r, _):
        _row_copy(ys_ref, 0, o_ref, 0, sem).wait()
        return 0

    lax.fori_loop(0, tm // 2, start, 0, unroll=4)
    lax.fori_loop(0, tm, wait, 0, unroll=8)


def _combine(dest, ys, t, tm):
    d = ys.shape[1]
    return pl.pallas_call(
        _combine_kernel,
        grid=(t // tm,),
        in_specs=[pl.BlockSpec((1, 1, tm), lambda i: (i, 0, 0), memory_space=pltpu.SMEM),
                  pl.BlockSpec(memory_space=pl.ANY)],
        out_specs=pl.BlockSpec((tm, d), lambda i: (i, 0)),
        out_shape=jax.ShapeDtypeStruct((t, d), F32),
        scratch_shapes=[pltpu.SemaphoreType.DMA(())],
        compiler_params=_params("arbitrary"),
        name="moe_combine",
    )(dest, ys)


def _moe_sorted(x, nw, routed, wg, wu, wd, nf, layer, final_norm):
    t, d = x.shape
    gates, grp, rank, cnt = routed
    tm = grp.shape[2]
    counts = [cnt[g, 0].astype(jnp.int32) for g in range(N_GROUPS)]
    tiles = [(c + tm - 1) // tm for c in counts]
    tile_end = [sum(tiles[:g + 1]) for g in range(N_GROUPS)]
    dest = rank + sum(jnp.where(grp == g, (tile_end[g] - tiles[g]) * tm, 0) for g in range(N_GROUPS))
    n_tiles = t // tm + N_GROUPS
    tile = jnp.arange(n_tiles, dtype=jnp.int32)
    tile_group = jnp.minimum(sum((tile >= e).astype(jnp.int32) for e in tile_end), N_GROUPS - 1)
    last = n_tiles - 1
    zero_tiles = jnp.stack([jnp.where(tiles[g] > 0, tile_end[g] - 1, last) for g in range(N_GROUPS)]
                           + [jnp.minimum(tile_end[-1] + k, last) for k in range(N_GROUPS)]).astype(jnp.int32)
    xs = _dispatch(zero_tiles, dest, x, gates, n_tiles, tm)
    split = lambda w: w.reshape((w.shape[0], N_GROUPS, EXPERTS_PER_GROUP) + w.shape[2:])
    ys = _moe_group(tile_group, tile_end[-1].reshape(1), xs, nw,
                    split(wg), split(wu), split(wd), nf, tm, layer, final_norm)
    return _combine(dest, ys, t, tm)


def _odd_kernel(x_ref, nw_ref, win_ref, lnw_ref, lnb_ref, wm_ref, bs_ref, wout_ref, *refs, emit_v):
    if emit_v:
        o_ref, v_ref, gated_ref = refs
    else:
        o_ref, gated_ref = refs
    tm = x_ref.shape[0]
    dc = lnw_ref.shape[1]
    gd = dc // H_C
    x = x_ref[...]
    hb = _rms(x, nw_ref[...]).astype(BF16)
    z = _dot(hb, win_ref[...])
    z = 0.5 * z * (1.0 + lax.erf(z * (2.0 ** -0.5)))
    v = z[:, dc:]
    mu = jnp.mean(v, axis=-1, keepdims=True)
    vc = v - mu
    vn = vc * lax.rsqrt(jnp.mean(vc * vc, axis=-1, keepdims=True) + EPS) * lnw_ref[...] + lnb_ref[...]
    if emit_v:
        v_ref[...] = vn
    vb = vn.astype(BF16)
    for ci in range(tm // CHUNK_C):
        rs = slice(ci * CHUNK_C, (ci + 1) * CHUNK_C)
        for g in range(H_C):
            cs = slice(g * gd, (g + 1) * gd)
            mixed = _dot(wm_ref[g], vb[rs, cs]) + jnp.concatenate([bs_ref[g]] * (gd // LANE), axis=1)
            gated_ref[rs, cs] = (z[rs, cs] * mixed).astype(BF16)
    o_ref[...] = x + _dot(gated_ref[...], wout_ref[...])


def _odd_mixer(x, nw, w_in, ln_w, ln_b, wm, bs, w_out, emit_v):
    t, d = x.shape
    dc = ln_w.shape[1]
    tm = _row_tile(t, 512)
    row = lambda w: pl.BlockSpec((tm, w), lambda i: (i, 0))
    full = lambda a: pl.BlockSpec(a.shape, lambda i: (0,) * a.ndim)
    out_specs = [row(d)] + ([row(dc)] if emit_v else [])
    out_shape = [jax.ShapeDtypeStruct((t, d), F32)] + ([jax.ShapeDtypeStruct((t, dc), F32)] if emit_v else [])
    return pl.pallas_call(
        functools.partial(_odd_kernel, emit_v=emit_v),
        grid=(t // tm,),
        in_specs=[row(d), full(nw), full(w_in), full(ln_w), full(ln_b), full(wm), full(bs), full(w_out)],
        out_specs=out_specs,
        out_shape=out_shape,
        scratch_shapes=[pltpu.VMEM((tm, dc), BF16)],
        compiler_params=_params("parallel"),
        name="odd_mixer",
    )(x, nw, w_in, ln_w, ln_b, wm, bs, w_out)


def _prep_even_weights(w_in, f_bias, a_log, dt_bias):
    d = w_in.shape[0]
    small = jnp.concatenate(
        [w_in[:, 1536:1544], w_in[:, 3080:3088], jnp.zeros((d, LANE - 16), w_in.dtype)], axis=1)
    w = jnp.concatenate([w_in[:, 0:512], w_in[:, 1544:3080], w_in[:, 3088:3600], small], axis=1)
    bias = jnp.zeros((1, LANE), F32).at[0, 0:8].set(f_bias).at[0, 8:12].set(dt_bias)
    alog = jnp.zeros((1, LANE), F32).at[0, 8:12].set(a_log)
    return w.astype(BF16), w_in[:, 512:1536].T.astype(BF16), bias, alog


def kernel(x_prompt, x_sample, cache_k, cache_v, cache_logf, state_delta, state_conv, page_table, norm_mix, norm_ffn, norm_final, w_in_even, w_out_even, fox_forget_bias, dn_conv_w, dn_a_log, dn_dt_bias, dn_norm_w, w_in_odd, gm_ln_w, gm_ln_b, gm_spatial_w, gm_spatial_b, w_out_odd, moe_router_group, moe_router_expert, moe_w_gate, moe_w_up, moe_w_down):
    n_p, seq, d = x_prompt.shape
    n_s, dec = x_sample.shape[:2]
    page = cache_k.shape[2]
    mix_a = H_A * D_A
    xp = x_prompt.reshape(n_p * seq, d)
    xs = x_sample.reshape(n_s * dec, d)
    row = lambda a: a[None, :]

    even_w = (row(norm_mix[0]),) + _prep_even_weights(w_in_even[0], fox_forget_bias[0], dn_a_log[0], dn_dt_bias[0])
    from_kt = lambda a, n, length: jnp.transpose(a.reshape(n, H_A, D_A, length), (0, 3, 1, 2))[None]
    w_out_e = w_out_even[0].astype(BF16)
    conv_w = jnp.zeros((8, dn_conv_w.shape[2]), F32).at[:CONV_W].set(dn_conv_w[0])
    onw = row(dn_norm_w[0])

    def delta(conv3, prefix, small3, gz3, s0, valid, nb):
        wv, wk, qd, kd, qk, gl = _delta_prep(conv3, prefix, conv_w, small3, valid)
        return _delta_scan(wv, wk, qd, kd, qk, gl, s0, gz3, onw, nb)

    q, kt, vt, c, gz, s = _even_in(xp, *even_w, n_p, seq)
    logf_p = s[:, :H_A].reshape(n_p, seq, H_A)
    csum = _cumsum_lanes(jnp.swapaxes(logf_p, 1, 2).reshape(n_p * H_A, seq)).reshape(n_p, H_A, seq)
    as3 = lambda a: a.reshape(n_p, seq, a.shape[-1])
    oa_p = _fox_prompt(as3(q), kt, vt, csum)
    conv_p = as3(c)
    ob_p, sd_p = delta(conv_p, jnp.zeros((n_p, 8, conv_p.shape[-1]), F32), as3(s), as3(gz),
                       jnp.zeros((n_p, H_B, DK_B, DK_B), F32), seq, 4)
    xp = _even_out(xp, oa_p.reshape(n_p * seq, mix_a), ob_p.reshape(n_p * seq, -1), w_out_e)
    k_p = from_kt(kt, n_p, seq)
    v_p = from_kt(vt, n_p, seq)
    conv_state_p = conv_p[:, seq - (CONV_W - 1):][None]

    q, kt, vt, c, gz, s = _even_in(xs, *even_w, 1, n_s * dec)
    as3 = lambda a: a.reshape(n_s, dec, a.shape[-1])
    padrows = lambda a, n: jnp.pad(a, ((0, 0), (0, n - a.shape[1]), (0, 0)))
    logf_s = s[:, :H_A].reshape(n_s, dec, H_A)
    pad_keys = lambda a: jnp.pad(a, ((0, 0), (0, 0), (0, page - dec)))
    per_seq = lambda a: pad_keys(jnp.swapaxes(a[0].reshape(mix_a, n_s, dec), 0, 1))
    oa_s = _fox_sample(page_table, as3(q), per_seq(kt), per_seq(vt), pad_keys(jnp.swapaxes(logf_s, 1, 2)),
                       jnp.transpose(cache_k[0], (0, 2, 3, 1)), jnp.transpose(cache_v[0], (0, 2, 3, 1)),
                       jnp.swapaxes(cache_logf[0], 1, 2))
    conv_s = as3(c)
    lp = DELTA_CHUNK
    prefix_s = jnp.pad(state_conv[0], ((0, 0), (8 - (CONV_W - 1), 0), (0, 0)))
    ob_s, sd_s = delta(padrows(conv_s, lp), prefix_s, padrows(as3(s), lp), padrows(as3(gz), lp),
                       state_delta[0], dec, 8)
    xs = _even_out(xs, oa_s.reshape(n_s * dec, mix_a).astype(BF16), ob_s[:, :dec].reshape(n_s * dec, -1), w_out_e)
    k_s = kt[0].T.reshape(1, n_s, dec, H_A, D_A)
    v_s = vt[0].T.reshape(1, n_s, dec, H_A, D_A)
    conv_state_s = jnp.concatenate([state_conv[0], conv_s], axis=1)[:, dec:][None]

    def moe(x, layer, final):
        wr = jnp.concatenate([moe_router_expert[layer], moe_router_group[layer],
                              jnp.zeros((d, LANE - N_EXPERTS - N_GROUPS), F32)], axis=1)
        wr_hi = wr.astype(BF16)
        wr = jnp.stack([wr_hi, (wr - wr_hi.astype(F32)).astype(BF16)])
        routed = _router(x, row(norm_ffn[layer]), wr)
        if x.shape[0] >= 4096:
            return _moe_sorted(x, row(norm_ffn[layer]), routed, wg, wu, wd, row(norm_final), layer, final)
        return _moe_dense(x, row(norm_ffn[layer]), routed[0], wg, wu, wd, row(norm_final), layer, final)

    wg, wu, wd = moe_w_gate.astype(BF16), moe_w_up.astype(BF16), moe_w_down.astype(BF16)
    xp = moe(xp, 0, False)
    xs = moe(xs, 0, False)

    idx = jnp.arange(CHUNK_C)
    w_m = jnp.where(idx[:, None] >= idx[None, :], gm_spatial_w[0], 0.0)
    b_s = gm_spatial_b[0]
    lanes = lambda b: jnp.broadcast_to(b[:, :, None], b.shape + (LANE,))
    reps = CHUNK_C // dec
    assert n_s * dec == CHUNK_C, "sample group must fill exactly one 128-row mixing tile"
    w_m_s = (jnp.eye(reps, dtype=F32)[None, :, None, :, None] * w_m[:, None, :dec, None, :dec]).reshape(H_C, CHUNK_C, CHUNK_C)
    b_s_s = jnp.tile(b_s[:, :dec], (1, reps))
    odd_w = (row(norm_mix[1]), w_in_odd[0].astype(BF16), row(gm_ln_w[0]), row(gm_ln_b[0]))
    w_out_o = w_out_odd[0].astype(BF16)
    (xp,) = _odd_mixer(xp, *odd_w, w_m.astype(BF16), lanes(b_s), w_out_o, False)
    xs, v_rows = _odd_mixer(xs, *odd_w, w_m_s.astype(BF16), lanes(b_s_s), w_out_o, True)

    y_p = moe(xp, 1, True).reshape(n_p, seq, d)
    y_s = moe(xs, 1, True).reshape(n_s, dec, d)
    return (y_p, y_s, k_p, v_p, logf_p[None], sd_p[None], conv_state_p,
            k_s, v_s, logf_s[None], sd_s[None], conv_state_s, v_rows.reshape(1, n_s, dec, -1))
```

```python
import functools

import jax
import jax.numpy as jnp
from jax import lax
from jax.experimental import pallas as pl
from jax.experimental.pallas import tpu as pltpu

F32 = jnp.float32
BF16 = jnp.bfloat16
EPS = 1e-6
LANE = 128
VMEM_LIMIT = 56 * 1024 * 1024
HIGHEST = lax.Precision.HIGHEST

H_A, D_A = 8, 64
H_B, DK_B = 4, 128
CONV_W = 4
DELTA_CHUNK = 64
N_GROUPS, EXPERTS_PER_GROUP = 4, 8
N_EXPERTS = N_GROUPS * EXPERTS_PER_GROUP
CHUNK_C = 128
H_C = 8


def _params(*sem):
    return pltpu.CompilerParams(dimension_semantics=sem, vmem_limit_bytes=VMEM_LIMIT)


def _row_tile(t, pref):
    return pref if t % pref == 0 else t


def _rms(x, w):
    return x * lax.rsqrt(jnp.mean(x * x, axis=-1, keepdims=True) + EPS) * w


def _softplus_tail(z):
    return jnp.log1p(jnp.exp(-jnp.abs(z)))


def _even_in_kernel(x_ref, nw_ref, w_ref, wkv_ref, bias_ref, alog_ref,
                    q_ref, k_ref, v_ref, c_ref, gz_ref, s_ref):
    hb = _rms(x_ref[...], nw_ref[...]).astype(BF16)

    def proj(lo, hi):
        return jnp.dot(hb, w_ref[:, lo:hi], preferred_element_type=F32)

    q_ref[...] = proj(0, 512)
    c_ref[...] = proj(512, 2048)
    gz_ref[...] = proj(2048, 2560)
    k_ref[0] = lax.dot_general(wkv_ref[:512, :], hb, (((1,), (1,)), ((), ())), preferred_element_type=F32)
    v_ref[0] = lax.dot_general(wkv_ref[512:, :], hb, (((1,), (1,)), ((), ())), preferred_element_type=F32)
    z = proj(2560, 2688) + bias_ref[...]
    tail = _softplus_tail(z)
    logf = jnp.minimum(z, 0.0) - tail
    g = -jnp.exp(alog_ref[...]) * (jnp.maximum(z, 0.0) + tail)
    beta = 1.0 / (1.0 + jnp.exp(-z))
    lane = lax.broadcasted_iota(jnp.int32, z.shape, 1)
    s_ref[...] = jnp.where(lane < 8, logf, jnp.where(lane < 12, g, jnp.where(lane < 16, beta, 0.0)))


def _even_in(x, nw, w, wkv, bias, alog, n, length):
    t, d = x.shape
    tm = _row_tile(length, 512)
    per_seq = length // tm
    mix = wkv.shape[0] // 2
    row = lambda wd: pl.BlockSpec((tm, wd), lambda i: (i, 0))
    full = lambda a: pl.BlockSpec(a.shape, lambda i: (0, 0))
    kt = pl.BlockSpec((1, mix, tm), lambda i: (i // per_seq, 0, i % per_seq))
    rows = lambda wd: jax.ShapeDtypeStruct((t, wd), F32)
    kts = jax.ShapeDtypeStruct((n, mix, length), F32)
    return pl.pallas_call(
        _even_in_kernel,
        grid=(t // tm,),
        in_specs=[row(d), full(nw), full(w), full(wkv), full(bias), full(alog)],
        out_specs=[row(mix), kt, kt, row(3 * mix), row(mix), row(LANE)],
        out_shape=[rows(mix), kts, kts, rows(3 * mix), rows(mix), rows(LANE)],
        compiler_params=_params("parallel"),
        name="even_in_proj",
    )(x, nw, w, wkv, bias, alog)


def _cumsum_kernel(x_ref, o_ref):
    rows, length = x_ref.shape
    r = lax.broadcasted_iota(jnp.int32, (LANE, LANE), 0)
    c = lax.broadcasted_iota(jnp.int32, (LANE, LANE), 1)
    upper = (r <= c).astype(F32)
    carry = jnp.zeros((rows, 1), F32)
    for b in range(length // LANE):
        blk = jnp.dot(x_ref[:, b * LANE:(b + 1) * LANE], upper,
                      preferred_element_type=F32, precision=HIGHEST) + carry
        o_ref[:, b * LANE:(b + 1) * LANE] = blk
        carry = blk[:, LANE - 1:LANE]


def _cumsum_lanes(x):
    rows, length = x.shape
    tr = 8
    return pl.pallas_call(
        _cumsum_kernel,
        grid=(rows // tr,),
        in_specs=[pl.BlockSpec((tr, length), lambda i: (i, 0))],
        out_specs=pl.BlockSpec((tr, length), lambda i: (i, 0)),
        out_shape=jax.ShapeDtypeStruct((rows, length), F32),
        compiler_params=_params("parallel"),
        name="logf_cumsum",
    )(x)


def _fox_prompt_kernel(q_ref, k_ref, v_ref, c_ref, o_ref, *, tq):
    qi = pl.program_id(2)
    hp = pl.program_id(1)
    scale = D_A ** -0.5
    lane = lax.broadcasted_iota(jnp.int32, (tq, LANE), 1)
    first = lane < D_A
    q = q_ref[0] * scale
    qs = [jnp.where(first, q, 0.0).astype(BF16), jnp.where(first, 0.0, q).astype(BF16)]
    q0 = pl.multiple_of(qi * tq, tq)

    def crow(h, start):
        return c_ref[0, pl.ds(2 * hp + h, 1), pl.ds(start, tq)]

    cq = [crow(h, q0)[:, 0:1] for h in range(2)]

    def block(j0, carry, mask):
        kb = k_ref[0, :, pl.ds(j0, tq)].astype(BF16)
        vb = v_ref[0, :, pl.ds(j0, tq)].astype(BF16)
        hh = range(2)
        s = [_dot(qs[h], kb) + (cq[h] - crow(h, j0)) for h in hh]
        if mask is not None:
            s = [jnp.where(mask, a, -jnp.inf) for a in s]
        m_new = [jnp.maximum(carry[h][0], jnp.max(s[h], axis=-1, keepdims=True)) for h in hh]
        alpha = [jnp.exp(carry[h][0] - m_new[h]) for h in hh]
        p = [jnp.exp(s[h] - m_new[h]) for h in hh]
        l = [alpha[h] * carry[h][1] + jnp.sum(p[h], axis=-1, keepdims=True) for h in hh]
        pv = [_dot_nt(p[h].astype(BF16), vb) for h in hh]
        acc = [alpha[h] * carry[h][2] + pv[h] for h in hh]
        return tuple((m_new[h], l[h], acc[h]) for h in hh)

    def body(j, carry):
        return block(pl.multiple_of(j * tq, tq), carry, None)

    init = (jnp.full((tq, 1), -jnp.inf, F32), jnp.zeros((tq, 1), F32), jnp.zeros((tq, LANE), F32))
    carry = lax.fori_loop(0, qi, body, (init, init))
    r = lax.broadcasted_iota(jnp.int32, (tq, tq), 0)
    cidx = lax.broadcasted_iota(jnp.int32, (tq, tq), 1)
    (_, l0, a0), (_, l1, a1) = block(q0, carry, cidx <= r)
    o_ref[0] = jnp.where(first, a0 / l0, a1 / l1).astype(o_ref.dtype)


def _fox_prompt(q, k, v, c):
    n, length, width = q.shape
    tq = next((r for r in (1024, 512) if length % r == 0), length)
    grid = (n, width // LANE, length // tq)
    return pl.pallas_call(
        functools.partial(_fox_prompt_kernel, tq=tq),
        grid=grid,
        in_specs=[
            pl.BlockSpec((1, tq, LANE), lambda b, h, i: (b, i, h)),
            pl.BlockSpec((1, LANE, length), lambda b, h, i: (b, h, 0)),
            pl.BlockSpec((1, LANE, length), lambda b, h, i: (b, h, 0)),
            pl.BlockSpec((1, H_A, length), lambda b, h, i: (b, 0, 0)),
        ],
        out_specs=pl.BlockSpec((1, tq, LANE), lambda b, h, i: (b, i, h)),
        out_shape=jax.ShapeDtypeStruct((n, length, width), BF16),
        compiler_params=_params("parallel", "parallel", "arbitrary"),
        name="fox_prompt",
    )(q, k, v, c)


def _even_out_kernel(x_ref, a_ref, b_ref, w_ref, nw_ref, wr_ref, o_ref, *route_refs):
    half = a_ref.shape[1]
    acc = jnp.dot(a_ref[...], w_ref[:half, :], preferred_element_type=F32)
    acc += jnp.dot(b_ref[...], w_ref[half:, :], preferred_element_type=F32)
    y = x_ref[...] + acc
    o_ref[...] = y
    _route(y, nw_ref, wr_ref, *route_refs)


def _even_out(x, oa, ob, w, nw, wr):
    t, d = x.shape
    tm = _row_tile(t, 512)
    nt = t // tm
    row = lambda a: pl.BlockSpec((tm, a.shape[1]), lambda i: (i, 0))
    full = lambda a: pl.BlockSpec(a.shape, lambda i: (0,) * a.ndim)
    lane_row = pl.BlockSpec((1, 1, tm), lambda i: (i, 0, 0))
    outs = pl.pallas_call(
        _even_out_kernel,
        grid=(nt,),
        in_specs=[row(x), row(oa), row(ob), full(w), full(nw), full(wr)],
        out_specs=[row(x), pl.BlockSpec((tm, LANE), lambda i: (i, 0)), lane_row, lane_row,
                   pl.BlockSpec((8, LANE), lambda i: (0, 0))],
        out_shape=[jax.ShapeDtypeStruct((t, d), F32), jax.ShapeDtypeStruct((t, LANE), F32),
                   jax.ShapeDtypeStruct((nt, 1, tm), jnp.int32), jax.ShapeDtypeStruct((nt, 1, tm), jnp.int32),
                   jax.ShapeDtypeStruct((8, LANE), F32)],
        scratch_shapes=[pltpu.VMEM((8, 1), F32)],
        compiler_params=_params("arbitrary"),
        name="even_out_proj",
    )(x, oa, ob, w, nw, wr)
    return outs[0], tuple(outs[1:])


def _split3(x):
    hi = x.astype(BF16)
    r1 = x - hi.astype(F32)
    mid = r1.astype(BF16)
    lo = (r1 - mid.astype(F32)).astype(BF16)
    return hi, mid, lo


def _dot(a, b):
    return jnp.dot(a, b, preferred_element_type=F32)


def _dot_nt(a, b):
    return lax.dot_general(a, b, (((1,), (1,)), ((), ())), preferred_element_type=F32)


def _dot_tn(a, b):
    return lax.dot_general(a, b, (((0,), (0,)), ((), ())), preferred_element_type=F32)


def _silu(x):
    half = 0.5 * x
    return half + half * jnp.tanh(half)


def _delta_prep_kernel(x_ref, xprev_ref, pre_ref, cw_ref, sm_ref,
                       wv_ref, wk_ref, qd_ref, kd_ref, qk_ref, gl_ref, *, ta, valid):
    i = pl.program_id(1)
    c = DELTA_CHUNK
    dk = DK_B
    rr = H_B * c
    x = x_ref[0]
    prev = jnp.where(i == 0, pre_ref[0], xprev_ref[0])
    xcat = jnp.concatenate([prev, x], axis=0)
    conv = x * cw_ref[CONV_W - 1:CONV_W, :]
    for s in range(1, CONV_W):
        conv += pltpu.roll(xcat, s, axis=0)[8:] * cw_ref[CONV_W - 1 - s:CONV_W - s, :]
    act = _silu(conv)
    sm = sm_ref[0]

    ri = lax.broadcasted_iota(jnp.int32, (rr, rr), 0)
    ci = lax.broadcasted_iota(jnp.int32, (rr, rr), 1)
    same_head = (ri // c) == (ci // c)
    same_bf = same_head.astype(BF16)
    tri = (same_head & (ri >= ci)).astype(BF16)
    lane = lax.broadcasted_iota(jnp.int32, (c, LANE), 1)
    wrow = lax.broadcasted_iota(jnp.int32, (c, rr), 0)
    wlane = lax.broadcasted_iota(jnp.int32, (c, rr), 1)
    whead = wlane // c
    wcol = wlane % c
    incl = wrow >= wcol
    strict = wrow > wcol
    r64 = lax.broadcasted_iota(jnp.int32, (c, c), 0)
    c64 = lax.broadcasted_iota(jnp.int32, (c, c), 1)
    tri64 = (r64 >= c64).astype(BF16)
    ones64 = jnp.ones((c, c), BF16)

    def block_diag(wide_bf):
        return jnp.concatenate([wide_bf] * H_B, axis=0) * same_bf

    def fold(bd):
        return sum(jnp.where(whead == h, bd[h * c:(h + 1) * c], 0.0) for h in range(H_B))

    qn, kn, va = [], [], []
    for h in range(H_B):
        qa = act[:, h * dk:(h + 1) * dk]
        ka = act[:, (H_B + h) * dk:(H_B + h + 1) * dk]
        qn.append(qa * lax.rsqrt(jnp.sum(qa * qa, axis=-1, keepdims=True) + EPS) * (dk ** -0.5))
        kn.append(ka * lax.rsqrt(jnp.sum(ka * ka, axis=-1, keepdims=True) + EPS))
        va.append(act[:, (2 * H_B + h) * dk:(2 * H_B + h + 1) * dk])

    chunks = range(ta // c)
    stack = lambda parts: jnp.concatenate(parts, axis=0)
    q, k, v, b_col, gcb, g_wide = [], [], [], [], [], []
    for s in chunks:
        r0 = s * c
        live = lax.broadcasted_iota(jnp.int32, (c, 1), 0) + (i * ta + r0) < valid
        smc = sm[r0:r0 + c]
        pick = lambda ln: jnp.where(live, jnp.sum(jnp.where(lane == ln, smc, 0.0), axis=-1, keepdims=True), 0.0)
        q.append(stack([a[r0:r0 + c] for a in qn]))
        k.append(stack([a[r0:r0 + c] for a in kn]))
        v.append(stack([a[r0:r0 + c] for a in va]))
        b_col.append(stack([pick(12 + h) for h in range(H_B)]))
        g_cols = [pick(8 + h) for h in range(H_B)]
        g_stack = stack([jnp.broadcast_to(g, (c, LANE)) for g in g_cols])
        gcb.append(sum(_dot(tri, p) for p in _split3(g_stack)))
        g_wide.append(sum(jnp.where(whead == h, g_cols[h], 0.0) for h in range(H_B)))

    gc_i = [sum(_dot(tri64, p) for p in _split3(g)) for g in g_wide]
    gc_j = [sum(_dot(ones64, p) for p in _split3(jnp.where(wrow <= wcol, g, 0.0))) for g in g_wide]
    decay = [jnp.exp(jnp.where(incl, a - b, -jnp.inf)) for a, b in zip(gc_i, gc_j)]
    eg = [jnp.exp(g) for g in gcb]
    kb = [a * b for a, b in zip(k, b_col)]
    kbf = [a.astype(BF16) for a in k]
    a_mat = [jnp.where(strict, fold(_dot_nt(a.astype(BF16), b)) * d, 0.0) for a, b, d in zip(kb, kbf, decay)]
    qk = [jnp.where(incl, fold(_dot_nt(a.astype(BF16), b)) * d, 0.0) for a, b, d in zip(q, kbf, decay)]
    y = [-a for a in a_mat]
    p = list(y)
    ybd = [block_diag(a.astype(BF16)) for a in y]
    for _ in range(5):
        y = [_dot(a.astype(BF16), bd) for a, bd in zip(y, ybd)]
        ybd = [block_diag(a.astype(BF16)) for a in y]
        p = [a + b + _dot(a.astype(BF16), bd) for a, b, bd in zip(p, y, ybd)]
    rhs = [jnp.concatenate([a * b, kbb * e], axis=1) for a, b, kbb, e in zip(v, b_col, kb, eg)]
    w = [r + _dot(block_diag(a.astype(BF16)), r.astype(BF16)) for a, r in zip(p, rhs)]

    for s in chunks:
        r0 = s * c
        for h in range(H_B):
            hr = slice(h * c, (h + 1) * c)
            hs = slice(h * dk, (h + 1) * dk)
            g_last = gcb[s][h * c + c - 1:(h + 1) * c, :]
            wv_ref[0, r0:r0 + c, hs] = w[s][hr, :dk]
            wk_ref[0, r0:r0 + c, hs] = w[s][hr, dk:].astype(BF16)
            qd_ref[0, r0:r0 + c, hs] = (q[s][hr] * eg[s][hr]).astype(BF16)
            kd_ref[0, r0:r0 + c, hs] = (k[s][hr] * jnp.exp(g_last - gcb[s][hr])).astype(BF16)
            qk_ref[0, h, r0:r0 + c, :] = qk[s][:, h * c:(h + 1) * c].astype(BF16)
            gl_ref[0, s, h:h + 1, :] = jnp.exp(g_last)


def _delta_prep(conv_in, prefix, conv_w, small, valid):
    n, lp, cd = conv_in.shape
    ta = next((r for r in (4 * LANE, 2 * LANE, LANE) if lp % r == 0), lp)
    nt = lp // ta
    tb = ta // 8
    f = lambda dt, w: jax.ShapeDtypeStruct((n, lp, w), dt)
    blk = lambda w: pl.BlockSpec((1, ta, w), lambda b, i: (b, i, 0))
    width = H_B * DK_B
    return pl.pallas_call(
        functools.partial(_delta_prep_kernel, ta=ta, valid=valid),
        grid=(n, nt),
        in_specs=[
            blk(cd),
            pl.BlockSpec((1, 8, cd), lambda b, i: (b, jnp.maximum(i * tb - 1, 0), 0)),
            pl.BlockSpec((1, 8, cd), lambda b, i: (b, 0, 0)),
            pl.BlockSpec((8, cd), lambda b, i: (0, 0)),
            blk(LANE),
        ],
        out_specs=[
            blk(width), blk(width), blk(width), blk(width),
            pl.BlockSpec((1, H_B, ta, DELTA_CHUNK), lambda b, i: (b, 0, i, 0)),
            pl.BlockSpec((1, ta // DELTA_CHUNK, H_B, LANE), lambda b, i: (b, i, 0, 0)),
        ],
        out_shape=[
            f(F32, width), f(BF16, width), f(BF16, width), f(BF16, width),
            jax.ShapeDtypeStruct((n, H_B, lp, DELTA_CHUNK), BF16),
            jax.ShapeDtypeStruct((n, lp // DELTA_CHUNK, H_B, LANE), F32),
        ],
        compiler_params=_params("parallel", "parallel"),
        name="delta_prep",
    )(conv_in, conv_in, prefix, conv_w, small)


def _delta_scan_kernel(wv_ref, wk_ref, qd_ref, kd_ref, qk_ref, gl_ref, s0_ref, gz_ref, onw_ref,
                       o_ref, s_ref, *, nb, n_chunks):
    c = DELTA_CHUNK
    dk = DK_B

    @pl.when(pl.program_id(1) == 0)
    def _():
        s_ref[...] = s0_ref[...]

    onw = onw_ref[...]

    chains = [(b, h) for b in range(nb) for h in range(H_B)]
    cols = lambda h: slice(h * dk, (h + 1) * dk)

    def body(ci, _):
        r0 = pl.multiple_of(ci * c, c)
        rows = pl.ds(r0, c)
        state = [s_ref[b, h] for b, h in chains]
        prod = [_dot(jnp.concatenate([wk_ref[b, rows, cols(h)], qd_ref[b, rows, cols(h)]], axis=0), s.astype(BF16))
                for (b, h), s in zip(chains, state)]
        vb = [(wv_ref[b, rows, cols(h)] - p[:c]).astype(BF16) for (b, h), p in zip(chains, prod)]
        o = [p[c:] + _dot(qk_ref[b, h, rows, :], v) for (b, h), p, v in zip(chains, prod, vb)]
        new = [s * gl_ref[b, pl.ds(ci, 1), h, :] + _dot_tn(kd_ref[b, rows, cols(h)], v)
               for (b, h), s, v in zip(chains, state, vb)]
        for (b, h), s, oo in zip(chains, new, o):
            s_ref[b, h] = s
            on = oo * lax.rsqrt(jnp.mean(oo * oo, axis=-1, keepdims=True) + EPS) * onw
            o_ref[b, rows, cols(h)] = (on * _silu(gz_ref[b, rows, cols(h)])).astype(o_ref.dtype)
        return 0

    lax.fori_loop(0, n_chunks, body, 0)


def _delta_scan(wv, wk, qd, kd, qk, gl, s0, gz, onw, nb):
    n, lp, width = wv.shape
    tl = _row_tile(lp, 512)
    n_chunks = tl // DELTA_CHUNK
    seq = pl.BlockSpec((nb, tl, width), lambda i, t: (i, t, 0))
    state = pl.BlockSpec((nb,) + s0.shape[1:], lambda i, t: (i, 0, 0, 0))
    return pl.pallas_call(
        functools.partial(_delta_scan_kernel, nb=nb, n_chunks=n_chunks),
        grid=(n // nb, lp // tl),
        in_specs=[seq, seq, seq, seq,
                  pl.BlockSpec((nb, H_B, tl, DELTA_CHUNK), lambda i, t: (i, 0, t, 0)),
                  pl.BlockSpec((nb, n_chunks, H_B, LANE), lambda i, t: (i, t, 0, 0)),
                  state, seq, pl.BlockSpec(onw.shape, lambda i, t: (0, 0))],
        out_specs=[seq, state],
        out_shape=[jax.ShapeDtypeStruct((n, lp, width), BF16), jax.ShapeDtypeStruct(s0.shape, F32)],
        compiler_params=_params("parallel", "arbitrary"),
        name="delta_scan",
    )(wv, wk, qd, kd, qk, gl, s0, gz, onw)


def _fox_sample_kernel(pt_ref, q_ref, kn_ref, vn_ref, lfn_ref, kc_hbm, vc_hbm, lfc_hbm, o_ref,
                       qbd_ref, m_ref, l_ref, acc_ref, carry_ref, kbuf, vbuf, lfbuf, sems,
                       *, pages_per_step, n_q):
    pps = pages_per_step
    b = pl.program_id(0)
    j = pl.program_id(1)
    n_b = pl.num_programs(0)
    n_steps = pl.num_programs(1)
    n_pages = pt_ref.shape[1]
    rows = n_q * H_A
    page = lfn_ref.shape[2]
    width = H_A * D_A
    ri = lax.broadcasted_iota(jnp.int32, (page, page), 0)
    ci = lax.broadcasted_iota(jnp.int32, (page, page), 1)

    per_seq = n_steps - 1
    n_slots = kbuf.shape[0]
    g = b * per_seq + j - 1

    def copies(gg, p):
        src = pt_ref[gg // per_seq, n_pages - 1 - ((gg % per_seq) * pps + p)]
        s = gg % n_slots
        return (pltpu.make_async_copy(kc_hbm.at[src], kbuf.at[s, p], sems.at[0, s]),
                pltpu.make_async_copy(vc_hbm.at[src], vbuf.at[s, p], sems.at[1, s]),
                pltpu.make_async_copy(lfc_hbm.at[src], lfbuf.at[s, p], sems.at[2, s]))

    def fetch(gg):
        @pl.when(gg < n_b * per_seq)
        def _():
            for p in range(pps):
                for cp in copies(gg, p):
                    cp.start()

    @pl.when((b == 0) & (j == 0))
    def _():
        for ahead in range(n_slots - 1):
            fetch(jnp.int32(ahead))

    @pl.when(j > 0)
    def _():
        for p in range(pps):
            for cp in copies(g, p):
                cp.wait()
        fetch(g + n_slots - 1)

    def attend(blocks):
        qbd = qbd_ref[...]
        ss = []
        for kt, _, bias8, mask in blocks:
            s = _dot(qbd, kt.astype(BF16)) + jnp.concatenate([bias8] * n_q, axis=0)
            ss.append(s if mask is None else jnp.where(mask, s, -jnp.inf))
        m_old = m_ref[...]
        m_new = m_old
        for s in ss:
            m_new = jnp.maximum(m_new, jnp.max(s, axis=-1, keepdims=True))
        alpha = jnp.exp(m_old - m_new)
        ps = [jnp.exp(s - m_new) for s in ss]
        sums = [jnp.sum(p, axis=-1, keepdims=True) for p in ps]
        pvs = [_dot_nt(p.astype(BF16), vt.astype(BF16)) for p, (_, vt, _, _) in zip(ps, blocks)]
        l_ref[...] = alpha * l_ref[...] + sum(sums)
        acc_ref[...] = alpha * acc_ref[...] + sum(pvs)
        m_ref[...] = m_new

    @pl.when(j == 0)
    def _():
        q = q_ref[0] * (D_A ** -0.5)
        qrep = jnp.concatenate([jnp.broadcast_to(q[t:t + 1], (H_A, width)) for t in range(n_q)], axis=0)
        r = lax.broadcasted_iota(jnp.int32, qrep.shape, 0)
        ln = lax.broadcasted_iota(jnp.int32, qrep.shape, 1)
        qbd_ref[...] = jnp.where(ln // D_A == r % H_A, qrep, 0.0).astype(BF16)
        m_ref[...] = jnp.full(m_ref.shape, -jnp.inf, F32)
        l_ref[...] = jnp.zeros(l_ref.shape, F32)
        acc_ref[...] = jnp.zeros(acc_ref.shape, F32)
        carry_ref[...] = jnp.zeros(carry_ref.shape, F32)
        upper = (ri <= ci).astype(BF16)
        csum = sum(_dot(p, upper) for p in _split3(lfn_ref[0]))
        r2 = lax.broadcasted_iota(jnp.int32, (rows, page), 0)
        c2 = lax.broadcasted_iota(jnp.int32, (rows, page), 1)
        attend([(kn_ref[0], vn_ref[0], -csum, c2 <= r2 // H_A)])

    @pl.when(j > 0)
    def _():
        after = (ri > ci).astype(BF16)
        carry = carry_ref[...]
        slot = g % n_slots
        lfs = [lfbuf[slot, p] for p in range(pps)]
        pieces = [_split3(lf) for lf in lfs]
        suffix = [sum(_dot(p, after) for p in ps) for ps in pieces]
        totals = [jnp.sum(lf, axis=-1, keepdims=True) for lf in lfs]
        blocks = []
        for p, (sfx, tot) in enumerate(zip(suffix, totals)):
            blocks.append((kbuf[slot, p].reshape(width, page), vbuf[slot, p].reshape(width, page), carry + sfx, None))
            carry = carry + tot
        attend(blocks)
        carry_ref[...] = carry

    @pl.when(j == pl.num_programs(1) - 1)
    def _():
        out = acc_ref[...] / l_ref[...]
        r = lax.broadcasted_iota(jnp.int32, out.shape, 0)
        ln = lax.broadcasted_iota(jnp.int32, out.shape, 1)
        out = jnp.where(ln // D_A == r % H_A, out, 0.0)
        o_ref[0] = jnp.sum(out.reshape(n_q, H_A, width), axis=1)


def _fox_sample(page_table, q, kn, vn, lfn, cache_k, cache_v, cache_lf):
    b, n_q, width = q.shape
    n_pages = page_table.shape[1]
    page = cache_k.shape[3]
    rows = n_q * H_A
    pps = next(p for p in (16, 8, 1) if n_pages % p == 0)
    n_slots = 3
    cur = lambda i, j, pt: (i, 0, 0)
    hbm = pl.BlockSpec(memory_space=pl.ANY)
    grid_spec = pltpu.PrefetchScalarGridSpec(
        num_scalar_prefetch=1,
        grid=(b, n_pages // pps + 1),
        in_specs=[
            pl.BlockSpec((1, n_q, width), cur),
            pl.BlockSpec((1, width, page), cur),
            pl.BlockSpec((1, width, page), cur),
            pl.BlockSpec((1, H_A, page), cur),
            hbm, hbm, hbm,
        ],
        out_specs=pl.BlockSpec((1, n_q, width), cur),
        scratch_shapes=[
            pltpu.VMEM((rows, width), BF16),
            pltpu.VMEM((rows, 1), F32),
            pltpu.VMEM((rows, 1), F32),
            pltpu.VMEM((rows, width), F32),
            pltpu.VMEM((H_A, 1), F32),
            pltpu.VMEM((n_slots, pps, H_A, D_A, page), F32),
            pltpu.VMEM((n_slots, pps, H_A, D_A, page), F32),
            pltpu.VMEM((n_slots, pps, H_A, page), F32),
            pltpu.SemaphoreType.DMA((3, n_slots)),
        ],
    )
    return pl.pallas_call(
        functools.partial(_fox_sample_kernel, pages_per_step=pps, n_q=n_q),
        grid_spec=grid_spec,
        out_shape=jax.ShapeDtypeStruct((b, n_q, width), F32),
        compiler_params=_params("arbitrary", "arbitrary"),
        name="fox_sample",
    )(page_table, q, kn, vn, lfn, cache_k, cache_v, cache_lf)


def _router_kernel(x_ref, nw_ref, wr_ref, g_ref, grp_ref, rank_ref, cnt_ref, run_ref):
    _route(x_ref[...], nw_ref, wr_ref, g_ref, grp_ref, rank_ref, cnt_ref, run_ref)


def _route(x, nw_ref, wr_ref, g_ref, grp_ref, rank_ref, cnt_ref, run_ref):
    step = pl.program_id(0)
    tm = x.shape[0]

    @pl.when(step == 0)
    def _():
        run_ref[...] = jnp.zeros(run_ref.shape, F32)

    h = _rms(x, nw_ref[...])
    h_hi = h.astype(BF16)
    h_lo = (h - h_hi.astype(F32)).astype(BF16)
    logits = _dot(h_hi, wr_ref[0]) + (_dot(h_lo, wr_ref[0]) + _dot(h_hi, wr_ref[1]))
    lane = lax.broadcasted_iota(jnp.int32, logits.shape, 1)
    lanef = lane.astype(F32)
    neg = -jnp.inf
    is_group = (lane >= N_EXPERTS) & (lane < N_EXPERTS + N_GROUPS)
    gl = jnp.where(is_group, logits, neg)
    gmax = jnp.max(gl, axis=-1, keepdims=True)
    g_sel = jnp.min(jnp.where(gl == gmax, lanef - N_EXPERTS, 1e9), axis=-1, keepdims=True)
    p_sel = 1.0 / jnp.sum(jnp.where(is_group, jnp.exp(logits - gmax), 0.0), axis=-1, keepdims=True)
    in_sel = (lane < N_EXPERTS) & ((lane // EXPERTS_PER_GROUP).astype(F32) == g_sel)
    el = jnp.where(in_sel, logits, neg)
    m1 = jnp.max(el, axis=-1, keepdims=True)
    i1 = jnp.min(jnp.where(el == m1, lanef, 1e9), axis=-1, keepdims=True)
    el2 = jnp.where(lanef == i1, neg, el)
    m2 = jnp.max(el2, axis=-1, keepdims=True)
    i2 = jnp.min(jnp.where(el2 == m2, lanef, 1e9), axis=-1, keepdims=True)
    e2 = jnp.exp(m2 - m1)
    w1 = p_sel / (1.0 + e2)
    g_ref[...] = jnp.where(lanef == i1, w1, jnp.where(lanef == i2, w1 * e2, jnp.where(lane == N_EXPERTS, g_sel, 0.0)))

    onehot_t = (lanef == g_sel).astype(F32).T[:8]
    ri = lax.broadcasted_iota(jnp.int32, (tm, tm), 0)
    ci = lax.broadcasted_iota(jnp.int32, (tm, tm), 1)
    earlier = _dot(onehot_t.astype(BF16), (ri < ci).astype(BF16))
    run = run_ref[...]
    gidx = lax.broadcasted_iota(jnp.int32, (8, tm), 0).astype(F32)
    rank_ref[0] = jnp.sum(onehot_t * (earlier + run), axis=0, keepdims=True).astype(jnp.int32)
    grp_ref[0] = jnp.sum(onehot_t * gidx, axis=0, keepdims=True).astype(jnp.int32)
    run = run + jnp.sum(onehot_t, axis=-1, keepdims=True)
    run_ref[...] = run
    cnt_ref[...] = jnp.broadcast_to(run, cnt_ref.shape)


def _router(x, nw, wr):
    t, d = x.shape
    tm = _row_tile(t, 512)
    nt = t // tm
    row = pl.BlockSpec((1, 1, tm), lambda i: (i, 0, 0))
    return pl.pallas_call(
        _router_kernel,
        grid=(nt,),
        in_specs=[pl.BlockSpec((tm, d), lambda i: (i, 0)), pl.BlockSpec(nw.shape, lambda i: (0, 0)),
                  pl.BlockSpec(wr.shape, lambda i: (0, 0, 0))],
        out_specs=[pl.BlockSpec((tm, LANE), lambda i: (i, 0)), row, row, pl.BlockSpec((8, LANE), lambda i: (0, 0))],
        out_shape=[jax.ShapeDtypeStruct((t, LANE), F32), jax.ShapeDtypeStruct((nt, 1, tm), jnp.int32),
                   jax.ShapeDtypeStruct((nt, 1, tm), jnp.int32), jax.ShapeDtypeStruct((8, LANE), F32)],
        scratch_shapes=[pltpu.VMEM((8, 1), F32)],
        compiler_params=_params("arbitrary"),
        name="moe_router",
    )(x, nw, wr)


def _moe_kernel(x_ref, nw_ref, g_ref, wg_ref, wu_ref, wd_ref, nf_ref, o_ref, hb_ref, acc_ref, *, final_norm):
    e = pl.program_id(1)

    @pl.when(e == 0)
    def _():
        hb_ref[...] = _rms(x_ref[...], nw_ref[...]).astype(BF16)
        acc_ref[...] = jnp.zeros(acc_ref.shape, F32)

    gates = g_ref[...]
    lane = lax.broadcasted_iota(jnp.int32, gates.shape, 1)
    gcol = jnp.sum(jnp.where(lane == e, gates, 0.0), axis=-1, keepdims=True)
    hb = hb_ref[...]
    act = _silu(_dot(hb, wg_ref[0, 0])) * _dot(hb, wu_ref[0, 0]) * gcol
    acc_ref[...] += _dot(act.astype(BF16), wd_ref[0, 0])

    @pl.when(e == pl.num_programs(1) - 1)
    def _():
        y = x_ref[...] + acc_ref[...]
        o_ref[...] = _rms(y, nf_ref[...]) if final_norm else y


def _moe_dense(x, nw, gates, wg, wu, wd, nf, layer, final_norm):
    t, d = x.shape
    tm = _row_tile(t, 1024)
    _, n_e, _, f = wg.shape
    return pl.pallas_call(
        functools.partial(_moe_kernel, final_norm=final_norm),
        grid=(t // tm, n_e),
        in_specs=[
            pl.BlockSpec((tm, d), lambda i, e: (i, 0)),
            pl.BlockSpec(nw.shape, lambda i, e: (0, 0)),
            pl.BlockSpec((tm, LANE), lambda i, e: (i, 0)),
            pl.BlockSpec((1, 1, d, f), lambda i, e: (layer, e, 0, 0)),
            pl.BlockSpec((1, 1, d, f), lambda i, e: (layer, e, 0, 0)),
            pl.BlockSpec((1, 1, f, d), lambda i, e: (layer, e, 0, 0)),
            pl.BlockSpec(nf.shape, lambda i, e: (0, 0)),
        ],
        out_specs=pl.BlockSpec((tm, d), lambda i, e: (i, 0)),
        out_shape=jax.ShapeDtypeStruct((t, d), F32),
        scratch_shapes=[pltpu.VMEM((tm, d), BF16), pltpu.VMEM((tm, d), F32)],
        compiler_params=_params("parallel", "arbitrary"),
        name="moe_experts",
    )(x, nw, gates, wg, wu, wd, nf)


def _row_copy(src, src_row, dst, dst_row, sem):
    return pltpu.make_async_copy(src.at[pl.ds(src_row, 1)], dst.at[pl.ds(dst_row, 1)], sem)


def _dispatch_kernel(zt_ref, dest_ref, x_ref, g_ref, o_ref, stage_ref, sem):
    tm, d = x_ref.shape

    @pl.when(pl.program_id(0) == 0)
    def _():
        stage_ref[...] = jnp.zeros(stage_ref.shape, F32)
        for k in range(zt_ref.shape[0]):
            cp = pltpu.make_async_copy(stage_ref, o_ref.at[pl.ds(pl.multiple_of(zt_ref[k] * tm, tm), tm)], sem)
            cp.start()
            cp.wait()

    stage_ref[:, :d] = x_ref[...]
    stage_ref[:, d:] = g_ref[...]

    def start(r2, _):
        for pr in range(2):
            r = 2 * r2 + pr
            _row_copy(stage_ref, r, o_ref, dest_ref[0, 0, r], sem).start(priority=pr)
        return 0

    def wait(r, _):
        _row_copy(stage_ref, 0, o_ref, 0, sem).wait()
        return 0

    lax.fori_loop(0, tm // 2, start, 0, unroll=4)
    lax.fori_loop(0, tm, wait, 0, unroll=8)


def _dispatch(zero_tiles, dest, x, gates, n_tiles, tm):
    t, d = x.shape
    grid_spec = pltpu.PrefetchScalarGridSpec(
        num_scalar_prefetch=1,
        grid=(t // tm,),
        in_specs=[pl.BlockSpec((1, 1, tm), lambda i, zt: (i, 0, 0), memory_space=pltpu.SMEM),
                  pl.BlockSpec((tm, d), lambda i, zt: (i, 0)),
                  pl.BlockSpec((tm, LANE), lambda i, zt: (i, 0))],
        out_specs=pl.BlockSpec(memory_space=pl.ANY),
        scratch_shapes=[pltpu.VMEM((tm, d + LANE), F32), pltpu.SemaphoreType.DMA(())],
    )
    return pl.pallas_call(
        _dispatch_kernel,
        grid_spec=grid_spec,
        out_shape=jax.ShapeDtypeStruct((n_tiles * tm, d + LANE), F32),
        compiler_params=_params("arbitrary"),
        name="moe_dispatch",
    )(zero_tiles, dest, x, gates)


def _moe_group_kernel(tg_ref, na_ref, xs_ref, nw_ref, wg_ref, wu_ref, wd_ref, nf_ref, o_ref, *, final_norm):
    i = pl.program_id(0)
    d = o_ref.shape[1]

    @pl.when(i < na_ref[0])
    def _():
        x = xs_ref[:, :d]
        gates = xs_ref[:, d:]
        hb = _rms(x, nw_ref[...]).astype(BF16)
        lane = lax.broadcasted_iota(jnp.int32, gates.shape, 1)
        first = tg_ref[i] * EXPERTS_PER_GROUP
        acc = jnp.zeros(x.shape, F32)
        for e in range(EXPERTS_PER_GROUP):
            gcol = jnp.sum(jnp.where(lane == first + e, gates, 0.0), axis=-1, keepdims=True)
            act = _silu(_dot(hb, wg_ref[0, 0, e])) * _dot(hb, wu_ref[0, 0, e]) * gcol
            acc += _dot(act.astype(BF16), wd_ref[0, 0, e])
        y = x + acc
        o_ref[...] = _rms(y, nf_ref[...]) if final_norm else y

    @pl.when(i >= na_ref[0])
    def _():
        o_ref[...] = jnp.zeros(o_ref.shape, F32)


def _moe_group(tile_group, n_active, xs, nw, wg, wu, wd, nf, tm, layer, final_norm):
    tp, dx = xs.shape
    d = dx - LANE
    wspec = lambda a: pl.BlockSpec((1, 1) + a.shape[2:], lambda i, tg, na: (layer, tg[i], 0, 0, 0))
    vec = lambda a: pl.BlockSpec(a.shape, lambda i, tg, na: (0, 0))
    grid_spec = pltpu.PrefetchScalarGridSpec(
        num_scalar_prefetch=2,
        grid=(tp // tm,),
        in_specs=[pl.BlockSpec((tm, dx), lambda i, tg, na: (i, 0)), vec(nw), wspec(wg), wspec(wu), wspec(wd), vec(nf)],
        out_specs=pl.BlockSpec((tm, d), lambda i, tg, na: (i, 0)),
    )
    return pl.pallas_call(
        functools.partial(_moe_group_kernel, final_norm=final_norm),
        grid_spec=grid_spec,
        out_shape=jax.ShapeDtypeStruct((tp, d), F32),
        compiler_params=_params("arbitrary"),
        name="moe_group_experts",
    )(tile_group, n_active, xs, nw, wg, wu, wd, nf)


def _combine_kernel(dest_ref, ys_ref, o_ref, sem):
    tm = o_ref.shape[0]

    def start(r2, _):
        for pr in range(2):
            r = 2 * r2 + pr
            _row_copy(ys_ref, dest_ref[0, 0, r], o_ref, r, sem).start(priority=pr)
        return 0

    def wait(r, _):
        _row_copy(ys_ref, 0, o_ref, 0, sem).wait()
        return 0

    lax.fori_loop(0, tm // 2, start, 0, unroll=4)
    lax.fori_loop(0, tm, wait, 0, unroll=8)


def _combine(dest, ys, t, tm):
    d = ys.shape[1]
    return pl.pallas_call(
        _combine_kernel,
        grid=(t // tm,),
        in_specs=[pl.BlockSpec((1, 1, tm), lambda i: (i, 0, 0), memory_space=pltpu.SMEM),
                  pl.BlockSpec(memory_space=pl.ANY)],
        out_specs=pl.BlockSpec((tm, d), lambda i: (i, 0)),
        out_shape=jax.ShapeDtypeStruct((t, d), F32),
        scratch_shapes=[pltpu.SemaphoreType.DMA(())],
        compiler_params=_params("arbitrary"),
        name="moe_combine",
    )(dest, ys)


def _moe_sorted(x, nw, routed, wg, wu, wd, nf, layer, final_norm):
    t, d = x.shape
    gates, grp, rank, cnt = routed
    tm = grp.shape[2]
    counts = [cnt[g, 0].astype(jnp.int32) for g in range(N_GROUPS)]
    tiles = [(c + tm - 1) // tm for c in counts]
    tile_end = [sum(tiles[:g + 1]) for g in range(N_GROUPS)]
    dest = rank + sum(jnp.where(grp == g, (tile_end[g] - tiles[g]) * tm, 0) for g in range(N_GROUPS))
    n_tiles = t // tm + N_GROUPS
    tile = jnp.arange(n_tiles, dtype=jnp.int32)
    tile_group = jnp.minimum(sum((tile >= e).astype(jnp.int32) for e in tile_end), N_GROUPS - 1)
    last = n_tiles - 1
    zero_tiles = jnp.stack([jnp.where(tiles[g] > 0, tile_end[g] - 1, last) for g in range(N_GROUPS)]
                           + [jnp.minimum(tile_end[-1] + k, last) for k in range(N_GROUPS)]).astype(jnp.int32)
    xs = _dispatch(zero_tiles, dest, x, gates, n_tiles, tm)
    split = lambda w: w.reshape((w.shape[0], N_GROUPS, EXPERTS_PER_GROUP) + w.shape[2:])
    ys = _moe_group(tile_group, tile_end[-1].reshape(1), xs, nw,
                    split(wg), split(wu), split(wd), nf, tm, layer, final_norm)
    return _combine(dest, ys, t, tm)


def _odd_kernel(x_ref, nw_ref, win_ref, lnw_ref, lnb_ref, wm_ref, bs_ref, wout_ref, *refs, emit_v):
    if emit_v:
        o_ref, v_ref, gated_ref = refs
    else:
        o_ref, gated_ref = refs
    tm = x_ref.shape[0]
    dc = lnw_ref.shape[1]
    gd = dc // H_C
    x = x_ref[...]
    hb = _rms(x, nw_ref[...]).astype(BF16)
    z = _dot(hb, win_ref[...])
    z = 0.5 * z * (1.0 + lax.erf(z * (2.0 ** -0.5)))
    v = z[:, dc:]
    mu = jnp.mean(v, axis=-1, keepdims=True)
    vc = v - mu
    vn = vc * lax.rsqrt(jnp.mean(vc * vc, axis=-1, keepdims=True) + EPS) * lnw_ref[...] + lnb_ref[...]
    if emit_v:
        v_ref[...] = vn
    vb = vn.astype(BF16)
    for ci in range(tm // CHUNK_C):
        rs = slice(ci * CHUNK_C, (ci + 1) * CHUNK_C)
        for g in range(H_C):
            cs = slice(g * gd, (g + 1) * gd)
            mixed = _dot(wm_ref[g], vb[rs, cs]) + jnp.concatenate([bs_ref[g]] * (gd // LANE), axis=1)
            gated_ref[rs, cs] = (z[rs, cs] * mixed).astype(BF16)
    o_ref[...] = x + _dot(gated_ref[...], wout_ref[...])


def _odd_mixer(x, nw, w_in, ln_w, ln_b, wm, bs, w_out, emit_v):
    t, d = x.shape
    dc = ln_w.shape[1]
    tm = _row_tile(t, 512)
    row = lambda w: pl.BlockSpec((tm, w), lambda i: (i, 0))
    full = lambda a: pl.BlockSpec(a.shape, lambda i: (0,) * a.ndim)
    out_specs = [row(d)] + ([row(dc)] if emit_v else [])
    out_shape = [jax.ShapeDtypeStruct((t, d), F32)] + ([jax.ShapeDtypeStruct((t, dc), F32)] if emit_v else [])
    return pl.pallas_call(
        functools.partial(_odd_kernel, emit_v=emit_v),
        grid=(t // tm,),
        in_specs=[row(d), full(nw), full(w_in), full(ln_w), full(ln_b), full(wm), full(bs), full(w_out)],
        out_specs=out_specs,
        out_shape=out_shape,
        scratch_shapes=[pltpu.VMEM((tm, dc), BF16)],
        compiler_params=_params("parallel"),
        name="odd_mixer",
    )(x, nw, w_in, ln_w, ln_b, wm, bs, w_out)


def _prep_even_weights(w_in, f_bias, a_log, dt_bias):
    d = w_in.shape[0]
    small = jnp.concatenate(
        [w_in[:, 1536:1544], w_in[:, 3080:3088], jnp.zeros((d, LANE - 16), w_in.dtype)], axis=1)
    w = jnp.concatenate([w_in[:, 0:512], w_in[:, 1544:3080], w_in[:, 3088:3600], small], axis=1)
    bias = jnp.zeros((1, LANE), F32).at[0, 0:8].set(f_bias).at[0, 8:12].set(dt_bias)
    alog = jnp.zeros((1, LANE), F32).at[0, 8:12].set(a_log)
    return w.astype(BF16), w_in[:, 512:1536].T.astype(BF16), bias, alog


def kernel(x_prompt, x_sample, cache_k, cache_v, cache_logf, state_delta, state_conv, page_table, norm_mix, norm_ffn, norm_final, w_in_even, w_out_even, fox_forget_bias, dn_conv_w, dn_a_log, dn_dt_bias, dn_norm_w, w_in_odd, gm_ln_w, gm_ln_b, gm_spatial_w, gm_spatial_b, w_out_odd, moe_router_group, moe_router_expert, moe_w_gate, moe_w_up, moe_w_down):
    n_p, seq, d = x_prompt.shape
    n_s, dec = x_sample.shape[:2]
    page = cache_k.shape[2]
    mix_a = H_A * D_A
    xp = x_prompt.reshape(n_p * seq, d)
    xs = x_sample.reshape(n_s * dec, d)
    row = lambda a: a[None, :]

    def router_w(layer):
        wr = jnp.concatenate([moe_router_expert[layer], moe_router_group[layer],
                              jnp.zeros((d, LANE - N_EXPERTS - N_GROUPS), F32)], axis=1)
        wr_hi = wr.astype(BF16)
        return jnp.stack([wr_hi, (wr - wr_hi.astype(F32)).astype(BF16)])

    even_w = (row(norm_mix[0]),) + _prep_even_weights(w_in_even[0], fox_forget_bias[0], dn_a_log[0], dn_dt_bias[0])
    from_kt = lambda a, n, length: jnp.transpose(a.reshape(n, H_A, D_A, length), (0, 3, 1, 2))[None]
    w_out_e = w_out_even[0].astype(BF16)
    conv_w = jnp.zeros((8, dn_conv_w.shape[2]), F32).at[:CONV_W].set(dn_conv_w[0])
    onw = row(dn_norm_w[0])

    def delta(conv3, prefix, small3, gz3, s0, valid, nb):
        wv, wk, qd, kd, qk, gl = _delta_prep(conv3, prefix, conv_w, small3, valid)
        return _delta_scan(wv, wk, qd, kd, qk, gl, s0, gz3, onw, nb)

    q, kt, vt, c, gz, s = _even_in(xp, *even_w, n_p, seq)
    logf_p = s[:, :H_A].reshape(n_p, seq, H_A)
    csum = _cumsum_lanes(jnp.swapaxes(logf_p, 1, 2).reshape(n_p * H_A, seq)).reshape(n_p, H_A, seq)
    as3 = lambda a: a.reshape(n_p, seq, a.shape[-1])
    oa_p = _fox_prompt(as3(q), kt, vt, csum)
    conv_p = as3(c)
    ob_p, sd_p = delta(conv_p, jnp.zeros((n_p, 8, conv_p.shape[-1]), F32), as3(s), as3(gz),
                       jnp.zeros((n_p, H_B, DK_B, DK_B), F32), seq, 4)
    xp, routed_p = _even_out(xp, oa_p.reshape(n_p * seq, mix_a), ob_p.reshape(n_p * seq, -1), w_out_e,
                             row(norm_ffn[0]), router_w(0))
    k_p = from_kt(kt, n_p, seq)
    v_p = from_kt(vt, n_p, seq)
    conv_state_p = conv_p[:, seq - (CONV_W - 1):][None]

    q, kt, vt, c, gz, s = _even_in(xs, *even_w, 1, n_s * dec)
    as3 = lambda a: a.reshape(n_s, dec, a.shape[-1])
    padrows = lambda a, n: jnp.pad(a, ((0, 0), (0, n - a.shape[1]), (0, 0)))
    logf_s = s[:, :H_A].reshape(n_s, dec, H_A)
    pad_keys = lambda a: jnp.pad(a, ((0, 0), (0, 0), (0, page - dec)))
    per_seq = lambda a: pad_keys(jnp.swapaxes(a[0].reshape(mix_a, n_s, dec), 0, 1))
    oa_s = _fox_sample(page_table, as3(q), per_seq(kt), per_seq(vt), pad_keys(jnp.swapaxes(logf_s, 1, 2)),
                       jnp.transpose(cache_k[0], (0, 2, 3, 1)), jnp.transpose(cache_v[0], (0, 2, 3, 1)),
                       jnp.swapaxes(cache_logf[0], 1, 2))
    conv_s = as3(c)
    lp = DELTA_CHUNK
    prefix_s = jnp.pad(state_conv[0], ((0, 0), (8 - (CONV_W - 1), 0), (0, 0)))
    ob_s, sd_s = delta(padrows(conv_s, lp), prefix_s, padrows(as3(s), lp), padrows(as3(gz), lp),
                       state_delta[0], dec, 8)
    xs, routed_s = _even_out(xs, oa_s.reshape(n_s * dec, mix_a).astype(BF16), ob_s[:, :dec].reshape(n_s * dec, -1),
                             w_out_e, row(norm_ffn[0]), router_w(0))
    k_s = kt[0].T.reshape(1, n_s, dec, H_A, D_A)
    v_s = vt[0].T.reshape(1, n_s, dec, H_A, D_A)
    conv_state_s = jnp.concatenate([state_conv[0], conv_s], axis=1)[:, dec:][None]

    def moe(x, layer, final, routed=None):
        if routed is None:
            routed = _router(x, row(norm_ffn[layer]), router_w(layer))
        if x.shape[0] >= 4096:
            return _moe_sorted(x, row(norm_ffn[layer]), routed, wg, wu, wd, row(norm_final), layer, final)
        return _moe_dense(x, row(norm_ffn[layer]), routed[0], wg, wu, wd, row(norm_final), layer, final)

    wg, wu, wd = moe_w_gate.astype(BF16), moe_w_up.astype(BF16), moe_w_down.astype(BF16)
    xp = moe(xp, 0, False, routed_p)
    xs = moe(xs, 0, False, routed_s)

    idx = jnp.arange(CHUNK_C)
    w_m = jnp.where(idx[:, None] >= idx[None, :], gm_spatial_w[0], 0.0)
    b_s = gm_spatial_b[0]
    lanes = lambda b: jnp.broadcast_to(b[:, :, None], b.shape + (LANE,))
    reps = CHUNK_C // dec
    assert n_s * dec == CHUNK_C, "sample group must fill exactly one 128-row mixing tile"
    w_m_s = (jnp.eye(reps, dtype=F32)[None, :, None, :, None] * w_m[:, None, :dec, None, :dec]).reshape(H_C, CHUNK_C, CHUNK_C)
    b_s_s = jnp.tile(b_s[:, :dec], (1, reps))
    odd_w = (row(norm_mix[1]), w_in_odd[0].astype(BF16), row(gm_ln_w[0]), row(gm_ln_b[0]))
    w_out_o = w_out_odd[0].astype(BF16)
    (xp,) = _odd_mixer(xp, *odd_w, w_m.astype(BF16), lanes(b_s), w_out_o, False)
    xs, v_rows = _odd_mixer(xs, *odd_w, w_m_s.astype(BF16), lanes(b_s_s), w_out_o, True)

    y_p = moe(xp, 1, True).reshape(n_p, seq, d)
    y_s = moe(xs, 1, True).reshape(n_s, dec, d)
    return (y_p, y_s, k_p, v_p, logf_p[None], sd_p[None], conv_state_p,
            k_s, v_s, logf_s[None], sd_s[None], conv_state_s, v_rows.reshape(1, n_s, dec, -1))
```
